```python
import math
import jax
import jax.numpy as jnp
from jax import lax
import numpy as np

D_MODEL = 1024
BATCH = 8
SEQ = 4096
DEPTH = 2

GRID_W = 64
CTX_LEN = 256
HEAD_DIM = 64
ROPE_THETA = 10000.0
NORM_EPS = 1e-6
Q_BLOCK = 128
NEG_INF = -1e30
N_MOD = 6

HALF_WIDTH = D_MODEL // 2

A_HEADS = HALF_WIDTH // HEAD_DIM
A_KV_HEADS = 2
A_GROUP = A_HEADS // A_KV_HEADS
A_SCALE = HEAD_DIM ** -0.5

B_NOPE = 64
B_ROPE = 32
B_VDIM = 64
B_HEADS = HALF_WIDTH // B_VDIM
B_Q_RANK = 384
B_KV_RANK = 256
B_SCALE = (B_NOPE + B_ROPE) ** -0.5

C_WIDTH = HALF_WIDTH
C_EMB_DIM = 5
C_BANDS = (C_EMB_DIM - 1) // 2
C_FILTER_WIDTH = 64
C_MIN_DECAY = math.log(1e-2) / 1.5
C_MAX_DECAY = math.log(1e-2) / 0.3

D_HEADS = HALF_WIDTH // HEAD_DIM
D_KV_HEADS = 2
D_GROUP = D_HEADS // D_KV_HEADS
D_SCALE = HEAD_DIM ** -0.5
WINDOW = 128

FFN_DIM = 2816

EV_SIZES = (A_HEADS * HEAD_DIM, B_Q_RANK, A_KV_HEADS * HEAD_DIM, A_KV_HEADS * HEAD_DIM, B_KV_RANK, B_ROPE)
EV_Q_COLS = A_HEADS * HEAD_DIM + B_Q_RANK
EV_COLS = EV_Q_COLS + 2 * A_KV_HEADS * HEAD_DIM + B_KV_RANK + B_ROPE
EV_OUT = A_HEADS * HEAD_DIM + B_HEADS * B_VDIM
OD_SIZES = (D_HEADS * HEAD_DIM, 3 * C_WIDTH, D_KV_HEADS * HEAD_DIM, D_KV_HEADS * HEAD_DIM)
OD_Q_COLS = D_HEADS * HEAD_DIM + 3 * C_WIDTH
OD_COLS = OD_Q_COLS + 2 * D_KV_HEADS * HEAD_DIM
OD_OUT = D_HEADS * HEAD_DIM + C_WIDTH

kernel_name = "hybrid_dit_prefix_trunk"


def _split(z, sizes):
    out, start = [], 0
    for s in sizes:
        out.append(z[..., start:start + s])
        start += s
    return out


def _rmsnorm(x, gain):
    xf = x.astype(jnp.float32)
    inv = lax.rsqrt(jnp.mean(xf * xf, axis=-1, keepdims=True) + NORM_EPS)
    return (xf * inv).astype(x.dtype) * gain


def _modulate(h, shift, scale):
    return h * (1 + scale) + shift


def _axial_rope(rows, rope_dim):
    row_idx = jnp.repeat(jnp.arange(rows), GRID_W).astype(jnp.float32)
    col_idx = jnp.tile(jnp.arange(GRID_W), rows).astype(jnp.float32)
    d_axis = rope_dim // 2
    inv_freq = ROPE_THETA ** (-jnp.arange(0, d_axis, 2, dtype=jnp.float32) / d_axis)
    ang = jnp.concatenate([row_idx[:, None] * inv_freq, col_idx[:, None] * inv_freq], axis=-1)
    return jnp.cos(ang), jnp.sin(ang)


def _apply_rope(x, cos, sin):
    shape = (1, cos.shape[0]) + (1,) * (x.ndim - 3) + (cos.shape[1],)
    cos = cos.reshape(shape).astype(x.dtype)
    sin = sin.reshape(shape).astype(x.dtype)
    xr = x.reshape(x.shape[:-1] + (-1, 2))
    x1, x2 = xr[..., 0], xr[..., 1]
    return jnp.stack([x1 * cos - x2 * sin, x1 * sin + x2 * cos], axis=-1).reshape(x.shape)


def _attend(q, k, v, scale, mask=None, sink=None):
    s = jnp.einsum("bqhgd,bkhd->bhgqk", q, k).astype(jnp.float32) * scale
    if mask is not None:
        s = jnp.where(mask, s, NEG_INF)
    if sink is not None:
        sink_col = jnp.broadcast_to(sink.astype(jnp.float32)[None, :, :, None, None], s.shape[:-1] + (1,))
        p = jax.nn.softmax(jnp.concatenate([s, sink_col], axis=-1), axis=-1)[..., :-1]
    else:
        p = jax.nn.softmax(s, axis=-1)
    return jnp.einsum("bhgqk,bkhd->bqhgd", p.astype(v.dtype), v)


def _blocked_attention(q, k, v, scale):
    bsz, n, hk, g, dq = q.shape
    nb = n // Q_BLOCK
    qb = jnp.moveaxis(q.reshape(bsz, nb, Q_BLOCK, hk, g, dq), 1, 0)
    ob = lax.map(lambda qi: _attend(qi, k, v, scale), qb)
    return jnp.moveaxis(ob, 0, 1).reshape(bsz, n, hk, g, v.shape[-1])


def _window_attention(q, k, v, k_ctx, v_ctx, sink, scale):
    bsz, n, hk, g, dh = q.shape
    nb = n // Q_BLOCK
    span = Q_BLOCK + 2 * WINDOW
    pad = ((0, 0), (WINDOW, WINDOW), (0, 0), (0, 0))
    kp = jnp.pad(k, pad)
    vp = jnp.pad(v, pad)
    qb = jnp.moveaxis(q.reshape(bsz, nb, Q_BLOCK, hk, g, dh), 1, 0)
    r = jnp.arange(Q_BLOCK)[:, None]
    j = jnp.arange(span)[None, :]
    band = (j >= r) & (j <= r + 2 * WINDOW)
    ctx_mask = jnp.ones((Q_BLOCK, k_ctx.shape[1]), dtype=bool)

    def block(args):
        i, qi = args
        start = i * Q_BLOCK
        ki = lax.dynamic_slice_in_dim(kp, start, span, axis=1)
        vi = lax.dynamic_slice_in_dim(vp, start, span, axis=1)
        pos = start - WINDOW + j
        valid = band & (pos >= 0) & (pos < n)
        mask = jnp.concatenate([ctx_mask, valid], axis=1)
        return _attend(qi, jnp.concatenate([k_ctx, ki], axis=1), jnp.concatenate([v_ctx, vi], axis=1), scale, mask, sink)

    ob = lax.map(block, (jnp.arange(nb), qb))
    return jnp.moveaxis(ob, 0, 1).reshape(bsz, n, hk, g, dh)


def _dwconv3(x, w, b):
    xp = jnp.pad(x, ((0, 0), (1, 1), (0, 0)))
    return xp[:, :-2] * w[0] + xp[:, 1:-1] * w[1] + xp[:, 2:] * w[2] + b


def _hyena_filters(n, w1, b1, w2, b2, w3, b3, w4, freq):
    t = jnp.linspace(0.0, 1.0, n, dtype=jnp.float32)[:, None]
    w = 2 * math.pi * jnp.arange(n, dtype=jnp.float32)[:, None] / n
    f = jnp.linspace(1e-4, C_BANDS - 1, C_BANDS, dtype=jnp.float32)[None, :]
    z = jnp.concatenate([t, jnp.cos(f * w), -jnp.sin(f * w)], axis=-1).astype(w1.dtype)
    h = jnp.sin(freq[0] * (z @ w1 + b1))
    h = jnp.sin(freq[1] * (h @ w2 + b2))
    h = jnp.sin(freq[2] * (h @ w3 + b3))
    h = (h @ w4).reshape(n, 2, C_WIDTH)
    deltas = jnp.abs(jnp.linspace(C_MIN_DECAY, C_MAX_DECAY, C_WIDTH, dtype=jnp.float32))
    h = h * jnp.exp(-t[:, :, None] * deltas)
    return h[:, 0], h[:, 1]


def _bidir_long_conv(u, h_fwd, h_bwd, bias):
    n = u.shape[1]
    n_fft = 2 * n
    k_full = jnp.concatenate([h_fwd, jnp.zeros_like(h_fwd[:1]), h_bwd[:0:-1]], axis=0).astype(jnp.float32)
    k_f = jnp.fft.rfft(k_full, n=n_fft, axis=0)
    uf = u.astype(jnp.float32)
    u_f = jnp.fft.rfft(uf, n=n_fft, axis=1)
    y = jnp.fft.irfft(u_f * k_f[None], n=n_fft, axis=1)[:, :n]
    return (y + uf * bias.astype(jnp.float32)).astype(u.dtype)


def _hyena(z, conv_w, conv_b, filt, bias):
    z = _dwconv3(z, conv_w, conv_b)
    x0, x1, v = _split(z, (C_WIDTH, C_WIDTH, C_WIDTH))
    h_fwd, h_bwd = _hyena_filters(z.shape[1], *filt)
    return x0 * _bidir_long_conv(v * x1, h_fwd, h_bwd, bias)


def _a_queries(za, gain, rope):
    bsz, n = za.shape[:2]
    q = _rmsnorm(za.reshape(bsz, n, A_KV_HEADS, A_GROUP, HEAD_DIM), gain)
    return q if rope is None else _apply_rope(q, *rope)


def _a_keys_values(zk, zv, gain, rope):
    bsz, n = zk.shape[:2]
    k = _rmsnorm(zk.reshape(bsz, n, A_KV_HEADS, HEAD_DIM), gain)
    if rope is not None:
        k = _apply_rope(k, *rope)
    return k, zv.reshape(bsz, n, A_KV_HEADS, HEAD_DIM)


def _b_queries(zcq, gain, w_uq, rope):
    bsz, n = zcq.shape[:2]
    q = (_rmsnorm(zcq, gain) @ w_uq).reshape(bsz, n, B_HEADS, 1, B_NOPE + B_ROPE)
    if rope is not None:
        q = jnp.concatenate([q[..., :B_NOPE], _apply_rope(q[..., B_NOPE:], *rope)], axis=-1)
    return q


def _b_keys_values(zckv, zkr, gain, w_ukv, rope):
    bsz, n = zckv.shape[:2]
    kv = (_rmsnorm(zckv, gain) @ w_ukv).reshape(bsz, n, B_HEADS, B_NOPE + B_VDIM)
    k_nope, v = _split(kv, (B_NOPE, B_VDIM))
    k_rope = zkr[:, :, None, :]
    if rope is not None:
        k_rope = _apply_rope(k_rope, *rope)
    k = jnp.concatenate([k_nope, jnp.broadcast_to(k_rope, (bsz, n, B_HEADS, B_ROPE))], axis=-1)
    return k, v


def _d_queries(zq, rope):
    bsz, n = zq.shape[:2]
    q = zq.reshape(bsz, n, D_KV_HEADS, D_GROUP, HEAD_DIM)
    return q if rope is None else _apply_rope(q, *rope)


def _d_keys_values(zk, zv, rope):
    bsz, n = zk.shape[:2]
    k = zk.reshape(bsz, n, D_KV_HEADS, HEAD_DIM)
    if rope is not None:
        k = _apply_rope(k, *rope)
    return k, zv.reshape(bsz, n, D_KV_HEADS, HEAD_DIM)


def _even_mixer(h_lat, h_ctx, need_ctx, rope_hd, rope_mla, w_in, w_out, a_qn, a_kn, b_qn, b_w_uq, b_kvn, b_w_ukv):
    bsz, n = h_lat.shape[:2]
    za_q, zb_q, za_k, za_v, zb_kv, zb_kr = _split(h_lat @ w_in, EV_SIZES)
    ca_k, ca_v, cb_kv, cb_kr = _split(h_ctx @ w_in[:, EV_Q_COLS:], EV_SIZES[2:])
    ka_c, va_c = _a_keys_values(ca_k, ca_v, a_kn, None)
    kb_c, vb_c = _b_keys_values(cb_kv, cb_kr, b_kvn, b_w_ukv, None)
    ka, va = _a_keys_values(za_k, za_v, a_kn, rope_hd)
    kb, vb = _b_keys_values(zb_kv, zb_kr, b_kvn, b_w_ukv, rope_mla)
    oa = _blocked_attention(_a_queries(za_q, a_qn, rope_hd),
                            jnp.concatenate([ka_c, ka], axis=1), jnp.concatenate([va_c, va], axis=1), A_SCALE)
    ob = _blocked_attention(_b_queries(zb_q, b_qn, b_w_uq, rope_mla),
                            jnp.concatenate([kb_c, kb], axis=1), jnp.concatenate([vb_c, vb], axis=1), B_SCALE)
    y_lat = jnp.concatenate([oa.reshape(bsz, n, -1), ob.reshape(bsz, n, -1)], axis=-1) @ w_out
    if not need_ctx:
        return y_lat, None
    m = h_ctx.shape[1]
    ca_q, cb_q = _split(h_ctx @ w_in[:, :EV_Q_COLS], EV_SIZES[:2])
    oa_c = _attend(_a_queries(ca_q, a_qn, None), ka_c, va_c, A_SCALE)
    ob_c = _attend(_b_queries(cb_q, b_qn, b_w_uq, None), kb_c, vb_c, B_SCALE)
    y_ctx = jnp.concatenate([oa_c.reshape(bsz, m, -1), ob_c.reshape(bsz, m, -1)], axis=-1) @ w_out
    return y_lat, y_ctx


def _odd_mixer(h_lat, h_ctx, need_ctx, rope_hd, w_in, w_out, sink, conv_w, conv_b, filt, c_bias):
    bsz, n = h_lat.shape[:2]
    sink = sink.reshape(D_KV_HEADS, D_GROUP)
    zd_q, zc, zd_k, zd_v = _split(h_lat @ w_in, OD_SIZES)
    cd_k, cd_v = _split(h_ctx @ w_in[:, OD_Q_COLS:], OD_SIZES[2:])
    kd_c, vd_c = _d_keys_values(cd_k, cd_v, None)
    kd, vd = _d_keys_values(zd_k, zd_v, rope_hd)
    od = _window_attention(_d_queries(zd_q, rope_hd), kd, vd, kd_c, vd_c, sink, D_SCALE)
    oc = _hyena(zc, conv_w, conv_b, filt, c_bias)
    y_lat = jnp.concatenate([od.reshape(bsz, n, -1), oc], axis=-1) @ w_out
    if not need_ctx:
        return y_lat, None
    m = h_ctx.shape[1]
    cd_q, cc = _split(h_ctx @ w_in[:, :OD_Q_COLS], OD_SIZES[:2])
    od_c = _attend(_d_queries(cd_q, None), kd_c, vd_c, D_SCALE, sink=sink)
    oc_c = _hyena(cc, conv_w, conv_b, filt, c_bias)
    y_ctx = jnp.concatenate([od_c.reshape(bsz, m, -1), oc_c], axis=-1) @ w_out
    return y_lat, y_ctx


def _conv_ffn(h, w_up, conv_w, conv_b, w_down):
    g, v = _split(_dwconv3(h @ w_up, conv_w, conv_b), (FFN_DIM, FFN_DIM))
    return (jax.nn.silu(g) * v) @ w_down


def setup_inputs(seed: int = 0) -> dict:
    key = jax.random.key(seed)
    keys = iter(jax.random.split(key, 40))

    def nrm(shape, scale):
        return jax.random.normal(next(keys), shape, jnp.float32) * scale

    def gain(shape):
        return 1.0 + nrm(shape, 0.05)

    n_even = (DEPTH + 1) // 2
    n_odd = DEPTH // 2
    return {
        "x": nrm((BATCH, SEQ, D_MODEL), 1.0),
        "c": nrm((BATCH, D_MODEL), 1.0),
        "ctx": nrm((BATCH, CTX_LEN, D_MODEL), 1.0),
        "c_ctx": nrm((D_MODEL,), 1.0),
        "w_mod": nrm((DEPTH, D_MODEL, N_MOD * D_MODEL), 0.5 * D_MODEL ** -0.5),
        "b_mod": nrm((DEPTH, N_MOD * D_MODEL), 0.02),
        "norm_mix": gain((DEPTH, D_MODEL)),
        "norm_ffn": gain((DEPTH, D_MODEL)),
        "ev_w_in": nrm((n_even, D_MODEL, EV_COLS), D_MODEL ** -0.5),
        "ev_w_out": nrm((n_even, EV_OUT, D_MODEL), EV_OUT ** -0.5),
        "a_q_norm": gain((n_even, HEAD_DIM)),
        "a_k_norm": gain((n_even, HEAD_DIM)),
        "b_q_norm": gain((n_even, B_Q_RANK)),
        "b_w_uq": nrm((n_even, B_Q_RANK, B_HEADS * (B_NOPE + B_ROPE)), B_Q_RANK ** -0.5),
        "b_kv_norm": gain((n_even, B_KV_RANK)),
        "b_w_ukv": nrm((n_even, B_KV_RANK, B_HEADS * (B_NOPE + B_VDIM)), B_KV_RANK ** -0.5),
        "od_w_in": nrm((n_odd, D_MODEL, OD_COLS), D_MODEL ** -0.5),
        "od_w_out": nrm((n_odd, OD_OUT, D_MODEL), OD_OUT ** -0.5),
        "d_sink": nrm((n_odd, D_HEADS), 0.5),
        "c_conv_w": nrm((n_odd, 3, 3 * C_WIDTH), 3 ** -0.5),
        "c_conv_b": nrm((n_odd, 3 * C_WIDTH), 0.02),
        "c_filt_w1": nrm((n_odd, C_EMB_DIM, C_FILTER_WIDTH), 1.0),
        "c_filt_b1": nrm((n_odd, C_FILTER_WIDTH), 0.1),
        "c_filt_w2": nrm((n_odd, C_FILTER_WIDTH, C_FILTER_WIDTH), C_FILTER_WIDTH ** -0.5),
        "c_filt_b2": nrm((n_odd, C_FILTER_WIDTH), 0.1),
        "c_filt_w3": nrm((n_odd, C_FILTER_WIDTH, C_FILTER_WIDTH), C_FILTER_WIDTH ** -0.5),
        "c_filt_b3": nrm((n_odd, C_FILTER_WIDTH), 0.1),
        "c_filt_w4": nrm((n_odd, C_FILTER_WIDTH, 2 * C_WIDTH), 0.05 * C_FILTER_WIDTH ** -0.5),
        "c_filt_freq": gain((n_odd, 3, C_FILTER_WIDTH)),
        "c_bias": nrm((n_odd, C_WIDTH), 0.5),
        "ffn_w_up": nrm((DEPTH, D_MODEL, 2 * FFN_DIM), D_MODEL ** -0.5),
        "ffn_conv_w": nrm((DEPTH, 3, 2 * FFN_DIM), 3 ** -0.5),
        "ffn_conv_b": nrm((DEPTH, 2 * FFN_DIM), 0.02),
        "ffn_w_down": nrm((DEPTH, FFN_DIM, D_MODEL), FFN_DIM ** -0.5),
        "final_norm": gain((D_MODEL,)),
    }


def reference(x, c, ctx, c_ctx, w_mod, b_mod, norm_mix, norm_ffn,
              ev_w_in, ev_w_out, a_q_norm, a_k_norm, b_q_norm, b_w_uq, b_kv_norm, b_w_ukv,
              od_w_in, od_w_out, d_sink, c_conv_w, c_conv_b,
              c_filt_w1, c_filt_b1, c_filt_w2, c_filt_b2, c_filt_w3, c_filt_b3, c_filt_w4, c_filt_freq, c_bias,
              ffn_w_up, ffn_conv_w, ffn_conv_b, ffn_w_down, final_norm):
    rows = x.shape[1] // GRID_W
    rope_hd = _axial_rope(rows, HEAD_DIM)
    rope_mla = _axial_rope(rows, B_ROPE)
    silu_c = jax.nn.silu(c)
    silu_cc = jax.nn.silu(c_ctx)
    for layer in range(DEPTH):
        need_ctx = layer < DEPTH - 1
        mods_l = _split(silu_c @ w_mod[layer] + b_mod[layer], (D_MODEL,) * N_MOD)
        sh_l, sc_l, gt_l, sh2_l, sc2_l, gt2_l = [m_[:, None, :] for m_ in mods_l]
        sh_c, sc_c, gt_c, sh2_c, sc2_c, gt2_c = _split(silu_cc @ w_mod[layer] + b_mod[layer], (D_MODEL,) * N_MOD)
        h_lat = _modulate(_rmsnorm(x, norm_mix[layer]), sh_l, sc_l)
        h_ctx = _modulate(_rmsnorm(ctx, norm_mix[layer]), sh_c, sc_c)
        if layer % 2 == 0:
            e = layer // 2
            y_lat, y_ctx = _even_mixer(h_lat, h_ctx, need_ctx, rope_hd, rope_mla, ev_w_in[e], ev_w_out[e],
                                       a_q_norm[e], a_k_norm[e], b_q_norm[e], b_w_uq[e], b_kv_norm[e], b_w_ukv[e])
        else:
            o = layer // 2
            filt = (c_filt_w1[o], c_filt_b1[o], c_filt_w2[o], c_filt_b2[o], c_filt_w3[o], c_filt_b3[o],
                    c_filt_w4[o], c_filt_freq[o])
            y_lat, y_ctx = _odd_mixer(h_lat, h_ctx, need_ctx, rope_hd, od_w_in[o], od_w_out[o], d_sink[o],
                                      c_conv_w[o], c_conv_b[o], filt, c_bias[o])
        x = x + gt_l * y_lat
        x = x + gt2_l * _conv_ffn(_modulate(_rmsnorm(x, norm_ffn[layer]), sh2_l, sc2_l),
                                  ffn_w_up[layer], ffn_conv_w[layer], ffn_conv_b[layer], ffn_w_down[layer])
        if need_ctx:
            ctx = ctx + gt_c * y_ctx
            ctx = ctx + gt2_c * _conv_ffn(_modulate(_rmsnorm(ctx, norm_ffn[layer]), sh2_c, sc2_c),
                                          ffn_w_up[layer], ffn_conv_w[layer], ffn_conv_b[layer], ffn_w_down[layer])
    return _rmsnorm(x, final_norm)
```

```python
import functools
import math

import jax
import jax.numpy as jnp
from jax import lax
from jax.experimental import pallas as pl
from jax.experimental.pallas import tpu as pltpu

F32 = jnp.float32
BF16 = jnp.bfloat16

GRID_W = 64
HEAD_DIM = 64
ROPE_THETA = 10000.0
NORM_EPS = 1e-6
NEG_INF = -1e30
N_MOD = 6
A_KV_HEADS = 2
B_NOPE = 64
B_ROPE = 32
B_VDIM = 64
B_Q_RANK = 384
B_KV_RANK = 256
C_EMB_DIM = 5
C_BANDS = (C_EMB_DIM - 1) // 2
C_MIN_DECAY = math.log(1e-2) / 1.5
C_MAX_DECAY = math.log(1e-2) / 0.3
WINDOW = 128
Q_BLOCK = 128

LANES = 128
MXU_DIM = 256
VMEM_LIMIT = 56 * 1024 * 1024
HEADS_PER_GROUP = MXU_DIM // HEAD_DIM


def _params(sem, vmem=VMEM_LIMIT):
    return pltpu.CompilerParams(dimension_semantics=sem, vmem_limit_bytes=vmem)


def _resident(shape, index_map):
    return pl.BlockSpec(shape, index_map, pipeline_mode=pl.Buffered(1))


def _norm_mod(x, gain, shift, scale):
    inv = lax.rsqrt(jnp.mean(x * x, axis=-1, keepdims=True) + NORM_EPS)
    return (x * inv) * gain * (1.0 + scale) + shift


def _rope_lanes(x, cos, sin_signed):
    lane = lax.broadcasted_iota(jnp.int32, x.shape, 1)
    nxt = pltpu.roll(x, LANES - 1, axis=1)
    prv = pltpu.roll(x, 1, axis=1)
    swapped = jnp.where(lane % 2 == 0, nxt, prv)
    return x * cos + swapped * sin_signed


def _head_rmsnorm_lanes(x, gain):
    lane = lax.broadcasted_iota(jnp.int32, x.shape, 1)
    lo = lane < HEAD_DIM
    sq = x * x
    s_lo = jnp.sum(jnp.where(lo, sq, 0.0), axis=-1, keepdims=True)
    s_hi = jnp.sum(jnp.where(lo, 0.0, sq), axis=-1, keepdims=True)
    ms = jnp.where(lo, s_lo, s_hi) * (1.0 / HEAD_DIM)
    return x * lax.rsqrt(ms + NORM_EPS) * gain


def _dup_heads(x):
    lane = lax.broadcasted_iota(jnp.int32, x.shape, 1)
    lo = lane < HEAD_DIM
    r = pltpu.roll(x, HEAD_DIM, axis=1)
    return jnp.where(lo, x, r), jnp.where(lo, r, x)


def _lanes4(vals, rows):
    lane = lax.broadcasted_iota(jnp.int32, (rows, MXU_DIM), 1)
    return jnp.where(lane < HEAD_DIM, vals[0],
                     jnp.where(lane < 2 * HEAD_DIM, vals[1],
                               jnp.where(lane < 3 * HEAD_DIM, vals[2], vals[3])))


def _split_bf16(x):
    hi = x.astype(BF16)
    lo = (x - hi.astype(F32)).astype(BF16)
    return hi, lo


def _dot3(a_hi, a_lo, b_hi, b_lo):
    d = functools.partial(jnp.dot, preferred_element_type=F32)
    return d(a_hi, b_hi) + d(a_hi, b_lo) + d(a_lo, b_hi)


def _mods_kernel(c_ref, w_ref, b_ref, o_ref):
    c = c_ref[...]
    s = c * (1.0 / (1.0 + jnp.exp(-c)))
    s_hi, s_lo = _split_bf16(s)
    w_hi, w_lo = _split_bf16(w_ref[0])
    o_ref[0] = _dot3(s_hi, s_lo, w_hi, w_lo) + b_ref[0]


def _mods(cvec, w_mod, b_mod):
    depth, d, n = w_mod.shape
    rows = cvec.shape[0]
    tn = 1536
    return pl.pallas_call(
        _mods_kernel,
        grid=(depth, n // tn),
        in_specs=[pl.BlockSpec((rows, d), lambda l, j: (0, 0)),
                  pl.BlockSpec((1, d, tn), lambda l, j: (l, 0, j)),
                  pl.BlockSpec((1, 1, tn), lambda l, j: (l, 0, j))],
        out_specs=pl.BlockSpec((1, rows, tn), lambda l, j: (l, 0, j)),
        out_shape=jax.ShapeDtypeStruct((depth, rows, n), F32),
        compiler_params=_params(("arbitrary", "arbitrary")),
        name="mods",
    )(cvec, w_mod, b_mod.reshape(depth, 1, n))


def _mod_spec(d, chunk, row):
    if row is None:
        return pl.BlockSpec((1, 1, d), lambda b, *_: (b, 0, chunk))
    return pl.BlockSpec((1, 1, d), lambda b, *_: (row, 0, chunk))


def _inproj_kernel(x_ref, g_ref, sh_ref, sc_ref, w_ref, o_ref):
    h = _norm_mod(x_ref[0], g_ref[...], sh_ref[0], sc_ref[0])
    o_ref[0] = jnp.dot(h.astype(BF16), w_ref[...], preferred_element_type=F32)


def _inproj(x, gain, mods3, mrow, w, tm):
    bsz, n, d = x.shape
    cols = w.shape[1]
    tm = min(tm, n)
    return pl.pallas_call(
        _inproj_kernel,
        grid=(bsz, n // tm),
        in_specs=[pl.BlockSpec((1, tm, d), lambda b, i: (b, i, 0)),
                  pl.BlockSpec((1, d), lambda b, i: (0, 0)),
                  _mod_spec(d, 0, mrow), _mod_spec(d, 1, mrow),
                  _resident((d, cols), lambda b, i: (0, 0))],
        out_specs=pl.BlockSpec((1, tm, cols), lambda b, i: (b, i, 0)),
        out_shape=jax.ShapeDtypeStruct((bsz, n, cols), F32),
        compiler_params=_params(("parallel", "parallel")),
        name="inproj",
    )(x, gain.reshape(1, d), mods3, mods3, w)


EV_AQ, EV_BQ, EV_AK, EV_AV, EV_BKV, EV_BKR, EV_END = 0, 512, 896, 1024, 1152, 1408, 1536


def _prep_even_kernel(z_ref, aqg_ref, akg_ref, bqg_ref, bkvg_ref, wuq_ref, wukn_ref, wuv_ref,
                      chd_ref, shd_ref, cb_ref, sb_ref,
                      qa_ref, ka_ref, va_ref, qb_ref, kb_ref, vb_ref, *, use_rope, a_scale, b_scale):
    def rope_hd(y):
        return _rope_lanes(y, chd_ref[...], shd_ref[...]) if use_rope else y

    def rope_b(y):
        return _rope_lanes(y, cb_ref[...], sb_ref[...]) if use_rope else y

    for c in range((EV_BQ - EV_AQ) // LANES):
        sl = slice(EV_AQ + c * LANES, EV_AQ + (c + 1) * LANES)
        y = rope_hd(_head_rmsnorm_lanes(z_ref[0, :, sl], aqg_ref[:, c * LANES:(c + 1) * LANES]))
        qa_ref[0, :, c * LANES:(c + 1) * LANES] = (y * a_scale).astype(BF16)
    k = rope_hd(_head_rmsnorm_lanes(z_ref[0, :, EV_AK:EV_AV], akg_ref[...]))
    for src, dst in ((k, ka_ref), (z_ref[0, :, EV_AV:EV_BKV], va_ref)):
        d0, d1 = _dup_heads(src)
        d0 = d0.astype(BF16)
        d1 = d1.astype(BF16)
        dst[0, :, 0 * LANES:1 * LANES] = d0
        dst[0, :, 1 * LANES:2 * LANES] = d0
        dst[0, :, 2 * LANES:3 * LANES] = d1
        dst[0, :, 3 * LANES:4 * LANES] = d1
    cq = z_ref[0, :, EV_BQ:EV_AK]
    cq = cq * lax.rsqrt(jnp.mean(cq * cq, axis=-1, keepdims=True) + NORM_EPS) * bqg_ref[...]
    qb = jnp.dot(cq.astype(BF16), wuq_ref[...], preferred_element_type=F32)
    for c in range(qb.shape[1] // MXU_DIM):
        lo = slice(c * MXU_DIM, c * MXU_DIM + LANES)
        hi = slice(c * MXU_DIM + LANES, (c + 1) * MXU_DIM)
        qb_ref[0, :, lo] = (qb[:, lo] * b_scale).astype(BF16)
        qb_ref[0, :, hi] = (rope_b(qb[:, hi]) * b_scale).astype(BF16)
    ckv = z_ref[0, :, EV_BKV:EV_BKR]
    ckv = (ckv * lax.rsqrt(jnp.mean(ckv * ckv, axis=-1, keepdims=True) + NORM_EPS) * bkvg_ref[...]).astype(BF16)
    kn = jnp.dot(ckv, wukn_ref[...], preferred_element_type=F32)
    vb_ref[0] = jnp.dot(ckv, wuv_ref[...], preferred_element_type=F32).astype(BF16)
    kr = rope_b(z_ref[0, :, EV_BKR:EV_END]).astype(BF16)
    for c in range(kn.shape[1] // LANES):
        kb_ref[0, :, c * MXU_DIM:c * MXU_DIM + LANES] = kn[:, c * LANES:(c + 1) * LANES].astype(BF16)
        kb_ref[0, :, c * MXU_DIM + LANES:(c + 1) * MXU_DIM] = kr


def _prep_even(z, wts, tabs, use_rope, tm):
    bsz, n, cols = z.shape
    tm = min(tm, n)
    aqg, akg, bqg, bkvg, wuq, wukn, wuv = wts
    chd, shd, cb, sb = tabs
    full = lambda a: _resident(a.shape, lambda b, i: (0,) * a.ndim)
    tab = lambda a: pl.BlockSpec((tm, LANES), (lambda b, i: (i, 0)) if use_rope else (lambda b, i: (0, 0)))
    outs = [(4 * LANES, BF16), (4 * LANES, BF16), (4 * LANES, BF16), (wuq.shape[1], BF16),
            (2 * wukn.shape[1], BF16), (wuv.shape[1], BF16)]
    kern = functools.partial(_prep_even_kernel, use_rope=use_rope, a_scale=HEAD_DIM ** -0.5,
                             b_scale=(B_NOPE + B_ROPE) ** -0.5)
    return pl.pallas_call(
        kern,
        grid=(bsz, n // tm),
        in_specs=[pl.BlockSpec((1, tm, cols), lambda b, i: (b, i, 0)),
                  full(aqg), full(akg), full(bqg), full(bkvg), full(wuq), full(wukn), full(wuv),
                  tab(chd), tab(shd), tab(cb), tab(sb)],
        out_specs=[pl.BlockSpec((1, tm, w), lambda b, i: (b, i, 0)) for w, _ in outs],
        out_shape=[jax.ShapeDtypeStruct((bsz, n, w), dt) for w, dt in outs],
        compiler_params=_params(("parallel", "parallel")),
        name="prep_even",
    )(z, aqg, akg, bqg, bkvg, wuq, wukn, wuv, chd, shd, cb, sb)


def _head_lane_masks(rows, nsub):
    lane = lax.broadcasted_iota(jnp.int32, (rows, MXU_DIM), 1)
    kms, vms = [], []
    for j in range(HEADS_PER_GROUP):
        v_lo = j * HEAD_DIM
        vm = (lane >= v_lo) & (lane < v_lo + HEAD_DIM)
        if nsub == 1:
            km = vm
        else:
            jj = j % 2
            km = ((lane >= jj * B_NOPE) & (lane < (jj + 1) * B_NOPE)) | \
                 ((lane >= 2 * B_NOPE + jj * B_ROPE) & (lane < 2 * B_NOPE + (jj + 1) * B_ROPE))
        kms.append(jnp.where(km, 1.0, 0.0).astype(BF16))
        vms.append(jnp.where(vm, 1.0, 0.0).astype(BF16))
    return kms, vms


def _flash_kernel(*refs, nsub, tk, nblk, lc):
    if nblk:
        q_ref, kc_ref, vc_ref, kl_ref, vl_ref, o_ref, kbc, vbc, kbl, vbl, acc_ref = refs
    else:
        q_ref, kc_ref, vc_ref, o_ref, kbc, vbc, acc_ref = refs
    hpu = HEADS_PER_GROUP // nsub
    tq = q_ref.shape[1]

    @pl.when(pl.program_id(2) == 0)
    def _build():
        kms, vms = _head_lane_masks(lc, nsub)
        for j in range(HEADS_PER_GROUP):
            u = j // hpu
            kbc[j * lc:(j + 1) * lc, :] = kc_ref[0, :, u * MXU_DIM:(u + 1) * MXU_DIM] * kms[j]
            vbc[j * lc:(j + 1) * lc, :] = vc_ref[0] * vms[j]
        if nblk:
            kms, vms = _head_lane_masks(tk, nsub)

            def body(i, carry):
                r0 = pl.multiple_of(i * tk, tk)
                for j in range(HEADS_PER_GROUP):
                    u = j // hpu
                    kbl[i, j * tk:(j + 1) * tk, :] = kl_ref[0, pl.ds(r0, tk), u * MXU_DIM:(u + 1) * MXU_DIM] * kms[j]
                    vbl[i, j * tk:(j + 1) * tk, :] = vl_ref[0, pl.ds(r0, tk), :] * vms[j]
                return carry

            lax.fori_loop(0, nblk, body, 0)

    q = q_ref[0]
    nt = (((1,), (1,)), ((), ()))

    def block(kb, vb, tkk, m, l, first):
        parts = [lax.dot_general(q[:, u * MXU_DIM:(u + 1) * MXU_DIM], kb[u * hpu * tkk:(u + 1) * hpu * tkk, :], nt,
                                 preferred_element_type=F32) for u in range(nsub)]
        ps, alphas, m_new, l_new = [], [], [], []
        for j in range(HEADS_PER_GROUP):
            sj = parts[j // hpu][:, (j % hpu) * tkk:(j % hpu + 1) * tkk]
            mj = jnp.maximum(m[j], jnp.max(sj, axis=-1, keepdims=True))
            a = jnp.exp(m[j] - mj)
            p = jnp.exp(sj - mj)
            l_new.append(a * l[j] + jnp.sum(p, axis=-1, keepdims=True))
            m_new.append(mj)
            alphas.append(a)
            ps.append(p.astype(BF16))
        pv = jnp.dot(jnp.concatenate(ps, axis=1), vb, preferred_element_type=F32)
        if first:
            acc_ref[...] = pv
        else:
            acc_ref[...] = acc_ref[...] * _lanes4(alphas, tq) + pv
        return m_new, l_new

    m0 = [jnp.full((tq, 1), NEG_INF, F32)] * HEADS_PER_GROUP
    l0 = [jnp.zeros((tq, 1), F32)] * HEADS_PER_GROUP
    m, l = block(kbc[...], vbc[...], lc, m0, l0, True)
    if nblk:
        def body(i, carry):
            mm, ll = block(kbl[i], vbl[i], tk, list(carry[:4]), list(carry[4:]), False)
            return tuple(mm) + tuple(ll)

        carry = lax.fori_loop(0, nblk, body, tuple(m) + tuple(l))
        l = list(carry[4:])
    o_ref[0] = (acc_ref[...] * _lanes4([1.0 / v for v in l], tq)).astype(BF16)


def _flash(q, kc, vc, kl, vl, nsub, tq, tk):
    bsz, nq, qw = q.shape
    groups = qw // (nsub * MXU_DIM)
    lc = kc.shape[1]
    tq = min(tq, nq)
    nblk = 0 if kl is None else kl.shape[1] // tk
    kw = nsub * MXU_DIM
    in_specs = [pl.BlockSpec((1, tq, kw), lambda b, g, i: (b, i, g)),
                pl.BlockSpec((1, lc, kw), lambda b, g, i: (b, 0, g)),
                pl.BlockSpec((1, lc, MXU_DIM), lambda b, g, i: (b, 0, g))]
    args = [q, kc, vc]
    scratch = [pltpu.VMEM((HEADS_PER_GROUP * lc, MXU_DIM), BF16), pltpu.VMEM((HEADS_PER_GROUP * lc, MXU_DIM), BF16)]
    if nblk:
        ll = kl.shape[1]
        in_specs += [pl.BlockSpec((1, ll, kw), lambda b, g, i: (b, 0, g)),
                     pl.BlockSpec((1, ll, MXU_DIM), lambda b, g, i: (b, 0, g))]
        args += [kl, vl]
        scratch += [pltpu.VMEM((nblk, HEADS_PER_GROUP * tk, MXU_DIM), BF16),
                    pltpu.VMEM((nblk, HEADS_PER_GROUP * tk, MXU_DIM), BF16)]
    scratch.append(pltpu.VMEM((tq, MXU_DIM), F32))
    return pl.pallas_call(
        functools.partial(_flash_kernel, nsub=nsub, tk=tk, nblk=nblk, lc=lc),
        grid=(bsz, groups, nq // tq),
        in_specs=in_specs,
        out_specs=pl.BlockSpec((1, tq, MXU_DIM), lambda b, g, i: (b, i, g)),
        out_shape=jax.ShapeDtypeStruct((bsz, nq, groups * MXU_DIM), BF16),
        scratch_shapes=scratch,
        compiler_params=_params(("parallel", "parallel", "arbitrary")),
        name="flash_attn",
    )(*args)


def _outproj_kernel(oa_ref, ob_ref, x_ref, gt_ref, wa_ref, wb_ref, o_ref):
    y = jnp.dot(oa_ref[0], wa_ref[...], preferred_element_type=F32)
    y = y + jnp.dot(ob_ref[0], wb_ref[...], preferred_element_type=F32)
    o_ref[0] = x_ref[0] + gt_ref[0] * y


def _outproj(oa, ob, x, mods3, mrow, w_out, tm):
    bsz, n, d = x.shape
    half = oa.shape[2]
    tm = min(tm, n)
    return pl.pallas_call(
        _outproj_kernel,
        grid=(bsz, n // tm),
        in_specs=[pl.BlockSpec((1, tm, half), lambda b, i: (b, i, 0)),
                  pl.BlockSpec((1, tm, half), lambda b, i: (b, i, 0)),
                  pl.BlockSpec((1, tm, d), lambda b, i: (b, i, 0)),
                  _mod_spec(d, 2, mrow),
                  _resident((half, d), lambda b, i: (0, 0)),
                  _resident((half, d), lambda b, i: (1, 0))],
        out_specs=pl.BlockSpec((1, tm, d), lambda b, i: (b, i, 0)),
        out_shape=jax.ShapeDtypeStruct((bsz, n, d), F32),
        compiler_params=_params(("parallel", "parallel")),
        name="outproj",
    )(oa, ob, x, mods3, w_out, w_out)


HALO = 16


def _ffn_kernel(xp_ref, x_ref, xn_ref, g_ref, sh_ref, sc_ref, gt_ref, wup_ref, cw_ref, wdn_ref, fg_ref,
                o_ref, h_ref, acc_ref, *, final_norm):
    i = pl.program_id(1)
    last = pl.num_programs(1) - 1
    tm = x_ref.shape[1]
    nf = wdn_ref.shape[0]
    gain, shift, scale = g_ref[...], sh_ref[0], sc_ref[0]
    keep_p = jnp.where(i > 0, 1.0, 0.0)
    keep_n = jnp.where(i < last, 1.0, 0.0)
    h_ref[0:HALO, :] = (_norm_mod(xp_ref[0], gain, shift, scale) * keep_p).astype(BF16)
    h_ref[HALO:HALO + tm, :] = _norm_mod(x_ref[0], gain, shift, scale).astype(BF16)
    h_ref[HALO + tm:, :] = (_norm_mod(xn_ref[0], gain, shift, scale) * keep_n).astype(BF16)
    rows = tm + 2 * HALO

    def conv(u, f, part):
        w = cw_ref[part, f]
        up = pltpu.roll(u, 1, axis=0)[HALO:HALO + tm]
        un = pltpu.roll(u, rows - 1, axis=0)[HALO:HALO + tm]
        return up * w[0:1] + u[HALO:HALO + tm] * w[1:2] + un * w[2:3] + w[3:4]

    def body(f, carry):
        h = h_ref[...]
        g = conv(jnp.dot(h, wup_ref[0, f], preferred_element_type=F32), f, 0)
        v = conv(jnp.dot(h, wup_ref[1, f], preferred_element_type=F32), f, 1)
        a = (g * (1.0 / (1.0 + jnp.exp(-g))) * v).astype(BF16)
        acc_ref[...] += jnp.dot(a, wdn_ref[f], preferred_element_type=F32)
        return carry

    acc_ref[...] = jnp.zeros(acc_ref.shape, F32)
    lax.fori_loop(0, nf, body, 0)
    out = x_ref[0] + gt_ref[0] * acc_ref[...]
    if final_norm:
        out = out * lax.rsqrt(jnp.mean(out * out, axis=-1, keepdims=True) + NORM_EPS) * fg_ref[...]
    o_ref[0] = out


def _ffn(x, gain, mods3, mrow, wup, cw, wdn, final_gain, tm):
    bsz, n, d = x.shape
    tm = min(tm, n)
    nf, tf = wdn.shape[0], wdn.shape[1]
    hb = tm // HALO
    nh = n // HALO
    final_norm = final_gain is not None
    fg = final_gain if final_norm else gain
    return pl.pallas_call(
        functools.partial(_ffn_kernel, final_norm=final_norm),
        grid=(bsz, n // tm),
        in_specs=[pl.BlockSpec((1, HALO, d), lambda b, i: (b, jnp.maximum(i * hb - 1, 0), 0)),
                  pl.BlockSpec((1, tm, d), lambda b, i: (b, i, 0)),
                  pl.BlockSpec((1, HALO, d), lambda b, i: (b, jnp.minimum((i + 1) * hb, nh - 1), 0)),
                  pl.BlockSpec((1, d), lambda b, i: (0, 0)),
                  _mod_spec(d, 3, mrow), _mod_spec(d, 4, mrow), _mod_spec(d, 5, mrow),
                  _resident(wup.shape, lambda b, i: (0, 0, 0, 0)),
                  _resident(cw.shape, lambda b, i: (0, 0, 0, 0)),
                  _resident(wdn.shape, lambda b, i: (0, 0, 0)),
                  pl.BlockSpec((1, d), lambda b, i: (0, 0))],
        out_specs=pl.BlockSpec((1, tm, d), lambda b, i: (b, i, 0)),
        out_shape=jax.ShapeDtypeStruct((bsz, n, d), F32),
        scratch_shapes=[pltpu.VMEM((tm + 2 * HALO, d), BF16), pltpu.VMEM((tm, d), F32)],
        compiler_params=_params(("parallel", "parallel")),
        name="conv_ffn",
    )(x, x, x, gain.reshape(1, d), mods3, mods3, mods3, wup, cw, wdn, fg.reshape(1, d))


OD_Q, OD_C, OD_K, OD_V, OD_END = 0, 512, 2048, 2176, 2304
SUB = 8


def _prep_odd_kernel(zp_ref, z_ref, zn_ref, cw_ref, chd_ref, shd_ref,
                     q_ref, k_ref, v_ref, x0_ref, u_ref, *, scale):
    i = pl.program_id(1)
    last = pl.num_programs(1) - 1
    tm = z_ref.shape[1]
    cwid = (OD_K - OD_C) // 3
    for c in range((OD_C - OD_Q) // LANES):
        sl = slice(OD_Q + c * LANES, OD_Q + (c + 1) * LANES)
        y = _rope_lanes(z_ref[0, :, sl], chd_ref[...], shd_ref[...])
        q_ref[0, :, sl] = (y * scale).astype(BF16)
    k = _rope_lanes(z_ref[0, :, OD_K:OD_V], chd_ref[...], shd_ref[...])
    for src, dst in ((k, k_ref), (z_ref[0, :, OD_V:OD_END], v_ref)):
        d0, d1 = _dup_heads(src)
        d0 = d0.astype(BF16)
        d1 = d1.astype(BF16)
        dst[0, :, 0 * LANES:1 * LANES] = d0
        dst[0, :, 1 * LANES:2 * LANES] = d0
        dst[0, :, 2 * LANES:3 * LANES] = d1
        dst[0, :, 3 * LANES:4 * LANES] = d1
    row = lax.broadcasted_iota(jnp.int32, (tm, cwid), 0)
    keep_p = jnp.where(i > 0, 1.0, 0.0)
    keep_n = jnp.where(i < last, 1.0, 0.0)

    def conv(part):
        sl = slice(OD_C + part * cwid, OD_C + (part + 1) * cwid)
        csl = slice(part * cwid, (part + 1) * cwid)
        u = z_ref[0, :, sl]
        up = jnp.where(row == 0, zp_ref[0, SUB - 1:SUB, sl] * keep_p, pltpu.roll(u, 1, axis=0))
        un = jnp.where(row == tm - 1, zn_ref[0, 0:1, sl] * keep_n, pltpu.roll(u, tm - 1, axis=0))
        return up * cw_ref[0:1, csl] + u * cw_ref[1:2, csl] + un * cw_ref[2:3, csl] + cw_ref[3:4, csl]

    x0_ref[0] = conv(0)
    u_ref[0] = conv(2) * conv(1)


def _prep_odd(z, cw, chd, shd, tm):
    bsz, n, cols = z.shape
    tm = min(tm, n)
    hb = tm // SUB
    nh = n // SUB
    cwid = (OD_K - OD_C) // 3
    outs = [(4 * LANES, BF16), (4 * LANES, BF16), (4 * LANES, BF16), (cwid, F32), (cwid, F32)]
    return pl.pallas_call(
        functools.partial(_prep_odd_kernel, scale=HEAD_DIM ** -0.5),
        grid=(bsz, n // tm),
        in_specs=[pl.BlockSpec((1, SUB, cols), lambda b, i: (b, jnp.maximum(i * hb - 1, 0), 0)),
                  pl.BlockSpec((1, tm, cols), lambda b, i: (b, i, 0)),
                  pl.BlockSpec((1, SUB, cols), lambda b, i: (b, jnp.minimum((i + 1) * hb, nh - 1), 0)),
                  pl.BlockSpec(cw.shape, lambda b, i: (0, 0)),
                  pl.BlockSpec((tm, LANES), lambda b, i: (i, 0)),
                  pl.BlockSpec((tm, LANES), lambda b, i: (i, 0))],
        out_specs=[pl.BlockSpec((1, tm, w), lambda b, i: (b, i, 0)) for w, _ in outs],
        out_shape=[jax.ShapeDtypeStruct((bsz, n, w), dt) for w, dt in outs],
        compiler_params=_params(("parallel", "parallel")),
        name="prep_odd",
    )(z, z, z, cw, chd, shd)


def _kv_ctx_odd_kernel(z_ref, k_ref, v_ref):
    for src, dst in ((z_ref[0, :, 0:LANES], k_ref), (z_ref[0, :, LANES:2 * LANES], v_ref)):
        d0, d1 = _dup_heads(src)
        d0 = d0.astype(BF16)
        d1 = d1.astype(BF16)
        dst[0, :, 0 * LANES:1 * LANES] = d0
        dst[0, :, 1 * LANES:2 * LANES] = d0
        dst[0, :, 2 * LANES:3 * LANES] = d1
        dst[0, :, 3 * LANES:4 * LANES] = d1


def _kv_ctx_odd(z):
    bsz, n, cols = z.shape
    return pl.pallas_call(
        _kv_ctx_odd_kernel,
        grid=(bsz,),
        in_specs=[pl.BlockSpec((1, n, cols), lambda b: (b, 0, 0))],
        out_specs=[pl.BlockSpec((1, n, 4 * LANES), lambda b: (b, 0, 0))] * 2,
        out_shape=[jax.ShapeDtypeStruct((bsz, n, 4 * LANES), BF16)] * 2,
        compiler_params=_params(("parallel",)),
        name="kv_ctx_odd",
    )(z)


def _window_kernel(q_ref, kc_ref, vc_ref, kl_ref, vl_ref, mask_ref, sink_ref, o_ref, kbc, vbc, kbl, vbl, *, lc, nblk):
    qi = pl.program_id(2)
    tq = q_ref.shape[1]
    blk = HEADS_PER_GROUP * Q_BLOCK

    @pl.when(qi == 0)
    def _build():
        kms, vms = _head_lane_masks(lc, 1)
        for j in range(HEADS_PER_GROUP):
            kbc[j * lc:(j + 1) * lc, :] = kc_ref[0] * kms[j]
            vbc[j * lc:(j + 1) * lc, :] = vc_ref[0] * vms[j]
        kms, vms = _head_lane_masks(Q_BLOCK, 1)
        zero = jnp.zeros((blk, MXU_DIM), BF16)
        kbl[0] = zero
        vbl[0] = zero
        kbl[nblk + 1] = zero
        vbl[nblk + 1] = zero

        def body(i, carry):
            r0 = pl.multiple_of(i * Q_BLOCK, Q_BLOCK)
            for j in range(HEADS_PER_GROUP):
                kbl[i + 1, j * Q_BLOCK:(j + 1) * Q_BLOCK, :] = kl_ref[0, pl.ds(r0, Q_BLOCK), :] * kms[j]
                vbl[i + 1, j * Q_BLOCK:(j + 1) * Q_BLOCK, :] = vl_ref[0, pl.ds(r0, Q_BLOCK), :] * vms[j]
            return carry

        lax.fori_loop(0, nblk, body, 0)

    q = q_ref[0]
    nt = (((1,), (1,)), ((), ()))
    s_c = lax.dot_general(q, kbc[...], nt, preferred_element_type=F32)
    kspan = kbl[pl.ds(qi, 3)].reshape(3 * blk, MXU_DIM)
    vspan = vbl[pl.ds(qi, 3)].reshape(3 * blk, MXU_DIM)
    s_s = lax.dot_general(q, kspan, nt, preferred_element_type=F32)
    s_s = jnp.where(mask_ref[0] > 0.0, s_s, NEG_INF)
    pcs, pss, invs = [], [[None] * HEADS_PER_GROUP for _ in range(3)], []
    for j in range(HEADS_PER_GROUP):
        sink = sink_ref[0, j][0:1, 0:1]
        segs = [s_c[:, j * lc:(j + 1) * lc]] + [s_s[:, b * blk + j * Q_BLOCK: b * blk + (j + 1) * Q_BLOCK]
                                                 for b in range(3)]
        m = jnp.maximum(sink, jnp.max(segs[0], axis=-1, keepdims=True))
        for sg in segs[1:]:
            m = jnp.maximum(m, jnp.max(sg, axis=-1, keepdims=True))
        ps = [jnp.exp(sg - m) for sg in segs]
        den = jnp.exp(sink - m)
        for p in ps:
            den = den + jnp.sum(p, axis=-1, keepdims=True)
        invs.append(1.0 / den)
        pcs.append(ps[0].astype(BF16))
        for b in range(3):
            pss[b][j] = ps[1 + b].astype(BF16)
    p_c = jnp.concatenate(pcs, axis=1)
    p_s = jnp.concatenate([p for b in range(3) for p in pss[b]], axis=1)
    o = jnp.dot(p_c, vbc[...], preferred_element_type=F32) + jnp.dot(p_s, vspan, preferred_element_type=F32)
    o_ref[0] = (o * _lanes4(invs, tq)).astype(BF16)


def _window_mask(n):
    r = jnp.arange(Q_BLOCK)[:, None]
    col = jnp.arange(3 * HEADS_PER_GROUP * Q_BLOCK)[None, :]
    b = col // (HEADS_PER_GROUP * Q_BLOCK)
    jj = b * Q_BLOCK + col % Q_BLOCK
    band = (jj >= r) & (jj <= r + 2 * WINDOW)
    first = band & (b > 0)
    lastm = band & (b < 2)
    return jnp.stack([first, band, lastm]).astype(F32)


def _window(q, kc, vc, kl, vl, sink):
    bsz, n, qw = q.shape
    groups = qw // MXU_DIM
    lc = kc.shape[1]
    nblk = n // Q_BLOCK
    blk = HEADS_PER_GROUP * Q_BLOCK
    mask = _window_mask(n)
    if nblk == 1:
        mask = (mask[0] * mask[2])[None]
    sink_t = jnp.broadcast_to(sink.reshape(groups, HEADS_PER_GROUP, 1, 1).astype(F32), (groups, HEADS_PER_GROUP, SUB, LANES))

    def mask_idx(b, g, i):
        if nblk == 1:
            return (0, 0, 0)
        return (jnp.where(i == 0, 0, jnp.where(i == nblk - 1, 2, 1)), 0, 0)

    return pl.pallas_call(
        functools.partial(_window_kernel, lc=lc, nblk=nblk),
        grid=(bsz, groups, nblk),
        in_specs=[pl.BlockSpec((1, Q_BLOCK, MXU_DIM), lambda b, g, i: (b, i, g)),
                  pl.BlockSpec((1, lc, MXU_DIM), lambda b, g, i: (b, 0, g)),
                  pl.BlockSpec((1, lc, MXU_DIM), lambda b, g, i: (b, 0, g)),
                  pl.BlockSpec((1, n, MXU_DIM), lambda b, g, i: (b, 0, g)),
                  pl.BlockSpec((1, n, MXU_DIM), lambda b, g, i: (b, 0, g)),
                  pl.BlockSpec((1, Q_BLOCK, 3 * blk), mask_idx),
                  pl.BlockSpec((1, HEADS_PER_GROUP, SUB, LANES), lambda b, g, i: (g, 0, 0, 0))],
        out_specs=pl.BlockSpec((1, Q_BLOCK, MXU_DIM), lambda b, g, i: (b, i, g)),
        out_shape=jax.ShapeDtypeStruct((bsz, n, qw), BF16),
        scratch_shapes=[pltpu.VMEM((HEADS_PER_GROUP * lc, MXU_DIM), BF16),
                        pltpu.VMEM((HEADS_PER_GROUP * lc, MXU_DIM), BF16),
                        pltpu.VMEM((nblk + 2, blk, MXU_DIM), BF16),
                        pltpu.VMEM((nblk + 2, blk, MXU_DIM), BF16)],
        compiler_params=_params(("parallel", "parallel", "arbitrary")),
        name="window_attn",
    )(q, kc, vc, kl, vl, mask, sink_t)


FFT_N2 = 128


def _filter_kernel(z_ref, w1_ref, b1_ref, w2_ref, b2_ref, w3_ref, b3_ref, w4_ref, fr_ref, dl_ref, o_ref):
    def lin(a, w_ref):
        a_hi, a_lo = _split_bf16(a)
        w_hi, w_lo = _split_bf16(w_ref[...])
        return _dot3(a_hi, a_lo, w_hi, w_lo)

    z = z_ref[...]
    h = b1_ref[...]
    for e in range(C_EMB_DIM):
        h = h + z[:, e:e + 1] * w1_ref[e:e + 1, :]
    h = jnp.sin(fr_ref[0:1] * h)
    h = jnp.sin(fr_ref[1:2] * (lin(h, w2_ref) + b2_ref[...]))
    h = jnp.sin(fr_ref[2:3] * (lin(h, w3_ref) + b3_ref[...]))
    h = lin(h, w4_ref)
    t = z[:, 0:1]
    cw = dl_ref.shape[1]
    dec = jnp.exp(-t * dl_ref[...])
    row = lax.broadcasted_iota(jnp.int32, dec.shape, 0) + pl.program_id(0) * z.shape[0]
    o_ref[0] = h[:, :cw] * dec
    o_ref[1] = jnp.where(row == 0, 0.0, h[:, cw:] * dec)


def _filters(zfeat, w1, b1, w2, b2, w3, b3, w4, freq, deltas, tm):
    n = zfeat.shape[0]
    tm = min(tm, n)
    cw = deltas.shape[1]
    full = lambda a: pl.BlockSpec(a.shape, lambda i: (0,) * a.ndim)
    ops = (w1, b1, w2, b2, w3, b3, w4, freq, deltas)
    return pl.pallas_call(
        _filter_kernel,
        grid=(n // tm,),
        in_specs=[pl.BlockSpec((tm, zfeat.shape[1]), lambda i: (i, 0))] + [full(a) for a in ops],
        out_specs=pl.BlockSpec((2, tm, cw), lambda i: (0, i, 0)),
        out_shape=jax.ShapeDtypeStruct((2, n, cw), F32),
        compiler_params=_params(("parallel",)),
        name="hyena_filters",
    )(zfeat, *ops)


def _dft_rows_kernel(fh_ref, fl_ref, x_ref, o_ref):
    x_hi, x_lo = _split_bf16(x_ref[0])
    o_ref[0] = _dot3(fh_ref[...], fl_ref[...], x_hi, x_lo)


def _dft_rows(f_hi, f_lo, x, tn):
    bsz, kdim, cols = x.shape
    m = f_hi.shape[0]
    tn = min(tn, cols)
    return pl.pallas_call(
        _dft_rows_kernel,
        grid=(bsz, cols // tn),
        in_specs=[pl.BlockSpec((m, kdim), lambda b, j: (0, 0)),
                  pl.BlockSpec((m, kdim), lambda b, j: (0, 0)),
                  pl.BlockSpec((1, kdim, tn), lambda b, j: (b, 0, j))],
        out_specs=pl.BlockSpec((1, m, tn), lambda b, j: (b, 0, j)),
        out_shape=jax.ShapeDtypeStruct((bsz, m, cols), F32),
        compiler_params=_params(("parallel", "parallel")),
        name="dft_rows",
    )(f_hi, f_lo, x)


def _spec_fwd(a_ref, mh_ref, ml_ref):
    a = a_ref[0, :, 0]
    a = a.reshape(2 * FFT_N2, a.shape[-1])
    a_hi, a_lo = _split_bf16(a)
    return _dot3(mh_ref[0], ml_ref[0], a_hi, a_lo)


def _filter_spec_kernel(a_ref, mh_ref, ml_ref, o_ref):
    x = _spec_fwd(a_ref, mh_ref, ml_ref)
    o_ref[0, :, 0] = x.reshape(2, FFT_N2, x.shape[-1])


def _conv_spec_kernel(a_ref, kf_ref, kb_ref, mh_ref, ml_ref, ih_ref, il_ref, o_ref):
    x = _spec_fwd(a_ref, mh_ref, ml_ref)
    xr, xi = x[:FFT_N2], x[FFT_N2:]
    kr = kf_ref[0, 0, 0] + kb_ref[0, 0, 0]
    ki = kf_ref[0, 1, 0] - kb_ref[0, 1, 0]
    y = jnp.concatenate([xr * kr - xi * ki, xr * ki + xi * kr], axis=0)
    y_hi, y_lo = _split_bf16(y)
    bv = _dot3(ih_ref[0], il_ref[0], y_hi, y_lo)
    o_ref[0, :, 0] = bv.reshape(2, FFT_N2, bv.shape[-1])


def _spec_specs(n1, cw):
    blk = pl.BlockSpec((1, 2, 1, FFT_N2, cw), lambda k, b: (b, 0, k, 0, 0))
    mat = pl.BlockSpec((1, 2 * FFT_N2, 2 * FFT_N2), lambda k, b: (k, 0, 0))
    return blk, mat


def _filter_spec(a, m_hi, m_lo):
    bsz, _, n1, _, cw = a.shape
    blk, mat = _spec_specs(n1, cw)
    return pl.pallas_call(
        _filter_spec_kernel,
        grid=(n1, bsz),
        in_specs=[blk, mat, mat],
        out_specs=blk,
        out_shape=jax.ShapeDtypeStruct(a.shape, F32),
        compiler_params=_params(("parallel", "parallel")),
        name="filter_spectrum",
    )(a, m_hi, m_lo)


def _conv_spec(a, kspec, m_hi, m_lo, i_hi, i_lo):
    bsz, _, n1, _, cw = a.shape
    blk, mat = _spec_specs(n1, cw)
    kf = pl.BlockSpec((1, 2, 1, FFT_N2, cw), lambda k, b: (0, 0, k, 0, 0))
    kb = pl.BlockSpec((1, 2, 1, FFT_N2, cw), lambda k, b: (1, 0, k, 0, 0))
    return pl.pallas_call(
        _conv_spec_kernel,
        grid=(n1, bsz),
        in_specs=[blk, kf, kb, mat, mat, mat, mat],
        out_specs=blk,
        out_shape=jax.ShapeDtypeStruct(a.shape, F32),
        compiler_params=_params(("parallel", "parallel")),
        name="conv_spectrum",
    )(a, kspec, kspec, m_hi, m_lo, i_hi, i_lo)


def _hyena_gate_kernel(y_ref, x0_ref, u_ref, b_ref, o_ref):
    o_ref[0] = (x0_ref[0] * (y_ref[0] + u_ref[0] * b_ref[...])).astype(BF16)


def _hyena_gate(y, x0, u, bias, tm):
    bsz, n, cw = y.shape
    tm = min(tm, n)
    blk = pl.BlockSpec((1, tm, cw), lambda b, i: (b, i, 0))
    return pl.pallas_call(
        _hyena_gate_kernel,
        grid=(bsz, n // tm),
        in_specs=[blk, blk, blk, pl.BlockSpec((1, cw), lambda b, i: (0, 0))],
        out_specs=blk,
        out_shape=jax.ShapeDtypeStruct((bsz, n, cw), BF16),
        compiler_params=_params(("parallel", "parallel")),
        name="hyena_gate",
    )(y, x0, u, bias.reshape(1, cw))


def _dft_tables(n):
    nfft = 2 * n
    n2 = FFT_N2
    n1 = nfft // n2
    n1h = n1 // 2
    two_pi = 2.0 * math.pi

    def cs(num, den):
        ang = (num % den).astype(F32) * (two_pi / den)
        return jnp.cos(ang), jnp.sin(ang)

    k1 = jnp.arange(n1, dtype=jnp.int32)
    c1, s1 = cs(k1[:, None] * jnp.arange(n1h, dtype=jnp.int32)[None, :], n1)
    f1 = jnp.concatenate([c1, -s1], axis=0)
    kk = k1[:, None, None] + n1 * jnp.arange(n2, dtype=jnp.int32)[None, :, None]
    cg, sg = cs(kk * jnp.arange(n2, dtype=jnp.int32)[None, None, :], nfft)
    g_re, g_im = cg, -sg
    m_fwd = jnp.concatenate([jnp.concatenate([g_re, -g_im], axis=2),
                             jnp.concatenate([g_im, g_re], axis=2)], axis=1)
    gt_re, gt_im = jnp.swapaxes(g_re, 1, 2), jnp.swapaxes(g_im, 1, 2)
    m_inv = jnp.concatenate([jnp.concatenate([gt_re, gt_im], axis=2),
                             jnp.concatenate([-gt_im, gt_re], axis=2)], axis=1)
    c2, s2 = cs(jnp.arange(n1h, dtype=jnp.int32)[:, None] * k1[None, :], n1)
    f2 = jnp.concatenate([c2, -s2], axis=1) * (1.0 / nfft)
    return tuple(_split_bf16(t) for t in (f1, m_fwd, m_inv, f2))


def _long_conv(u, hcat, tabs):
    (f1h, f1l), (mfh, mfl), (mih, mil), (f2h, f2l) = tabs
    bsz, n, cw = u.shape
    n2 = FFT_N2
    n1 = 2 * n // n2
    n1h = n1 // 2
    tn = 8192
    a_f = _dft_rows(f1h, f1l, hcat.reshape(2, n1h, n2 * cw), tn).reshape(2, 2, n1, n2, cw)
    kspec = _filter_spec(a_f, mfh, mfl)
    a_u = _dft_rows(f1h, f1l, u.reshape(bsz, n1h, n2 * cw), tn).reshape(bsz, 2, n1, n2, cw)
    b_u = _conv_spec(a_u, kspec, mfh, mfl, mih, mil)
    y = _dft_rows(f2h, f2l, b_u.reshape(bsz, 2 * n1, n2 * cw), tn)
    return y.reshape(bsz, n, cw)


def _axial_angles(rows, rope_dim):
    row_idx = jnp.repeat(jnp.arange(rows), GRID_W).astype(F32)
    col_idx = jnp.tile(jnp.arange(GRID_W), rows).astype(F32)
    d_axis = rope_dim // 2
    inv_freq = ROPE_THETA ** (-jnp.arange(0, d_axis, 2, dtype=F32) / d_axis)
    ang = jnp.concatenate([row_idx[:, None] * inv_freq, col_idx[:, None] * inv_freq], axis=-1)
    return jnp.cos(ang), jnp.sin(ang)


def _rope_tables(n):
    rows = n // GRID_W
    sign = jnp.tile(jnp.array([-1.0, 1.0], F32), LANES // 2)[None, :]
    cos, sin = _axial_angles(rows, HEAD_DIM)
    chd = jnp.tile(jnp.repeat(cos, 2, axis=1), (1, LANES // HEAD_DIM))
    shd = jnp.tile(jnp.repeat(sin, 2, axis=1), (1, LANES // HEAD_DIM)) * sign
    cos, sin = _axial_angles(rows, B_ROPE)
    ones = jnp.ones((n, LANES - 2 * B_ROPE), F32)
    cb = jnp.concatenate([jnp.tile(jnp.repeat(cos, 2, axis=1), (1, 2)), ones], axis=1)
    sb = jnp.concatenate([jnp.tile(jnp.repeat(sin, 2, axis=1), (1, 2)), 0.0 * ones], axis=1) * sign
    return chd, shd, cb, sb


def _even_weights(w_in, a_qn, a_kn, b_qn, b_w_uq, b_kvn, b_w_ukv):
    d = w_in.shape[0]
    aq_w = w_in.shape[1] - (B_Q_RANK + 2 * A_KV_HEADS * HEAD_DIM + B_KV_RANK + B_ROPE)
    o = [0, aq_w, aq_w + B_Q_RANK]
    o += [o[-1] + A_KV_HEADS * HEAD_DIM, o[-1] + 2 * A_KV_HEADS * HEAD_DIM]
    o += [o[-1] + B_KV_RANK, o[-1] + B_KV_RANK + B_ROPE]
    kr = w_in[:, o[5]:o[6]]
    w_aug = jnp.concatenate([w_in[:, :o[5]], kr, kr, jnp.zeros((d, LANES - 2 * B_ROPE), w_in.dtype)], axis=1)
    heads = b_w_uq.shape[1] // (B_NOPE + B_ROPE)
    uq = b_w_uq.reshape(B_Q_RANK, heads // 2, 2, B_NOPE + B_ROPE)
    wuq = jnp.concatenate([uq[:, :, 0, :B_NOPE], uq[:, :, 1, :B_NOPE], uq[:, :, 0, B_NOPE:], uq[:, :, 1, B_NOPE:],
                           jnp.zeros((B_Q_RANK, heads // 2, MXU_DIM - 2 * (B_NOPE + B_ROPE)), b_w_uq.dtype)], axis=2)
    wuq = wuq.reshape(B_Q_RANK, heads // 2 * MXU_DIM)
    ukv = b_w_ukv.reshape(B_KV_RANK, heads, B_NOPE + B_VDIM)
    wukn = ukv[:, :, :B_NOPE].reshape(B_KV_RANK, heads * B_NOPE)
    wuv = ukv[:, :, B_NOPE:].reshape(B_KV_RANK, heads * B_VDIM)
    gains = (jnp.tile(a_qn, aq_w // HEAD_DIM)[None, :], jnp.tile(a_kn, A_KV_HEADS)[None, :],
             b_qn[None, :], b_kvn[None, :])
    return w_aug.astype(BF16), gains + (wuq.astype(BF16), wukn.astype(BF16), wuv.astype(BF16))


def _ffn_weights(w_up, conv_w, conv_b, w_down, tf):
    d, f2 = w_up.shape
    f = f2 // 2
    nf = f // tf
    wup = w_up.reshape(d, 2, nf, tf).transpose(1, 2, 0, 3).astype(BF16)
    cw = jnp.concatenate([conv_w, conv_b[None, :], jnp.zeros((SUB - 4, f2), conv_w.dtype)], axis=0)
    cw = cw.reshape(SUB, 2, nf, tf).transpose(1, 2, 0, 3)
    wdn = w_down.reshape(nf, tf, d).astype(BF16)
    return wup, cw, wdn


FFN_TF = 256


def kernel(x, c, ctx, c_ctx, w_mod, b_mod, norm_mix, norm_ffn, ev_w_in, ev_w_out, a_q_norm, a_k_norm, b_q_norm, b_w_uq, b_kv_norm, b_w_ukv, od_w_in, od_w_out, d_sink, c_conv_w, c_conv_b, c_filt_w1, c_filt_b1, c_filt_w2, c_filt_b2, c_filt_w3, c_filt_b3, c_filt_w4, c_filt_freq, c_bias, ffn_w_up, ffn_conv_w, ffn_conv_b, ffn_w_down, final_norm):
    bsz, n, d = x.shape
    depth = w_mod.shape[0]
    assert depth == 2 and n % Q_BLOCK == 0 and d % LANES == 0
    tm = 512
    rows = -(-(bsz + 1) // SUB) * SUB
    cvec = jnp.concatenate([c, c_ctx[None, :], jnp.zeros((rows - bsz - 1, d), F32)], axis=0)
    mods = _mods(cvec, w_mod, b_mod)
    chd, shd, cb, sb = _rope_tables(n)
    tabs = (chd, shd, cb, sb)

    m3 = mods[0].reshape(rows, 1, N_MOD * d)
    w_aug, prep_w = _even_weights(ev_w_in[0], a_q_norm[0], a_k_norm[0], b_q_norm[0], b_w_uq[0],
                                  b_kv_norm[0], b_w_ukv[0])
    z_l = _inproj(x, norm_mix[0], m3, None, w_aug, tm)
    z_c = _inproj(ctx, norm_mix[0], m3, bsz, w_aug, tm)
    qa_l, ka_l, va_l, qb_l, kb_l, vb_l = _prep_even(z_l, prep_w, tabs, True, tm)
    qa_c, ka_c, va_c, qb_c, kb_c, vb_c = _prep_even(z_c, prep_w, tabs, False, tm)
    tq, tk = 512, 256
    oa_l = _flash(qa_l, ka_c, va_c, ka_l, va_l, 1, tq, tk)
    ob_l = _flash(qb_l, kb_c, vb_c, kb_l, vb_l, 2, tq, tk)
    oa_c = _flash(qa_c, ka_c, va_c, None, None, 1, tq, tk)
    ob_c = _flash(qb_c, kb_c, vb_c, None, None, 2, tq, tk)
    w_out = ev_w_out[0].astype(BF16)
    x = _outproj(oa_l, ob_l, x, m3, None, w_out, tm)
    ctx = _outproj(oa_c, ob_c, ctx, m3, bsz, w_out, tm)
    ffn_w = _ffn_weights(ffn_w_up[0], ffn_conv_w[0], ffn_conv_b[0], ffn_w_down[0], FFN_TF)
    x = _ffn(x, norm_ffn[0], m3, None, *ffn_w, None, tm)
    ctx = _ffn(ctx, norm_ffn[0], m3, bsz, *ffn_w, None, tm)

    m3 = mods[1].reshape(rows, 1, N_MOD * d)
    w_in = od_w_in[0].astype(BF16)
    z_l = _inproj(x, norm_mix[1], m3, None, w_in, tm)
    z_c = _inproj(ctx, norm_mix[1], m3, bsz, w_in[:, OD_K:], tm)
    cw = jnp.concatenate([c_conv_w[0], c_conv_b[0][None, :], jnp.zeros((SUB - 4, OD_K - OD_C), F32)], axis=0)
    qd, kd, vd, x0, u = _prep_odd(z_l, cw, chd, shd, tm)
    kd_c, vd_c = _kv_ctx_odd(z_c)
    od = _window(qd, kd_c, vd_c, kd, vd, d_sink[0])
    t = jnp.linspace(0.0, 1.0, n, dtype=F32)[:, None]
    wpos = 2 * math.pi * jnp.arange(n, dtype=F32)[:, None] / n
    fb = jnp.linspace(1e-4, C_BANDS - 1, C_BANDS, dtype=F32)[None, :]
    zfeat = jnp.concatenate([t, jnp.cos(fb * wpos), -jnp.sin(fb * wpos), jnp.zeros((n, SUB - C_EMB_DIM), F32)], axis=-1)
    cwid = c_bias.shape[1]
    deltas = jnp.abs(jnp.linspace(C_MIN_DECAY, C_MAX_DECAY, cwid, dtype=F32))[None, :]
    w1 = jnp.concatenate([c_filt_w1[0], jnp.zeros((SUB - C_EMB_DIM, c_filt_w1.shape[2]), F32)], axis=0)
    hcat = _filters(zfeat, w1, c_filt_b1[0][None, :], c_filt_w2[0], c_filt_b2[0][None, :], c_filt_w3[0],
                    c_filt_b3[0][None, :], c_filt_w4[0], jnp.concatenate([c_filt_freq[0], jnp.zeros((SUB - 3, c_filt_freq.shape[2]), F32)], axis=0),
                    deltas, tm)
    y = _long_conv(u, hcat, _dft_tables(n))
    oc = _hyena_gate(y, x0, u, c_bias[0], tm)
    x = _outproj(od, oc, x, m3, None, od_w_out[0].astype(BF16), tm)
    ffn_w = _ffn_weights(ffn_w_up[1], ffn_conv_w[1], ffn_conv_b[1], ffn_w_down[1], FFN_TF)
    return _ffn(x, norm_ffn[1], m3, None, *ffn_w, final_norm, tm)
```

```python
import functools
import math

import jax
import jax.numpy as jnp
from jax import lax
from jax.experimental import pallas as pl
from jax.experimental.pallas import tpu as pltpu

F32 = jnp.float32
BF16 = jnp.bfloat16

GRID_W = 64
HEAD_DIM = 64
ROPE_THETA = 10000.0
NORM_EPS = 1e-6
NEG_INF = -1e30
N_MOD = 6
A_KV_HEADS = 2
B_NOPE = 64
B_ROPE = 32
B_VDIM = 64
B_Q_RANK = 384
B_KV_RANK = 256
C_EMB_DIM = 5
C_BANDS = (C_EMB_DIM - 1) // 2
C_MIN_DECAY = math.log(1e-2) / 1.5
C_MAX_DECAY = math.log(1e-2) / 0.3
WINDOW = 128
Q_BLOCK = 128
LOG2E = math.log2(math.e)

LANES = 128
MXU_DIM = 256
VMEM_LIMIT = 56 * 1024 * 1024
HEADS_PER_GROUP = MXU_DIM // HEAD_DIM


def _params(sem, vmem=VMEM_LIMIT):
    return pltpu.CompilerParams(dimension_semantics=sem, vmem_limit_bytes=vmem)


def _resident(shape, index_map):
    return pl.BlockSpec(shape, index_map, pipeline_mode=pl.Buffered(1))


def _norm_mod(x, gain, shift, scale):
    inv = lax.rsqrt(jnp.mean(x * x, axis=-1, keepdims=True) + NORM_EPS)
    return (x * inv) * gain * (1.0 + scale) + shift


def _rope_lanes(x, cos, sin_signed):
    lane = lax.broadcasted_iota(jnp.int32, x.shape, 1)
    nxt = pltpu.roll(x, LANES - 1, axis=1)
    prv = pltpu.roll(x, 1, axis=1)
    swapped = jnp.where(lane % 2 == 0, nxt, prv)
    return x * cos + swapped * sin_signed


def _head_rmsnorm_lanes(x, gain):
    lane = lax.broadcasted_iota(jnp.int32, x.shape, 1)
    lo = lane < HEAD_DIM
    sq = x * x
    s_lo = jnp.sum(jnp.where(lo, sq, 0.0), axis=-1, keepdims=True)
    s_hi = jnp.sum(jnp.where(lo, 0.0, sq), axis=-1, keepdims=True)
    ms = jnp.where(lo, s_lo, s_hi) * (1.0 / HEAD_DIM)
    return x * lax.rsqrt(ms + NORM_EPS) * gain


def _dup_heads(x):
    lane = lax.broadcasted_iota(jnp.int32, x.shape, 1)
    lo = lane < HEAD_DIM
    r = pltpu.roll(x, HEAD_DIM, axis=1)
    return jnp.where(lo, x, r), jnp.where(lo, r, x)


def _lanes4(vals, rows):
    lane = lax.broadcasted_iota(jnp.int32, (rows, MXU_DIM), 1)
    return jnp.where(lane < HEAD_DIM, vals[0],
                     jnp.where(lane < 2 * HEAD_DIM, vals[1],
                               jnp.where(lane < 3 * HEAD_DIM, vals[2], vals[3])))


def _split_bf16(x):
    hi = x.astype(BF16)
    lo = (x - hi.astype(F32)).astype(BF16)
    return hi, lo


def _dot3(a_hi, a_lo, b_hi, b_lo):
    d = functools.partial(jnp.dot, preferred_element_type=F32)
    return d(a_hi, b_hi) + d(a_hi, b_lo) + d(a_lo, b_hi)


def _mods_kernel(c_ref, w_ref, b_ref, o_ref):
    c = c_ref[...]
    s = c * (1.0 / (1.0 + jnp.exp(-c)))
    s_hi, s_lo = _split_bf16(s)
    w_hi, w_lo = _split_bf16(w_ref[0])
    o_ref[0] = _dot3(s_hi, s_lo, w_hi, w_lo) + b_ref[0]


def _mods(cvec, w_mod, b_mod):
    depth, d, n = w_mod.shape
    rows = cvec.shape[0]
    tn = 1536
    return pl.pallas_call(
        _mods_kernel,
        grid=(depth, n // tn),
        in_specs=[pl.BlockSpec((rows, d), lambda l, j: (0, 0)),
                  pl.BlockSpec((1, d, tn), lambda l, j: (l, 0, j)),
                  pl.BlockSpec((1, 1, tn), lambda l, j: (l, 0, j))],
        out_specs=pl.BlockSpec((1, rows, tn), lambda l, j: (l, 0, j)),
        out_shape=jax.ShapeDtypeStruct((depth, rows, n), F32),
        compiler_params=_params(("arbitrary", "arbitrary")),
        name="mods",
    )(cvec, w_mod, b_mod.reshape(depth, 1, n))


def _mod_spec(d, chunk, row):
    if row is None:
        return pl.BlockSpec((1, 1, d), lambda b, *_: (b, 0, chunk))
    return pl.BlockSpec((1, 1, d), lambda b, *_: (row, 0, chunk))


def _inproj_kernel(x_ref, g_ref, sh_ref, sc_ref, w_ref, o_ref):
    h = _norm_mod(x_ref[0], g_ref[...], sh_ref[0], sc_ref[0])
    o_ref[0] = jnp.dot(h.astype(BF16), w_ref[...], preferred_element_type=F32)


def _inproj(x, gain, mods3, mrow, w, tm):
    bsz, n, d = x.shape
    cols = w.shape[1]
    tm = min(tm, n)
    return pl.pallas_call(
        _inproj_kernel,
        grid=(bsz, n // tm),
        in_specs=[pl.BlockSpec((1, tm, d), lambda b, i: (b, i, 0)),
                  pl.BlockSpec((1, d), lambda b, i: (0, 0)),
                  _mod_spec(d, 0, mrow), _mod_spec(d, 1, mrow),
                  _resident((d, cols), lambda b, i: (0, 0))],
        out_specs=pl.BlockSpec((1, tm, cols), lambda b, i: (b, i, 0)),
        out_shape=jax.ShapeDtypeStruct((bsz, n, cols), F32),
        compiler_params=_params(("parallel", "parallel")),
        name="inproj",
    )(x, gain.reshape(1, d), mods3, mods3, w)


EV_AQ, EV_BQ, EV_AK, EV_AV, EV_BKV, EV_BKR, EV_END = 0, 512, 896, 1024, 1152, 1408, 1536


def _prep_even_kernel(z_ref, aqg_ref, akg_ref, bqg_ref, bkvg_ref, wuq_ref, wukn_ref, wuv_ref,
                      chd_ref, shd_ref, cb_ref, sb_ref,
                      qa_ref, ka_ref, va_ref, qb_ref, kb_ref, vb_ref, *, use_rope, a_scale, b_scale):
    def rope_hd(y):
        return _rope_lanes(y, chd_ref[...], shd_ref[...]) if use_rope else y

    def rope_b(y):
        return _rope_lanes(y, cb_ref[...], sb_ref[...]) if use_rope else y

    def put_t(dst, c, y):
        dst[0, c * LANES:(c + 1) * LANES, :] = y.T.astype(BF16)

    for c in range((EV_BQ - EV_AQ) // LANES):
        sl = slice(EV_AQ + c * LANES, EV_AQ + (c + 1) * LANES)
        y = rope_hd(_head_rmsnorm_lanes(z_ref[0, :, sl], aqg_ref[:, c * LANES:(c + 1) * LANES]))
        put_t(qa_ref, c, y * a_scale)
    k = rope_hd(_head_rmsnorm_lanes(z_ref[0, :, EV_AK:EV_AV], akg_ref[...]))
    d0, d1 = _dup_heads(k)
    d0 = d0.astype(BF16)
    d1 = d1.astype(BF16)
    ka_ref[0, :, 0 * LANES:1 * LANES] = d0
    ka_ref[0, :, 1 * LANES:2 * LANES] = d0
    ka_ref[0, :, 2 * LANES:3 * LANES] = d1
    ka_ref[0, :, 3 * LANES:4 * LANES] = d1
    d0, d1 = _dup_heads(z_ref[0, :, EV_AV:EV_BKV])
    for c, dv in enumerate((d0, d0, d1, d1)):
        put_t(va_ref, c, dv)
    cq = z_ref[0, :, EV_BQ:EV_AK]
    cq = cq * lax.rsqrt(jnp.mean(cq * cq, axis=-1, keepdims=True) + NORM_EPS) * bqg_ref[...]
    qb = jnp.dot(cq.astype(BF16), wuq_ref[...], preferred_element_type=F32)
    for c in range(qb.shape[1] // MXU_DIM):
        lo = slice(c * MXU_DIM, c * MXU_DIM + LANES)
        hi = slice(c * MXU_DIM + LANES, (c + 1) * MXU_DIM)
        put_t(qb_ref, 2 * c, qb[:, lo] * b_scale)
        put_t(qb_ref, 2 * c + 1, rope_b(qb[:, hi]) * b_scale)
    ckv = z_ref[0, :, EV_BKV:EV_BKR]
    ckv = (ckv * lax.rsqrt(jnp.mean(ckv * ckv, axis=-1, keepdims=True) + NORM_EPS) * bkvg_ref[...]).astype(BF16)
    kn = jnp.dot(ckv, wukn_ref[...], preferred_element_type=F32)
    vb = jnp.dot(ckv, wuv_ref[...], preferred_element_type=F32)
    for c in range(vb.shape[1] // LANES):
        put_t(vb_ref, c, vb[:, c * LANES:(c + 1) * LANES])
    kr = rope_b(z_ref[0, :, EV_BKR:EV_END]).astype(BF16)
    for c in range(kn.shape[1] // LANES):
        kb_ref[0, :, c * MXU_DIM:c * MXU_DIM + LANES] = kn[:, c * LANES:(c + 1) * LANES].astype(BF16)
        kb_ref[0, :, c * MXU_DIM + LANES:(c + 1) * MXU_DIM] = kr


def _prep_even(z, wts, tabs, use_rope, tm):
    bsz, n, cols = z.shape
    tm = min(tm, n)
    aqg, akg, bqg, bkvg, wuq, wukn, wuv = wts
    chd, shd, cb, sb = tabs
    full = lambda a: _resident(a.shape, lambda b, i: (0,) * a.ndim)
    tab = lambda a: pl.BlockSpec((tm, LANES), (lambda b, i: (i, 0)) if use_rope else (lambda b, i: (0, 0)))
    outs = [(4 * LANES, True), (4 * LANES, False), (4 * LANES, True), (wuq.shape[1], True),
            (2 * wukn.shape[1], False), (wuv.shape[1], True)]
    kern = functools.partial(_prep_even_kernel, use_rope=use_rope, a_scale=HEAD_DIM ** -0.5 * LOG2E,
                             b_scale=(B_NOPE + B_ROPE) ** -0.5 * LOG2E)
    return pl.pallas_call(
        kern,
        grid=(bsz, n // tm),
        in_specs=[pl.BlockSpec((1, tm, cols), lambda b, i: (b, i, 0)),
                  full(aqg), full(akg), full(bqg), full(bkvg), full(wuq), full(wukn), full(wuv),
                  tab(chd), tab(shd), tab(cb), tab(sb)],
        out_specs=[pl.BlockSpec((1, w, tm), lambda b, i: (b, 0, i)) if t else
                   pl.BlockSpec((1, tm, w), lambda b, i: (b, i, 0)) for w, t in outs],
        out_shape=[jax.ShapeDtypeStruct((bsz, w, n) if t else (bsz, n, w), BF16) for w, t in outs],
        compiler_params=_params(("parallel", "parallel")),
        name="prep_even",
    )(z, aqg, akg, bqg, bkvg, wuq, wukn, wuv, chd, shd, cb, sb)


def _head_lane_masks(rows, nsub):
    lane = lax.broadcasted_iota(jnp.int32, (rows, MXU_DIM), 1)
    kms, vms = [], []
    for j in range(HEADS_PER_GROUP):
        v_lo = j * HEAD_DIM
        vm = (lane >= v_lo) & (lane < v_lo + HEAD_DIM)
        if nsub == 1:
            km = vm
        else:
            jj = j % 2
            km = ((lane >= jj * B_NOPE) & (lane < (jj + 1) * B_NOPE)) | \
                 ((lane >= 2 * B_NOPE + jj * B_ROPE) & (lane < 2 * B_NOPE + (jj + 1) * B_ROPE))
        kms.append(jnp.where(km, 1.0, 0.0).astype(BF16))
        vms.append(jnp.where(vm, 1.0, 0.0).astype(BF16))
    return kms, vms


KV_UNROLL = 8


def _head_row_masks(cols):
    row = lax.broadcasted_iota(jnp.int32, (MXU_DIM, cols), 0)
    return [jnp.where((row >= j * HEAD_DIM) & (row < (j + 1) * HEAD_DIM), 1.0, 0.0).astype(BF16)
            for j in range(HEADS_PER_GROUP)]


def _rows4(vals, cols):
    return jnp.concatenate([jnp.broadcast_to(v, (HEAD_DIM, cols)) for v in vals], axis=0)


def _flash_kernel(*refs, nsub, tk, nsrc, nchain):
    qt_ref = refs[0]
    srcs = [(refs[1 + 2 * s], refs[2 + 2 * s]) for s in range(nsrc)]
    o_ref, kb_ref, vbt_ref, acc_ref = refs[1 + 2 * nsrc:]
    hpu = HEADS_PER_GROUP // nsub
    tq = qt_ref.shape[2]
    tqc = tq // nchain
    nb = kb_ref.shape[0]

    @pl.when(pl.program_id(2) == 0)
    def _build():
        kms, _ = _head_lane_masks(tk, nsub)
        vms = _head_row_masks(tk)
        base = 0
        for k_ref, vt_ref in srcs:
            nblk = k_ref.shape[1] // tk

            def body(i, carry, k_ref=k_ref, base=base):
                r0 = pl.multiple_of(i * tk, tk)
                for j in range(HEADS_PER_GROUP):
                    u = j // hpu
                    kb_ref[base + i, j * tk:(j + 1) * tk, :] = \
                        k_ref[0, pl.ds(r0, tk), u * MXU_DIM:(u + 1) * MXU_DIM] * kms[j]
                return carry

            lax.fori_loop(0, nblk, body, 0)
            for i in range(nblk):
                for j in range(HEADS_PER_GROUP):
                    vbt_ref[base + i, :, j * tk:(j + 1) * tk] = vt_ref[0, :, i * tk:(i + 1) * tk] * vms[j]
            base += nblk

    def block(i, state):
        kb, vbt = kb_ref[i], vbt_ref[i]
        new_state = []
        for h in range(nchain):
            cs = slice(h * tqc, (h + 1) * tqc)
            m, l = state[h]
            parts = [jnp.dot(kb[u * hpu * tk:(u + 1) * hpu * tk, :], qt_ref[0, u * MXU_DIM:(u + 1) * MXU_DIM, cs],
                             preferred_element_type=F32) for u in range(nsub)]
            ps, alphas, m_new, l_new = [], [], [], []
            for j in range(HEADS_PER_GROUP):
                sj = parts[j // hpu][(j % hpu) * tk:(j % hpu + 1) * tk, :]
                mj = jnp.maximum(m[j], jnp.max(sj, axis=0, keepdims=True))
                a = jnp.exp2(m[j] - mj)
                p = jnp.exp2(sj - mj)
                l_new.append(a * l[j] + jnp.sum(p, axis=0, keepdims=True))
                m_new.append(mj)
                alphas.append(a)
                ps.append(p.astype(BF16))
            pv = jnp.dot(vbt, jnp.concatenate(ps, axis=0), preferred_element_type=F32)
            acc_ref[:, cs] = acc_ref[:, cs] * _rows4(alphas, tqc) + pv
            new_state.append((m_new, l_new))
        return new_state

    def flat(st):
        return tuple(v for m, l in st for v in (*m, *l))

    def unflat(c):
        n = 2 * HEADS_PER_GROUP
        return [(list(c[h * n:h * n + HEADS_PER_GROUP]), list(c[h * n + HEADS_PER_GROUP:(h + 1) * n]))
                for h in range(nchain)]

    m0 = [jnp.full((1, tqc), NEG_INF, F32)] * HEADS_PER_GROUP
    l0 = [jnp.zeros((1, tqc), F32)] * HEADS_PER_GROUP
    acc_ref[...] = jnp.zeros(acc_ref.shape, F32)
    state = unflat(lax.fori_loop(0, nb, lambda i, c: flat(block(i, unflat(c))), flat([(m0, l0)] * nchain),
                                 unroll=min(KV_UNROLL, nb)))
    for h in range(nchain):
        cs = slice(h * tqc, (h + 1) * tqc)
        ot = acc_ref[:, cs] * _rows4([1.0 / v for v in state[h][1]], tqc)
        o_ref[0, cs, :] = ot.T.astype(BF16)


def _flash(qt, ksrcs, vtsrcs, nsub, tq, tk):
    bsz, qw, nq = qt.shape
    kw = nsub * MXU_DIM
    groups = qw // kw
    tq = min(tq, nq)
    nchain = 2 if tq >= 2 * MXU_DIM else 1
    in_specs = [pl.BlockSpec((1, kw, tq), lambda b, g, i: (b, g, i))]
    args = [qt]
    nb = 0
    for k, vt in zip(ksrcs, vtsrcs):
        lk = k.shape[1]
        assert lk % tk == 0
        nb += lk // tk
        in_specs += [pl.BlockSpec((1, lk, kw), lambda b, g, i: (b, 0, g)),
                     pl.BlockSpec((1, MXU_DIM, lk), lambda b, g, i: (b, g, 0))]
        args += [k, vt]
    scratch = [pltpu.VMEM((nb, HEADS_PER_GROUP * tk, MXU_DIM), BF16),
               pltpu.VMEM((nb, MXU_DIM, HEADS_PER_GROUP * tk), BF16),
               pltpu.VMEM((MXU_DIM, tq), F32)]
    return pl.pallas_call(
        functools.partial(_flash_kernel, nsub=nsub, tk=tk, nsrc=len(ksrcs), nchain=nchain),
        grid=(bsz, groups, nq // tq),
        in_specs=in_specs,
        out_specs=pl.BlockSpec((1, tq, MXU_DIM), lambda b, g, i: (b, i, g)),
        out_shape=jax.ShapeDtypeStruct((bsz, nq, groups * MXU_DIM), BF16),
        scratch_shapes=scratch,
        compiler_params=_params(("parallel", "parallel", "arbitrary")),
        name="flash_attn",
    )(*args)


def _outproj_kernel(oa_ref, ob_ref, x_ref, gt_ref, wa_ref, wb_ref, o_ref):
    y = jnp.dot(oa_ref[0], wa_ref[...], preferred_element_type=F32)
    y = y + jnp.dot(ob_ref[0], wb_ref[...], preferred_element_type=F32)
    o_ref[0] = x_ref[0] + gt_ref[0] * y


def _outproj(oa, ob, x, mods3, mrow, w_out, tm):
    bsz, n, d = x.shape
    half = oa.shape[2]
    tm = min(tm, n)
    return pl.pallas_call(
        _outproj_kernel,
        grid=(bsz, n // tm),
        in_specs=[pl.BlockSpec((1, tm, half), lambda b, i: (b, i, 0)),
                  pl.BlockSpec((1, tm, half), lambda b, i: (b, i, 0)),
                  pl.BlockSpec((1, tm, d), lambda b, i: (b, i, 0)),
                  _mod_spec(d, 2, mrow),
                  _resident((half, d), lambda b, i: (0, 0)),
                  _resident((half, d), lambda b, i: (1, 0))],
        out_specs=pl.BlockSpec((1, tm, d), lambda b, i: (b, i, 0)),
        out_shape=jax.ShapeDtypeStruct((bsz, n, d), F32),
        compiler_params=_params(("parallel", "parallel")),
        name="outproj",
    )(oa, ob, x, mods3, w_out, w_out)


HALO = 16


def _ffn_kernel(xp_ref, x_ref, xn_ref, g_ref, sh_ref, sc_ref, gt_ref, wup_ref, cw_ref, wdn_ref, fg_ref,
                o_ref, h_ref, *, final_norm):
    i = pl.program_id(1)
    last = pl.num_programs(1) - 1
    tm = x_ref.shape[1]
    nf = wdn_ref.shape[0]
    gain, shift, scale = g_ref[...], sh_ref[0], sc_ref[0]
    keep_p = jnp.where(i > 0, 1.0, 0.0)
    keep_n = jnp.where(i < last, 1.0, 0.0)
    h_ref[0:HALO, :] = (_norm_mod(xp_ref[0], gain, shift, scale) * keep_p).astype(BF16)
    h_ref[HALO:HALO + tm, :] = _norm_mod(x_ref[0], gain, shift, scale).astype(BF16)
    h_ref[HALO + tm:, :] = (_norm_mod(xn_ref[0], gain, shift, scale) * keep_n).astype(BF16)
    rows = tm + 2 * HALO

    def conv(u, f, part):
        w = cw_ref[part, f]
        up = pltpu.roll(u, 1, axis=0)[HALO:HALO + tm]
        un = pltpu.roll(u, rows - 1, axis=0)[HALO:HALO + tm]
        return up * w[0:1] + u[HALO:HALO + tm] * w[1:2] + un * w[2:3] + w[3:4]

    h = h_ref[...]
    up = lambda f: (jnp.dot(h, wup_ref[0, f], preferred_element_type=F32),
                    jnp.dot(h, wup_ref[1, f], preferred_element_type=F32))
    y = None
    ug, uv = up(0)
    for f in range(nf):
        nxt = up(f + 1) if f + 1 < nf else None
        g = conv(ug, f, 0)
        v = conv(uv, f, 1)
        a = (g * (1.0 / (1.0 + jnp.exp(-g))) * v).astype(BF16)
        yf = jnp.dot(a, wdn_ref[f], preferred_element_type=F32)
        y = yf if y is None else y + yf
        if nxt is not None:
            ug, uv = nxt
    out = x_ref[0] + gt_ref[0] * y
    if final_norm:
        out = out * lax.rsqrt(jnp.mean(out * out, axis=-1, keepdims=True) + NORM_EPS) * fg_ref[...]
    o_ref[0] = out


def _ffn(x, gain, mods3, mrow, wup, cw, wdn, final_gain, tm):
    bsz, n, d = x.shape
    tm = min(tm, n)
    nf, tf = wdn.shape[0], wdn.shape[1]
    hb = tm // HALO
    nh = n // HALO
    final_norm = final_gain is not None
    fg = final_gain if final_norm else gain
    return pl.pallas_call(
        functools.partial(_ffn_kernel, final_norm=final_norm),
        grid=(bsz, n // tm),
        in_specs=[pl.BlockSpec((1, HALO, d), lambda b, i: (b, jnp.maximum(i * hb - 1, 0), 0)),
                  pl.BlockSpec((1, tm, d), lambda b, i: (b, i, 0)),
                  pl.BlockSpec((1, HALO, d), lambda b, i: (b, jnp.minimum((i + 1) * hb, nh - 1), 0)),
                  pl.BlockSpec((1, d), lambda b, i: (0, 0)),
                  _mod_spec(d, 3, mrow), _mod_spec(d, 4, mrow), _mod_spec(d, 5, mrow),
                  _resident(wup.shape, lambda b, i: (0, 0, 0, 0)),
                  _resident(cw.shape, lambda b, i: (0, 0, 0, 0)),
                  _resident(wdn.shape, lambda b, i: (0, 0, 0)),
                  pl.BlockSpec((1, d), lambda b, i: (0, 0))],
        out_specs=pl.BlockSpec((1, tm, d), lambda b, i: (b, i, 0)),
        out_shape=jax.ShapeDtypeStruct((bsz, n, d), F32),
        scratch_shapes=[pltpu.VMEM((tm + 2 * HALO, d), BF16)],
        compiler_params=_params(("parallel", "parallel")),
        name="conv_ffn",
    )(x, x, x, gain.reshape(1, d), mods3, mods3, mods3, wup, cw, wdn, fg.reshape(1, d))


OD_Q, OD_C, OD_K, OD_V, OD_END = 0, 512, 2048, 2176, 2304
SUB = 8


def _prep_odd_kernel(zp_ref, z_ref, zn_ref, cw_ref, chd_ref, shd_ref,
                     q_ref, k_ref, v_ref, x0_ref, u_ref, *, scale):
    i = pl.program_id(1)
    last = pl.num_programs(1) - 1
    tm = z_ref.shape[1]
    cwid = (OD_K - OD_C) // 3
    for c in range((OD_C - OD_Q) // LANES):
        sl = slice(OD_Q + c * LANES, OD_Q + (c + 1) * LANES)
        y = _rope_lanes(z_ref[0, :, sl], chd_ref[...], shd_ref[...])
        q_ref[0, :, sl] = (y * scale).astype(BF16)
    k = _rope_lanes(z_ref[0, :, OD_K:OD_V], chd_ref[...], shd_ref[...])
    for src, dst in ((k, k_ref), (z_ref[0, :, OD_V:OD_END], v_ref)):
        d0, d1 = _dup_heads(src)
        d0 = d0.astype(BF16)
        d1 = d1.astype(BF16)
        dst[0, :, 0 * LANES:1 * LANES] = d0
        dst[0, :, 1 * LANES:2 * LANES] = d0
        dst[0, :, 2 * LANES:3 * LANES] = d1
        dst[0, :, 3 * LANES:4 * LANES] = d1
    row = lax.broadcasted_iota(jnp.int32, (tm, cwid), 0)
    keep_p = jnp.where(i > 0, 1.0, 0.0)
    keep_n = jnp.where(i < last, 1.0, 0.0)

    def conv(part):
        sl = slice(OD_C + part * cwid, OD_C + (part + 1) * cwid)
        csl = slice(part * cwid, (part + 1) * cwid)
        u = z_ref[0, :, sl]
        up = jnp.where(row == 0, zp_ref[0, SUB - 1:SUB, sl] * keep_p, pltpu.roll(u, 1, axis=0))
        un = jnp.where(row == tm - 1, zn_ref[0, 0:1, sl] * keep_n, pltpu.roll(u, tm - 1, axis=0))
        return up * cw_ref[0:1, csl] + u * cw_ref[1:2, csl] + un * cw_ref[2:3, csl] + cw_ref[3:4, csl]

    x0_ref[0] = conv(0)
    u_ref[0] = conv(2) * conv(1)


def _prep_odd(z, cw, chd, shd, tm):
    bsz, n, cols = z.shape
    tm = min(tm, n)
    hb = tm // SUB
    nh = n // SUB
    cwid = (OD_K - OD_C) // 3
    outs = [(4 * LANES, BF16), (4 * LANES, BF16), (4 * LANES, BF16), (cwid, F32), (cwid, F32)]
    return pl.pallas_call(
        functools.partial(_prep_odd_kernel, scale=HEAD_DIM ** -0.5),
        grid=(bsz, n // tm),
        in_specs=[pl.BlockSpec((1, SUB, cols), lambda b, i: (b, jnp.maximum(i * hb - 1, 0), 0)),
                  pl.BlockSpec((1, tm, cols), lambda b, i: (b, i, 0)),
                  pl.BlockSpec((1, SUB, cols), lambda b, i: (b, jnp.minimum((i + 1) * hb, nh - 1), 0)),
                  pl.BlockSpec(cw.shape, lambda b, i: (0, 0)),
                  pl.BlockSpec((tm, LANES), lambda b, i: (i, 0)),
                  pl.BlockSpec((tm, LANES), lambda b, i: (i, 0))],
        out_specs=[pl.BlockSpec((1, tm, w), lambda b, i: (b, i, 0)) for w, _ in outs],
        out_shape=[jax.ShapeDtypeStruct((bsz, n, w), dt) for w, dt in outs],
        compiler_params=_params(("parallel", "parallel")),
        name="prep_odd",
    )(z, z, z, cw, chd, shd)


def _kv_ctx_odd_kernel(z_ref, k_ref, v_ref):
    for src, dst in ((z_ref[0, :, 0:LANES], k_ref), (z_ref[0, :, LANES:2 * LANES], v_ref)):
        d0, d1 = _dup_heads(src)
        d0 = d0.astype(BF16)
        d1 = d1.astype(BF16)
        dst[0, :, 0 * LANES:1 * LANES] = d0
        dst[0, :, 1 * LANES:2 * LANES] = d0
        dst[0, :, 2 * LANES:3 * LANES] = d1
        dst[0, :, 3 * LANES:4 * LANES] = d1


def _kv_ctx_odd(z):
    bsz, n, cols = z.shape
    return pl.pallas_call(
        _kv_ctx_odd_kernel,
        grid=(bsz,),
        in_specs=[pl.BlockSpec((1, n, cols), lambda b: (b, 0, 0))],
        out_specs=[pl.BlockSpec((1, n, 4 * LANES), lambda b: (b, 0, 0))] * 2,
        out_shape=[jax.ShapeDtypeStruct((bsz, n, 4 * LANES), BF16)] * 2,
        compiler_params=_params(("parallel",)),
        name="kv_ctx_odd",
    )(z)


def _window_kernel(q_ref, kc_ref, vc_ref, kl_ref, vl_ref, mask_ref, sink_ref, o_ref, kbc, vbc, kbl, vbl, *, lc, nblk):
    qi = pl.program_id(2)
    tq = q_ref.shape[1]
    blk = HEADS_PER_GROUP * Q_BLOCK

    @pl.when(qi == 0)
    def _build():
        kms, vms = _head_lane_masks(lc, 1)
        for j in range(HEADS_PER_GROUP):
            kbc[j * lc:(j + 1) * lc, :] = kc_ref[0] * kms[j]
            vbc[j * lc:(j + 1) * lc, :] = vc_ref[0] * vms[j]
        kms, vms = _head_lane_masks(Q_BLOCK, 1)
        zero = jnp.zeros((blk, MXU_DIM), BF16)
        kbl[0] = zero
        vbl[0] = zero
        kbl[nblk + 1] = zero
        vbl[nblk + 1] = zero

        def body(i, carry):
            r0 = pl.multiple_of(i * Q_BLOCK, Q_BLOCK)
            for j in range(HEADS_PER_GROUP):
                kbl[i + 1, j * Q_BLOCK:(j + 1) * Q_BLOCK, :] = kl_ref[0, pl.ds(r0, Q_BLOCK), :] * kms[j]
                vbl[i + 1, j * Q_BLOCK:(j + 1) * Q_BLOCK, :] = vl_ref[0, pl.ds(r0, Q_BLOCK), :] * vms[j]
            return carry

        lax.fori_loop(0, nblk, body, 0)

    q = q_ref[0]
    nt = (((1,), (1,)), ((), ()))
    s_c = lax.dot_general(q, kbc[...], nt, preferred_element_type=F32)
    kspan = kbl[pl.ds(qi, 3)].reshape(3 * blk, MXU_DIM)
    vspan = vbl[pl.ds(qi, 3)].reshape(3 * blk, MXU_DIM)
    s_s = lax.dot_general(q, kspan, nt, preferred_element_type=F32)
    s_s = jnp.where(mask_ref[0] > 0.0, s_s, NEG_INF)
    pcs, pss, invs = [], [[None] * HEADS_PER_GROUP for _ in range(3)], []
    for j in range(HEADS_PER_GROUP):
        sink = sink_ref[0, j][0:1, 0:1]
        segs = [s_c[:, j * lc:(j + 1) * lc]] + [s_s[:, b * blk + j * Q_BLOCK: b * blk + (j + 1) * Q_BLOCK]
                                                 for b in range(3)]
        m = jnp.maximum(sink, jnp.max(segs[0], axis=-1, keepdims=True))
        for sg in segs[1:]:
            m = jnp.maximum(m, jnp.max(sg, axis=-1, keepdims=True))
        ps = [jnp.exp(sg - m) for sg in segs]
        den = jnp.exp(sink - m)
        for p in ps:
            den = den + jnp.sum(p, axis=-1, keepdims=True)
        invs.append(1.0 / den)
        pcs.append(ps[0].astype(BF16))
        for b in range(3):
            pss[b][j] = ps[1 + b].astype(BF16)
    p_c = jnp.concatenate(pcs, axis=1)
    p_s = jnp.concatenate([p for b in range(3) for p in pss[b]], axis=1)
    o = jnp.dot(p_c, vbc[...], preferred_element_type=F32) + jnp.dot(p_s, vspan, preferred_element_type=F32)
    o_ref[0] = (o * _lanes4(invs, tq)).astype(BF16)


def _window_mask(n):
    r = jnp.arange(Q_BLOCK)[:, None]
    col = jnp.arange(3 * HEADS_PER_GROUP * Q_BLOCK)[None, :]
    b = col // (HEADS_PER_GROUP * Q_BLOCK)
    jj = b * Q_BLOCK + col % Q_BLOCK
    band = (jj >= r) & (jj <= r + 2 * WINDOW)
    first = band & (b > 0)
    lastm = band & (b < 2)
    return jnp.stack([first, band, lastm]).astype(F32)


def _window(q, kc, vc, kl, vl, sink):
    bsz, n, qw = q.shape
    groups = qw // MXU_DIM
    lc = kc.shape[1]
    nblk = n // Q_BLOCK
    blk = HEADS_PER_GROUP * Q_BLOCK
    mask = _window_mask(n)
    if nblk == 1:
        mask = (mask[0] * mask[2])[None]
    sink_t = jnp.broadcast_to(sink.reshape(groups, HEADS_PER_GROUP, 1, 1).astype(F32), (groups, HEADS_PER_GROUP, SUB, LANES))

    def mask_idx(b, g, i):
        if nblk == 1:
            return (0, 0, 0)
        return (jnp.where(i == 0, 0, jnp.where(i == nblk - 1, 2, 1)), 0, 0)

    return pl.pallas_call(
        functools.partial(_window_kernel, lc=lc, nblk=nblk),
        grid=(bsz, groups, nblk),
        in_specs=[pl.BlockSpec((1, Q_BLOCK, MXU_DIM), lambda b, g, i: (b, i, g)),
                  pl.BlockSpec((1, lc, MXU_DIM), lambda b, g, i: (b, 0, g)),
                  pl.BlockSpec((1, lc, MXU_DIM), lambda b, g, i: (b, 0, g)),
                  pl.BlockSpec((1, n, MXU_DIM), lambda b, g, i: (b, 0, g)),
                  pl.BlockSpec((1, n, MXU_DIM), lambda b, g, i: (b, 0, g)),
                  pl.BlockSpec((1, Q_BLOCK, 3 * blk), mask_idx),
                  pl.BlockSpec((1, HEADS_PER_GROUP, SUB, LANES), lambda b, g, i: (g, 0, 0, 0))],
        out_specs=pl.BlockSpec((1, Q_BLOCK, MXU_DIM), lambda b, g, i: (b, i, g)),
        out_shape=jax.ShapeDtypeStruct((bsz, n, qw), BF16),
        scratch_shapes=[pltpu.VMEM((HEADS_PER_GROUP * lc, MXU_DIM), BF16),
                        pltpu.VMEM((HEADS_PER_GROUP * lc, MXU_DIM), BF16),
                        pltpu.VMEM((nblk + 2, blk, MXU_DIM), BF16),
                        pltpu.VMEM((nblk + 2, blk, MXU_DIM), BF16)],
        compiler_params=_params(("parallel", "parallel", "arbitrary")),
        name="window_attn",
    )(q, kc, vc, kl, vl, mask, sink_t)


FFT_N2 = 128


def _filter_kernel(z_ref, w1_ref, b1_ref, w2_ref, b2_ref, w3_ref, b3_ref, w4_ref, fr_ref, dl_ref, o_ref):
    def lin(a, w_ref):
        a_hi, a_lo = _split_bf16(a)
        w_hi, w_lo = _split_bf16(w_ref[...])
        return _dot3(a_hi, a_lo, w_hi, w_lo)

    z = z_ref[...]
    h = b1_ref[...]
    for e in range(C_EMB_DIM):
        h = h + z[:, e:e + 1] * w1_ref[e:e + 1, :]
    h = jnp.sin(fr_ref[0:1] * h)
    h = jnp.sin(fr_ref[1:2] * (lin(h, w2_ref) + b2_ref[...]))
    h = jnp.sin(fr_ref[2:3] * (lin(h, w3_ref) + b3_ref[...]))
    h = lin(h, w4_ref)
    t = z[:, 0:1]
    cw = dl_ref.shape[1]
    dec = jnp.exp(-t * dl_ref[...])
    row = lax.broadcasted_iota(jnp.int32, dec.shape, 0) + pl.program_id(0) * z.shape[0]
    o_ref[0] = h[:, :cw] * dec
    o_ref[1] = jnp.where(row == 0, 0.0, h[:, cw:] * dec)


def _filters(zfeat, w1, b1, w2, b2, w3, b3, w4, freq, deltas, tm):
    n = zfeat.shape[0]
    tm = min(tm, n)
    cw = deltas.shape[1]
    full = lambda a: pl.BlockSpec(a.shape, lambda i: (0,) * a.ndim)
    ops = (w1, b1, w2, b2, w3, b3, w4, freq, deltas)
    return pl.pallas_call(
        _filter_kernel,
        grid=(n // tm,),
        in_specs=[pl.BlockSpec((tm, zfeat.shape[1]), lambda i: (i, 0))] + [full(a) for a in ops],
        out_specs=pl.BlockSpec((2, tm, cw), lambda i: (0, i, 0)),
        out_shape=jax.ShapeDtypeStruct((2, n, cw), F32),
        compiler_params=_params(("parallel",)),
        name="hyena_filters",
    )(zfeat, *ops)


def _dft_rows_kernel(fh_ref, fl_ref, x_ref, o_ref):
    x_hi, x_lo = _split_bf16(x_ref[0])
    o_ref[0] = _dot3(fh_ref[...], fl_ref[...], x_hi, x_lo)


def _dft_rows(f_hi, f_lo, x, tn):
    bsz, kdim, cols = x.shape
    m = f_hi.shape[0]
    tn = min(tn, cols)
    return pl.pallas_call(
        _dft_rows_kernel,
        grid=(bsz, cols // tn),
        in_specs=[pl.BlockSpec((m, kdim), lambda b, j: (0, 0)),
                  pl.BlockSpec((m, kdim), lambda b, j: (0, 0)),
                  pl.BlockSpec((1, kdim, tn), lambda b, j: (b, 0, j))],
        out_specs=pl.BlockSpec((1, m, tn), lambda b, j: (b, 0, j)),
        out_shape=jax.ShapeDtypeStruct((bsz, m, cols), F32),
        compiler_params=_params(("parallel", "parallel")),
        name="dft_rows",
    )(f_hi, f_lo, x)


def _spec_fwd(a_ref, mh_ref, ml_ref):
    a = a_ref[0, :, 0]
    a = a.reshape(2 * FFT_N2, a.shape[-1])
    a_hi, a_lo = _split_bf16(a)
    return _dot3(mh_ref[0], ml_ref[0], a_hi, a_lo)


def _filter_spec_kernel(a_ref, mh_ref, ml_ref, o_ref):
    x = _spec_fwd(a_ref, mh_ref, ml_ref)
    o_ref[0, :, 0] = x.reshape(2, FFT_N2, x.shape[-1])


def _conv_spec_kernel(a_ref, kf_ref, kb_ref, mh_ref, ml_ref, ih_ref, il_ref, o_ref):
    x = _spec_fwd(a_ref, mh_ref, ml_ref)
    xr, xi = x[:FFT_N2], x[FFT_N2:]
    kr = kf_ref[0, 0, 0] + kb_ref[0, 0, 0]
    ki = kf_ref[0, 1, 0] - kb_ref[0, 1, 0]
    y = jnp.concatenate([xr * kr - xi * ki, xr * ki + xi * kr], axis=0)
    y_hi, y_lo = _split_bf16(y)
    bv = _dot3(ih_ref[0], il_ref[0], y_hi, y_lo)
    o_ref[0, :, 0] = bv.reshape(2, FFT_N2, bv.shape[-1])


def _spec_specs(n1, cw):
    blk = pl.BlockSpec((1, 2, 1, FFT_N2, cw), lambda k, b: (b, 0, k, 0, 0))
    mat = pl.BlockSpec((1, 2 * FFT_N2, 2 * FFT_N2), lambda k, b: (k, 0, 0))
    return blk, mat


def _filter_spec(a, m_hi, m_lo):
    bsz, _, n1, _, cw = a.shape
    blk, mat = _spec_specs(n1, cw)
    return pl.pallas_call(
        _filter_spec_kernel,
        grid=(n1, bsz),
        in_specs=[blk, mat, mat],
        out_specs=blk,
        out_shape=jax.ShapeDtypeStruct(a.shape, F32),
        compiler_params=_params(("parallel", "parallel")),
        name="filter_spectrum",
    )(a, m_hi, m_lo)


def _conv_spec(a, kspec, m_hi, m_lo, i_hi, i_lo):
    bsz, _, n1, _, cw = a.shape
    blk, mat = _spec_specs(n1, cw)
    kf = pl.BlockSpec((1, 2, 1, FFT_N2, cw), lambda k, b: (0, 0, k, 0, 0))
    kb = pl.BlockSpec((1, 2, 1, FFT_N2, cw), lambda k, b: (1, 0, k, 0, 0))
    return pl.pallas_call(
        _conv_spec_kernel,
        grid=(n1, bsz),
        in_specs=[blk, kf, kb, mat, mat, mat, mat],
        out_specs=blk,
        out_shape=jax.ShapeDtypeStruct(a.shape, F32),
        compiler_params=_params(("parallel", "parallel")),
        name="conv_spectrum",
    )(a, kspec, kspec, m_hi, m_lo, i_hi, i_lo)


def _hyena_gate_kernel(y_ref, x0_ref, u_ref, b_ref, o_ref):
    o_ref[0] = (x0_ref[0] * (y_ref[0] + u_ref[0] * b_ref[...])).astype(BF16)


def _hyena_gate(y, x0, u, bias, tm):
    bsz, n, cw = y.shape
    tm = min(tm, n)
    blk = pl.BlockSpec((1, tm, cw), lambda b, i: (b, i, 0))
    return pl.pallas_call(
        _hyena_gate_kernel,
        grid=(bsz, n // tm),
        in_specs=[blk, blk, blk, pl.BlockSpec((1, cw), lambda b, i: (0, 0))],
        out_specs=blk,
        out_shape=jax.ShapeDtypeStruct((bsz, n, cw), BF16),
        compiler_params=_params(("parallel", "parallel")),
        name="hyena_gate",
    )(y, x0, u, bias.reshape(1, cw))


def _dft_tables(n):
    nfft = 2 * n
    n2 = FFT_N2
    n1 = nfft // n2
    n1h = n1 // 2
    two_pi = 2.0 * math.pi

    def cs(num, den):
        ang = (num % den).astype(F32) * (two_pi / den)
        return jnp.cos(ang), jnp.sin(ang)

    k1 = jnp.arange(n1, dtype=jnp.int32)
    c1, s1 = cs(k1[:, None] * jnp.arange(n1h, dtype=jnp.int32)[None, :], n1)
    f1 = jnp.concatenate([c1, -s1], axis=0)
    kk = k1[:, None, None] + n1 * jnp.arange(n2, dtype=jnp.int32)[None, :, None]
    cg, sg = cs(kk * jnp.arange(n2, dtype=jnp.int32)[None, None, :], nfft)
    g_re, g_im = cg, -sg
    m_fwd = jnp.concatenate([jnp.concatenate([g_re, -g_im], axis=2),
                             jnp.concatenate([g_im, g_re], axis=2)], axis=1)
    gt_re, gt_im = jnp.swapaxes(g_re, 1, 2), jnp.swapaxes(g_im, 1, 2)
    m_inv = jnp.concatenate([jnp.concatenate([gt_re, gt_im], axis=2),
                             jnp.concatenate([-gt_im, gt_re], axis=2)], axis=1)
    c2, s2 = cs(jnp.arange(n1h, dtype=jnp.int32)[:, None] * k1[None, :], n1)
    f2 = jnp.concatenate([c2, -s2], axis=1) * (1.0 / nfft)
    return tuple(_split_bf16(t) for t in (f1, m_fwd, m_inv, f2))


def _long_conv(u, hcat, tabs):
    (f1h, f1l), (mfh, mfl), (mih, mil), (f2h, f2l) = tabs
    bsz, n, cw = u.shape
    n2 = FFT_N2
    n1 = 2 * n // n2
    n1h = n1 // 2
    tn = 8192
    a_f = _dft_rows(f1h, f1l, hcat.reshape(2, n1h, n2 * cw), tn).reshape(2, 2, n1, n2, cw)
    kspec = _filter_spec(a_f, mfh, mfl)
    a_u = _dft_rows(f1h, f1l, u.reshape(bsz, n1h, n2 * cw), tn).reshape(bsz, 2, n1, n2, cw)
    b_u = _conv_spec(a_u, kspec, mfh, mfl, mih, mil)
    y = _dft_rows(f2h, f2l, b_u.reshape(bsz, 2 * n1, n2 * cw), tn)
    return y.reshape(bsz, n, cw)


def _axial_angles(rows, rope_dim):
    row_idx = jnp.repeat(jnp.arange(rows), GRID_W).astype(F32)
    col_idx = jnp.tile(jnp.arange(GRID_W), rows).astype(F32)
    d_axis = rope_dim // 2
    inv_freq = ROPE_THETA ** (-jnp.arange(0, d_axis, 2, dtype=F32) / d_axis)
    ang = jnp.concatenate([row_idx[:, None] * inv_freq, col_idx[:, None] * inv_freq], axis=-1)
    return jnp.cos(ang), jnp.sin(ang)


def _rope_tables(n):
    rows = n // GRID_W
    sign = jnp.tile(jnp.array([-1.0, 1.0], F32), LANES // 2)[None, :]
    cos, sin = _axial_angles(rows, HEAD_DIM)
    chd = jnp.tile(jnp.repeat(cos, 2, axis=1), (1, LANES // HEAD_DIM))
    shd = jnp.tile(jnp.repeat(sin, 2, axis=1), (1, LANES // HEAD_DIM)) * sign
    cos, sin = _axial_angles(rows, B_ROPE)
    ones = jnp.ones((n, LANES - 2 * B_ROPE), F32)
    cb = jnp.concatenate([jnp.tile(jnp.repeat(cos, 2, axis=1), (1, 2)), ones], axis=1)
    sb = jnp.concatenate([jnp.tile(jnp.repeat(sin, 2, axis=1), (1, 2)), 0.0 * ones], axis=1) * sign
    return chd, shd, cb, sb


def _even_weights(w_in, a_qn, a_kn, b_qn, b_w_uq, b_kvn, b_w_ukv):
    d = w_in.shape[0]
    aq_w = w_in.shape[1] - (B_Q_RANK + 2 * A_KV_HEADS * HEAD_DIM + B_KV_RANK + B_ROPE)
    o = [0, aq_w, aq_w + B_Q_RANK]
    o += [o[-1] + A_KV_HEADS * HEAD_DIM, o[-1] + 2 * A_KV_HEADS * HEAD_DIM]
    o += [o[-1] + B_KV_RANK, o[-1] + B_KV_RANK + B_ROPE]
    kr = w_in[:, o[5]:o[6]]
    w_aug = jnp.concatenate([w_in[:, :o[5]], kr, kr, jnp.zeros((d, LANES - 2 * B_ROPE), w_in.dtype)], axis=1)
    heads = b_w_uq.shape[1] // (B_NOPE + B_ROPE)
    uq = b_w_uq.reshape(B_Q_RANK, heads // 2, 2, B_NOPE + B_ROPE)
    wuq = jnp.concatenate([uq[:, :, 0, :B_NOPE], uq[:, :, 1, :B_NOPE], uq[:, :, 0, B_NOPE:], uq[:, :, 1, B_NOPE:],
                           jnp.zeros((B_Q_RANK, heads // 2, MXU_DIM - 2 * (B_NOPE + B_ROPE)), b_w_uq.dtype)], axis=2)
    wuq = wuq.reshape(B_Q_RANK, heads // 2 * MXU_DIM)
    ukv = b_w_ukv.reshape(B_KV_RANK, heads, B_NOPE + B_VDIM)
    wukn = ukv[:, :, :B_NOPE].reshape(B_KV_RANK, heads * B_NOPE)
    wuv = ukv[:, :, B_NOPE:].reshape(B_KV_RANK, heads * B_VDIM)
    gains = (jnp.tile(a_qn, aq_w // HEAD_DIM)[None, :], jnp.tile(a_kn, A_KV_HEADS)[None, :],
             b_qn[None, :], b_kvn[None, :])
    return w_aug.astype(BF16), gains + (wuq.astype(BF16), wukn.astype(BF16), wuv.astype(BF16))


def _ffn_weights(w_up, conv_w, conv_b, w_down, tf):
    d, f2 = w_up.shape
    f = f2 // 2
    nf = f // tf
    wup = w_up.reshape(d, 2, nf, tf).transpose(1, 2, 0, 3).astype(BF16)
    cw = jnp.concatenate([conv_w, conv_b[None, :], jnp.zeros((SUB - 4, f2), conv_w.dtype)], axis=0)
    cw = cw.reshape(SUB, 2, nf, tf).transpose(1, 2, 0, 3)
    wdn = w_down.reshape(nf, tf, d).astype(BF16)
    return wup, cw, wdn


FFN_TF = 256


def kernel(x, c, ctx, c_ctx, w_mod, b_mod, norm_mix, norm_ffn, ev_w_in, ev_w_out, a_q_norm, a_k_norm, b_q_norm, b_w_uq, b_kv_norm, b_w_ukv, od_w_in, od_w_out, d_sink, c_conv_w, c_conv_b, c_filt_w1, c_filt_b1, c_filt_w2, c_filt_b2, c_filt_w3, c_filt_b3, c_filt_w4, c_filt_freq, c_bias, ffn_w_up, ffn_conv_w, ffn_conv_b, ffn_w_down, final_norm):
    bsz, n, d = x.shape
    depth = w_mod.shape[0]
    assert depth == 2 and n % Q_BLOCK == 0 and d % LANES == 0
    tm = 512
    rows = -(-(bsz + 1) // SUB) * SUB
    cvec = jnp.concatenate([c, c_ctx[None, :], jnp.zeros((rows - bsz - 1, d), F32)], axis=0)
    mods = _mods(cvec, w_mod, b_mod)
    chd, shd, cb, sb = _rope_tables(n)
    tabs = (chd, shd, cb, sb)

    m3 = mods[0].reshape(rows, 1, N_MOD * d)
    w_aug, prep_w = _even_weights(ev_w_in[0], a_q_norm[0], a_k_norm[0], b_q_norm[0], b_w_uq[0],
                                  b_kv_norm[0], b_w_ukv[0])
    z_l = _inproj(x, norm_mix[0], m3, None, w_aug, tm)
    z_c = _inproj(ctx, norm_mix[0], m3, bsz, w_aug, tm)
    qa_l, ka_l, va_l, qb_l, kb_l, vb_l = _prep_even(z_l, prep_w, tabs, True, tm)
    qa_c, ka_c, va_c, qb_c, kb_c, vb_c = _prep_even(z_c, prep_w, tabs, False, tm)
    tq, tk = 512, 256
    oa_l = _flash(qa_l, (ka_c, ka_l), (va_c, va_l), 1, tq, tk)
    ob_l = _flash(qb_l, (kb_c, kb_l), (vb_c, vb_l), 2, tq, tk)
    oa_c = _flash(qa_c, (ka_c,), (va_c,), 1, tq, tk)
    ob_c = _flash(qb_c, (kb_c,), (vb_c,), 2, tq, tk)
    w_out = ev_w_out[0].astype(BF16)
    x = _outproj(oa_l, ob_l, x, m3, None, w_out, tm)
    ctx = _outproj(oa_c, ob_c, ctx, m3, bsz, w_out, tm)
    ffn_w = _ffn_weights(ffn_w_up[0], ffn_conv_w[0], ffn_conv_b[0], ffn_w_down[0], FFN_TF)
    x = _ffn(x, norm_ffn[0], m3, None, *ffn_w, None, tm)
    ctx = _ffn(ctx, norm_ffn[0], m3, bsz, *ffn_w, None, tm)

    m3 = mods[1].reshape(rows, 1, N_MOD * d)
    w_in = od_w_in[0].astype(BF16)
    z_l = _inproj(x, norm_mix[1], m3, None, w_in, tm)
    z_c = _inproj(ctx, norm_mix[1], m3, bsz, w_in[:, OD_K:], tm)
    cw = jnp.concatenate([c_conv_w[0], c_conv_b[0][None, :], jnp.zeros((SUB - 4, OD_K - OD_C), F32)], axis=0)
    qd, kd, vd, x0, u = _prep_odd(z_l, cw, chd, shd, tm)
    kd_c, vd_c = _kv_ctx_odd(z_c)
    od = _window(qd, kd_c, vd_c, kd, vd, d_sink[0])
    t = jnp.linspace(0.0, 1.0, n, dtype=F32)[:, None]
    wpos = 2 * math.pi * jnp.arange(n, dtype=F32)[:, None] / n
    fb = jnp.linspace(1e-4, C_BANDS - 1, C_BANDS, dtype=F32)[None, :]
    zfeat = jnp.concatenate([t, jnp.cos(fb * wpos), -jnp.sin(fb * wpos), jnp.zeros((n, SUB - C_EMB_DIM), F32)], axis=-1)
    cwid = c_bias.shape[1]
    deltas = jnp.abs(jnp.linspace(C_MIN_DECAY, C_MAX_DECAY, cwid, dtype=F32))[None, :]
    w1 = jnp.concatenate([c_filt_w1[0], jnp.zeros((SUB - C_EMB_DIM, c_filt_w1.shape[2]), F32)], axis=0)
    hcat = _filters(zfeat, w1, c_filt_b1[0][None, :], c_filt_w2[0], c_filt_b2[0][None, :], c_filt_w3[0],
                    c_filt_b3[0][None, :], c_filt_w4[0], jnp.concatenate([c_filt_freq[0], jnp.zeros((SUB - 3, c_filt_freq.shape[2]), F32)], axis=0),
                    deltas, tm)
    y = _long_conv(u, hcat, _dft_tables(n))
    oc = _hyena_gate(y, x0, u, c_bias[0], tm)
    x = _outproj(od, oc, x, m3, None, od_w_out[0].astype(BF16), tm)
    ffn_w = _ffn_weights(ffn_w_up[1], ffn_conv_w[1], ffn_conv_b[1], ffn_w_down[1], FFN_TF)
    return _ffn(x, norm_ffn[1], m3, None, *ffn_w, final_norm, tm)
```

```python
import functools
import math

import jax
import jax.numpy as jnp
from jax import lax
from jax.experimental import pallas as pl
from jax.experimental.pallas import tpu as pltpu

F32 = jnp.float32
BF16 = jnp.bfloat16

GRID_W = 64
HEAD_DIM = 64
ROPE_THETA = 10000.0
NORM_EPS = 1e-6
NEG_INF = -1e30
N_MOD = 6
A_KV_HEADS = 2
B_NOPE = 64
B_ROPE = 32
B_VDIM = 64
B_Q_RANK = 384
B_KV_RANK = 256
C_EMB_DIM = 5
C_BANDS = (C_EMB_DIM - 1) // 2
C_MIN_DECAY = math.log(1e-2) / 1.5
C_MAX_DECAY = math.log(1e-2) / 0.3
WINDOW = 128
Q_BLOCK = 128
LOG2E = math.log2(math.e)

LANES = 128
MXU_DIM = 256
VMEM_LIMIT = 56 * 1024 * 1024
HEADS_PER_GROUP = MXU_DIM // HEAD_DIM


def _params(sem, vmem=VMEM_LIMIT):
    return pltpu.CompilerParams(dimension_semantics=sem, vmem_limit_bytes=vmem)


def _resident(shape, index_map):
    return pl.BlockSpec(shape, index_map, pipeline_mode=pl.Buffered(1))


def _norm_mod(x, gain, shift, scale):
    inv = lax.rsqrt(jnp.mean(x * x, axis=-1, keepdims=True) + NORM_EPS)
    return (x * inv) * gain * (1.0 + scale) + shift


def _rope_lanes(x, cos, sin_signed):
    lane = lax.broadcasted_iota(jnp.int32, x.shape, 1)
    nxt = pltpu.roll(x, LANES - 1, axis=1)
    prv = pltpu.roll(x, 1, axis=1)
    swapped = jnp.where(lane % 2 == 0, nxt, prv)
    return x * cos + swapped * sin_signed


def _head_rmsnorm_lanes(x, gain):
    lane = lax.broadcasted_iota(jnp.int32, x.shape, 1)
    lo = lane < HEAD_DIM
    sq = x * x
    s_lo = jnp.sum(jnp.where(lo, sq, 0.0), axis=-1, keepdims=True)
    s_hi = jnp.sum(jnp.where(lo, 0.0, sq), axis=-1, keepdims=True)
    ms = jnp.where(lo, s_lo, s_hi) * (1.0 / HEAD_DIM)
    return x * lax.rsqrt(ms + NORM_EPS) * gain


def _dup_heads(x):
    lane = lax.broadcasted_iota(jnp.int32, x.shape, 1)
    lo = lane < HEAD_DIM
    r = pltpu.roll(x, HEAD_DIM, axis=1)
    return jnp.where(lo, x, r), jnp.where(lo, r, x)


def _lanes4(vals, rows):
    lane = lax.broadcasted_iota(jnp.int32, (rows, MXU_DIM), 1)
    return jnp.where(lane < HEAD_DIM, vals[0],
                     jnp.where(lane < 2 * HEAD_DIM, vals[1],
                               jnp.where(lane < 3 * HEAD_DIM, vals[2], vals[3])))


def _split_bf16(x):
    hi = x.astype(BF16)
    lo = (x - hi.astype(F32)).astype(BF16)
    return hi, lo


def _dot3(a_hi, a_lo, b_hi, b_lo):
    d = functools.partial(jnp.dot, preferred_element_type=F32)
    return d(a_hi, b_hi) + d(a_hi, b_lo) + d(a_lo, b_hi)


def _mods_kernel(c_ref, w_ref, b_ref, o_ref):
    c = c_ref[...]
    s = c * (1.0 / (1.0 + jnp.exp(-c)))
    s_hi, s_lo = _split_bf16(s)
    w_hi, w_lo = _split_bf16(w_ref[0])
    o_ref[0] = _dot3(s_hi, s_lo, w_hi, w_lo) + b_ref[0]


def _mods(cvec, w_mod, b_mod):
    depth, d, n = w_mod.shape
    rows = cvec.shape[0]
    tn = 1536
    return pl.pallas_call(
        _mods_kernel,
        grid=(depth, n // tn),
        in_specs=[pl.BlockSpec((rows, d), lambda l, j: (0, 0)),
                  pl.BlockSpec((1, d, tn), lambda l, j: (l, 0, j)),
                  pl.BlockSpec((1, 1, tn), lambda l, j: (l, 0, j))],
        out_specs=pl.BlockSpec((1, rows, tn), lambda l, j: (l, 0, j)),
        out_shape=jax.ShapeDtypeStruct((depth, rows, n), F32),
        compiler_params=_params(("arbitrary", "arbitrary")),
        name="mods",
    )(cvec, w_mod, b_mod.reshape(depth, 1, n))


def _mod_spec(d, chunk, row):
    if row is None:
        return pl.BlockSpec((1, 1, d), lambda b, *_: (b, 0, chunk))
    return pl.BlockSpec((1, 1, d), lambda b, *_: (row, 0, chunk))


def _inproj_kernel(x_ref, g_ref, sh_ref, sc_ref, w_ref, o_ref):
    h = _norm_mod(x_ref[0], g_ref[...], sh_ref[0], sc_ref[0])
    o_ref[0] = jnp.dot(h.astype(BF16), w_ref[...], preferred_element_type=F32)


def _inproj(x, gain, mods3, mrow, w, tm):
    bsz, n, d = x.shape
    cols = w.shape[1]
    tm = min(tm, n)
    return pl.pallas_call(
        _inproj_kernel,
        grid=(bsz, n // tm),
        in_specs=[pl.BlockSpec((1, tm, d), lambda b, i: (b, i, 0)),
                  pl.BlockSpec((1, d), lambda b, i: (0, 0)),
                  _mod_spec(d, 0, mrow), _mod_spec(d, 1, mrow),
                  _resident((d, cols), lambda b, i: (0, 0))],
        out_specs=pl.BlockSpec((1, tm, cols), lambda b, i: (b, i, 0)),
        out_shape=jax.ShapeDtypeStruct((bsz, n, cols), F32),
        compiler_params=_params(("parallel", "parallel")),
        name="inproj",
    )(x, gain.reshape(1, d), mods3, mods3, w)


EV_AQ, EV_BQ, EV_AK, EV_AV, EV_BKV, EV_BKR, EV_END = 0, 512, 896, 1024, 1152, 1408, 1536


def _prep_even_kernel(z_ref, aqg_ref, akg_ref, bqg_ref, bkvg_ref, wuq_ref, wukn_ref, wuv_ref,
                      chd_ref, shd_ref, cb_ref, sb_ref,
                      qa_ref, ka_ref, va_ref, qb_ref, kb_ref, vb_ref, *, use_rope, a_scale, b_scale):
    def rope_hd(y):
        return _rope_lanes(y, chd_ref[...], shd_ref[...]) if use_rope else y

    def rope_b(y):
        return _rope_lanes(y, cb_ref[...], sb_ref[...]) if use_rope else y

    def put_t(dst, c, y):
        dst[0, c * LANES:(c + 1) * LANES, :] = y.T.astype(BF16)

    for c in range((EV_BQ - EV_AQ) // LANES):
        sl = slice(EV_AQ + c * LANES, EV_AQ + (c + 1) * LANES)
        y = rope_hd(_head_rmsnorm_lanes(z_ref[0, :, sl], aqg_ref[:, c * LANES:(c + 1) * LANES]))
        put_t(qa_ref, c, y * a_scale)
    k = rope_hd(_head_rmsnorm_lanes(z_ref[0, :, EV_AK:EV_AV], akg_ref[...]))
    d0, d1 = _dup_heads(k)
    d0 = d0.astype(BF16)
    d1 = d1.astype(BF16)
    ka_ref[0, :, 0 * LANES:1 * LANES] = d0
    ka_ref[0, :, 1 * LANES:2 * LANES] = d0
    ka_ref[0, :, 2 * LANES:3 * LANES] = d1
    ka_ref[0, :, 3 * LANES:4 * LANES] = d1
    d0, d1 = _dup_heads(z_ref[0, :, EV_AV:EV_BKV])
    for c, dv in enumerate((d0, d0, d1, d1)):
        put_t(va_ref, c, dv)
    cq = z_ref[0, :, EV_BQ:EV_AK]
    cq = cq * lax.rsqrt(jnp.mean(cq * cq, axis=-1, keepdims=True) + NORM_EPS) * bqg_ref[...]
    qb = jnp.dot(cq.astype(BF16), wuq_ref[...], preferred_element_type=F32)
    for c in range(qb.shape[1] // MXU_DIM):
        lo = slice(c * MXU_DIM, c * MXU_DIM + LANES)
        hi = slice(c * MXU_DIM + LANES, (c + 1) * MXU_DIM)
        put_t(qb_ref, 2 * c, qb[:, lo] * b_scale)
        put_t(qb_ref, 2 * c + 1, rope_b(qb[:, hi]) * b_scale)
    ckv = z_ref[0, :, EV_BKV:EV_BKR]
    ckv = (ckv * lax.rsqrt(jnp.mean(ckv * ckv, axis=-1, keepdims=True) + NORM_EPS) * bkvg_ref[...]).astype(BF16)
    kn = jnp.dot(ckv, wukn_ref[...], preferred_element_type=F32)
    vb = jnp.dot(ckv, wuv_ref[...], preferred_element_type=F32)
    for c in range(vb.shape[1] // LANES):
        put_t(vb_ref, c, vb[:, c * LANES:(c + 1) * LANES])
    kr = rope_b(z_ref[0, :, EV_BKR:EV_END]).astype(BF16)
    for c in range(kn.shape[1] // LANES):
        kb_ref[0, :, c * MXU_DIM:c * MXU_DIM + LANES] = kn[:, c * LANES:(c + 1) * LANES].astype(BF16)
        kb_ref[0, :, c * MXU_DIM + LANES:(c + 1) * MXU_DIM] = kr


def _prep_even(z, wts, tabs, use_rope, tm):
    bsz, n, cols = z.shape
    tm = min(tm, n)
    aqg, akg, bqg, bkvg, wuq, wukn, wuv = wts
    chd, shd, cb, sb = tabs
    full = lambda a: _resident(a.shape, lambda b, i: (0,) * a.ndim)
    tab = lambda a: pl.BlockSpec((tm, LANES), (lambda b, i: (i, 0)) if use_rope else (lambda b, i: (0, 0)))
    outs = [(4 * LANES, True), (4 * LANES, False), (4 * LANES, True), (wuq.shape[1], True),
            (2 * wukn.shape[1], False), (wuv.shape[1], True)]
    kern = functools.partial(_prep_even_kernel, use_rope=use_rope, a_scale=HEAD_DIM ** -0.5 * LOG2E,
                             b_scale=(B_NOPE + B_ROPE) ** -0.5 * LOG2E)
    return pl.pallas_call(
        kern,
        grid=(bsz, n // tm),
        in_specs=[pl.BlockSpec((1, tm, cols), lambda b, i: (b, i, 0)),
                  full(aqg), full(akg), full(bqg), full(bkvg), full(wuq), full(wukn), full(wuv),
                  tab(chd), tab(shd), tab(cb), tab(sb)],
        out_specs=[pl.BlockSpec((1, w, tm), lambda b, i: (b, 0, i)) if t else
                   pl.BlockSpec((1, tm, w), lambda b, i: (b, i, 0)) for w, t in outs],
        out_shape=[jax.ShapeDtypeStruct((bsz, w, n) if t else (bsz, n, w), BF16) for w, t in outs],
        compiler_params=_params(("parallel", "parallel")),
        name="prep_even",
    )(z, aqg, akg, bqg, bkvg, wuq, wukn, wuv, chd, shd, cb, sb)


def _head_lane_masks(rows, nsub):
    lane = lax.broadcasted_iota(jnp.int32, (rows, MXU_DIM), 1)
    kms, vms = [], []
    for j in range(HEADS_PER_GROUP):
        v_lo = j * HEAD_DIM
        vm = (lane >= v_lo) & (lane < v_lo + HEAD_DIM)
        if nsub == 1:
            km = vm
        else:
            jj = j % 2
            km = ((lane >= jj * B_NOPE) & (lane < (jj + 1) * B_NOPE)) | \
                 ((lane >= 2 * B_NOPE + jj * B_ROPE) & (lane < 2 * B_NOPE + (jj + 1) * B_ROPE))
        kms.append(jnp.where(km, 1.0, 0.0).astype(BF16))
        vms.append(jnp.where(vm, 1.0, 0.0).astype(BF16))
    return kms, vms


KV_UNROLL = 8


def _head_row_masks(cols):
    row = lax.broadcasted_iota(jnp.int32, (MXU_DIM, cols), 0)
    return [jnp.where((row >= j * HEAD_DIM) & (row < (j + 1) * HEAD_DIM), 1.0, 0.0).astype(BF16)
            for j in range(HEADS_PER_GROUP)]


def _rows4(vals, cols):
    return jnp.concatenate([jnp.broadcast_to(v, (HEAD_DIM, cols)) for v in vals], axis=0)


def _flash_kernel(*refs, nsub, tk, nsrc, nchain):
    qt_ref = refs[0]
    srcs = [(refs[1 + 2 * s], refs[2 + 2 * s]) for s in range(nsrc)]
    o_ref, kb_ref, vbt_ref, acc_ref = refs[1 + 2 * nsrc:]
    hpu = HEADS_PER_GROUP // nsub
    tq = qt_ref.shape[2]
    tqc = tq // nchain
    nb = kb_ref.shape[0]

    @pl.when(pl.program_id(2) == 0)
    def _build():
        kms, _ = _head_lane_masks(tk, nsub)
        vms = _head_row_masks(tk)
        base = 0
        for k_ref, vt_ref in srcs:
            nblk = k_ref.shape[1] // tk

            def body(i, carry, k_ref=k_ref, base=base):
                r0 = pl.multiple_of(i * tk, tk)
                for j in range(HEADS_PER_GROUP):
                    u = j // hpu
                    kb_ref[base + i, j * tk:(j + 1) * tk, :] = \
                        k_ref[0, pl.ds(r0, tk), u * MXU_DIM:(u + 1) * MXU_DIM] * kms[j]
                return carry

            lax.fori_loop(0, nblk, body, 0)
            for i in range(nblk):
                for j in range(HEADS_PER_GROUP):
                    vbt_ref[base + i, :, j * tk:(j + 1) * tk] = vt_ref[0, :, i * tk:(i + 1) * tk] * vms[j]
            base += nblk

    def block(i, state):
        kb, vbt = kb_ref[i], vbt_ref[i]
        new_state = []
        for h in range(nchain):
            cs = slice(h * tqc, (h + 1) * tqc)
            m, l = state[h]
            parts = [jnp.dot(kb[u * hpu * tk:(u + 1) * hpu * tk, :], qt_ref[0, u * MXU_DIM:(u + 1) * MXU_DIM, cs],
                             preferred_element_type=F32) for u in range(nsub)]
            ps, alphas, m_new, l_new = [], [], [], []
            for j in range(HEADS_PER_GROUP):
                sj = parts[j // hpu][(j % hpu) * tk:(j % hpu + 1) * tk, :]
                mj = jnp.maximum(m[j], jnp.max(sj, axis=0, keepdims=True))
                a = jnp.exp2(m[j] - mj)
                p = jnp.exp2(sj - mj)
                l_new.append(a * l[j] + jnp.sum(p, axis=0, keepdims=True))
                m_new.append(mj)
                alphas.append(a)
                ps.append(p.astype(BF16))
            pv = jnp.dot(vbt, jnp.concatenate(ps, axis=0), preferred_element_type=F32)
            acc_ref[:, cs] = acc_ref[:, cs] * _rows4(alphas, tqc) + pv
            new_state.append((m_new, l_new))
        return new_state

    def flat(st):
        return tuple(v for m, l in st for v in (*m, *l))

    def unflat(c):
        n = 2 * HEADS_PER_GROUP
        return [(list(c[h * n:h * n + HEADS_PER_GROUP]), list(c[h * n + HEADS_PER_GROUP:(h + 1) * n]))
                for h in range(nchain)]

    m0 = [jnp.full((1, tqc), NEG_INF, F32)] * HEADS_PER_GROUP
    l0 = [jnp.zeros((1, tqc), F32)] * HEADS_PER_GROUP
    acc_ref[...] = jnp.zeros(acc_ref.shape, F32)
    state = unflat(lax.fori_loop(0, nb, lambda i, c: flat(block(i, unflat(c))), flat([(m0, l0)] * nchain),
                                 unroll=min(KV_UNROLL, nb)))
    for h in range(nchain):
        cs = slice(h * tqc, (h + 1) * tqc)
        ot = acc_ref[:, cs] * _rows4([1.0 / v for v in state[h][1]], tqc)
        o_ref[0, cs, :] = ot.T.astype(BF16)


def _flash(qt, ksrcs, vtsrcs, nsub, tq, tk):
    bsz, qw, nq = qt.shape
    kw = nsub * MXU_DIM
    groups = qw // kw
    tq = min(tq, nq)
    nchain = 2 if tq >= 2 * MXU_DIM else 1
    in_specs = [pl.BlockSpec((1, kw, tq), lambda b, g, i: (b, g, i))]
    args = [qt]
    nb = 0
    for k, vt in zip(ksrcs, vtsrcs):
        lk = k.shape[1]
        assert lk % tk == 0
        nb += lk // tk
        in_specs += [pl.BlockSpec((1, lk, kw), lambda b, g, i: (b, 0, g)),
                     pl.BlockSpec((1, MXU_DIM, lk), lambda b, g, i: (b, g, 0))]
        args += [k, vt]
    scratch = [pltpu.VMEM((nb, HEADS_PER_GROUP * tk, MXU_DIM), BF16),
               pltpu.VMEM((nb, MXU_DIM, HEADS_PER_GROUP * tk), BF16),
               pltpu.VMEM((MXU_DIM, tq), F32)]
    return pl.pallas_call(
        functools.partial(_flash_kernel, nsub=nsub, tk=tk, nsrc=len(ksrcs), nchain=nchain),
        grid=(bsz, groups, nq // tq),
        in_specs=in_specs,
        out_specs=pl.BlockSpec((1, tq, MXU_DIM), lambda b, g, i: (b, i, g)),
        out_shape=jax.ShapeDtypeStruct((bsz, nq, groups * MXU_DIM), BF16),
        scratch_shapes=scratch,
        compiler_params=_params(("parallel", "parallel", "arbitrary")),
        name="flash_attn",
    )(*args)


def _outproj_kernel(oa_ref, ob_ref, x_ref, gt_ref, wa_ref, wb_ref, o_ref):
    y = jnp.dot(oa_ref[0], wa_ref[...], preferred_element_type=F32)
    y = y + jnp.dot(ob_ref[0], wb_ref[...], preferred_element_type=F32)
    o_ref[0] = x_ref[0] + gt_ref[0] * y


def _outproj(oa, ob, x, mods3, mrow, w_out, tm):
    bsz, n, d = x.shape
    half = oa.shape[2]
    tm = min(tm, n)
    return pl.pallas_call(
        _outproj_kernel,
        grid=(bsz, n // tm),
        in_specs=[pl.BlockSpec((1, tm, half), lambda b, i: (b, i, 0)),
                  pl.BlockSpec((1, tm, half), lambda b, i: (b, i, 0)),
                  pl.BlockSpec((1, tm, d), lambda b, i: (b, i, 0)),
                  _mod_spec(d, 2, mrow),
                  _resident((half, d), lambda b, i: (0, 0)),
                  _resident((half, d), lambda b, i: (1, 0))],
        out_specs=pl.BlockSpec((1, tm, d), lambda b, i: (b, i, 0)),
        out_shape=jax.ShapeDtypeStruct((bsz, n, d), F32),
        compiler_params=_params(("parallel", "parallel")),
        name="outproj",
    )(oa, ob, x, mods3, w_out, w_out)


HALO = 16


def _ffn_kernel(xp_ref, x_ref, xn_ref, g_ref, sh_ref, sc_ref, gt_ref, wup_ref, cw_ref, wdn_ref, fg_ref,
                o_ref, h_ref, *, final_norm):
    i = pl.program_id(1)
    last = pl.num_programs(1) - 1
    tm = x_ref.shape[1]
    nf = wdn_ref.shape[0]
    gain, shift, scale = g_ref[...], sh_ref[0], sc_ref[0]
    keep_p = jnp.where(i > 0, 1.0, 0.0)
    keep_n = jnp.where(i < last, 1.0, 0.0)
    h_ref[0:HALO, :] = (_norm_mod(xp_ref[0], gain, shift, scale) * keep_p).astype(BF16)
    h_ref[HALO:HALO + tm, :] = _norm_mod(x_ref[0], gain, shift, scale).astype(BF16)
    h_ref[HALO + tm:, :] = (_norm_mod(xn_ref[0], gain, shift, scale) * keep_n).astype(BF16)
    rows = tm + 2 * HALO

    def conv(u, f, part):
        w = cw_ref[part, f]
        up = pltpu.roll(u, 1, axis=0)[HALO:HALO + tm]
        un = pltpu.roll(u, rows - 1, axis=0)[HALO:HALO + tm]
        return up * w[0:1] + u[HALO:HALO + tm] * w[1:2] + un * w[2:3] + w[3:4]

    h = h_ref[...]
    up = lambda f: (jnp.dot(h, wup_ref[0, f], preferred_element_type=F32),
                    jnp.dot(h, wup_ref[1, f], preferred_element_type=F32))
    y = None
    ug, uv = up(0)
    for f in range(nf):
        nxt = up(f + 1) if f + 1 < nf else None
        g = conv(ug, f, 0)
        v = conv(uv, f, 1)
        a = (g * (1.0 / (1.0 + jnp.exp(-g))) * v).astype(BF16)
        yf = jnp.dot(a, wdn_ref[f], preferred_element_type=F32)
        y = yf if y is None else y + yf
        if nxt is not None:
            ug, uv = nxt
    out = x_ref[0] + gt_ref[0] * y
    if final_norm:
        out = out * lax.rsqrt(jnp.mean(out * out, axis=-1, keepdims=True) + NORM_EPS) * fg_ref[...]
    o_ref[0] = out


def _ffn(x, gain, mods3, mrow, wup, cw, wdn, final_gain, tm):
    bsz, n, d = x.shape
    tm = min(tm, n)
    nf, tf = wdn.shape[0], wdn.shape[1]
    hb = tm // HALO
    nh = n // HALO
    final_norm = final_gain is not None
    fg = final_gain if final_norm else gain
    return pl.pallas_call(
        functools.partial(_ffn_kernel, final_norm=final_norm),
        grid=(bsz, n // tm),
        in_specs=[pl.BlockSpec((1, HALO, d), lambda b, i: (b, jnp.maximum(i * hb - 1, 0), 0)),
                  pl.BlockSpec((1, tm, d), lambda b, i: (b, i, 0)),
                  pl.BlockSpec((1, HALO, d), lambda b, i: (b, jnp.minimum((i + 1) * hb, nh - 1), 0)),
                  pl.BlockSpec((1, d), lambda b, i: (0, 0)),
                  _mod_spec(d, 3, mrow), _mod_spec(d, 4, mrow), _mod_spec(d, 5, mrow),
                  _resident(wup.shape, lambda b, i: (0, 0, 0, 0)),
                  _resident(cw.shape, lambda b, i: (0, 0, 0, 0)),
                  _resident(wdn.shape, lambda b, i: (0, 0, 0)),
                  pl.BlockSpec((1, d), lambda b, i: (0, 0))],
        out_specs=pl.BlockSpec((1, tm, d), lambda b, i: (b, i, 0)),
        out_shape=jax.ShapeDtypeStruct((bsz, n, d), F32),
        scratch_shapes=[pltpu.VMEM((tm + 2 * HALO, d), BF16)],
        compiler_params=_params(("parallel", "parallel")),
        name="conv_ffn",
    )(x, x, x, gain.reshape(1, d), mods3, mods3, mods3, wup, cw, wdn, fg.reshape(1, d))


OD_Q, OD_C, OD_K, OD_V, OD_END = 0, 512, 2048, 2176, 2304
SUB = 8


def _prep_odd_kernel(zp_ref, z_ref, zn_ref, cw_ref, chd_ref, shd_ref,
                     q_ref, k_ref, v_ref, x0_ref, u_ref, *, scale):
    i = pl.program_id(1)
    last = pl.num_programs(1) - 1
    tm = z_ref.shape[1]
    cwid = (OD_K - OD_C) // 3
    for c in range((OD_C - OD_Q) // LANES):
        sl = slice(OD_Q + c * LANES, OD_Q + (c + 1) * LANES)
        y = _rope_lanes(z_ref[0, :, sl], chd_ref[...], shd_ref[...])
        q_ref[0, sl, :] = (y * scale).T.astype(BF16)
    k = _rope_lanes(z_ref[0, :, OD_K:OD_V], chd_ref[...], shd_ref[...])
    d0, d1 = _dup_heads(k)
    for c, dk in enumerate((d0, d0, d1, d1)):
        k_ref[0, :, c * LANES:(c + 1) * LANES] = dk.astype(BF16)
    d0, d1 = _dup_heads(z_ref[0, :, OD_V:OD_END])
    for c, dv in enumerate((d0, d0, d1, d1)):
        v_ref[0, c * LANES:(c + 1) * LANES, :] = dv.T.astype(BF16)
    row = lax.broadcasted_iota(jnp.int32, (tm, cwid), 0)
    keep_p = jnp.where(i > 0, 1.0, 0.0)
    keep_n = jnp.where(i < last, 1.0, 0.0)

    def conv(part):
        sl = slice(OD_C + part * cwid, OD_C + (part + 1) * cwid)
        csl = slice(part * cwid, (part + 1) * cwid)
        u = z_ref[0, :, sl]
        up = jnp.where(row == 0, zp_ref[0, SUB - 1:SUB, sl] * keep_p, pltpu.roll(u, 1, axis=0))
        un = jnp.where(row == tm - 1, zn_ref[0, 0:1, sl] * keep_n, pltpu.roll(u, tm - 1, axis=0))
        return up * cw_ref[0:1, csl] + u * cw_ref[1:2, csl] + un * cw_ref[2:3, csl] + cw_ref[3:4, csl]

    x0_ref[0] = conv(0)
    u_ref[0] = conv(2) * conv(1)


def _prep_odd(z, cw, chd, shd, tm):
    bsz, n, cols = z.shape
    tm = min(tm, n)
    hb = tm // SUB
    nh = n // SUB
    cwid = (OD_K - OD_C) // 3
    row_blk = lambda w: pl.BlockSpec((1, tm, w), lambda b, i: (b, i, 0))
    col_blk = lambda w: pl.BlockSpec((1, w, tm), lambda b, i: (b, 0, i))
    slab = 4 * LANES
    return pl.pallas_call(
        functools.partial(_prep_odd_kernel, scale=HEAD_DIM ** -0.5 * LOG2E),
        grid=(bsz, n // tm),
        in_specs=[pl.BlockSpec((1, SUB, cols), lambda b, i: (b, jnp.maximum(i * hb - 1, 0), 0)),
                  pl.BlockSpec((1, tm, cols), lambda b, i: (b, i, 0)),
                  pl.BlockSpec((1, SUB, cols), lambda b, i: (b, jnp.minimum((i + 1) * hb, nh - 1), 0)),
                  pl.BlockSpec(cw.shape, lambda b, i: (0, 0)),
                  pl.BlockSpec((tm, LANES), lambda b, i: (i, 0)),
                  pl.BlockSpec((tm, LANES), lambda b, i: (i, 0))],
        out_specs=[col_blk(slab), row_blk(slab), col_blk(slab), row_blk(cwid), row_blk(cwid)],
        out_shape=[jax.ShapeDtypeStruct((bsz, slab, n), BF16), jax.ShapeDtypeStruct((bsz, n, slab), BF16),
                   jax.ShapeDtypeStruct((bsz, slab, n), BF16), jax.ShapeDtypeStruct((bsz, n, cwid), F32),
                   jax.ShapeDtypeStruct((bsz, n, cwid), F32)],
        compiler_params=_params(("parallel", "parallel")),
        name="prep_odd",
    )(z, z, z, cw, chd, shd)


def _kv_ctx_odd_kernel(z_ref, k_ref, vt_ref):
    d0, d1 = _dup_heads(z_ref[0, :, 0:LANES])
    for c, dk in enumerate((d0, d0, d1, d1)):
        k_ref[0, :, c * LANES:(c + 1) * LANES] = dk.astype(BF16)
    d0, d1 = _dup_heads(z_ref[0, :, LANES:2 * LANES])
    for c, dv in enumerate((d0, d0, d1, d1)):
        vt_ref[0, c * LANES:(c + 1) * LANES, :] = dv.T.astype(BF16)


def _kv_ctx_odd(z):
    bsz, n, cols = z.shape
    slab = 4 * LANES
    return pl.pallas_call(
        _kv_ctx_odd_kernel,
        grid=(bsz,),
        in_specs=[pl.BlockSpec((1, n, cols), lambda b: (b, 0, 0))],
        out_specs=[pl.BlockSpec((1, n, slab), lambda b: (b, 0, 0)), pl.BlockSpec((1, slab, n), lambda b: (b, 0, 0))],
        out_shape=[jax.ShapeDtypeStruct((bsz, n, slab), BF16), jax.ShapeDtypeStruct((bsz, slab, n), BF16)],
        compiler_params=_params(("parallel",)),
        name="kv_ctx_odd",
    )(z)


WIN_QB = 2


def _window_kernel(qt_ref, kc_ref, vct_ref, kl_ref, vlt_ref, bias_ref, sink_ref, o_ref, kbc, vbct, kbl, vblt,
                   *, lc, nblk, qb, nch):
    qi = pl.program_id(2)
    tq = qb * Q_BLOCK
    nsp = qb + 2
    blk = HEADS_PER_GROUP * Q_BLOCK

    @pl.when(qi == 0)
    def _build():
        kms, _ = _head_lane_masks(lc, 1)
        vms = _head_row_masks(lc)
        for j in range(HEADS_PER_GROUP):
            kbc[j * lc:(j + 1) * lc, :] = kc_ref[0] * kms[j]
            vbct[:, j * lc:(j + 1) * lc] = vct_ref[0] * vms[j]
        kms, _ = _head_lane_masks(Q_BLOCK, 1)
        vms = _head_row_masks(Q_BLOCK)
        for e in (0, nblk + 1):
            kbl[e] = jnp.zeros((blk, MXU_DIM), BF16)
            vblt[e] = jnp.zeros((MXU_DIM, blk), BF16)

        def body(i, carry):
            r0 = pl.multiple_of(i * Q_BLOCK, Q_BLOCK)
            for j in range(HEADS_PER_GROUP):
                kbl[i + 1, j * Q_BLOCK:(j + 1) * Q_BLOCK, :] = kl_ref[0, pl.ds(r0, Q_BLOCK), :] * kms[j]
            return carry

        lax.fori_loop(0, nblk, body, 0)
        for i in range(nblk):
            for j in range(HEADS_PER_GROUP):
                vblt[i + 1, :, j * Q_BLOCK:(j + 1) * Q_BLOCK] = vlt_ref[0, :, i * Q_BLOCK:(i + 1) * Q_BLOCK] * vms[j]

    ntile = nblk // qb
    for h in range(nch):
        t = qi * nch + h
        qt = qt_ref[0, :, h * tq:(h + 1) * tq]
        s_c = jnp.dot(kbc[...], qt, preferred_element_type=F32)
        kspan = kbl[pl.ds(qb * t, nsp)].reshape(nsp * blk, MXU_DIM)
        s_s = jnp.dot(kspan, qt, preferred_element_type=F32) + bias_ref[...]
        edge = {0: jnp.where(t == 0, NEG_INF, 0.0), nsp - 1: jnp.where(t == ntile - 1, NEG_INF, 0.0)}
        pcs, pss, invs = [], [[None] * HEADS_PER_GROUP for _ in range(nsp)], []
        for j in range(HEADS_PER_GROUP):
            sink = sink_ref[0, j][0:1, :]
            segs = [s_c[j * lc:(j + 1) * lc, :]]
            for b in range(nsp):
                sg = s_s[b * blk + j * Q_BLOCK: b * blk + (j + 1) * Q_BLOCK, :]
                segs.append(sg + edge[b] if b in edge else sg)
            m = sink
            for sg in segs:
                m = jnp.maximum(m, jnp.max(sg, axis=0, keepdims=True))
            ps = [jnp.exp2(sg - m) for sg in segs]
            den = jnp.exp2(sink - m)
            for p in ps:
                den = den + jnp.sum(p, axis=0, keepdims=True)
            invs.append(1.0 / den)
            pcs.append(ps[0].astype(BF16))
            for b in range(nsp):
                pss[b][j] = ps[1 + b].astype(BF16)
        ot = jnp.dot(vbct[...], jnp.concatenate(pcs, axis=0), preferred_element_type=F32)
        for b in range(nsp):
            ot = ot + jnp.dot(vblt[qb * t + b], jnp.concatenate(pss[b], axis=0), preferred_element_type=F32)
        o_ref[0, h * tq:(h + 1) * tq, :] = (ot * _rows4(invs, tq)).T.astype(BF16)


def _window_bias(qb):
    nsp = qb + 2
    row = jnp.arange(nsp * HEADS_PER_GROUP * Q_BLOCK)[:, None]
    r = jnp.arange(qb * Q_BLOCK)[None, :]
    b = row // (HEADS_PER_GROUP * Q_BLOCK)
    jj = b * Q_BLOCK + row % Q_BLOCK
    band = (jj >= r) & (jj <= r + 2 * WINDOW)
    return jnp.where(band, 0.0, NEG_INF).astype(F32)


def _window(qt, kc, vct, kl, vlt, sink):
    bsz, qw, n = qt.shape
    groups = qw // MXU_DIM
    lc = kc.shape[1]
    nblk = n // Q_BLOCK
    qb = min(WIN_QB, nblk)
    tq = qb * Q_BLOCK
    nch = 2 if n % (2 * tq) == 0 else 1
    nq = n // (tq * nch)
    blk = HEADS_PER_GROUP * Q_BLOCK
    bias = _window_bias(qb)
    sink_t = jnp.broadcast_to((sink.astype(F32) * LOG2E).reshape(groups, HEADS_PER_GROUP, 1, 1),
                              (groups, HEADS_PER_GROUP, SUB, tq))

    return pl.pallas_call(
        functools.partial(_window_kernel, lc=lc, nblk=nblk, qb=qb, nch=nch),
        grid=(bsz, groups, nq),
        in_specs=[pl.BlockSpec((1, MXU_DIM, tq * nch), lambda b, g, i: (b, g, i)),
                  pl.BlockSpec((1, lc, MXU_DIM), lambda b, g, i: (b, 0, g)),
                  pl.BlockSpec((1, MXU_DIM, lc), lambda b, g, i: (b, g, 0)),
                  pl.BlockSpec((1, n, MXU_DIM), lambda b, g, i: (b, 0, g)),
                  pl.BlockSpec((1, MXU_DIM, n), lambda b, g, i: (b, g, 0)),
                  _resident(bias.shape, lambda b, g, i: (0, 0)),
                  pl.BlockSpec((1, HEADS_PER_GROUP, SUB, tq), lambda b, g, i: (g, 0, 0, 0))],
        out_specs=pl.BlockSpec((1, tq * nch, MXU_DIM), lambda b, g, i: (b, i, g)),
        out_shape=jax.ShapeDtypeStruct((bsz, n, qw), BF16),
        scratch_shapes=[pltpu.VMEM((HEADS_PER_GROUP * lc, MXU_DIM), BF16),
                        pltpu.VMEM((MXU_DIM, HEADS_PER_GROUP * lc), BF16),
                        pltpu.VMEM((nblk + 2, blk, MXU_DIM), BF16),
                        pltpu.VMEM((nblk + 2, MXU_DIM, blk), BF16)],
        compiler_params=_params(("parallel", "parallel", "arbitrary")),
        name="window_attn",
    )(qt, kc, vct, kl, vlt, bias, sink_t)


FFT_N2 = 128


def _filter_kernel(z_ref, w1_ref, b1_ref, w2_ref, b2_ref, w3_ref, b3_ref, w4_ref, fr_ref, dl_ref, o_ref):
    def lin(a, w_ref):
        a_hi, a_lo = _split_bf16(a)
        w_hi, w_lo = _split_bf16(w_ref[...])
        return _dot3(a_hi, a_lo, w_hi, w_lo)

    z = z_ref[...]
    h = b1_ref[...]
    for e in range(C_EMB_DIM):
        h = h + z[:, e:e + 1] * w1_ref[e:e + 1, :]
    h = jnp.sin(fr_ref[0:1] * h)
    h = jnp.sin(fr_ref[1:2] * (lin(h, w2_ref) + b2_ref[...]))
    h = jnp.sin(fr_ref[2:3] * (lin(h, w3_ref) + b3_ref[...]))
    h = lin(h, w4_ref)
    t = z[:, 0:1]
    cw = dl_ref.shape[1]
    dec = jnp.exp(-t * dl_ref[...])
    row = lax.broadcasted_iota(jnp.int32, dec.shape, 0) + pl.program_id(0) * z.shape[0]
    o_ref[0] = h[:, :cw] * dec
    o_ref[1] = jnp.where(row == 0, 0.0, h[:, cw:] * dec)


def _filters(zfeat, w1, b1, w2, b2, w3, b3, w4, freq, deltas, tm):
    n = zfeat.shape[0]
    tm = min(tm, n)
    cw = deltas.shape[1]
    full = lambda a: pl.BlockSpec(a.shape, lambda i: (0,) * a.ndim)
    ops = (w1, b1, w2, b2, w3, b3, w4, freq, deltas)
    return pl.pallas_call(
        _filter_kernel,
        grid=(n // tm,),
        in_specs=[pl.BlockSpec((tm, zfeat.shape[1]), lambda i: (i, 0))] + [full(a) for a in ops],
        out_specs=pl.BlockSpec((2, tm, cw), lambda i: (0, i, 0)),
        out_shape=jax.ShapeDtypeStruct((2, n, cw), F32),
        compiler_params=_params(("parallel",)),
        name="hyena_filters",
    )(zfeat, *ops)


def _dft_first_kernel(fh_ref, fl_ref, x_ref, o_ref):
    _, n1h, tm2, cw = x_ref.shape
    x_hi, x_lo = _split_bf16(x_ref[0].reshape(n1h * tm2, cw))
    r = _dot3(fh_ref[...], fl_ref[...], x_hi, x_lo)
    o_ref[0] = r.reshape(o_ref.shape[1:])


def _dft_first(f_hi, f_lo, x):
    bsz, n1h, n2, cw = x.shape
    tm2 = SUB
    nkp = f_hi.shape[0] // (2 * tm2)
    return pl.pallas_call(
        _dft_first_kernel,
        grid=(bsz, n2 // tm2),
        in_specs=[pl.BlockSpec(f_hi.shape, lambda b, j: (0, 0)),
                  pl.BlockSpec(f_lo.shape, lambda b, j: (0, 0)),
                  pl.BlockSpec((1, n1h, tm2, cw), lambda b, j: (b, 0, j, 0))],
        out_specs=pl.BlockSpec((1, 2, nkp, tm2, cw), lambda b, j: (b, 0, 0, j, 0)),
        out_shape=jax.ShapeDtypeStruct((bsz, 2, nkp, n2, cw), F32),
        compiler_params=_params(("parallel", "parallel")),
        name="dft_first",
    )(f_hi, f_lo, x)


def _dft_last_gate_kernel(fh_ref, fl_ref, b_ref, x0_ref, u_ref, bias_ref, o_ref, y_ref):
    _, _, nkp, tm2, cw = b_ref.shape
    n1h = y_ref.shape[0]
    for s in range(tm2 // SUB):
        bm = b_ref[0, :, :, s * SUB:(s + 1) * SUB, :].reshape(2 * nkp * SUB, cw)
        b_hi, b_lo = _split_bf16(bm)
        y = _dot3(fh_ref[...], fl_ref[...], b_hi, b_lo)
        y_ref[:, s * SUB:(s + 1) * SUB, :] = y.reshape(n1h, SUB, cw)
    o_ref[0] = (x0_ref[0] * (y_ref[...] + u_ref[0] * bias_ref[...])).astype(BF16)


def _dft_last_gate(f_hi, f_lo, b, x0, u, bias):
    bsz, _, nkp, n2, cw = b.shape
    n1h = f_hi.shape[0] // SUB
    tm2 = HALO
    tblk = pl.BlockSpec((1, n1h, tm2, cw), lambda b_, j: (b_, 0, j, 0))
    return pl.pallas_call(
        _dft_last_gate_kernel,
        grid=(bsz, n2 // tm2),
        in_specs=[pl.BlockSpec(f_hi.shape, lambda b_, j: (0, 0)),
                  pl.BlockSpec(f_lo.shape, lambda b_, j: (0, 0)),
                  pl.BlockSpec((1, 2, nkp, tm2, cw), lambda b_, j: (b_, 0, 0, j, 0)),
                  tblk, tblk, pl.BlockSpec((1, cw), lambda b_, j: (0, 0))],
        out_specs=tblk,
        out_shape=jax.ShapeDtypeStruct((bsz, n1h, n2, cw), BF16),
        scratch_shapes=[pltpu.VMEM((n1h, tm2, cw), F32)],
        compiler_params=_params(("parallel", "parallel")),
        name="dft_last_gate",
    )(f_hi, f_lo, b, x0, u, bias.reshape(1, cw))


def _spec_fwd(a_ref, mh_ref, ml_ref):
    a = a_ref[0, :, 0]
    a = a.reshape(2 * FFT_N2, a.shape[-1])
    a_hi, a_lo = _split_bf16(a)
    return _dot3(mh_ref[0], ml_ref[0], a_hi, a_lo)


def _filter_spec_kernel(a_ref, mh_ref, ml_ref, o_ref):
    x = _spec_fwd(a_ref, mh_ref, ml_ref)
    o_ref[0, :, 0] = x.reshape(2, FFT_N2, x.shape[-1])


def _conv_spec_kernel(a_ref, kf_ref, kb_ref, mh_ref, ml_ref, ih_ref, il_ref, o_ref):
    x = _spec_fwd(a_ref, mh_ref, ml_ref)
    xr, xi = x[:FFT_N2], x[FFT_N2:]
    kr = kf_ref[0, 0, 0] + kb_ref[0, 0, 0]
    ki = kf_ref[0, 1, 0] - kb_ref[0, 1, 0]
    y = jnp.concatenate([xr * kr - xi * ki, xr * ki + xi * kr], axis=0)
    y_hi, y_lo = _split_bf16(y)
    bv = _dot3(ih_ref[0], il_ref[0], y_hi, y_lo)
    o_ref[0, :, 0] = bv.reshape(2, FFT_N2, bv.shape[-1])


def _spec_specs(n1, cw):
    blk = pl.BlockSpec((1, 2, 1, FFT_N2, cw), lambda k, b: (b, 0, k, 0, 0))
    mat = pl.BlockSpec((1, 2 * FFT_N2, 2 * FFT_N2), lambda k, b: (k, 0, 0))
    return blk, mat


def _filter_spec(a, m_hi, m_lo):
    bsz, _, n1, _, cw = a.shape
    blk, mat = _spec_specs(n1, cw)
    return pl.pallas_call(
        _filter_spec_kernel,
        grid=(n1, bsz),
        in_specs=[blk, mat, mat],
        out_specs=blk,
        out_shape=jax.ShapeDtypeStruct(a.shape, F32),
        compiler_params=_params(("parallel", "parallel")),
        name="filter_spectrum",
    )(a, m_hi, m_lo)


def _conv_spec(a, kspec, m_hi, m_lo, i_hi, i_lo):
    bsz, _, n1, _, cw = a.shape
    blk, mat = _spec_specs(n1, cw)
    kf = pl.BlockSpec((1, 2, 1, FFT_N2, cw), lambda k, b: (0, 0, k, 0, 0))
    kb = pl.BlockSpec((1, 2, 1, FFT_N2, cw), lambda k, b: (1, 0, k, 0, 0))
    return pl.pallas_call(
        _conv_spec_kernel,
        grid=(n1, bsz),
        in_specs=[blk, kf, kb, mat, mat, mat, mat],
        out_specs=blk,
        out_shape=jax.ShapeDtypeStruct(a.shape, F32),
        compiler_params=_params(("parallel", "parallel")),
        name="conv_spectrum",
    )(a, kspec, kspec, m_hi, m_lo, i_hi, i_lo)


def _dft_tables(n):
    nfft = 2 * n
    n2 = FFT_N2
    n1 = nfft // n2
    n1h = n1 // 2
    nk = n1h + 1
    nkp = -(-nk // SUB) * SUB
    two_pi = 2.0 * math.pi

    def cs(num, den):
        ang = (num % den).astype(F32) * (two_pi / den)
        return jnp.cos(ang), jnp.sin(ang)

    def pad_k(t, axis):
        widths = [(0, 0)] * t.ndim
        widths[axis] = (0, nkp - nk)
        return jnp.pad(t, widths)

    k1 = jnp.arange(nk, dtype=jnp.int32)
    c1, s1 = cs(k1[:, None] * jnp.arange(n1h, dtype=jnp.int32)[None, :], n1)
    f1 = jnp.concatenate([pad_k(c1, 0), pad_k(-s1, 0)], axis=0)
    kk = k1[:, None, None] + n1 * jnp.arange(n2, dtype=jnp.int32)[None, :, None]
    cg, sg = cs(kk * jnp.arange(n2, dtype=jnp.int32)[None, None, :], nfft)
    g_re, g_im = cg, -sg
    m_fwd = jnp.concatenate([jnp.concatenate([g_re, -g_im], axis=2),
                             jnp.concatenate([g_im, g_re], axis=2)], axis=1)
    gt_re, gt_im = jnp.swapaxes(g_re, 1, 2), jnp.swapaxes(g_im, 1, 2)
    m_inv = jnp.concatenate([jnp.concatenate([gt_re, gt_im], axis=2),
                             jnp.concatenate([-gt_im, gt_re], axis=2)], axis=1)
    wk = jnp.where((k1 == 0) | (k1 == n1h), 1.0, 2.0)[None, :] * (1.0 / nfft)
    c2, s2 = cs(jnp.arange(n1h, dtype=jnp.int32)[:, None] * k1[None, :], n1)
    f2 = jnp.concatenate([pad_k(c2 * wk, 1), pad_k(-s2 * wk, 1)], axis=1)
    eye = jnp.eye(SUB, dtype=F32)
    return tuple(_split_bf16(t) for t in (jnp.kron(f1, eye), pad_k(m_fwd, 0), pad_k(m_inv, 0), jnp.kron(f2, eye)))


def _hyena_long_conv(u, x0, bias, hcat, tabs):
    (f1h, f1l), (mfh, mfl), (mih, mil), (f2h, f2l) = tabs
    bsz, n, cw = u.shape
    n2 = FFT_N2
    n1h = n // n2
    kspec = _filter_spec(_dft_first(f1h, f1l, hcat.reshape(2, n1h, n2, cw)), mfh, mfl)
    u4 = u.reshape(bsz, n1h, n2, cw)
    b_u = _conv_spec(_dft_first(f1h, f1l, u4), kspec, mfh, mfl, mih, mil)
    return _dft_last_gate(f2h, f2l, b_u, x0.reshape(bsz, n1h, n2, cw), u4, bias).reshape(bsz, n, cw)


def _axial_angles(rows, rope_dim):
    row_idx = jnp.repeat(jnp.arange(rows), GRID_W).astype(F32)
    col_idx = jnp.tile(jnp.arange(GRID_W), rows).astype(F32)
    d_axis = rope_dim // 2
    inv_freq = ROPE_THETA ** (-jnp.arange(0, d_axis, 2, dtype=F32) / d_axis)
    ang = jnp.concatenate([row_idx[:, None] * inv_freq, col_idx[:, None] * inv_freq], axis=-1)
    return jnp.cos(ang), jnp.sin(ang)


def _rope_tables(n):
    rows = n // GRID_W
    sign = jnp.tile(jnp.array([-1.0, 1.0], F32), LANES // 2)[None, :]
    cos, sin = _axial_angles(rows, HEAD_DIM)
    chd = jnp.tile(jnp.repeat(cos, 2, axis=1), (1, LANES // HEAD_DIM))
    shd = jnp.tile(jnp.repeat(sin, 2, axis=1), (1, LANES // HEAD_DIM)) * sign
    cos, sin = _axial_angles(rows, B_ROPE)
    ones = jnp.ones((n, LANES - 2 * B_ROPE), F32)
    cb = jnp.concatenate([jnp.tile(jnp.repeat(cos, 2, axis=1), (1, 2)), ones], axis=1)
    sb = jnp.concatenate([jnp.tile(jnp.repeat(sin, 2, axis=1), (1, 2)), 0.0 * ones], axis=1) * sign
    return chd, shd, cb, sb


def _even_weights(w_in, a_qn, a_kn, b_qn, b_w_uq, b_kvn, b_w_ukv):
    d = w_in.shape[0]
    aq_w = w_in.shape[1] - (B_Q_RANK + 2 * A_KV_HEADS * HEAD_DIM + B_KV_RANK + B_ROPE)
    o = [0, aq_w, aq_w + B_Q_RANK]
    o += [o[-1] + A_KV_HEADS * HEAD_DIM, o[-1] + 2 * A_KV_HEADS * HEAD_DIM]
    o += [o[-1] + B_KV_RANK, o[-1] + B_KV_RANK + B_ROPE]
    kr = w_in[:, o[5]:o[6]]
    w_aug = jnp.concatenate([w_in[:, :o[5]], kr, kr, jnp.zeros((d, LANES - 2 * B_ROPE), w_in.dtype)], axis=1)
    heads = b_w_uq.shape[1] // (B_NOPE + B_ROPE)
    uq = b_w_uq.reshape(B_Q_RANK, heads // 2, 2, B_NOPE + B_ROPE)
    wuq = jnp.concatenate([uq[:, :, 0, :B_NOPE], uq[:, :, 1, :B_NOPE], uq[:, :, 0, B_NOPE:], uq[:, :, 1, B_NOPE:],
                           jnp.zeros((B_Q_RANK, heads // 2, MXU_DIM - 2 * (B_NOPE + B_ROPE)), b_w_uq.dtype)], axis=2)
    wuq = wuq.reshape(B_Q_RANK, heads // 2 * MXU_DIM)
    ukv = b_w_ukv.reshape(B_KV_RANK, heads, B_NOPE + B_VDIM)
    wukn = ukv[:, :, :B_NOPE].reshape(B_KV_RANK, heads * B_NOPE)
    wuv = ukv[:, :, B_NOPE:].reshape(B_KV_RANK, heads * B_VDIM)
    gains = (jnp.tile(a_qn, aq_w // HEAD_DIM)[None, :], jnp.tile(a_kn, A_KV_HEADS)[None, :],
             b_qn[None, :], b_kvn[None, :])
    return w_aug.astype(BF16), gains + (wuq.astype(BF16), wukn.astype(BF16), wuv.astype(BF16))


def _ffn_weights(w_up, conv_w, conv_b, w_down, tf):
    d, f2 = w_up.shape
    f = f2 // 2
    nf = f // tf
    wup = w_up.reshape(d, 2, nf, tf).transpose(1, 2, 0, 3).astype(BF16)
    cw = jnp.concatenate([conv_w, conv_b[None, :], jnp.zeros((SUB - 4, f2), conv_w.dtype)], axis=0)
    cw = cw.reshape(SUB, 2, nf, tf).transpose(1, 2, 0, 3)
    wdn = w_down.reshape(nf, tf, d).astype(BF16)
    return wup, cw, wdn


FFN_TF = 256


def kernel(x, c, ctx, c_ctx, w_mod, b_mod, norm_mix, norm_ffn, ev_w_in, ev_w_out, a_q_norm, a_k_norm, b_q_norm, b_w_uq, b_kv_norm, b_w_ukv, od_w_in, od_w_out, d_sink, c_conv_w, c_conv_b, c_filt_w1, c_filt_b1, c_filt_w2, c_filt_b2, c_filt_w3, c_filt_b3, c_filt_w4, c_filt_freq, c_bias, ffn_w_up, ffn_conv_w, ffn_conv_b, ffn_w_down, final_norm):
    bsz, n, d = x.shape
    depth = w_mod.shape[0]
    assert depth == 2 and n % Q_BLOCK == 0 and d % LANES == 0
    tm = 512
    rows = -(-(bsz + 1) // SUB) * SUB
    cvec = jnp.concatenate([c, c_ctx[None, :], jnp.zeros((rows - bsz - 1, d), F32)], axis=0)
    mods = _mods(cvec, w_mod, b_mod)
    chd, shd, cb, sb = _rope_tables(n)
    tabs = (chd, shd, cb, sb)

    m3 = mods[0].reshape(rows, 1, N_MOD * d)
    w_aug, prep_w = _even_weights(ev_w_in[0], a_q_norm[0], a_k_norm[0], b_q_norm[0], b_w_uq[0],
                                  b_kv_norm[0], b_w_ukv[0])
    z_l = _inproj(x, norm_mix[0], m3, None, w_aug, tm)
    z_c = _inproj(ctx, norm_mix[0], m3, bsz, w_aug, tm)
    qa_l, ka_l, va_l, qb_l, kb_l, vb_l = _prep_even(z_l, prep_w, tabs, True, tm)
    qa_c, ka_c, va_c, qb_c, kb_c, vb_c = _prep_even(z_c, prep_w, tabs, False, tm)
    tq, tk = 512, 256
    oa_l = _flash(qa_l, (ka_c, ka_l), (va_c, va_l), 1, tq, tk)
    ob_l = _flash(qb_l, (kb_c, kb_l), (vb_c, vb_l), 2, tq, tk)
    oa_c = _flash(qa_c, (ka_c,), (va_c,), 1, tq, tk)
    ob_c = _flash(qb_c, (kb_c,), (vb_c,), 2, tq, tk)
    w_out = ev_w_out[0].astype(BF16)
    x = _outproj(oa_l, ob_l, x, m3, None, w_out, tm)
    ctx = _outproj(oa_c, ob_c, ctx, m3, bsz, w_out, tm)
    ffn_w = _ffn_weights(ffn_w_up[0], ffn_conv_w[0], ffn_conv_b[0], ffn_w_down[0], FFN_TF)
    x = _ffn(x, norm_ffn[0], m3, None, *ffn_w, None, tm)
    ctx = _ffn(ctx, norm_ffn[0], m3, bsz, *ffn_w, None, tm)

    m3 = mods[1].reshape(rows, 1, N_MOD * d)
    w_in = od_w_in[0].astype(BF16)
    z_l = _inproj(x, norm_mix[1], m3, None, w_in, tm)
    z_c = _inproj(ctx, norm_mix[1], m3, bsz, w_in[:, OD_K:], tm)
    cw = jnp.concatenate([c_conv_w[0], c_conv_b[0][None, :], jnp.zeros((SUB - 4, OD_K - OD_C), F32)], axis=0)
    qd, kd, vd, x0, u = _prep_odd(z_l, cw, chd, shd, tm)
    kd_c, vd_c = _kv_ctx_odd(z_c)
    od = _window(qd, kd_c, vd_c, kd, vd, d_sink[0])
    t = jnp.linspace(0.0, 1.0, n, dtype=F32)[:, None]
    wpos = 2 * math.pi * jnp.arange(n, dtype=F32)[:, None] / n
    fb = jnp.linspace(1e-4, C_BANDS - 1, C_BANDS, dtype=F32)[None, :]
    zfeat = jnp.concatenate([t, jnp.cos(fb * wpos), -jnp.sin(fb * wpos), jnp.zeros((n, SUB - C_EMB_DIM), F32)], axis=-1)
    cwid = c_bias.shape[1]
    deltas = jnp.abs(jnp.linspace(C_MIN_DECAY, C_MAX_DECAY, cwid, dtype=F32))[None, :]
    w1 = jnp.concatenate([c_filt_w1[0], jnp.zeros((SUB - C_EMB_DIM, c_filt_w1.shape[2]), F32)], axis=0)
    hcat = _filters(zfeat, w1, c_filt_b1[0][None, :], c_filt_w2[0], c_filt_b2[0][None, :], c_filt_w3[0],
                    c_filt_b3[0][None, :], c_filt_w4[0], jnp.concatenate([c_filt_freq[0], jnp.zeros((SUB - 3, c_filt_freq.shape[2]), F32)], axis=0),
                    deltas, tm)
    oc = _hyena_long_conv(u, x0, c_bias[0], hcat, _dft_tables(n))
    x = _outproj(od, oc, x, m3, None, od_w_out[0].astype(BF16), tm)
    ffn_w = _ffn_weights(ffn_w_up[1], ffn_conv_w[1], ffn_conv_b[1], ffn_w_down[1], FFN_TF)
    return _ffn(x, norm_ffn[1], m3, None, *ffn_w, final_norm, tm)
```

```python
import functools
import math

import jax
import jax.numpy as jnp
from jax import lax
from jax.experimental import pallas as pl
from jax.experimental.pallas import tpu as pltpu

F32 = jnp.float32
BF16 = jnp.bfloat16

GRID_W = 64
HEAD_DIM = 64
ROPE_THETA = 10000.0
NORM_EPS = 1e-6
NEG_INF = -1e30
N_MOD = 6
A_KV_HEADS = 2
B_NOPE = 64
B_ROPE = 32
B_VDIM = 64
B_Q_RANK = 384
B_KV_RANK = 256
C_EMB_DIM = 5
C_BANDS = (C_EMB_DIM - 1) // 2
C_MIN_DECAY = math.log(1e-2) / 1.5
C_MAX_DECAY = math.log(1e-2) / 0.3
WINDOW = 128
Q_BLOCK = 128
LOG2E = math.log2(math.e)

LANES = 128
MXU_DIM = 256
VMEM_LIMIT = 56 * 1024 * 1024
HEADS_PER_GROUP = MXU_DIM // HEAD_DIM


def _params(sem, vmem=VMEM_LIMIT):
    return pltpu.CompilerParams(dimension_semantics=sem, vmem_limit_bytes=vmem)


def _resident(shape, index_map):
    return pl.BlockSpec(shape, index_map, pipeline_mode=pl.Buffered(1))


def _norm_mod(x, gain, shift, scale):
    inv = lax.rsqrt(jnp.mean(x * x, axis=-1, keepdims=True) + NORM_EPS)
    return (x * inv) * gain * (1.0 + scale) + shift


def _rope_lanes(x, cos, sin_signed):
    lane = lax.broadcasted_iota(jnp.int32, x.shape, 1)
    nxt = pltpu.roll(x, LANES - 1, axis=1)
    prv = pltpu.roll(x, 1, axis=1)
    swapped = jnp.where(lane % 2 == 0, nxt, prv)
    return x * cos + swapped * sin_signed


def _head_rmsnorm_lanes(x, gain):
    lane = lax.broadcasted_iota(jnp.int32, x.shape, 1)
    lo = lane < HEAD_DIM
    sq = x * x
    s_lo = jnp.sum(jnp.where(lo, sq, 0.0), axis=-1, keepdims=True)
    s_hi = jnp.sum(jnp.where(lo, 0.0, sq), axis=-1, keepdims=True)
    ms = jnp.where(lo, s_lo, s_hi) * (1.0 / HEAD_DIM)
    return x * lax.rsqrt(ms + NORM_EPS) * gain


def _dup_heads(x):
    lane = lax.broadcasted_iota(jnp.int32, x.shape, 1)
    lo = lane < HEAD_DIM
    r = pltpu.roll(x, HEAD_DIM, axis=1)
    return jnp.where(lo, x, r), jnp.where(lo, r, x)


def _lanes4(vals, rows):
    lane = lax.broadcasted_iota(jnp.int32, (rows, MXU_DIM), 1)
    return jnp.where(lane < HEAD_DIM, vals[0],
                     jnp.where(lane < 2 * HEAD_DIM, vals[1],
                               jnp.where(lane < 3 * HEAD_DIM, vals[2], vals[3])))


def _split_bf16(x):
    hi = x.astype(BF16)
    lo = (x - hi.astype(F32)).astype(BF16)
    return hi, lo


def _dot3(a_hi, a_lo, b_hi, b_lo):
    d = functools.partial(jnp.dot, preferred_element_type=F32)
    return d(a_hi, b_hi) + d(a_hi, b_lo) + d(a_lo, b_hi)


def _mods_kernel(c_ref, w_ref, b_ref, o_ref):
    c = c_ref[...]
    s = c * (1.0 / (1.0 + jnp.exp(-c)))
    s_hi, s_lo = _split_bf16(s)
    w_hi, w_lo = _split_bf16(w_ref[0])
    o_ref[0] = _dot3(s_hi, s_lo, w_hi, w_lo) + b_ref[0]


def _mods(cvec, w_mod, b_mod):
    depth, d, n = w_mod.shape
    rows = cvec.shape[0]
    tn = 1536
    return pl.pallas_call(
        _mods_kernel,
        grid=(depth, n // tn),
        in_specs=[pl.BlockSpec((rows, d), lambda l, j: (0, 0)),
                  pl.BlockSpec((1, d, tn), lambda l, j: (l, 0, j)),
                  pl.BlockSpec((1, 1, tn), lambda l, j: (l, 0, j))],
        out_specs=pl.BlockSpec((1, rows, tn), lambda l, j: (l, 0, j)),
        out_shape=jax.ShapeDtypeStruct((depth, rows, n), F32),
        compiler_params=_params(("arbitrary", "arbitrary")),
        name="mods",
    )(cvec, w_mod, b_mod.reshape(depth, 1, n))


def _mod_spec(d, chunk, row):
    if row is None:
        return pl.BlockSpec((1, 1, d), lambda b, *_: (b, 0, chunk))
    return pl.BlockSpec((1, 1, d), lambda b, *_: (row, 0, chunk))


def _inproj_kernel(x_ref, g_ref, sh_ref, sc_ref, w_ref, o_ref):
    h = _norm_mod(x_ref[0], g_ref[...], sh_ref[0], sc_ref[0])
    o_ref[0] = jnp.dot(h.astype(BF16), w_ref[...], preferred_element_type=F32)


def _inproj(x, gain, mods3, mrow, w, tm):
    bsz, n, d = x.shape
    cols = w.shape[1]
    tm = min(tm, n)
    return pl.pallas_call(
        _inproj_kernel,
        grid=(bsz, n // tm),
        in_specs=[pl.BlockSpec((1, tm, d), lambda b, i: (b, i, 0)),
                  pl.BlockSpec((1, d), lambda b, i: (0, 0)),
                  _mod_spec(d, 0, mrow), _mod_spec(d, 1, mrow),
                  _resident((d, cols), lambda b, i: (0, 0))],
        out_specs=pl.BlockSpec((1, tm, cols), lambda b, i: (b, i, 0)),
        out_shape=jax.ShapeDtypeStruct((bsz, n, cols), F32),
        compiler_params=_params(("parallel", "parallel")),
        name="inproj",
    )(x, gain.reshape(1, d), mods3, mods3, w)


EV_AQ, EV_BQ, EV_AK, EV_AV, EV_BKV, EV_BKR, EV_END = 0, 512, 896, 1024, 1152, 1408, 1536


def _prep_even_kernel(z_ref, aqg_ref, akg_ref, bqg_ref, bkvg_ref, wuq_ref, wukn_ref, wuv_ref,
                      chd_ref, shd_ref, cb_ref, sb_ref,
                      qa_ref, ka_ref, va_ref, qb_ref, kb_ref, vb_ref, *, use_rope, a_scale, b_scale):
    def rope_hd(y):
        return _rope_lanes(y, chd_ref[...], shd_ref[...]) if use_rope else y

    def rope_b(y):
        return _rope_lanes(y, cb_ref[...], sb_ref[...]) if use_rope else y

    def put_t(dst, c, y):
        dst[0, c * LANES:(c + 1) * LANES, :] = y.T.astype(BF16)

    for c in range((EV_BQ - EV_AQ) // LANES):
        sl = slice(EV_AQ + c * LANES, EV_AQ + (c + 1) * LANES)
        y = rope_hd(_head_rmsnorm_lanes(z_ref[0, :, sl], aqg_ref[:, c * LANES:(c + 1) * LANES]))
        put_t(qa_ref, c, y * a_scale)
    k = rope_hd(_head_rmsnorm_lanes(z_ref[0, :, EV_AK:EV_AV], akg_ref[...]))
    d0, d1 = _dup_heads(k)
    d0 = d0.astype(BF16)
    d1 = d1.astype(BF16)
    ka_ref[0, :, 0 * LANES:1 * LANES] = d0
    ka_ref[0, :, 1 * LANES:2 * LANES] = d0
    ka_ref[0, :, 2 * LANES:3 * LANES] = d1
    ka_ref[0, :, 3 * LANES:4 * LANES] = d1
    d0, d1 = _dup_heads(z_ref[0, :, EV_AV:EV_BKV])
    for c, dv in enumerate((d0, d0, d1, d1)):
        put_t(va_ref, c, dv)
    cq = z_ref[0, :, EV_BQ:EV_AK]
    cq = cq * lax.rsqrt(jnp.mean(cq * cq, axis=-1, keepdims=True) + NORM_EPS) * bqg_ref[...]
    qb = jnp.dot(cq.astype(BF16), wuq_ref[...], preferred_element_type=F32)
    for c in range(qb.shape[1] // MXU_DIM):
        lo = slice(c * MXU_DIM, c * MXU_DIM + LANES)
        hi = slice(c * MXU_DIM + LANES, (c + 1) * MXU_DIM)
        put_t(qb_ref, 2 * c, qb[:, lo] * b_scale)
        put_t(qb_ref, 2 * c + 1, rope_b(qb[:, hi]) * b_scale)
    ckv = z_ref[0, :, EV_BKV:EV_BKR]
    ckv = (ckv * lax.rsqrt(jnp.mean(ckv * ckv, axis=-1, keepdims=True) + NORM_EPS) * bkvg_ref[...]).astype(BF16)
    kn = jnp.dot(ckv, wukn_ref[...], preferred_element_type=F32)
    vb = jnp.dot(ckv, wuv_ref[...], preferred_element_type=F32)
    for c in range(vb.shape[1] // LANES):
        put_t(vb_ref, c, vb[:, c * LANES:(c + 1) * LANES])
    kr = rope_b(z_ref[0, :, EV_BKR:EV_END]).astype(BF16)
    for c in range(kn.shape[1] // LANES):
        kb_ref[0, :, c * MXU_DIM:c * MXU_DIM + LANES] = kn[:, c * LANES:(c + 1) * LANES].astype(BF16)
        kb_ref[0, :, c * MXU_DIM + LANES:(c + 1) * MXU_DIM] = kr


def _prep_even(z, wts, tabs, use_rope, tm):
    bsz, n, cols = z.shape
    tm = min(tm, n)
    aqg, akg, bqg, bkvg, wuq, wukn, wuv = wts
    chd, shd, cb, sb = tabs
    full = lambda a: _resident(a.shape, lambda b, i: (0,) * a.ndim)
    tab = lambda a: pl.BlockSpec((tm, LANES), (lambda b, i: (i, 0)) if use_rope else (lambda b, i: (0, 0)))
    outs = [(4 * LANES, True), (4 * LANES, False), (4 * LANES, True), (wuq.shape[1], True),
            (2 * wukn.shape[1], False), (wuv.shape[1], True)]
    kern = functools.partial(_prep_even_kernel, use_rope=use_rope, a_scale=HEAD_DIM ** -0.5 * LOG2E,
                             b_scale=(B_NOPE + B_ROPE) ** -0.5 * LOG2E)
    return pl.pallas_call(
        kern,
        grid=(bsz, n // tm),
        in_specs=[pl.BlockSpec((1, tm, cols), lambda b, i: (b, i, 0)),
                  full(aqg), full(akg), full(bqg), full(bkvg), full(wuq), full(wukn), full(wuv),
                  tab(chd), tab(shd), tab(cb), tab(sb)],
        out_specs=[pl.BlockSpec((1, w, tm), lambda b, i: (b, 0, i)) if t else
                   pl.BlockSpec((1, tm, w), lambda b, i: (b, i, 0)) for w, t in outs],
        out_shape=[jax.ShapeDtypeStruct((bsz, w, n) if t else (bsz, n, w), BF16) for w, t in outs],
        compiler_params=_params(("parallel", "parallel")),
        name="prep_even",
    )(z, aqg, akg, bqg, bkvg, wuq, wukn, wuv, chd, shd, cb, sb)


def _head_lane_masks(rows, nsub):
    lane = lax.broadcasted_iota(jnp.int32, (rows, MXU_DIM), 1)
    kms, vms = [], []
    for j in range(HEADS_PER_GROUP):
        v_lo = j * HEAD_DIM
        vm = (lane >= v_lo) & (lane < v_lo + HEAD_DIM)
        if nsub == 1:
            km = vm
        else:
            jj = j % 2
            km = ((lane >= jj * B_NOPE) & (lane < (jj + 1) * B_NOPE)) | \
                 ((lane >= 2 * B_NOPE + jj * B_ROPE) & (lane < 2 * B_NOPE + (jj + 1) * B_ROPE))
        kms.append(jnp.where(km, 1.0, 0.0).astype(BF16))
        vms.append(jnp.where(vm, 1.0, 0.0).astype(BF16))
    return kms, vms


KV_UNROLL = 17


def _head_row_masks(cols):
    row = lax.broadcasted_iota(jnp.int32, (MXU_DIM, cols), 0)
    return [jnp.where((row >= j * HEAD_DIM) & (row < (j + 1) * HEAD_DIM), 1.0, 0.0).astype(BF16)
            for j in range(HEADS_PER_GROUP)]


def _rows4(vals, cols):
    return jnp.concatenate([jnp.broadcast_to(v, (HEAD_DIM, cols)) for v in vals], axis=0)


def _flash_kernel(*refs, nsub, tk, nsrc, nchain):
    qt_ref = refs[0]
    srcs = [(refs[1 + 2 * s], refs[2 + 2 * s]) for s in range(nsrc)]
    o_ref, kb_ref, vbt_ref, acc_ref = refs[1 + 2 * nsrc:]
    hpu = HEADS_PER_GROUP // nsub
    tq = qt_ref.shape[2]
    tqc = tq // nchain
    nb = kb_ref.shape[0]

    @pl.when(pl.program_id(2) == 0)
    def _build():
        kms, _ = _head_lane_masks(tk, nsub)
        vms = _head_row_masks(tk)
        base = 0
        for k_ref, vt_ref in srcs:
            nblk = k_ref.shape[1] // tk

            def body(i, carry, k_ref=k_ref, base=base):
                r0 = pl.multiple_of(i * tk, tk)
                for j in range(HEADS_PER_GROUP):
                    u = j // hpu
                    kb_ref[base + i, j * tk:(j + 1) * tk, :] = \
                        k_ref[0, pl.ds(r0, tk), u * MXU_DIM:(u + 1) * MXU_DIM] * kms[j]
                return carry

            lax.fori_loop(0, nblk, body, 0)
            for i in range(nblk):
                for j in range(HEADS_PER_GROUP):
                    vbt_ref[base + i, :, j * tk:(j + 1) * tk] = vt_ref[0, :, i * tk:(i + 1) * tk] * vms[j]
            base += nblk

    def block(i, state):
        kb, vbt = kb_ref[i], vbt_ref[i]
        new_state = []
        for h in range(nchain):
            cs = slice(h * tqc, (h + 1) * tqc)
            m, l = state[h]
            parts = [jnp.dot(kb[u * hpu * tk:(u + 1) * hpu * tk, :], qt_ref[0, u * MXU_DIM:(u + 1) * MXU_DIM, cs],
                             preferred_element_type=F32) for u in range(nsub)]
            ps, alphas, m_new, l_new = [], [], [], []
            for j in range(HEADS_PER_GROUP):
                sj = parts[j // hpu][(j % hpu) * tk:(j % hpu + 1) * tk, :]
                mj = jnp.maximum(m[j], jnp.max(sj, axis=0, keepdims=True))
                a = jnp.exp2(m[j] - mj)
                p = jnp.exp2(sj - mj)
                l_new.append(a * l[j] + jnp.sum(p, axis=0, keepdims=True))
                m_new.append(mj)
                alphas.append(a)
                ps.append(p.astype(BF16))
            pv = jnp.dot(vbt, jnp.concatenate(ps, axis=0), preferred_element_type=F32)
            acc_ref[:, cs] = acc_ref[:, cs] * _rows4(alphas, tqc) + pv
            new_state.append((m_new, l_new))
        return new_state

    def flat(st):
        return tuple(v for m, l in st for v in (*m, *l))

    def unflat(c):
        n = 2 * HEADS_PER_GROUP
        return [(list(c[h * n:h * n + HEADS_PER_GROUP]), list(c[h * n + HEADS_PER_GROUP:(h + 1) * n]))
                for h in range(nchain)]

    m0 = [jnp.full((1, tqc), NEG_INF, F32)] * HEADS_PER_GROUP
    l0 = [jnp.zeros((1, tqc), F32)] * HEADS_PER_GROUP
    acc_ref[...] = jnp.zeros(acc_ref.shape, F32)
    state = unflat(lax.fori_loop(0, nb, lambda i, c: flat(block(i, unflat(c))), flat([(m0, l0)] * nchain),
                                 unroll=min(KV_UNROLL, nb)))
    for h in range(nchain):
        cs = slice(h * tqc, (h + 1) * tqc)
        ot = acc_ref[:, cs] * _rows4([1.0 / v for v in state[h][1]], tqc)
        o_ref[0, cs, :] = ot.T.astype(BF16)


def _flash(qt, ksrcs, vtsrcs, nsub, tq, tk):
    bsz, qw, nq = qt.shape
    kw = nsub * MXU_DIM
    groups = qw // kw
    tq = min(tq, nq)
    nchain = max(tq // MXU_DIM, 1)
    in_specs = [pl.BlockSpec((1, kw, tq), lambda b, g, i: (b, g, i))]
    args = [qt]
    nb = 0
    for k, vt in zip(ksrcs, vtsrcs):
        lk = k.shape[1]
        assert lk % tk == 0
        nb += lk // tk
        in_specs += [pl.BlockSpec((1, lk, kw), lambda b, g, i: (b, 0, g)),
                     pl.BlockSpec((1, MXU_DIM, lk), lambda b, g, i: (b, g, 0))]
        args += [k, vt]
    scratch = [pltpu.VMEM((nb, HEADS_PER_GROUP * tk, MXU_DIM), BF16),
               pltpu.VMEM((nb, MXU_DIM, HEADS_PER_GROUP * tk), BF16),
               pltpu.VMEM((MXU_DIM, tq), F32)]
    return pl.pallas_call(
        functools.partial(_flash_kernel, nsub=nsub, tk=tk, nsrc=len(ksrcs), nchain=nchain),
        grid=(bsz, groups, nq // tq),
        in_specs=in_specs,
        out_specs=pl.BlockSpec((1, tq, MXU_DIM), lambda b, g, i: (b, i, g)),
        out_shape=jax.ShapeDtypeStruct((bsz, nq, groups * MXU_DIM), BF16),
        scratch_shapes=scratch,
        compiler_params=_params(("parallel", "parallel", "arbitrary")),
        name="flash_attn",
    )(*args)


def _outproj_kernel(oa_ref, ob_ref, x_ref, gt_ref, wa_ref, wb_ref, o_ref):
    y = jnp.dot(oa_ref[0], wa_ref[...], preferred_element_type=F32)
    y = y + jnp.dot(ob_ref[0], wb_ref[...], preferred_element_type=F32)
    o_ref[0] = x_ref[0] + gt_ref[0] * y


def _outproj(oa, ob, x, mods3, mrow, w_out, tm):
    bsz, n, d = x.shape
    half = oa.shape[2]
    tm = min(tm, n)
    return pl.pallas_call(
        _outproj_kernel,
        grid=(bsz, n // tm),
        in_specs=[pl.BlockSpec((1, tm, half), lambda b, i: (b, i, 0)),
                  pl.BlockSpec((1, tm, half), lambda b, i: (b, i, 0)),
                  pl.BlockSpec((1, tm, d), lambda b, i: (b, i, 0)),
                  _mod_spec(d, 2, mrow),
                  _resident((half, d), lambda b, i: (0, 0)),
                  _resident((half, d), lambda b, i: (1, 0))],
        out_specs=pl.BlockSpec((1, tm, d), lambda b, i: (b, i, 0)),
        out_shape=jax.ShapeDtypeStruct((bsz, n, d), F32),
        compiler_params=_params(("parallel", "parallel")),
        name="outproj",
    )(oa, ob, x, mods3, w_out, w_out)


HALO = 16


def _ffn_kernel(xp_ref, x_ref, xn_ref, g_ref, sh_ref, sc_ref, gt_ref, wup_ref, cw_ref, wdn_ref, fg_ref,
                o_ref, h_ref, *, final_norm):
    i = pl.program_id(1)
    last = pl.num_programs(1) - 1
    tm = x_ref.shape[1]
    nf = wdn_ref.shape[0]
    gain, shift, scale = g_ref[...], sh_ref[0], sc_ref[0]
    keep_p = jnp.where(i > 0, 1.0, 0.0)
    keep_n = jnp.where(i < last, 1.0, 0.0)
    h_ref[0:HALO, :] = (_norm_mod(xp_ref[0], gain, shift, scale) * keep_p).astype(BF16)
    h_ref[HALO:HALO + tm, :] = _norm_mod(x_ref[0], gain, shift, scale).astype(BF16)
    h_ref[HALO + tm:, :] = (_norm_mod(xn_ref[0], gain, shift, scale) * keep_n).astype(BF16)
    rows = tm + 2 * HALO

    def conv(u, f, part):
        w = cw_ref[part, f]
        up = pltpu.roll(u, 1, axis=0)[HALO:HALO + tm]
        un = pltpu.roll(u, rows - 1, axis=0)[HALO:HALO + tm]
        return up * w[0:1] + u[HALO:HALO + tm] * w[1:2] + un * w[2:3] + w[3:4]

    h = h_ref[...]
    up = lambda f: (jnp.dot(h, wup_ref[0, f], preferred_element_type=F32),
                    jnp.dot(h, wup_ref[1, f], preferred_element_type=F32))
    y = None
    ug, uv = up(0)
    for f in range(nf):
        nxt = up(f + 1) if f + 1 < nf else None
        g = conv(ug, f, 0)
        v = conv(uv, f, 1)
        a = (g * (1.0 / (1.0 + jnp.exp(-g))) * v).astype(BF16)
        yf = jnp.dot(a, wdn_ref[f], preferred_element_type=F32)
        y = yf if y is None else y + yf
        if nxt is not None:
            ug, uv = nxt
    out = x_ref[0] + gt_ref[0] * y
    if final_norm:
        out = out * lax.rsqrt(jnp.mean(out * out, axis=-1, keepdims=True) + NORM_EPS) * fg_ref[...]
    o_ref[0] = out


def _ffn(x, gain, mods3, mrow, wup, cw, wdn, final_gain, tm):
    bsz, n, d = x.shape
    tm = min(tm, n)
    nf, tf = wdn.shape[0], wdn.shape[1]
    hb = tm // HALO
    nh = n // HALO
    final_norm = final_gain is not None
    fg = final_gain if final_norm else gain
    return pl.pallas_call(
        functools.partial(_ffn_kernel, final_norm=final_norm),
        grid=(bsz, n // tm),
        in_specs=[pl.BlockSpec((1, HALO, d), lambda b, i: (b, jnp.maximum(i * hb - 1, 0), 0)),
                  pl.BlockSpec((1, tm, d), lambda b, i: (b, i, 0)),
                  pl.BlockSpec((1, HALO, d), lambda b, i: (b, jnp.minimum((i + 1) * hb, nh - 1), 0)),
                  pl.BlockSpec((1, d), lambda b, i: (0, 0)),
                  _mod_spec(d, 3, mrow), _mod_spec(d, 4, mrow), _mod_spec(d, 5, mrow),
                  _resident(wup.shape, lambda b, i: (0, 0, 0, 0)),
                  _resident(cw.shape, lambda b, i: (0, 0, 0, 0)),
                  _resident(wdn.shape, lambda b, i: (0, 0, 0)),
                  pl.BlockSpec((1, d), lambda b, i: (0, 0))],
        out_specs=pl.BlockSpec((1, tm, d), lambda b, i: (b, i, 0)),
        out_shape=jax.ShapeDtypeStruct((bsz, n, d), F32),
        scratch_shapes=[pltpu.VMEM((tm + 2 * HALO, d), BF16)],
        compiler_params=_params(("parallel", "parallel")),
        name="conv_ffn",
    )(x, x, x, gain.reshape(1, d), mods3, mods3, mods3, wup, cw, wdn, fg.reshape(1, d))


OD_Q, OD_C, OD_K, OD_V, OD_END = 0, 512, 2048, 2176, 2304
SUB = 8


def _prep_odd_kernel(zp_ref, z_ref, zn_ref, cw_ref, chd_ref, shd_ref,
                     q_ref, k_ref, v_ref, x0_ref, u_ref, *, scale):
    i = pl.program_id(1)
    last = pl.num_programs(1) - 1
    tm = z_ref.shape[1]
    cwid = (OD_K - OD_C) // 3
    for c in range((OD_C - OD_Q) // LANES):
        sl = slice(OD_Q + c * LANES, OD_Q + (c + 1) * LANES)
        y = _rope_lanes(z_ref[0, :, sl], chd_ref[...], shd_ref[...])
        q_ref[0, sl, :] = (y * scale).T.astype(BF16)
    k = _rope_lanes(z_ref[0, :, OD_K:OD_V], chd_ref[...], shd_ref[...])
    d0, d1 = _dup_heads(k)
    for c, dk in enumerate((d0, d0, d1, d1)):
        k_ref[0, :, c * LANES:(c + 1) * LANES] = dk.astype(BF16)
    d0, d1 = _dup_heads(z_ref[0, :, OD_V:OD_END])
    for c, dv in enumerate((d0, d0, d1, d1)):
        v_ref[0, c * LANES:(c + 1) * LANES, :] = dv.T.astype(BF16)
    row = lax.broadcasted_iota(jnp.int32, (tm, cwid), 0)
    keep_p = jnp.where(i > 0, 1.0, 0.0)
    keep_n = jnp.where(i < last, 1.0, 0.0)

    def conv(part):
        sl = slice(OD_C + part * cwid, OD_C + (part + 1) * cwid)
        csl = slice(part * cwid, (part + 1) * cwid)
        u = z_ref[0, :, sl]
        up = jnp.where(row == 0, zp_ref[0, SUB - 1:SUB, sl] * keep_p, pltpu.roll(u, 1, axis=0))
        un = jnp.where(row == tm - 1, zn_ref[0, 0:1, sl] * keep_n, pltpu.roll(u, tm - 1, axis=0))
        return up * cw_ref[0:1, csl] + u * cw_ref[1:2, csl] + un * cw_ref[2:3, csl] + cw_ref[3:4, csl]

    x0_ref[0] = conv(0)
    u_ref[0] = conv(2) * conv(1)


def _prep_odd(z, cw, chd, shd, tm):
    bsz, n, cols = z.shape
    tm = min(tm, n)
    hb = tm // SUB
    nh = n // SUB
    cwid = (OD_K - OD_C) // 3
    row_blk = lambda w: pl.BlockSpec((1, tm, w), lambda b, i: (b, i, 0))
    col_blk = lambda w: pl.BlockSpec((1, w, tm), lambda b, i: (b, 0, i))
    slab = 4 * LANES
    return pl.pallas_call(
        functools.partial(_prep_odd_kernel, scale=HEAD_DIM ** -0.5 * LOG2E),
        grid=(bsz, n // tm),
        in_specs=[pl.BlockSpec((1, SUB, cols), lambda b, i: (b, jnp.maximum(i * hb - 1, 0), 0)),
                  pl.BlockSpec((1, tm, cols), lambda b, i: (b, i, 0)),
                  pl.BlockSpec((1, SUB, cols), lambda b, i: (b, jnp.minimum((i + 1) * hb, nh - 1), 0)),
                  pl.BlockSpec(cw.shape, lambda b, i: (0, 0)),
                  pl.BlockSpec((tm, LANES), lambda b, i: (i, 0)),
                  pl.BlockSpec((tm, LANES), lambda b, i: (i, 0))],
        out_specs=[col_blk(slab), row_blk(slab), col_blk(slab), row_blk(cwid), row_blk(cwid)],
        out_shape=[jax.ShapeDtypeStruct((bsz, slab, n), BF16), jax.ShapeDtypeStruct((bsz, n, slab), BF16),
                   jax.ShapeDtypeStruct((bsz, slab, n), BF16), jax.ShapeDtypeStruct((bsz, n, cwid), F32),
                   jax.ShapeDtypeStruct((bsz, n, cwid), F32)],
        compiler_params=_params(("parallel", "parallel")),
        name="prep_odd",
    )(z, z, z, cw, chd, shd)


def _kv_ctx_odd_kernel(z_ref, k_ref, vt_ref):
    d0, d1 = _dup_heads(z_ref[0, :, 0:LANES])
    for c, dk in enumerate((d0, d0, d1, d1)):
        k_ref[0, :, c * LANES:(c + 1) * LANES] = dk.astype(BF16)
    d0, d1 = _dup_heads(z_ref[0, :, LANES:2 * LANES])
    for c, dv in enumerate((d0, d0, d1, d1)):
        vt_ref[0, c * LANES:(c + 1) * LANES, :] = dv.T.astype(BF16)


def _kv_ctx_odd(z):
    bsz, n, cols = z.shape
    slab = 4 * LANES
    return pl.pallas_call(
        _kv_ctx_odd_kernel,
        grid=(bsz,),
        in_specs=[pl.BlockSpec((1, n, cols), lambda b: (b, 0, 0))],
        out_specs=[pl.BlockSpec((1, n, slab), lambda b: (b, 0, 0)), pl.BlockSpec((1, slab, n), lambda b: (b, 0, 0))],
        out_shape=[jax.ShapeDtypeStruct((bsz, n, slab), BF16), jax.ShapeDtypeStruct((bsz, slab, n), BF16)],
        compiler_params=_params(("parallel",)),
        name="kv_ctx_odd",
    )(z)


WIN_QB = 2


def _window_kernel(qt_ref, kc_ref, vct_ref, kl_ref, vlt_ref, bias_ref, sink_ref, o_ref, kbc, vbct, kbl, vblt,
                   *, lc, nblk, qb, nch):
    qi = pl.program_id(2)
    tq = qb * Q_BLOCK
    nsp = qb + 2
    blk = HEADS_PER_GROUP * Q_BLOCK

    @pl.when(qi == 0)
    def _build():
        kms, _ = _head_lane_masks(lc, 1)
        vms = _head_row_masks(lc)
        for j in range(HEADS_PER_GROUP):
            kbc[j * lc:(j + 1) * lc, :] = kc_ref[0] * kms[j]
            vbct[:, j * lc:(j + 1) * lc] = vct_ref[0] * vms[j]
        kms, _ = _head_lane_masks(Q_BLOCK, 1)
        vms = _head_row_masks(Q_BLOCK)
        for e in (0, nblk + 1):
            kbl[e] = jnp.zeros((blk, MXU_DIM), BF16)
            vblt[e] = jnp.zeros((MXU_DIM, blk), BF16)

        def body(i, carry):
            r0 = pl.multiple_of(i * Q_BLOCK, Q_BLOCK)
            for j in range(HEADS_PER_GROUP):
                kbl[i + 1, j * Q_BLOCK:(j + 1) * Q_BLOCK, :] = kl_ref[0, pl.ds(r0, Q_BLOCK), :] * kms[j]
            return carry

        lax.fori_loop(0, nblk, body, 0)
        for i in range(nblk):
            for j in range(HEADS_PER_GROUP):
                vblt[i + 1, :, j * Q_BLOCK:(j + 1) * Q_BLOCK] = vlt_ref[0, :, i * Q_BLOCK:(i + 1) * Q_BLOCK] * vms[j]

    ntile = nblk // qb
    for h in range(nch):
        t = qi * nch + h
        qt = qt_ref[0, :, h * tq:(h + 1) * tq]
        s_c = jnp.dot(kbc[...], qt, preferred_element_type=F32)
        kspan = kbl[pl.ds(qb * t, nsp)].reshape(nsp * blk, MXU_DIM)
        s_s = jnp.dot(kspan, qt, preferred_element_type=F32) + bias_ref[...]
        edge = {0: jnp.where(t == 0, NEG_INF, 0.0), nsp - 1: jnp.where(t == ntile - 1, NEG_INF, 0.0)}
        pcs, pss, invs = [], [[None] * HEADS_PER_GROUP for _ in range(nsp)], []
        for j in range(HEADS_PER_GROUP):
            sink = sink_ref[0, j][0:1, :]
            segs = [s_c[j * lc:(j + 1) * lc, :]]
            for b in range(nsp):
                sg = s_s[b * blk + j * Q_BLOCK: b * blk + (j + 1) * Q_BLOCK, :]
                segs.append(sg + edge[b] if b in edge else sg)
            m = sink
            for sg in segs:
                m = jnp.maximum(m, jnp.max(sg, axis=0, keepdims=True))
            ps = [jnp.exp2(sg - m) for sg in segs]
            den = jnp.exp2(sink - m)
            for p in ps:
                den = den + jnp.sum(p, axis=0, keepdims=True)
            invs.append(1.0 / den)
            pcs.append(ps[0].astype(BF16))
            for b in range(nsp):
                pss[b][j] = ps[1 + b].astype(BF16)
        ot = jnp.dot(vbct[...], jnp.concatenate(pcs, axis=0), preferred_element_type=F32)
        for b in range(nsp):
            ot = ot + jnp.dot(vblt[qb * t + b], jnp.concatenate(pss[b], axis=0), preferred_element_type=F32)
        o_ref[0, h * tq:(h + 1) * tq, :] = (ot * _rows4(invs, tq)).T.astype(BF16)


def _window_bias(qb):
    nsp = qb + 2
    row = jnp.arange(nsp * HEADS_PER_GROUP * Q_BLOCK)[:, None]
    r = jnp.arange(qb * Q_BLOCK)[None, :]
    b = row // (HEADS_PER_GROUP * Q_BLOCK)
    jj = b * Q_BLOCK + row % Q_BLOCK
    band = (jj >= r) & (jj <= r + 2 * WINDOW)
    return jnp.where(band, 0.0, NEG_INF).astype(F32)


def _window(qt, kc, vct, kl, vlt, sink):
    bsz, qw, n = qt.shape
    groups = qw // MXU_DIM
    lc = kc.shape[1]
    nblk = n // Q_BLOCK
    qb = min(WIN_QB, nblk)
    tq = qb * Q_BLOCK
    nch = 2 if n % (2 * tq) == 0 else 1
    nq = n // (tq * nch)
    blk = HEADS_PER_GROUP * Q_BLOCK
    bias = _window_bias(qb)
    sink_t = jnp.broadcast_to((sink.astype(F32) * LOG2E).reshape(groups, HEADS_PER_GROUP, 1, 1),
                              (groups, HEADS_PER_GROUP, SUB, tq))

    return pl.pallas_call(
        functools.partial(_window_kernel, lc=lc, nblk=nblk, qb=qb, nch=nch),
        grid=(bsz, groups, nq),
        in_specs=[pl.BlockSpec((1, MXU_DIM, tq * nch), lambda b, g, i: (b, g, i)),
                  pl.BlockSpec((1, lc, MXU_DIM), lambda b, g, i: (b, 0, g)),
                  pl.BlockSpec((1, MXU_DIM, lc), lambda b, g, i: (b, g, 0)),
                  pl.BlockSpec((1, n, MXU_DIM), lambda b, g, i: (b, 0, g)),
                  pl.BlockSpec((1, MXU_DIM, n), lambda b, g, i: (b, g, 0)),
                  _resident(bias.shape, lambda b, g, i: (0, 0)),
                  pl.BlockSpec((1, HEADS_PER_GROUP, SUB, tq), lambda b, g, i: (g, 0, 0, 0))],
        out_specs=pl.BlockSpec((1, tq * nch, MXU_DIM), lambda b, g, i: (b, i, g)),
        out_shape=jax.ShapeDtypeStruct((bsz, n, qw), BF16),
        scratch_shapes=[pltpu.VMEM((HEADS_PER_GROUP * lc, MXU_DIM), BF16),
                        pltpu.VMEM((MXU_DIM, HEADS_PER_GROUP * lc), BF16),
                        pltpu.VMEM((nblk + 2, blk, MXU_DIM), BF16),
                        pltpu.VMEM((nblk + 2, MXU_DIM, blk), BF16)],
        compiler_params=_params(("parallel", "parallel", "arbitrary")),
        name="window_attn",
    )(qt, kc, vct, kl, vlt, bias, sink_t)


FFT_N2 = 128


def _filter_kernel(z_ref, w1_ref, b1_ref, w2_ref, b2_ref, w3_ref, b3_ref, w4_ref, fr_ref, dl_ref, o_ref):
    def lin(a, w_ref):
        a_hi, a_lo = _split_bf16(a)
        w_hi, w_lo = _split_bf16(w_ref[...])
        return _dot3(a_hi, a_lo, w_hi, w_lo)

    z = z_ref[...]
    h = b1_ref[...]
    for e in range(C_EMB_DIM):
        h = h + z[:, e:e + 1] * w1_ref[e:e + 1, :]
    h = jnp.sin(fr_ref[0:1] * h)
    h = jnp.sin(fr_ref[1:2] * (lin(h, w2_ref) + b2_ref[...]))
    h = jnp.sin(fr_ref[2:3] * (lin(h, w3_ref) + b3_ref[...]))
    h = lin(h, w4_ref)
    t = z[:, 0:1]
    cw = dl_ref.shape[1]
    dec = jnp.exp(-t * dl_ref[...])
    row = lax.broadcasted_iota(jnp.int32, dec.shape, 0) + pl.program_id(0) * z.shape[0]
    o_ref[0] = h[:, :cw] * dec
    o_ref[1] = jnp.where(row == 0, 0.0, h[:, cw:] * dec)


def _filters(zfeat, w1, b1, w2, b2, w3, b3, w4, freq, deltas, tm):
    n = zfeat.shape[0]
    tm = min(tm, n)
    cw = deltas.shape[1]
    full = lambda a: pl.BlockSpec(a.shape, lambda i: (0,) * a.ndim)
    ops = (w1, b1, w2, b2, w3, b3, w4, freq, deltas)
    return pl.pallas_call(
        _filter_kernel,
        grid=(n // tm,),
        in_specs=[pl.BlockSpec((tm, zfeat.shape[1]), lambda i: (i, 0))] + [full(a) for a in ops],
        out_specs=pl.BlockSpec((2, tm, cw), lambda i: (0, i, 0)),
        out_shape=jax.ShapeDtypeStruct((2, n, cw), F32),
        compiler_params=_params(("parallel",)),
        name="hyena_filters",
    )(zfeat, *ops)


def _dft_first_kernel(f_ref, x_ref, o_ref):
    _, n1h, tm2, cw = x_ref.shape
    x = x_ref[0].reshape(n1h * tm2, cw).astype(BF16)
    r = jnp.dot(f_ref[...], x, preferred_element_type=F32)
    o_ref[0] = r.astype(BF16).reshape(o_ref.shape[1:])


def _dft_first(f1, x):
    bsz, n1h, n2, cw = x.shape
    tm2 = HALO
    nkp = f1.shape[0] // (2 * tm2)
    return pl.pallas_call(
        _dft_first_kernel,
        grid=(bsz, n2 // tm2),
        in_specs=[pl.BlockSpec(f1.shape, lambda b, j: (0, 0)),
                  pl.BlockSpec((1, n1h, tm2, cw), lambda b, j: (b, 0, j, 0))],
        out_specs=pl.BlockSpec((1, 2, nkp, tm2, cw), lambda b, j: (b, 0, 0, j, 0)),
        out_shape=jax.ShapeDtypeStruct((bsz, 2, nkp, n2, cw), BF16),
        compiler_params=_params(("parallel", "parallel")),
        name="dft_first",
    )(f1, x)


def _dft_last_gate_kernel(f_ref, b_ref, x0_ref, u_ref, bias_ref, o_ref):
    _, _, nkp, tm2, cw = b_ref.shape
    y = jnp.dot(f_ref[...], b_ref[0].reshape(2 * nkp * tm2, cw), preferred_element_type=F32)
    y = y.reshape(x0_ref.shape[1:])
    o_ref[0] = (x0_ref[0] * (y + u_ref[0] * bias_ref[...])).astype(BF16)


def _dft_last_gate(f2, b, x0, u, bias):
    bsz, _, nkp, n2, cw = b.shape
    tm2 = HALO
    n1h = f2.shape[0] // tm2
    tblk = pl.BlockSpec((1, n1h, tm2, cw), lambda b_, j: (b_, 0, j, 0))
    return pl.pallas_call(
        _dft_last_gate_kernel,
        grid=(bsz, n2 // tm2),
        in_specs=[pl.BlockSpec(f2.shape, lambda b_, j: (0, 0)),
                  pl.BlockSpec((1, 2, nkp, tm2, cw), lambda b_, j: (b_, 0, 0, j, 0)),
                  tblk, tblk, pl.BlockSpec((1, cw), lambda b_, j: (0, 0))],
        out_specs=tblk,
        out_shape=jax.ShapeDtypeStruct((bsz, n1h, n2, cw), BF16),
        compiler_params=_params(("parallel", "parallel")),
        name="dft_last_gate",
    )(f2, b, x0, u, bias.reshape(1, cw))


def _spec_fwd(a, m_ref):
    return jnp.dot(m_ref[0], a.reshape(2 * FFT_N2, a.shape[-1]), preferred_element_type=F32)


def _filter_spec_kernel(a_ref, m_ref, o_ref):
    xf = _spec_fwd(a_ref[0, :, 0], m_ref)
    xb = _spec_fwd(a_ref[1, :, 0], m_ref)
    o_ref[0, 0] = xf[:FFT_N2] + xb[:FFT_N2]
    o_ref[1, 0] = xf[FFT_N2:] - xb[FFT_N2:]


def _conv_spec_kernel(a_ref, k_ref, m_ref, i_ref, o_ref):
    x = _spec_fwd(a_ref[0, :, 0], m_ref)
    xr, xi = x[:FFT_N2], x[FFT_N2:]
    kr, ki = k_ref[0, 0], k_ref[1, 0]
    y = jnp.concatenate([xr * kr - xi * ki, xr * ki + xi * kr], axis=0).astype(BF16)
    bv = jnp.dot(i_ref[0], y, preferred_element_type=F32)
    o_ref[0, :, 0] = bv.astype(BF16).reshape(2, FFT_N2, bv.shape[-1])


def _filter_spec(a, m_fwd):
    _, _, nkp, _, cw = a.shape
    return pl.pallas_call(
        _filter_spec_kernel,
        grid=(nkp,),
        in_specs=[pl.BlockSpec((2, 2, 1, FFT_N2, cw), lambda k: (0, 0, k, 0, 0)),
                  pl.BlockSpec((1, 2 * FFT_N2, 2 * FFT_N2), lambda k: (k, 0, 0))],
        out_specs=pl.BlockSpec((2, 1, FFT_N2, cw), lambda k: (0, k, 0, 0)),
        out_shape=jax.ShapeDtypeStruct((2, nkp, FFT_N2, cw), F32),
        compiler_params=_params(("parallel",)),
        name="filter_spectrum",
    )(a, m_fwd)


def _conv_spec(a, kspec, m_fwd, m_inv):
    bsz, _, nkp, _, cw = a.shape
    blk = pl.BlockSpec((1, 2, 1, FFT_N2, cw), lambda k, b: (b, 0, k, 0, 0))
    mat = pl.BlockSpec((1, 2 * FFT_N2, 2 * FFT_N2), lambda k, b: (k, 0, 0))
    return pl.pallas_call(
        _conv_spec_kernel,
        grid=(nkp, bsz),
        in_specs=[blk, pl.BlockSpec((2, 1, FFT_N2, cw), lambda k, b: (0, k, 0, 0)), mat, mat],
        out_specs=blk,
        out_shape=jax.ShapeDtypeStruct(a.shape, BF16),
        compiler_params=_params(("parallel", "parallel")),
        name="conv_spectrum",
    )(a, kspec, m_fwd, m_inv)


def _dft_tables(n):
    nfft = 2 * n
    n2 = FFT_N2
    n1 = nfft // n2
    n1h = n1 // 2
    nk = n1h + 1
    nkp = -(-nk // SUB) * SUB
    two_pi = 2.0 * math.pi

    def cs(num, den):
        ang = (num % den).astype(F32) * (two_pi / den)
        return jnp.cos(ang), jnp.sin(ang)

    def pad_k(t, axis):
        widths = [(0, 0)] * t.ndim
        widths[axis] = (0, nkp - nk)
        return jnp.pad(t, widths)

    k1 = jnp.arange(nk, dtype=jnp.int32)
    c1, s1 = cs(k1[:, None] * jnp.arange(n1h, dtype=jnp.int32)[None, :], n1)
    f1 = jnp.concatenate([pad_k(c1, 0), pad_k(-s1, 0)], axis=0)
    kk = k1[:, None, None] + n1 * jnp.arange(n2, dtype=jnp.int32)[None, :, None]
    cg, sg = cs(kk * jnp.arange(n2, dtype=jnp.int32)[None, None, :], nfft)
    g_re, g_im = cg, -sg
    m_fwd = jnp.concatenate([jnp.concatenate([g_re, -g_im], axis=2),
                             jnp.concatenate([g_im, g_re], axis=2)], axis=1)
    gt_re, gt_im = jnp.swapaxes(g_re, 1, 2), jnp.swapaxes(g_im, 1, 2)
    m_inv = jnp.concatenate([jnp.concatenate([gt_re, gt_im], axis=2),
                             jnp.concatenate([-gt_im, gt_re], axis=2)], axis=1)
    wk = jnp.where((k1 == 0) | (k1 == n1h), 1.0, 2.0)[None, :] * (1.0 / nfft)
    c2, s2 = cs(jnp.arange(n1h, dtype=jnp.int32)[:, None] * k1[None, :], n1)
    f2 = jnp.concatenate([pad_k(c2 * wk, 1), pad_k(-s2 * wk, 1)], axis=1)
    eye = jnp.eye(HALO, dtype=F32)
    return tuple(t.astype(BF16) for t in (jnp.kron(f1, eye), pad_k(m_fwd, 0), pad_k(m_inv, 0), jnp.kron(f2, eye)))


def _hyena_long_conv(u, x0, bias, hcat, tabs):
    f1, m_fwd, m_inv, f2 = tabs
    bsz, n, cw = u.shape
    n2 = FFT_N2
    n1h = n // n2
    kspec = _filter_spec(_dft_first(f1, hcat.reshape(2, n1h, n2, cw)), m_fwd)
    u4 = u.reshape(bsz, n1h, n2, cw)
    b_u = _conv_spec(_dft_first(f1, u4), kspec, m_fwd, m_inv)
    return _dft_last_gate(f2, b_u, x0.reshape(bsz, n1h, n2, cw), u4, bias).reshape(bsz, n, cw)


def _axial_angles(rows, rope_dim):
    row_idx = jnp.repeat(jnp.arange(rows), GRID_W).astype(F32)
    col_idx = jnp.tile(jnp.arange(GRID_W), rows).astype(F32)
    d_axis = rope_dim // 2
    inv_freq = ROPE_THETA ** (-jnp.arange(0, d_axis, 2, dtype=F32) / d_axis)
    ang = jnp.concatenate([row_idx[:, None] * inv_freq, col_idx[:, None] * inv_freq], axis=-1)
    return jnp.cos(ang), jnp.sin(ang)


def _rope_tables(n):
    rows = n // GRID_W
    sign = jnp.tile(jnp.array([-1.0, 1.0], F32), LANES // 2)[None, :]
    cos, sin = _axial_angles(rows, HEAD_DIM)
    chd = jnp.tile(jnp.repeat(cos, 2, axis=1), (1, LANES // HEAD_DIM))
    shd = jnp.tile(jnp.repeat(sin, 2, axis=1), (1, LANES // HEAD_DIM)) * sign
    cos, sin = _axial_angles(rows, B_ROPE)
    ones = jnp.ones((n, LANES - 2 * B_ROPE), F32)
    cb = jnp.concatenate([jnp.tile(jnp.repeat(cos, 2, axis=1), (1, 2)), ones], axis=1)
    sb = jnp.concatenate([jnp.tile(jnp.repeat(sin, 2, axis=1), (1, 2)), 0.0 * ones], axis=1) * sign
    return chd, shd, cb, sb


def _even_weights(w_in, a_qn, a_kn, b_qn, b_w_uq, b_kvn, b_w_ukv):
    d = w_in.shape[0]
    aq_w = w_in.shape[1] - (B_Q_RANK + 2 * A_KV_HEADS * HEAD_DIM + B_KV_RANK + B_ROPE)
    o = [0, aq_w, aq_w + B_Q_RANK]
    o += [o[-1] + A_KV_HEADS * HEAD_DIM, o[-1] + 2 * A_KV_HEADS * HEAD_DIM]
    o += [o[-1] + B_KV_RANK, o[-1] + B_KV_RANK + B_ROPE]
    kr = w_in[:, o[5]:o[6]]
    w_aug = jnp.concatenate([w_in[:, :o[5]], kr, kr, jnp.zeros((d, LANES - 2 * B_ROPE), w_in.dtype)], axis=1)
    heads = b_w_uq.shape[1] // (B_NOPE + B_ROPE)
    uq = b_w_uq.reshape(B_Q_RANK, heads // 2, 2, B_NOPE + B_ROPE)
    wuq = jnp.concatenate([uq[:, :, 0, :B_NOPE], uq[:, :, 1, :B_NOPE], uq[:, :, 0, B_NOPE:], uq[:, :, 1, B_NOPE:],
                           jnp.zeros((B_Q_RANK, heads // 2, MXU_DIM - 2 * (B_NOPE + B_ROPE)), b_w_uq.dtype)], axis=2)
    wuq = wuq.reshape(B_Q_RANK, heads // 2 * MXU_DIM)
    ukv = b_w_ukv.reshape(B_KV_RANK, heads, B_NOPE + B_VDIM)
    wukn = ukv[:, :, :B_NOPE].reshape(B_KV_RANK, heads * B_NOPE)
    wuv = ukv[:, :, B_NOPE:].reshape(B_KV_RANK, heads * B_VDIM)
    gains = (jnp.tile(a_qn, aq_w // HEAD_DIM)[None, :], jnp.tile(a_kn, A_KV_HEADS)[None, :],
             b_qn[None, :], b_kvn[None, :])
    return w_aug.astype(BF16), gains + (wuq.astype(BF16), wukn.astype(BF16), wuv.astype(BF16))


def _ffn_weights(w_up, conv_w, conv_b, w_down, nf):
    d, f2 = w_up.shape
    f = f2 // 2
    tf = f // nf
    wup = w_up.reshape(d, 2, nf, tf).transpose(1, 2, 0, 3).astype(BF16)
    cw = jnp.concatenate([conv_w, conv_b[None, :], jnp.zeros((SUB - 4, f2), conv_w.dtype)], axis=0)
    cw = cw.reshape(SUB, 2, nf, tf).transpose(1, 2, 0, 3)
    wdn = w_down.reshape(nf, tf, d).astype(BF16)
    return wup, cw, wdn


FFN_CHUNKS = 1


def kernel(x, c, ctx, c_ctx, w_mod, b_mod, norm_mix, norm_ffn, ev_w_in, ev_w_out, a_q_norm, a_k_norm, b_q_norm, b_w_uq, b_kv_norm, b_w_ukv, od_w_in, od_w_out, d_sink, c_conv_w, c_conv_b, c_filt_w1, c_filt_b1, c_filt_w2, c_filt_b2, c_filt_w3, c_filt_b3, c_filt_w4, c_filt_freq, c_bias, ffn_w_up, ffn_conv_w, ffn_conv_b, ffn_w_down, final_norm):
    bsz, n, d = x.shape
    depth = w_mod.shape[0]
    assert depth == 2 and n % Q_BLOCK == 0 and d % LANES == 0
    tm = 512
    rows = -(-(bsz + 1) // SUB) * SUB
    cvec = jnp.concatenate([c, c_ctx[None, :], jnp.zeros((rows - bsz - 1, d), F32)], axis=0)
    mods = _mods(cvec, w_mod, b_mod)
    chd, shd, cb, sb = _rope_tables(n)
    tabs = (chd, shd, cb, sb)

    m3 = mods[0].reshape(rows, 1, N_MOD * d)
    w_aug, prep_w = _even_weights(ev_w_in[0], a_q_norm[0], a_k_norm[0], b_q_norm[0], b_w_uq[0],
                                  b_kv_norm[0], b_w_ukv[0])
    z_l = _inproj(x, norm_mix[0], m3, None, w_aug, tm)
    z_c = _inproj(ctx, norm_mix[0], m3, bsz, w_aug, tm)
    qa_l, ka_l, va_l, qb_l, kb_l, vb_l = _prep_even(z_l, prep_w, tabs, True, tm)
    qa_c, ka_c, va_c, qb_c, kb_c, vb_c = _prep_even(z_c, prep_w, tabs, False, tm)
    tq, tk = 1024, 256
    oa_l = _flash(qa_l, (ka_c, ka_l), (va_c, va_l), 1, tq, tk)
    ob_l = _flash(qb_l, (kb_c, kb_l), (vb_c, vb_l), 2, tq, tk)
    oa_c = _flash(qa_c, (ka_c,), (va_c,), 1, tq, tk)
    ob_c = _flash(qb_c, (kb_c,), (vb_c,), 2, tq, tk)
    w_out = ev_w_out[0].astype(BF16)
    x = _outproj(oa_l, ob_l, x, m3, None, w_out, tm)
    ctx = _outproj(oa_c, ob_c, ctx, m3, bsz, w_out, tm)
    ffn_w = _ffn_weights(ffn_w_up[0], ffn_conv_w[0], ffn_conv_b[0], ffn_w_down[0], FFN_CHUNKS)
    x = _ffn(x, norm_ffn[0], m3, None, *ffn_w, None, tm)
    ctx = _ffn(ctx, norm_ffn[0], m3, bsz, *ffn_w, None, tm)

    m3 = mods[1].reshape(rows, 1, N_MOD * d)
    w_in = od_w_in[0].astype(BF16)
    z_l = _inproj(x, norm_mix[1], m3, None, w_in, tm)
    z_c = _inproj(ctx, norm_mix[1], m3, bsz, w_in[:, OD_K:], tm)
    cw = jnp.concatenate([c_conv_w[0], c_conv_b[0][None, :], jnp.zeros((SUB - 4, OD_K - OD_C), F32)], axis=0)
    qd, kd, vd, x0, u = _prep_odd(z_l, cw, chd, shd, tm)
    kd_c, vd_c = _kv_ctx_odd(z_c)
    od = _window(qd, kd_c, vd_c, kd, vd, d_sink[0])
    t = jnp.linspace(0.0, 1.0, n, dtype=F32)[:, None]
    wpos = 2 * math.pi * jnp.arange(n, dtype=F32)[:, None] / n
    fb = jnp.linspace(1e-4, C_BANDS - 1, C_BANDS, dtype=F32)[None, :]
    zfeat = jnp.concatenate([t, jnp.cos(fb * wpos), -jnp.sin(fb * wpos), jnp.zeros((n, SUB - C_EMB_DIM), F32)], axis=-1)
    cwid = c_bias.shape[1]
    deltas = jnp.abs(jnp.linspace(C_MIN_DECAY, C_MAX_DECAY, cwid, dtype=F32))[None, :]
    w1 = jnp.concatenate([c_filt_w1[0], jnp.zeros((SUB - C_EMB_DIM, c_filt_w1.shape[2]), F32)], axis=0)
    hcat = _filters(zfeat, w1, c_filt_b1[0][None, :], c_filt_w2[0], c_filt_b2[0][None, :], c_filt_w3[0],
                    c_filt_b3[0][None, :], c_filt_w4[0], jnp.concatenate([c_filt_freq[0], jnp.zeros((SUB - 3, c_filt_freq.shape[2]), F32)], axis=0),
                    deltas, tm)
    oc = _hyena_long_conv(u, x0, c_bias[0], hcat, _dft_tables(n))
    x = _outproj(od, oc, x, m3, None, od_w_out[0].astype(BF16), tm)
    ffn_w = _ffn_weights(ffn_w_up[1], ffn_conv_w[1], ffn_conv_b[1], ffn_w_down[1], FFN_CHUNKS)
    return _ffn(x, norm_ffn[1], m3, None, *ffn_w, final_norm, tm)
```

```python
import functools
import math

import jax
import jax.numpy as jnp
from jax import lax
from jax.experimental import pallas as pl
from jax.experimental.pallas import tpu as pltpu

F32 = jnp.float32
BF16 = jnp.bfloat16

GRID_W = 64
HEAD_DIM = 64
ROPE_THETA = 10000.0
NORM_EPS = 1e-6
NEG_INF = -1e30
N_MOD = 6
A_KV_HEADS = 2
B_NOPE = 64
B_ROPE = 32
B_VDIM = 64
B_Q_RANK = 384
B_KV_RANK = 256
C_EMB_DIM = 5
C_BANDS = (C_EMB_DIM - 1) // 2
C_MIN_DECAY = math.log(1e-2) / 1.5
C_MAX_DECAY = math.log(1e-2) / 0.3
WINDOW = 128
Q_BLOCK = 128
LOG2E = math.log2(math.e)

LANES = 128
MXU_DIM = 256
VMEM_LIMIT = 56 * 1024 * 1024
HEADS_PER_GROUP = MXU_DIM // HEAD_DIM


def _params(sem, vmem=VMEM_LIMIT):
    return pltpu.CompilerParams(dimension_semantics=sem, vmem_limit_bytes=vmem)


def _resident(shape, index_map):
    return pl.BlockSpec(shape, index_map, pipeline_mode=pl.Buffered(1))


def _norm_mod(x, gain, shift, scale):
    inv = lax.rsqrt(jnp.mean(x * x, axis=-1, keepdims=True) + NORM_EPS)
    return (x * inv) * gain * (1.0 + scale) + shift


def _rope_lanes(x, cos, sin_signed):
    lane = lax.broadcasted_iota(jnp.int32, x.shape, 1)
    nxt = pltpu.roll(x, LANES - 1, axis=1)
    prv = pltpu.roll(x, 1, axis=1)
    swapped = jnp.where(lane % 2 == 0, nxt, prv)
    return x * cos + swapped * sin_signed


def _head_rmsnorm_lanes(x, gain):
    lane = lax.broadcasted_iota(jnp.int32, x.shape, 1)
    lo = lane < HEAD_DIM
    sq = x * x
    s_lo = jnp.sum(jnp.where(lo, sq, 0.0), axis=-1, keepdims=True)
    s_hi = jnp.sum(jnp.where(lo, 0.0, sq), axis=-1, keepdims=True)
    ms = jnp.where(lo, s_lo, s_hi) * (1.0 / HEAD_DIM)
    return x * lax.rsqrt(ms + NORM_EPS) * gain


def _dup_heads(x):
    lane = lax.broadcasted_iota(jnp.int32, x.shape, 1)
    lo = lane < HEAD_DIM
    r = pltpu.roll(x, HEAD_DIM, axis=1)
    return jnp.where(lo, x, r), jnp.where(lo, r, x)


def _lanes4(vals, rows):
    lane = lax.broadcasted_iota(jnp.int32, (rows, MXU_DIM), 1)
    return jnp.where(lane < HEAD_DIM, vals[0],
                     jnp.where(lane < 2 * HEAD_DIM, vals[1],
                               jnp.where(lane < 3 * HEAD_DIM, vals[2], vals[3])))


def _split_bf16(x):
    hi = x.astype(BF16)
    lo = (x - hi.astype(F32)).astype(BF16)
    return hi, lo


def _dot3(a_hi, a_lo, b_hi, b_lo):
    d = functools.partial(jnp.dot, preferred_element_type=F32)
    return d(a_hi, b_hi) + d(a_hi, b_lo) + d(a_lo, b_hi)


def _mods_kernel(c_ref, w_ref, b_ref, o_ref):
    c = c_ref[...]
    s = c * (1.0 / (1.0 + jnp.exp(-c)))
    s_hi, s_lo = _split_bf16(s)
    w_hi, w_lo = _split_bf16(w_ref[0])
    o_ref[0] = _dot3(s_hi, s_lo, w_hi, w_lo) + b_ref[0]


def _mods(cvec, w_mod, b_mod):
    depth, d, n = w_mod.shape
    rows = cvec.shape[0]
    tn = 1536
    return pl.pallas_call(
        _mods_kernel,
        grid=(depth, n // tn),
        in_specs=[pl.BlockSpec((rows, d), lambda l, j: (0, 0)),
                  pl.BlockSpec((1, d, tn), lambda l, j: (l, 0, j)),
                  pl.BlockSpec((1, 1, tn), lambda l, j: (l, 0, j))],
        out_specs=pl.BlockSpec((1, rows, tn), lambda l, j: (l, 0, j)),
        out_shape=jax.ShapeDtypeStruct((depth, rows, n), F32),
        compiler_params=_params(("arbitrary", "arbitrary")),
        name="mods",
    )(cvec, w_mod, b_mod.reshape(depth, 1, n))


def _mod_spec(d, chunk, row):
    if row is None:
        return pl.BlockSpec((1, 1, d), lambda b, *_: (b, 0, chunk))
    return pl.BlockSpec((1, 1, d), lambda b, *_: (row, 0, chunk))


def _inproj_kernel(x_ref, g_ref, sh_ref, sc_ref, w_ref, o_ref):
    h = _norm_mod(x_ref[0], g_ref[...], sh_ref[0], sc_ref[0])
    o_ref[0] = jnp.dot(h.astype(BF16), w_ref[...], preferred_element_type=F32)


def _inproj(x, gain, mods3, mrow, w, tm):
    bsz, n, d = x.shape
    cols = w.shape[1]
    tm = min(tm, n)
    return pl.pallas_call(
        _inproj_kernel,
        grid=(bsz, n // tm),
        in_specs=[pl.BlockSpec((1, tm, d), lambda b, i: (b, i, 0)),
                  pl.BlockSpec((1, d), lambda b, i: (0, 0)),
                  _mod_spec(d, 0, mrow), _mod_spec(d, 1, mrow),
                  _resident((d, cols), lambda b, i: (0, 0))],
        out_specs=pl.BlockSpec((1, tm, cols), lambda b, i: (b, i, 0)),
        out_shape=jax.ShapeDtypeStruct((bsz, n, cols), F32),
        compiler_params=_params(("parallel", "parallel")),
        name="inproj",
    )(x, gain.reshape(1, d), mods3, mods3, w)


EV_AQ, EV_BQ, EV_AK, EV_AV, EV_BKV, EV_BKR, EV_END = 0, 512, 896, 1024, 1152, 1408, 1536


def _prep_even_kernel(x_ref, g_ref, sh_ref, sc_ref, win_ref,
                      aqg_ref, akg_ref, bqg_ref, bkvg_ref, wuq_ref, wukn_ref, wuv_ref,
                      chd_ref, shd_ref, cb_ref, sb_ref,
                      qa_ref, ka_ref, va_ref, qb_ref, kb_ref, vb_ref, z_ref, *, use_rope, a_scale, b_scale):
    h = _norm_mod(x_ref[0], g_ref[...], sh_ref[0], sc_ref[0])
    z_ref[0] = jnp.dot(h.astype(BF16), win_ref[...], preferred_element_type=F32)

    def rope_hd(y):
        return _rope_lanes(y, chd_ref[...], shd_ref[...]) if use_rope else y

    def rope_b(y):
        return _rope_lanes(y, cb_ref[...], sb_ref[...]) if use_rope else y

    def put_t(dst, c, y):
        dst[0, c * LANES:(c + 1) * LANES, :] = y.T.astype(BF16)

    for c in range((EV_BQ - EV_AQ) // LANES):
        sl = slice(EV_AQ + c * LANES, EV_AQ + (c + 1) * LANES)
        y = rope_hd(_head_rmsnorm_lanes(z_ref[0, :, sl], aqg_ref[:, c * LANES:(c + 1) * LANES]))
        put_t(qa_ref, c, y * a_scale)
    k = rope_hd(_head_rmsnorm_lanes(z_ref[0, :, EV_AK:EV_AV], akg_ref[...]))
    d0, d1 = _dup_heads(k)
    d0 = d0.astype(BF16)
    d1 = d1.astype(BF16)
    ka_ref[0, :, 0 * LANES:1 * LANES] = d0
    ka_ref[0, :, 1 * LANES:2 * LANES] = d0
    ka_ref[0, :, 2 * LANES:3 * LANES] = d1
    ka_ref[0, :, 3 * LANES:4 * LANES] = d1
    d0, d1 = _dup_heads(z_ref[0, :, EV_AV:EV_BKV])
    for c, dv in enumerate((d0, d0, d1, d1)):
        put_t(va_ref, c, dv)
    cq = z_ref[0, :, EV_BQ:EV_AK]
    cq = cq * lax.rsqrt(jnp.mean(cq * cq, axis=-1, keepdims=True) + NORM_EPS) * bqg_ref[...]
    qb = jnp.dot(cq.astype(BF16), wuq_ref[...], preferred_element_type=F32)
    for c in range(qb.shape[1] // MXU_DIM):
        lo = slice(c * MXU_DIM, c * MXU_DIM + LANES)
        hi = slice(c * MXU_DIM + LANES, (c + 1) * MXU_DIM)
        put_t(qb_ref, 2 * c, qb[:, lo] * b_scale)
        put_t(qb_ref, 2 * c + 1, rope_b(qb[:, hi]) * b_scale)
    ckv = z_ref[0, :, EV_BKV:EV_BKR]
    ckv = (ckv * lax.rsqrt(jnp.mean(ckv * ckv, axis=-1, keepdims=True) + NORM_EPS) * bkvg_ref[...]).astype(BF16)
    kn = jnp.dot(ckv, wukn_ref[...], preferred_element_type=F32)
    vb = jnp.dot(ckv, wuv_ref[...], preferred_element_type=F32)
    for c in range(vb.shape[1] // LANES):
        put_t(vb_ref, c, vb[:, c * LANES:(c + 1) * LANES])
    kr = rope_b(z_ref[0, :, EV_BKR:EV_END]).astype(BF16)
    for c in range(kn.shape[1] // LANES):
        kb_ref[0, :, c * MXU_DIM:c * MXU_DIM + LANES] = kn[:, c * LANES:(c + 1) * LANES].astype(BF16)
        kb_ref[0, :, c * MXU_DIM + LANES:(c + 1) * MXU_DIM] = kr


def _prep_even(x, gain, mods3, mrow, w_in, wts, tabs, use_rope, tm):
    bsz, n, d = x.shape
    cols = w_in.shape[1]
    tm = min(tm, n)
    aqg, akg, bqg, bkvg, wuq, wukn, wuv = wts
    chd, shd, cb, sb = tabs
    full = lambda a: _resident(a.shape, lambda b, i: (0,) * a.ndim)
    tab = lambda a: pl.BlockSpec((tm, LANES), (lambda b, i: (i, 0)) if use_rope else (lambda b, i: (0, 0)))
    outs = [(4 * LANES, True), (4 * LANES, False), (4 * LANES, True), (wuq.shape[1], True),
            (2 * wukn.shape[1], False), (wuv.shape[1], True)]
    kern = functools.partial(_prep_even_kernel, use_rope=use_rope, a_scale=HEAD_DIM ** -0.5 * LOG2E,
                             b_scale=(B_NOPE + B_ROPE) ** -0.5 * LOG2E)
    return pl.pallas_call(
        kern,
        grid=(bsz, n // tm),
        in_specs=[pl.BlockSpec((1, tm, d), lambda b, i: (b, i, 0)),
                  pl.BlockSpec((1, d), lambda b, i: (0, 0)),
                  _mod_spec(d, 0, mrow), _mod_spec(d, 1, mrow), full(w_in),
                  full(aqg), full(akg), full(bqg), full(bkvg), full(wuq), full(wukn), full(wuv),
                  tab(chd), tab(shd), tab(cb), tab(sb)],
        out_specs=[pl.BlockSpec((1, w, tm), lambda b, i: (b, 0, i)) if t else
                   pl.BlockSpec((1, tm, w), lambda b, i: (b, i, 0)) for w, t in outs],
        out_shape=[jax.ShapeDtypeStruct((bsz, w, n) if t else (bsz, n, w), BF16) for w, t in outs],
        scratch_shapes=[pltpu.VMEM((1, tm, cols), F32)],
        compiler_params=_params(("parallel", "parallel")),
        name="prep_even",
    )(x, gain.reshape(1, d), mods3, mods3, w_in, aqg, akg, bqg, bkvg, wuq, wukn, wuv, chd, shd, cb, sb)


def _head_lane_masks(rows, nsub):
    lane = lax.broadcasted_iota(jnp.int32, (rows, MXU_DIM), 1)
    kms, vms = [], []
    for j in range(HEADS_PER_GROUP):
        v_lo = j * HEAD_DIM
        vm = (lane >= v_lo) & (lane < v_lo + HEAD_DIM)
        if nsub == 1:
            km = vm
        else:
            jj = j % 2
            km = ((lane >= jj * B_NOPE) & (lane < (jj + 1) * B_NOPE)) | \
                 ((lane >= 2 * B_NOPE + jj * B_ROPE) & (lane < 2 * B_NOPE + (jj + 1) * B_ROPE))
        kms.append(jnp.where(km, 1.0, 0.0).astype(BF16))
        vms.append(jnp.where(vm, 1.0, 0.0).astype(BF16))
    return kms, vms


KV_UNROLL = 17


def _head_row_masks(cols):
    row = lax.broadcasted_iota(jnp.int32, (MXU_DIM, cols), 0)
    return [jnp.where((row >= j * HEAD_DIM) & (row < (j + 1) * HEAD_DIM), 1.0, 0.0).astype(BF16)
            for j in range(HEADS_PER_GROUP)]


def _rows4(vals, cols):
    return jnp.concatenate([jnp.broadcast_to(v, (HEAD_DIM, cols)) for v in vals], axis=0)


def _flash_kernel(*refs, nsub, tk, nsrc, nchain):
    qt_ref = refs[0]
    srcs = [(refs[1 + 2 * s], refs[2 + 2 * s]) for s in range(nsrc)]
    o_ref, kb_ref, vbt_ref, acc_ref = refs[1 + 2 * nsrc:]
    hpu = HEADS_PER_GROUP // nsub
    tq = qt_ref.shape[2]
    tqc = tq // nchain
    nb = kb_ref.shape[0]

    @pl.when(pl.program_id(2) == 0)
    def _build():
        kms, _ = _head_lane_masks(tk, nsub)
        vms = _head_row_masks(tk)
        base = 0
        for k_ref, vt_ref in srcs:
            nblk = k_ref.shape[1] // tk

            def body(i, carry, k_ref=k_ref, base=base):
                r0 = pl.multiple_of(i * tk, tk)
                for j in range(HEADS_PER_GROUP):
                    u = j // hpu
                    kb_ref[base + i, j * tk:(j + 1) * tk, :] = \
                        k_ref[0, pl.ds(r0, tk), u * MXU_DIM:(u + 1) * MXU_DIM] * kms[j]
                return carry

            lax.fori_loop(0, nblk, body, 0)
            for i in range(nblk):
                for j in range(HEADS_PER_GROUP):
                    vbt_ref[base + i, :, j * tk:(j + 1) * tk] = vt_ref[0, :, i * tk:(i + 1) * tk] * vms[j]
            base += nblk

    def block(i, state):
        kb, vbt = kb_ref[i], vbt_ref[i]
        new_state = []
        for h in range(nchain):
            cs = slice(h * tqc, (h + 1) * tqc)
            m, l = state[h]
            parts = [jnp.dot(kb[u * hpu * tk:(u + 1) * hpu * tk, :], qt_ref[0, u * MXU_DIM:(u + 1) * MXU_DIM, cs],
                             preferred_element_type=F32) for u in range(nsub)]
            ps, alphas, m_new, l_new = [], [], [], []
            for j in range(HEADS_PER_GROUP):
                sj = parts[j // hpu][(j % hpu) * tk:(j % hpu + 1) * tk, :]
                mj = jnp.maximum(m[j], jnp.max(sj, axis=0, keepdims=True))
                a = jnp.exp2(m[j] - mj)
                p = jnp.exp2(sj - mj)
                l_new.append(a * l[j] + jnp.sum(p, axis=0, keepdims=True))
                m_new.append(mj)
                alphas.append(a)
                ps.append(p.astype(BF16))
            pv = jnp.dot(vbt, jnp.concatenate(ps, axis=0), preferred_element_type=F32)
            acc_ref[:, cs] = acc_ref[:, cs] * _rows4(alphas, tqc) + pv
            new_state.append((m_new, l_new))
        return new_state

    def flat(st):
        return tuple(v for m, l in st for v in (*m, *l))

    def unflat(c):
        n = 2 * HEADS_PER_GROUP
        return [(list(c[h * n:h * n + HEADS_PER_GROUP]), list(c[h * n + HEADS_PER_GROUP:(h + 1) * n]))
                for h in range(nchain)]

    m0 = [jnp.full((1, tqc), NEG_INF, F32)] * HEADS_PER_GROUP
    l0 = [jnp.zeros((1, tqc), F32)] * HEADS_PER_GROUP
    acc_ref[...] = jnp.zeros(acc_ref.shape, F32)
    state = unflat(lax.fori_loop(0, nb, lambda i, c: flat(block(i, unflat(c))), flat([(m0, l0)] * nchain),
                                 unroll=min(KV_UNROLL, nb)))
    for h in range(nchain):
        cs = slice(h * tqc, (h + 1) * tqc)
        ot = acc_ref[:, cs] * _rows4([1.0 / v for v in state[h][1]], tqc)
        o_ref[0, cs, :] = ot.T.astype(BF16)


def _flash(qt, ksrcs, vtsrcs, nsub, tq, tk):
    bsz, qw, nq = qt.shape
    kw = nsub * MXU_DIM
    groups = qw // kw
    tq = min(tq, nq)
    nchain = max(tq // MXU_DIM, 1)
    in_specs = [pl.BlockSpec((1, kw, tq), lambda b, g, i: (b, g, i))]
    args = [qt]
    nb = 0
    for k, vt in zip(ksrcs, vtsrcs):
        lk = k.shape[1]
        assert lk % tk == 0
        nb += lk // tk
        in_specs += [pl.BlockSpec((1, lk, kw), lambda b, g, i: (b, 0, g)),
                     pl.BlockSpec((1, MXU_DIM, lk), lambda b, g, i: (b, g, 0))]
        args += [k, vt]
    scratch = [pltpu.VMEM((nb, HEADS_PER_GROUP * tk, MXU_DIM), BF16),
               pltpu.VMEM((nb, MXU_DIM, HEADS_PER_GROUP * tk), BF16),
               pltpu.VMEM((MXU_DIM, tq), F32)]
    return pl.pallas_call(
        functools.partial(_flash_kernel, nsub=nsub, tk=tk, nsrc=len(ksrcs), nchain=nchain),
        grid=(bsz, groups, nq // tq),
        in_specs=in_specs,
        out_specs=pl.BlockSpec((1, tq, MXU_DIM), lambda b, g, i: (b, i, g)),
        out_shape=jax.ShapeDtypeStruct((bsz, nq, groups * MXU_DIM), BF16),
        scratch_shapes=scratch,
        compiler_params=_params(("parallel", "parallel", "arbitrary")),
        name="flash_attn",
    )(*args)


HALO = 16


def _ffn_kernel(xp_ref, x_ref, xn_ref, oap_ref, oa_ref, oan_ref, obp_ref, ob_ref, obn_ref, wo_ref, gtm_ref,
                g_ref, sh_ref, sc_ref, gt_ref, wup_ref, cw_ref, wdn_ref, fg_ref,
                o_ref, h_ref, om_ref, *, final_norm):
    i = pl.program_id(1)
    last = pl.num_programs(1) - 1
    tm = x_ref.shape[1]
    nf = wdn_ref.shape[0]
    half = oa_ref.shape[2]
    rows = tm + 2 * HALO
    own = slice(HALO, HALO + tm)
    for r, (a_ref, b_ref) in ((slice(0, HALO), (oap_ref, obp_ref)), (own, (oa_ref, ob_ref)),
                              (slice(HALO + tm, rows), (oan_ref, obn_ref))):
        om_ref[r, :half] = a_ref[0]
        om_ref[r, half:] = b_ref[0]
    ym = gtm_ref[0] * jnp.dot(om_ref[...], wo_ref[...], preferred_element_type=F32)
    x1 = x_ref[0] + ym[own]
    gain, shift, scale = g_ref[...], sh_ref[0], sc_ref[0]
    keep_p = jnp.where(i > 0, 1.0, 0.0)
    keep_n = jnp.where(i < last, 1.0, 0.0)
    h_ref[0:HALO, :] = (_norm_mod(xp_ref[0] + ym[0:HALO], gain, shift, scale) * keep_p).astype(BF16)
    h_ref[own, :] = _norm_mod(x1, gain, shift, scale).astype(BF16)
    h_ref[HALO + tm:, :] = (_norm_mod(xn_ref[0] + ym[HALO + tm:], gain, shift, scale) * keep_n).astype(BF16)

    def conv(u, f, part):
        w = cw_ref[part, f]
        up = pltpu.roll(u, 1, axis=0)[HALO:HALO + tm]
        un = pltpu.roll(u, rows - 1, axis=0)[HALO:HALO + tm]
        return up * w[0:1] + u[HALO:HALO + tm] * w[1:2] + un * w[2:3] + w[3:4]

    h = h_ref[...]
    up = lambda f: (jnp.dot(h, wup_ref[0, f], preferred_element_type=F32),
                    jnp.dot(h, wup_ref[1, f], preferred_element_type=F32))
    y = None
    ug, uv = up(0)
    for f in range(nf):
        nxt = up(f + 1) if f + 1 < nf else None
        g = conv(ug, f, 0)
        v = conv(uv, f, 1)
        a = (g * (1.0 / (1.0 + jnp.exp(-g))) * v).astype(BF16)
        yf = jnp.dot(a, wdn_ref[f], preferred_element_type=F32)
        y = yf if y is None else y + yf
        if nxt is not None:
            ug, uv = nxt
    out = x1 + gt_ref[0] * y
    if final_norm:
        out = out * lax.rsqrt(jnp.mean(out * out, axis=-1, keepdims=True) + NORM_EPS) * fg_ref[...]
    o_ref[0] = out


def _mix_ffn(x, oa, ob, w_out, gain, mods3, mrow, wup, cw, wdn, final_gain, tm):
    bsz, n, d = x.shape
    half = oa.shape[2]
    tm = min(tm, n)
    hb = tm // HALO
    nh = n // HALO
    final_norm = final_gain is not None
    fg = final_gain if final_norm else gain
    prev = lambda w: pl.BlockSpec((1, HALO, w), lambda b, i: (b, jnp.maximum(i * hb - 1, 0), 0))
    main = lambda w: pl.BlockSpec((1, tm, w), lambda b, i: (b, i, 0))
    nxt = lambda w: pl.BlockSpec((1, HALO, w), lambda b, i: (b, jnp.minimum((i + 1) * hb, nh - 1), 0))
    return pl.pallas_call(
        functools.partial(_ffn_kernel, final_norm=final_norm),
        grid=(bsz, n // tm),
        in_specs=[prev(d), main(d), nxt(d), prev(half), main(half), nxt(half), prev(half), main(half), nxt(half),
                  _resident(w_out.shape, lambda b, i: (0, 0)), _mod_spec(d, 2, mrow),
                  pl.BlockSpec((1, d), lambda b, i: (0, 0)),
                  _mod_spec(d, 3, mrow), _mod_spec(d, 4, mrow), _mod_spec(d, 5, mrow),
                  _resident(wup.shape, lambda b, i: (0, 0, 0, 0)),
                  _resident(cw.shape, lambda b, i: (0, 0, 0, 0)),
                  _resident(wdn.shape, lambda b, i: (0, 0, 0)),
                  pl.BlockSpec((1, d), lambda b, i: (0, 0))],
        out_specs=main(d),
        out_shape=jax.ShapeDtypeStruct((bsz, n, d), F32),
        scratch_shapes=[pltpu.VMEM((tm + 2 * HALO, d), BF16), pltpu.VMEM((tm + 2 * HALO, 2 * half), BF16)],
        compiler_params=_params(("parallel", "parallel")),
        name="mix_ffn",
    )(x, x, x, oa, oa, oa, ob, ob, ob, w_out, mods3, gain.reshape(1, d), mods3, mods3, mods3, wup, cw, wdn,
      fg.reshape(1, d))


OD_Q, OD_C, OD_K, OD_V, OD_END = 0, 512, 2048, 2176, 2304
SUB = 8


def _prep_odd_kernel(xp_ref, x_ref, xn_ref, g_ref, sh_ref, sc_ref, win_ref, cw_ref, chd_ref, shd_ref,
                     q_ref, k_ref, v_ref, x0_ref, u_ref, h_ref, z_ref, *, scale):
    i = pl.program_id(1)
    last = pl.num_programs(1) - 1
    tm = x_ref.shape[1]
    rows = tm + 2 * HALO
    own = slice(HALO, HALO + tm)
    cwid = (OD_K - OD_C) // 3
    gain, shift, mscale = g_ref[...], sh_ref[0], sc_ref[0]
    keep_p = jnp.where(i > 0, 1.0, 0.0)
    keep_n = jnp.where(i < last, 1.0, 0.0)
    h_ref[0:HALO, :] = (_norm_mod(xp_ref[0], gain, shift, mscale) * keep_p).astype(BF16)
    h_ref[own, :] = _norm_mod(x_ref[0], gain, shift, mscale).astype(BF16)
    h_ref[HALO + tm:, :] = (_norm_mod(xn_ref[0], gain, shift, mscale) * keep_n).astype(BF16)
    z_ref[...] = jnp.dot(h_ref[...], win_ref[...], preferred_element_type=F32)
    for c in range((OD_C - OD_Q) // LANES):
        sl = slice(OD_Q + c * LANES, OD_Q + (c + 1) * LANES)
        y = _rope_lanes(z_ref[own, sl], chd_ref[...], shd_ref[...])
        q_ref[0, sl, :] = (y * scale).T.astype(BF16)
    k = _rope_lanes(z_ref[own, OD_K:OD_V], chd_ref[...], shd_ref[...])
    d0, d1 = _dup_heads(k)
    for c, dk in enumerate((d0, d0, d1, d1)):
        k_ref[0, :, c * LANES:(c + 1) * LANES] = dk.astype(BF16)
    d0, d1 = _dup_heads(z_ref[own, OD_V:OD_END])
    for c, dv in enumerate((d0, d0, d1, d1)):
        v_ref[0, c * LANES:(c + 1) * LANES, :] = dv.T.astype(BF16)

    def conv(part):
        sl = slice(OD_C + part * cwid, OD_C + (part + 1) * cwid)
        csl = slice(part * cwid, (part + 1) * cwid)
        ue = z_ref[:, sl]
        up = pltpu.roll(ue, 1, axis=0)[own]
        un = pltpu.roll(ue, rows - 1, axis=0)[own]
        return up * cw_ref[0:1, csl] + ue[own] * cw_ref[1:2, csl] + un * cw_ref[2:3, csl] + cw_ref[3:4, csl]

    x0_ref[0] = conv(0)
    u_ref[0] = conv(2) * conv(1)


def _prep_odd(x, gain, mods3, w_in, cw, chd, shd, tm):
    bsz, n, d = x.shape
    cols = w_in.shape[1]
    tm = min(tm, n)
    hb = tm // HALO
    nh = n // HALO
    cwid = (OD_K - OD_C) // 3
    row_blk = lambda w: pl.BlockSpec((1, tm, w), lambda b, i: (b, i, 0))
    col_blk = lambda w: pl.BlockSpec((1, w, tm), lambda b, i: (b, 0, i))
    slab = 4 * LANES
    return pl.pallas_call(
        functools.partial(_prep_odd_kernel, scale=HEAD_DIM ** -0.5 * LOG2E),
        grid=(bsz, n // tm),
        in_specs=[pl.BlockSpec((1, HALO, d), lambda b, i: (b, jnp.maximum(i * hb - 1, 0), 0)),
                  pl.BlockSpec((1, tm, d), lambda b, i: (b, i, 0)),
                  pl.BlockSpec((1, HALO, d), lambda b, i: (b, jnp.minimum((i + 1) * hb, nh - 1), 0)),
                  pl.BlockSpec((1, d), lambda b, i: (0, 0)),
                  _mod_spec(d, 0, None), _mod_spec(d, 1, None),
                  _resident(w_in.shape, lambda b, i: (0, 0)),
                  pl.BlockSpec(cw.shape, lambda b, i: (0, 0)),
                  pl.BlockSpec((tm, LANES), lambda b, i: (i, 0)),
                  pl.BlockSpec((tm, LANES), lambda b, i: (i, 0))],
        out_specs=[col_blk(slab), row_blk(slab), col_blk(slab), row_blk(cwid), row_blk(cwid)],
        out_shape=[jax.ShapeDtypeStruct((bsz, slab, n), BF16), jax.ShapeDtypeStruct((bsz, n, slab), BF16),
                   jax.ShapeDtypeStruct((bsz, slab, n), BF16), jax.ShapeDtypeStruct((bsz, n, cwid), F32),
                   jax.ShapeDtypeStruct((bsz, n, cwid), F32)],
        scratch_shapes=[pltpu.VMEM((tm + 2 * HALO, d), BF16), pltpu.VMEM((tm + 2 * HALO, cols), F32)],
        compiler_params=_params(("parallel", "parallel")),
        name="prep_odd",
    )(x, x, x, gain.reshape(1, d), mods3, mods3, w_in, cw, chd, shd)


def _kv_ctx_odd_kernel(z_ref, k_ref, vt_ref):
    d0, d1 = _dup_heads(z_ref[0, :, 0:LANES])
    for c, dk in enumerate((d0, d0, d1, d1)):
        k_ref[0, :, c * LANES:(c + 1) * LANES] = dk.astype(BF16)
    d0, d1 = _dup_heads(z_ref[0, :, LANES:2 * LANES])
    for c, dv in enumerate((d0, d0, d1, d1)):
        vt_ref[0, c * LANES:(c + 1) * LANES, :] = dv.T.astype(BF16)


def _kv_ctx_odd(z):
    bsz, n, cols = z.shape
    slab = 4 * LANES
    return pl.pallas_call(
        _kv_ctx_odd_kernel,
        grid=(bsz,),
        in_specs=[pl.BlockSpec((1, n, cols), lambda b: (b, 0, 0))],
        out_specs=[pl.BlockSpec((1, n, slab), lambda b: (b, 0, 0)), pl.BlockSpec((1, slab, n), lambda b: (b, 0, 0))],
        out_shape=[jax.ShapeDtypeStruct((bsz, n, slab), BF16), jax.ShapeDtypeStruct((bsz, slab, n), BF16)],
        compiler_params=_params(("parallel",)),
        name="kv_ctx_odd",
    )(z)


WIN_QB = 2


def _window_kernel(qt_ref, kc_ref, vct_ref, kl_ref, vlt_ref, bias_ref, sink_ref, o_ref, kbc, vbct, kbl, vblt,
                   *, lc, nblk, qb, nch):
    qi = pl.program_id(2)
    tq = qb * Q_BLOCK
    nsp = qb + 2
    blk = HEADS_PER_GROUP * Q_BLOCK

    @pl.when(qi == 0)
    def _build():
        kms, _ = _head_lane_masks(lc, 1)
        vms = _head_row_masks(lc)
        for j in range(HEADS_PER_GROUP):
            kbc[j * lc:(j + 1) * lc, :] = kc_ref[0] * kms[j]
            vbct[:, j * lc:(j + 1) * lc] = vct_ref[0] * vms[j]
        kms, _ = _head_lane_masks(Q_BLOCK, 1)
        vms = _head_row_masks(Q_BLOCK)
        for e in (0, nblk + 1):
            kbl[e] = jnp.zeros((blk, MXU_DIM), BF16)
            vblt[e] = jnp.zeros((MXU_DIM, blk), BF16)

        def body(i, carry):
            r0 = pl.multiple_of(i * Q_BLOCK, Q_BLOCK)
            for j in range(HEADS_PER_GROUP):
                kbl[i + 1, j * Q_BLOCK:(j + 1) * Q_BLOCK, :] = kl_ref[0, pl.ds(r0, Q_BLOCK), :] * kms[j]
            return carry

        lax.fori_loop(0, nblk, body, 0)
        for i in range(nblk):
            for j in range(HEADS_PER_GROUP):
                vblt[i + 1, :, j * Q_BLOCK:(j + 1) * Q_BLOCK] = vlt_ref[0, :, i * Q_BLOCK:(i + 1) * Q_BLOCK] * vms[j]

    ntile = nblk // qb
    for h in range(nch):
        t = qi * nch + h
        qt = qt_ref[0, :, h * tq:(h + 1) * tq]
        s_c = jnp.dot(kbc[...], qt, preferred_element_type=F32)
        kspan = kbl[pl.ds(qb * t, nsp)].reshape(nsp * blk, MXU_DIM)
        s_s = jnp.dot(kspan, qt, preferred_element_type=F32) + bias_ref[...]
        edge = {0: jnp.where(t == 0, NEG_INF, 0.0), nsp - 1: jnp.where(t == ntile - 1, NEG_INF, 0.0)}
        pcs, pss, invs = [], [[None] * HEADS_PER_GROUP for _ in range(nsp)], []
        for j in range(HEADS_PER_GROUP):
            sink = sink_ref[0, j][0:1, :]
            segs = [s_c[j * lc:(j + 1) * lc, :]]
            for b in range(nsp):
                sg = s_s[b * blk + j * Q_BLOCK: b * blk + (j + 1) * Q_BLOCK, :]
                segs.append(sg + edge[b] if b in edge else sg)
            m = sink
            for sg in segs:
                m = jnp.maximum(m, jnp.max(sg, axis=0, keepdims=True))
            ps = [jnp.exp2(sg - m) for sg in segs]
            den = jnp.exp2(sink - m)
            for p in ps:
                den = den + jnp.sum(p, axis=0, keepdims=True)
            invs.append(1.0 / den)
            pcs.append(ps[0].astype(BF16))
            for b in range(nsp):
                pss[b][j] = ps[1 + b].astype(BF16)
        ot = jnp.dot(vbct[...], jnp.concatenate(pcs, axis=0), preferred_element_type=F32)
        for b in range(nsp):
            ot = ot + jnp.dot(vblt[qb * t + b], jnp.concatenate(pss[b], axis=0), preferred_element_type=F32)
        o_ref[0, h * tq:(h + 1) * tq, :] = (ot * _rows4(invs, tq)).T.astype(BF16)


def _window_bias(qb):
    nsp = qb + 2
    row = jnp.arange(nsp * HEADS_PER_GROUP * Q_BLOCK)[:, None]
    r = jnp.arange(qb * Q_BLOCK)[None, :]
    b = row // (HEADS_PER_GROUP * Q_BLOCK)
    jj = b * Q_BLOCK + row % Q_BLOCK
    band = (jj >= r) & (jj <= r + 2 * WINDOW)
    return jnp.where(band, 0.0, NEG_INF).astype(F32)


def _window(qt, kc, vct, kl, vlt, sink):
    bsz, qw, n = qt.shape
    groups = qw // MXU_DIM
    lc = kc.shape[1]
    nblk = n // Q_BLOCK
    qb = min(WIN_QB, nblk)
    tq = qb * Q_BLOCK
    nch = 2 if n % (2 * tq) == 0 else 1
    nq = n // (tq * nch)
    blk = HEADS_PER_GROUP * Q_BLOCK
    bias = _window_bias(qb)
    sink_t = jnp.broadcast_to((sink.astype(F32) * LOG2E).reshape(groups, HEADS_PER_GROUP, 1, 1),
                              (groups, HEADS_PER_GROUP, SUB, tq))

    return pl.pallas_call(
        functools.partial(_window_kernel, lc=lc, nblk=nblk, qb=qb, nch=nch),
        grid=(bsz, groups, nq),
        in_specs=[pl.BlockSpec((1, MXU_DIM, tq * nch), lambda b, g, i: (b, g, i)),
                  pl.BlockSpec((1, lc, MXU_DIM), lambda b, g, i: (b, 0, g)),
                  pl.BlockSpec((1, MXU_DIM, lc), lambda b, g, i: (b, g, 0)),
                  pl.BlockSpec((1, n, MXU_DIM), lambda b, g, i: (b, 0, g)),
                  pl.BlockSpec((1, MXU_DIM, n), lambda b, g, i: (b, g, 0)),
                  _resident(bias.shape, lambda b, g, i: (0, 0)),
                  pl.BlockSpec((1, HEADS_PER_GROUP, SUB, tq), lambda b, g, i: (g, 0, 0, 0))],
        out_specs=pl.BlockSpec((1, tq * nch, MXU_DIM), lambda b, g, i: (b, i, g)),
        out_shape=jax.ShapeDtypeStruct((bsz, n, qw), BF16),
        scratch_shapes=[pltpu.VMEM((HEADS_PER_GROUP * lc, MXU_DIM), BF16),
                        pltpu.VMEM((MXU_DIM, HEADS_PER_GROUP * lc), BF16),
                        pltpu.VMEM((nblk + 2, blk, MXU_DIM), BF16),
                        pltpu.VMEM((nblk + 2, MXU_DIM, blk), BF16)],
        compiler_params=_params(("parallel", "parallel", "arbitrary")),
        name="window_attn",
    )(qt, kc, vct, kl, vlt, bias, sink_t)


FFT_N2 = 128


def _filter_kernel(z_ref, w1_ref, b1_ref, w2_ref, b2_ref, w3_ref, b3_ref, w4_ref, fr_ref, dl_ref, o_ref):
    def lin(a, w_ref):
        a_hi, a_lo = _split_bf16(a)
        w_hi, w_lo = _split_bf16(w_ref[...])
        return _dot3(a_hi, a_lo, w_hi, w_lo)

    z = z_ref[...]
    h = b1_ref[...]
    for e in range(C_EMB_DIM):
        h = h + z[:, e:e + 1] * w1_ref[e:e + 1, :]
    h = jnp.sin(fr_ref[0:1] * h)
    h = jnp.sin(fr_ref[1:2] * (lin(h, w2_ref) + b2_ref[...]))
    h = jnp.sin(fr_ref[2:3] * (lin(h, w3_ref) + b3_ref[...]))
    h = lin(h, w4_ref)
    t = z[:, 0:1]
    cw = dl_ref.shape[1]
    dec = jnp.exp(-t * dl_ref[...])
    row = lax.broadcasted_iota(jnp.int32, dec.shape, 0) + pl.program_id(0) * z.shape[0]
    o_ref[0] = h[:, :cw] * dec
    o_ref[1] = jnp.where(row == 0, 0.0, h[:, cw:] * dec)


def _filters(zfeat, w1, b1, w2, b2, w3, b3, w4, freq, deltas, tm):
    n = zfeat.shape[0]
    tm = min(tm, n)
    cw = deltas.shape[1]
    full = lambda a: pl.BlockSpec(a.shape, lambda i: (0,) * a.ndim)
    ops = (w1, b1, w2, b2, w3, b3, w4, freq, deltas)
    return pl.pallas_call(
        _filter_kernel,
        grid=(n // tm,),
        in_specs=[pl.BlockSpec((tm, zfeat.shape[1]), lambda i: (i, 0))] + [full(a) for a in ops],
        out_specs=pl.BlockSpec((2, tm, cw), lambda i: (0, i, 0)),
        out_shape=jax.ShapeDtypeStruct((2, n, cw), F32),
        compiler_params=_params(("parallel",)),
        name="hyena_filters",
    )(zfeat, *ops)


def _dft_first_kernel(f_ref, x_ref, o_ref):
    _, n1h, tm2, cw = x_ref.shape
    x = x_ref[0].reshape(n1h * tm2, cw).astype(BF16)
    r = jnp.dot(f_ref[...], x, preferred_element_type=F32)
    o_ref[0] = r.astype(BF16).reshape(o_ref.shape[1:])


def _dft_first(f1, x):
    bsz, n1h, n2, cw = x.shape
    tm2 = HALO
    nkp = f1.shape[0] // (2 * tm2)
    return pl.pallas_call(
        _dft_first_kernel,
        grid=(bsz, n2 // tm2),
        in_specs=[pl.BlockSpec(f1.shape, lambda b, j: (0, 0)),
                  pl.BlockSpec((1, n1h, tm2, cw), lambda b, j: (b, 0, j, 0))],
        out_specs=pl.BlockSpec((1, 2, nkp, tm2, cw), lambda b, j: (b, 0, 0, j, 0)),
        out_shape=jax.ShapeDtypeStruct((bsz, 2, nkp, n2, cw), BF16),
        compiler_params=_params(("parallel", "parallel")),
        name="dft_first",
    )(f1, x)


def _dft_last_gate_kernel(f_ref, b_ref, x0_ref, u_ref, bias_ref, o_ref):
    _, _, nkp, tm2, cw = b_ref.shape
    y = jnp.dot(f_ref[...], b_ref[0].reshape(2 * nkp * tm2, cw), preferred_element_type=F32)
    y = y.reshape(x0_ref.shape[1:])
    o_ref[0] = (x0_ref[0] * (y + u_ref[0] * bias_ref[...])).astype(BF16)


def _dft_last_gate(f2, b, x0, u, bias):
    bsz, _, nkp, n2, cw = b.shape
    tm2 = HALO
    n1h = f2.shape[0] // tm2
    tblk = pl.BlockSpec((1, n1h, tm2, cw), lambda b_, j: (b_, 0, j, 0))
    return pl.pallas_call(
        _dft_last_gate_kernel,
        grid=(bsz, n2 // tm2),
        in_specs=[pl.BlockSpec(f2.shape, lambda b_, j: (0, 0)),
                  pl.BlockSpec((1, 2, nkp, tm2, cw), lambda b_, j: (b_, 0, 0, j, 0)),
                  tblk, tblk, pl.BlockSpec((1, cw), lambda b_, j: (0, 0))],
        out_specs=tblk,
        out_shape=jax.ShapeDtypeStruct((bsz, n1h, n2, cw), BF16),
        compiler_params=_params(("parallel", "parallel")),
        name="dft_last_gate",
    )(f2, b, x0, u, bias.reshape(1, cw))


def _spec_fwd(a, m_ref):
    return jnp.dot(m_ref[0], a.reshape(2 * FFT_N2, a.shape[-1]), preferred_element_type=F32)


def _filter_spec_kernel(a_ref, m_ref, o_ref):
    xf = _spec_fwd(a_ref[0, :, 0], m_ref)
    xb = _spec_fwd(a_ref[1, :, 0], m_ref)
    o_ref[0, 0] = xf[:FFT_N2] + xb[:FFT_N2]
    o_ref[1, 0] = xf[FFT_N2:] - xb[FFT_N2:]


def _conv_spec_kernel(a_ref, k_ref, m_ref, i_ref, o_ref):
    kr, ki = k_ref[0, 0], k_ref[1, 0]
    for b in range(a_ref.shape[0]):
        x = _spec_fwd(a_ref[b, :, 0], m_ref)
        xr, xi = x[:FFT_N2], x[FFT_N2:]
        y = jnp.concatenate([xr * kr - xi * ki, xr * ki + xi * kr], axis=0).astype(BF16)
        bv = jnp.dot(i_ref[0], y, preferred_element_type=F32)
        o_ref[b, :, 0] = bv.astype(BF16).reshape(2, FFT_N2, bv.shape[-1])


def _filter_spec(a, m_fwd):
    _, _, nkp, _, cw = a.shape
    return pl.pallas_call(
        _filter_spec_kernel,
        grid=(nkp,),
        in_specs=[pl.BlockSpec((2, 2, 1, FFT_N2, cw), lambda k: (0, 0, k, 0, 0)),
                  pl.BlockSpec((1, 2 * FFT_N2, 2 * FFT_N2), lambda k: (k, 0, 0))],
        out_specs=pl.BlockSpec((2, 1, FFT_N2, cw), lambda k: (0, k, 0, 0)),
        out_shape=jax.ShapeDtypeStruct((2, nkp, FFT_N2, cw), F32),
        compiler_params=_params(("parallel",)),
        name="filter_spectrum",
    )(a, m_fwd)


def _conv_spec(a, kspec, m_fwd, m_inv):
    bsz, _, nkp, _, cw = a.shape
    blk = pl.BlockSpec((bsz, 2, 1, FFT_N2, cw), lambda k: (0, 0, k, 0, 0))
    mat = pl.BlockSpec((1, 2 * FFT_N2, 2 * FFT_N2), lambda k: (k, 0, 0))
    return pl.pallas_call(
        _conv_spec_kernel,
        grid=(nkp,),
        in_specs=[blk, pl.BlockSpec((2, 1, FFT_N2, cw), lambda k: (0, k, 0, 0)), mat, mat],
        out_specs=blk,
        out_shape=jax.ShapeDtypeStruct(a.shape, BF16),
        compiler_params=_params(("parallel",)),
        name="conv_spectrum",
    )(a, kspec, m_fwd, m_inv)


def _dft_tables(n):
    nfft = 2 * n
    n2 = FFT_N2
    n1 = nfft // n2
    n1h = n1 // 2
    nk = n1h + 1
    nkp = -(-nk // SUB) * SUB
    two_pi = 2.0 * math.pi

    def cs(num, den):
        ang = (num % den).astype(F32) * (two_pi / den)
        return jnp.cos(ang), jnp.sin(ang)

    def pad_k(t, axis):
        widths = [(0, 0)] * t.ndim
        widths[axis] = (0, nkp - nk)
        return jnp.pad(t, widths)

    k1 = jnp.arange(nk, dtype=jnp.int32)
    c1, s1 = cs(k1[:, None] * jnp.arange(n1h, dtype=jnp.int32)[None, :], n1)
    f1 = jnp.concatenate([pad_k(c1, 0), pad_k(-s1, 0)], axis=0)
    kk = k1[:, None, None] + n1 * jnp.arange(n2, dtype=jnp.int32)[None, :, None]
    cg, sg = cs(kk * jnp.arange(n2, dtype=jnp.int32)[None, None, :], nfft)
    g_re, g_im = cg, -sg
    m_fwd = jnp.concatenate([jnp.concatenate([g_re, -g_im], axis=2),
                             jnp.concatenate([g_im, g_re], axis=2)], axis=1)
    gt_re, gt_im = jnp.swapaxes(g_re, 1, 2), jnp.swapaxes(g_im, 1, 2)
    m_inv = jnp.concatenate([jnp.concatenate([gt_re, gt_im], axis=2),
                             jnp.concatenate([-gt_im, gt_re], axis=2)], axis=1)
    wk = jnp.where((k1 == 0) | (k1 == n1h), 1.0, 2.0)[None, :] * (1.0 / nfft)
    c2, s2 = cs(jnp.arange(n1h, dtype=jnp.int32)[:, None] * k1[None, :], n1)
    f2 = jnp.concatenate([pad_k(c2 * wk, 1), pad_k(-s2 * wk, 1)], axis=1)
    eye = jnp.eye(HALO, dtype=F32)
    return tuple(t.astype(BF16) for t in (jnp.kron(f1, eye), pad_k(m_fwd, 0), pad_k(m_inv, 0), jnp.kron(f2, eye)))


def _hyena_long_conv(u, x0, bias, hcat, tabs):
    f1, m_fwd, m_inv, f2 = tabs
    bsz, n, cw = u.shape
    n2 = FFT_N2
    n1h = n // n2
    kspec = _filter_spec(_dft_first(f1, hcat.reshape(2, n1h, n2, cw)), m_fwd)
    u4 = u.reshape(bsz, n1h, n2, cw)
    b_u = _conv_spec(_dft_first(f1, u4), kspec, m_fwd, m_inv)
    return _dft_last_gate(f2, b_u, x0.reshape(bsz, n1h, n2, cw), u4, bias).reshape(bsz, n, cw)


def _axial_angles(rows, rope_dim):
    row_idx = jnp.repeat(jnp.arange(rows), GRID_W).astype(F32)
    col_idx = jnp.tile(jnp.arange(GRID_W), rows).astype(F32)
    d_axis = rope_dim // 2
    inv_freq = ROPE_THETA ** (-jnp.arange(0, d_axis, 2, dtype=F32) / d_axis)
    ang = jnp.concatenate([row_idx[:, None] * inv_freq, col_idx[:, None] * inv_freq], axis=-1)
    return jnp.cos(ang), jnp.sin(ang)


def _rope_tables(n):
    rows = n // GRID_W
    sign = jnp.tile(jnp.array([-1.0, 1.0], F32), LANES // 2)[None, :]
    cos, sin = _axial_angles(rows, HEAD_DIM)
    chd = jnp.tile(jnp.repeat(cos, 2, axis=1), (1, LANES // HEAD_DIM))
    shd = jnp.tile(jnp.repeat(sin, 2, axis=1), (1, LANES // HEAD_DIM)) * sign
    cos, sin = _axial_angles(rows, B_ROPE)
    ones = jnp.ones((n, LANES - 2 * B_ROPE), F32)
    cb = jnp.concatenate([jnp.tile(jnp.repeat(cos, 2, axis=1), (1, 2)), ones], axis=1)
    sb = jnp.concatenate([jnp.tile(jnp.repeat(sin, 2, axis=1), (1, 2)), 0.0 * ones], axis=1) * sign
    return chd, shd, cb, sb


def _even_weights(w_in, a_qn, a_kn, b_qn, b_w_uq, b_kvn, b_w_ukv):
    d = w_in.shape[0]
    aq_w = w_in.shape[1] - (B_Q_RANK + 2 * A_KV_HEADS * HEAD_DIM + B_KV_RANK + B_ROPE)
    o = [0, aq_w, aq_w + B_Q_RANK]
    o += [o[-1] + A_KV_HEADS * HEAD_DIM, o[-1] + 2 * A_KV_HEADS * HEAD_DIM]
    o += [o[-1] + B_KV_RANK, o[-1] + B_KV_RANK + B_ROPE]
    kr = w_in[:, o[5]:o[6]]
    w_aug = jnp.concatenate([w_in[:, :o[5]], kr, kr, jnp.zeros((d, LANES - 2 * B_ROPE), w_in.dtype)], axis=1)
    heads = b_w_uq.shape[1] // (B_NOPE + B_ROPE)
    uq = b_w_uq.reshape(B_Q_RANK, heads // 2, 2, B_NOPE + B_ROPE)
    wuq = jnp.concatenate([uq[:, :, 0, :B_NOPE], uq[:, :, 1, :B_NOPE], uq[:, :, 0, B_NOPE:], uq[:, :, 1, B_NOPE:],
                           jnp.zeros((B_Q_RANK, heads // 2, MXU_DIM - 2 * (B_NOPE + B_ROPE)), b_w_uq.dtype)], axis=2)
    wuq = wuq.reshape(B_Q_RANK, heads // 2 * MXU_DIM)
    ukv = b_w_ukv.reshape(B_KV_RANK, heads, B_NOPE + B_VDIM)
    wukn = ukv[:, :, :B_NOPE].reshape(B_KV_RANK, heads * B_NOPE)
    wuv = ukv[:, :, B_NOPE:].reshape(B_KV_RANK, heads * B_VDIM)
    gains = (jnp.tile(a_qn, aq_w // HEAD_DIM)[None, :], jnp.tile(a_kn, A_KV_HEADS)[None, :],
             b_qn[None, :], b_kvn[None, :])
    return w_aug.astype(BF16), gains + (wuq.astype(BF16), wukn.astype(BF16), wuv.astype(BF16))


def _ffn_weights(w_up, conv_w, conv_b, w_down, nf):
    d, f2 = w_up.shape
    f = f2 // 2
    tf = f // nf
    wup = w_up.reshape(d, 2, nf, tf).transpose(1, 2, 0, 3).astype(BF16)
    cw = jnp.concatenate([conv_w, conv_b[None, :], jnp.zeros((SUB - 4, f2), conv_w.dtype)], axis=0)
    cw = cw.reshape(SUB, 2, nf, tf).transpose(1, 2, 0, 3)
    wdn = w_down.reshape(nf, tf, d).astype(BF16)
    return wup, cw, wdn


FFN_CHUNKS = 1


def kernel(x, c, ctx, c_ctx, w_mod, b_mod, norm_mix, norm_ffn, ev_w_in, ev_w_out, a_q_norm, a_k_norm, b_q_norm, b_w_uq, b_kv_norm, b_w_ukv, od_w_in, od_w_out, d_sink, c_conv_w, c_conv_b, c_filt_w1, c_filt_b1, c_filt_w2, c_filt_b2, c_filt_w3, c_filt_b3, c_filt_w4, c_filt_freq, c_bias, ffn_w_up, ffn_conv_w, ffn_conv_b, ffn_w_down, final_norm):
    bsz, n, d = x.shape
    depth = w_mod.shape[0]
    assert depth == 2 and n % Q_BLOCK == 0 and d % LANES == 0
    tm = 512
    rows = -(-(bsz + 1) // SUB) * SUB
    cvec = jnp.concatenate([c, c_ctx[None, :], jnp.zeros((rows - bsz - 1, d), F32)], axis=0)
    mods = _mods(cvec, w_mod, b_mod)
    chd, shd, cb, sb = _rope_tables(n)
    tabs = (chd, shd, cb, sb)

    m3 = mods[0].reshape(rows, 1, N_MOD * d)
    w_aug, prep_w = _even_weights(ev_w_in[0], a_q_norm[0], a_k_norm[0], b_q_norm[0], b_w_uq[0],
                                  b_kv_norm[0], b_w_ukv[0])
    qa_l, ka_l, va_l, qb_l, kb_l, vb_l = _prep_even(x, norm_mix[0], m3, None, w_aug, prep_w, tabs, True, tm)
    qa_c, ka_c, va_c, qb_c, kb_c, vb_c = _prep_even(ctx, norm_mix[0], m3, bsz, w_aug, prep_w, tabs, False, tm)
    tq, tk = 1024, 256
    oa_l = _flash(qa_l, (ka_c, ka_l), (va_c, va_l), 1, tq, tk)
    ob_l = _flash(qb_l, (kb_c, kb_l), (vb_c, vb_l), 2, tq, tk)
    oa_c = _flash(qa_c, (ka_c,), (va_c,), 1, tq, tk)
    ob_c = _flash(qb_c, (kb_c,), (vb_c,), 2, tq, tk)
    w_out = ev_w_out[0].astype(BF16)
    ffn_w = _ffn_weights(ffn_w_up[0], ffn_conv_w[0], ffn_conv_b[0], ffn_w_down[0], FFN_CHUNKS)
    x = _mix_ffn(x, oa_l, ob_l, w_out, norm_ffn[0], m3, None, *ffn_w, None, tm)
    ctx = _mix_ffn(ctx, oa_c, ob_c, w_out, norm_ffn[0], m3, bsz, *ffn_w, None, tm)

    m3 = mods[1].reshape(rows, 1, N_MOD * d)
    w_in = od_w_in[0].astype(BF16)
    z_c = _inproj(ctx, norm_mix[1], m3, bsz, w_in[:, OD_K:], tm)
    cw = jnp.concatenate([c_conv_w[0], c_conv_b[0][None, :], jnp.zeros((SUB - 4, OD_K - OD_C), F32)], axis=0)
    qd, kd, vd, x0, u = _prep_odd(x, norm_mix[1], m3, w_in, cw, chd, shd, tm)
    kd_c, vd_c = _kv_ctx_odd(z_c)
    od = _window(qd, kd_c, vd_c, kd, vd, d_sink[0])
    t = jnp.linspace(0.0, 1.0, n, dtype=F32)[:, None]
    wpos = 2 * math.pi * jnp.arange(n, dtype=F32)[:, None] / n
    fb = jnp.linspace(1e-4, C_BANDS - 1, C_BANDS, dtype=F32)[None, :]
    zfeat = jnp.concatenate([t, jnp.cos(fb * wpos), -jnp.sin(fb * wpos), jnp.zeros((n, SUB - C_EMB_DIM), F32)], axis=-1)
    cwid = c_bias.shape[1]
    deltas = jnp.abs(jnp.linspace(C_MIN_DECAY, C_MAX_DECAY, cwid, dtype=F32))[None, :]
    w1 = jnp.concatenate([c_filt_w1[0], jnp.zeros((SUB - C_EMB_DIM, c_filt_w1.shape[2]), F32)], axis=0)
    hcat = _filters(zfeat, w1, c_filt_b1[0][None, :], c_filt_w2[0], c_filt_b2[0][None, :], c_filt_w3[0],
                    c_filt_b3[0][None, :], c_filt_w4[0], jnp.concatenate([c_filt_freq[0], jnp.zeros((SUB - 3, c_filt_freq.shape[2]), F32)], axis=0),
                    deltas, tm)
    oc = _hyena_long_conv(u, x0, c_bias[0], hcat, _dft_tables(n))
    ffn_w = _ffn_weights(ffn_w_up[1], ffn_conv_w[1], ffn_conv_b[1], ffn_w_down[1], FFN_CHUNKS)
    return _mix_ffn(x, od, oc, od_w_out[0].astype(BF16), norm_ffn[1], m3, None, *ffn_w, final_norm, tm)
```

```python
import functools
import math

import jax
import jax.numpy as jnp
from jax import lax
from jax.experimental import pallas as pl
from jax.experimental.pallas import tpu as pltpu

F32 = jnp.float32
BF16 = jnp.bfloat16

GRID_W = 64
HEAD_DIM = 64
ROPE_THETA = 10000.0
NORM_EPS = 1e-6
NEG_INF = -1e30
N_MOD = 6
A_KV_HEADS = 2
B_NOPE = 64
B_ROPE = 32
B_VDIM = 64
B_Q_RANK = 384
B_KV_RANK = 256
C_EMB_DIM = 5
C_BANDS = (C_EMB_DIM - 1) // 2
C_MIN_DECAY = math.log(1e-2) / 1.5
C_MAX_DECAY = math.log(1e-2) / 0.3
WINDOW = 128
Q_BLOCK = 128
LOG2E = math.log2(math.e)

LANES = 128
MXU_DIM = 256
VMEM_LIMIT = 56 * 1024 * 1024
HEADS_PER_GROUP = MXU_DIM // HEAD_DIM

ROW_TILE = 512
ATTN_TQ = 1024
ATTN_TK = 256


def _params(sem, vmem=VMEM_LIMIT):
    return pltpu.CompilerParams(dimension_semantics=sem, vmem_limit_bytes=vmem)


def _resident(shape, index_map):
    return pl.BlockSpec(shape, index_map, pipeline_mode=pl.Buffered(1))


def _norm_mod(x, gain, shift, scale):
    inv = lax.rsqrt(jnp.mean(x * x, axis=-1, keepdims=True) + NORM_EPS)
    return (x * inv) * gain * (1.0 + scale) + shift


def _rope_lanes(x, cos, sin_signed):
    lane = lax.broadcasted_iota(jnp.int32, x.shape, 1)
    nxt = pltpu.roll(x, LANES - 1, axis=1)
    prv = pltpu.roll(x, 1, axis=1)
    swapped = jnp.where(lane % 2 == 0, nxt, prv)
    return x * cos + swapped * sin_signed


def _head_rmsnorm_lanes(x, gain):
    lane = lax.broadcasted_iota(jnp.int32, x.shape, 1)
    lo = lane < HEAD_DIM
    sq = x * x
    s_lo = jnp.sum(jnp.where(lo, sq, 0.0), axis=-1, keepdims=True)
    s_hi = jnp.sum(jnp.where(lo, 0.0, sq), axis=-1, keepdims=True)
    ms = jnp.where(lo, s_lo, s_hi) * (1.0 / HEAD_DIM)
    return x * lax.rsqrt(ms + NORM_EPS) * gain


def _dup_heads(x):
    lane = lax.broadcasted_iota(jnp.int32, x.shape, 1)
    lo = lane < HEAD_DIM
    r = pltpu.roll(x, HEAD_DIM, axis=1)
    return jnp.where(lo, x, r), jnp.where(lo, r, x)


def _split_bf16(x):
    hi = x.astype(BF16)
    lo = (x - hi.astype(F32)).astype(BF16)
    return hi, lo


def _dot3(a_hi, a_lo, b_hi, b_lo):
    d = functools.partial(jnp.dot, preferred_element_type=F32)
    return d(a_hi, b_hi) + d(a_hi, b_lo) + d(a_lo, b_hi)


def _mods_kernel(c_ref, w_ref, b_ref, o_ref):
    c = c_ref[...]
    s = c * (1.0 / (1.0 + jnp.exp(-c)))
    s_hi, s_lo = _split_bf16(s)
    w_hi, w_lo = _split_bf16(w_ref[0])
    o_ref[0] = _dot3(s_hi, s_lo, w_hi, w_lo) + b_ref[0]


def _mods(cvec, w_mod, b_mod):
    depth, d, n = w_mod.shape
    rows = cvec.shape[0]
    tn = 1536
    return pl.pallas_call(
        _mods_kernel,
        grid=(depth, n // tn),
        in_specs=[pl.BlockSpec((rows, d), lambda l, j: (0, 0)),
                  pl.BlockSpec((1, d, tn), lambda l, j: (l, 0, j)),
                  pl.BlockSpec((1, 1, tn), lambda l, j: (l, 0, j))],
        out_specs=pl.BlockSpec((1, rows, tn), lambda l, j: (l, 0, j)),
        out_shape=jax.ShapeDtypeStruct((depth, rows, n), F32),
        compiler_params=_params(("arbitrary", "arbitrary")),
        name="mods",
    )(cvec, w_mod, b_mod.reshape(depth, 1, n))


def _mod_spec(d, chunk, row):
    if row is None:
        return pl.BlockSpec((1, 1, d), lambda b, *_: (b, 0, chunk))
    return pl.BlockSpec((1, 1, d), lambda b, *_: (row, 0, chunk))


def _inproj_kernel(x_ref, g_ref, sh_ref, sc_ref, w_ref, o_ref):
    h = _norm_mod(x_ref[0], g_ref[...], sh_ref[0], sc_ref[0])
    o_ref[0] = jnp.dot(h.astype(BF16), w_ref[...], preferred_element_type=F32)


def _inproj(x, gain, mods3, mrow, w, tm):
    bsz, n, d = x.shape
    cols = w.shape[1]
    tm = min(tm, n)
    return pl.pallas_call(
        _inproj_kernel,
        grid=(bsz, n // tm),
        in_specs=[pl.BlockSpec((1, tm, d), lambda b, i: (b, i, 0)),
                  pl.BlockSpec((1, d), lambda b, i: (0, 0)),
                  _mod_spec(d, 0, mrow), _mod_spec(d, 1, mrow),
                  _resident((d, cols), lambda b, i: (0, 0))],
        out_specs=pl.BlockSpec((1, tm, cols), lambda b, i: (b, i, 0)),
        out_shape=jax.ShapeDtypeStruct((bsz, n, cols), F32),
        compiler_params=_params(("parallel", "parallel")),
        name="inproj",
    )(x, gain.reshape(1, d), mods3, mods3, w)


EV_AQ, EV_BQ, EV_AK, EV_AV, EV_BKV, EV_BKR, EV_END = 0, 512, 896, 1024, 1152, 1408, 1536


def _prep_even_kernel(x_ref, g_ref, sh_ref, sc_ref, win_ref,
                      aqg_ref, akg_ref, bqg_ref, bkvg_ref, wuq_ref, wukn_ref, wuv_ref,
                      chd_ref, shd_ref, cb_ref, sb_ref,
                      qa_ref, ka_ref, va_ref, qb_ref, kb_ref, vb_ref, z_ref, *, use_rope, a_scale, b_scale):
    h = _norm_mod(x_ref[0], g_ref[...], sh_ref[0], sc_ref[0])
    z_ref[0] = jnp.dot(h.astype(BF16), win_ref[...], preferred_element_type=F32)

    def rope_hd(y):
        return _rope_lanes(y, chd_ref[...], shd_ref[...]) if use_rope else y

    def rope_b(y):
        return _rope_lanes(y, cb_ref[...], sb_ref[...]) if use_rope else y

    def put_t(dst, c, y):
        dst[0, c * LANES:(c + 1) * LANES, :] = y.T.astype(BF16)

    for c in range((EV_BQ - EV_AQ) // LANES):
        sl = slice(EV_AQ + c * LANES, EV_AQ + (c + 1) * LANES)
        y = rope_hd(_head_rmsnorm_lanes(z_ref[0, :, sl], aqg_ref[:, c * LANES:(c + 1) * LANES]))
        put_t(qa_ref, c, y * a_scale)
    k = rope_hd(_head_rmsnorm_lanes(z_ref[0, :, EV_AK:EV_AV], akg_ref[...]))
    d0, d1 = _dup_heads(k)
    d0 = d0.astype(BF16)
    d1 = d1.astype(BF16)
    ka_ref[0, :, 0 * LANES:1 * LANES] = d0
    ka_ref[0, :, 1 * LANES:2 * LANES] = d0
    ka_ref[0, :, 2 * LANES:3 * LANES] = d1
    ka_ref[0, :, 3 * LANES:4 * LANES] = d1
    d0, d1 = _dup_heads(z_ref[0, :, EV_AV:EV_BKV])
    for c, dv in enumerate((d0, d0, d1, d1)):
        put_t(va_ref, c, dv)
    cq = z_ref[0, :, EV_BQ:EV_AK]
    cq = cq * lax.rsqrt(jnp.mean(cq * cq, axis=-1, keepdims=True) + NORM_EPS) * bqg_ref[...]
    qb = jnp.dot(cq.astype(BF16), wuq_ref[...], preferred_element_type=F32)
    for c in range(qb.shape[1] // MXU_DIM):
        lo = slice(c * MXU_DIM, c * MXU_DIM + LANES)
        hi = slice(c * MXU_DIM + LANES, (c + 1) * MXU_DIM)
        put_t(qb_ref, 2 * c, qb[:, lo] * b_scale)
        put_t(qb_ref, 2 * c + 1, rope_b(qb[:, hi]) * b_scale)
    ckv = z_ref[0, :, EV_BKV:EV_BKR]
    ckv = (ckv * lax.rsqrt(jnp.mean(ckv * ckv, axis=-1, keepdims=True) + NORM_EPS) * bkvg_ref[...]).astype(BF16)
    kn = jnp.dot(ckv, wukn_ref[...], preferred_element_type=F32)
    vb = jnp.dot(ckv, wuv_ref[...], preferred_element_type=F32)
    for c in range(vb.shape[1] // LANES):
        put_t(vb_ref, c, vb[:, c * LANES:(c + 1) * LANES])
    kr = rope_b(z_ref[0, :, EV_BKR:EV_END]).astype(BF16)
    for c in range(kn.shape[1] // LANES):
        kb_ref[0, :, c * MXU_DIM:c * MXU_DIM + LANES] = kn[:, c * LANES:(c + 1) * LANES].astype(BF16)
        kb_ref[0, :, c * MXU_DIM + LANES:(c + 1) * MXU_DIM] = kr


def _prep_even(x, gain, mods3, mrow, w_in, wts, tabs, use_rope, tm):
    bsz, n, d = x.shape
    cols = w_in.shape[1]
    tm = min(tm, n)
    aqg, akg, bqg, bkvg, wuq, wukn, wuv = wts
    chd, shd, cb, sb = tabs
    full = lambda a: _resident(a.shape, lambda b, i: (0,) * a.ndim)
    tab = lambda a: pl.BlockSpec((tm, LANES), (lambda b, i: (i, 0)) if use_rope else (lambda b, i: (0, 0)))
    outs = [(4 * LANES, True), (4 * LANES, False), (4 * LANES, True), (wuq.shape[1], True),
            (2 * wukn.shape[1], False), (wuv.shape[1], True)]
    kern = functools.partial(_prep_even_kernel, use_rope=use_rope, a_scale=HEAD_DIM ** -0.5 * LOG2E,
                             b_scale=(B_NOPE + B_ROPE) ** -0.5 * LOG2E)
    return pl.pallas_call(
        kern,
        grid=(bsz, n // tm),
        in_specs=[pl.BlockSpec((1, tm, d), lambda b, i: (b, i, 0)),
                  pl.BlockSpec((1, d), lambda b, i: (0, 0)),
                  _mod_spec(d, 0, mrow), _mod_spec(d, 1, mrow), full(w_in),
                  full(aqg), full(akg), full(bqg), full(bkvg), full(wuq), full(wukn), full(wuv),
                  tab(chd), tab(shd), tab(cb), tab(sb)],
        out_specs=[pl.BlockSpec((1, w, tm), lambda b, i: (b, 0, i)) if t else
                   pl.BlockSpec((1, tm, w), lambda b, i: (b, i, 0)) for w, t in outs],
        out_shape=[jax.ShapeDtypeStruct((bsz, w, n) if t else (bsz, n, w), BF16) for w, t in outs],
        scratch_shapes=[pltpu.VMEM((1, tm, cols), F32)],
        compiler_params=_params(("parallel", "parallel")),
        name="prep_even",
    )(x, gain.reshape(1, d), mods3, mods3, w_in, aqg, akg, bqg, bkvg, wuq, wukn, wuv, chd, shd, cb, sb)


def _head_lane_masks(rows, nsub):
    lane = lax.broadcasted_iota(jnp.int32, (rows, MXU_DIM), 1)
    kms = []
    for j in range(HEADS_PER_GROUP):
        if nsub == 1:
            km = (lane >= j * HEAD_DIM) & (lane < (j + 1) * HEAD_DIM)
        else:
            jj = j % 2
            km = ((lane >= jj * B_NOPE) & (lane < (jj + 1) * B_NOPE)) | \
                 ((lane >= 2 * B_NOPE + jj * B_ROPE) & (lane < 2 * B_NOPE + (jj + 1) * B_ROPE))
        kms.append(jnp.where(km, 1.0, 0.0).astype(BF16))
    return kms


def _head_row_masks(cols):
    row = lax.broadcasted_iota(jnp.int32, (MXU_DIM, cols), 0)
    return [jnp.where((row >= j * HEAD_DIM) & (row < (j + 1) * HEAD_DIM), 1.0, 0.0).astype(BF16)
            for j in range(HEADS_PER_GROUP)]


def _rows4(vals, cols):
    return jnp.concatenate([jnp.broadcast_to(v, (HEAD_DIM, cols)) for v in vals], axis=0)


def _flash_kernel(*refs, nsub, tk, nsrc, nchain):
    qt_ref = refs[0]
    srcs = [(refs[1 + 2 * s], refs[2 + 2 * s]) for s in range(nsrc)]
    o_ref, kb_ref, vbt_ref, acc_ref = refs[1 + 2 * nsrc:]
    hpu = HEADS_PER_GROUP // nsub
    tq = qt_ref.shape[2]
    tqc = tq // nchain
    nb = kb_ref.shape[0]

    @pl.when(pl.program_id(2) == 0)
    def _build():
        kms = _head_lane_masks(tk, nsub)
        vms = _head_row_masks(tk)
        base = 0
        for k_ref, vt_ref in srcs:
            nblk = k_ref.shape[1] // tk

            def body(i, carry, k_ref=k_ref, base=base):
                r0 = pl.multiple_of(i * tk, tk)
                for j in range(HEADS_PER_GROUP):
                    u = j // hpu
                    kb_ref[base + i, j * tk:(j + 1) * tk, :] = \
                        k_ref[0, pl.ds(r0, tk), u * MXU_DIM:(u + 1) * MXU_DIM] * kms[j]
                return carry

            lax.fori_loop(0, nblk, body, 0)
            for i in range(nblk):
                for j in range(HEADS_PER_GROUP):
                    vbt_ref[base + i, :, j * tk:(j + 1) * tk] = vt_ref[0, :, i * tk:(i + 1) * tk] * vms[j]
            base += nblk

    def block(i, state):
        kb, vbt = kb_ref[i], vbt_ref[i]
        new_state = []
        for h in range(nchain):
            cs = slice(h * tqc, (h + 1) * tqc)
            m, l = state[h]
            parts = [jnp.dot(kb[u * hpu * tk:(u + 1) * hpu * tk, :], qt_ref[0, u * MXU_DIM:(u + 1) * MXU_DIM, cs],
                             preferred_element_type=F32) for u in range(nsub)]
            ps, alphas, m_new, l_new = [], [], [], []
            for j in range(HEADS_PER_GROUP):
                sj = parts[j // hpu][(j % hpu) * tk:(j % hpu + 1) * tk, :]
                mj = jnp.maximum(m[j], jnp.max(sj, axis=0, keepdims=True))
                a = jnp.exp2(m[j] - mj)
                p = jnp.exp2(sj - mj)
                l_new.append(a * l[j] + jnp.sum(p, axis=0, keepdims=True))
                m_new.append(mj)
                alphas.append(a)
                ps.append(p.astype(BF16))
            pv = jnp.dot(vbt, jnp.concatenate(ps, axis=0), preferred_element_type=F32)
            acc_ref[:, cs] = acc_ref[:, cs] * _rows4(alphas, tqc) + pv
            new_state.append((m_new, l_new))
        return new_state

    m0 = [jnp.full((1, tqc), NEG_INF, F32)] * HEADS_PER_GROUP
    l0 = [jnp.zeros((1, tqc), F32)] * HEADS_PER_GROUP
    acc_ref[...] = jnp.zeros(acc_ref.shape, F32)
    state = [(m0, l0)] * nchain
    for i in range(nb):
        state = block(i, state)
    for h in range(nchain):
        cs = slice(h * tqc, (h + 1) * tqc)
        ot = acc_ref[:, cs] * _rows4([1.0 / v for v in state[h][1]], tqc)
        o_ref[0, cs, :] = ot.T.astype(BF16)


def _flash(qt, ksrcs, vtsrcs, nsub, tq, tk):
    bsz, qw, nq = qt.shape
    kw = nsub * MXU_DIM
    groups = qw // kw
    tq = min(tq, nq)
    nchain = max(tq // MXU_DIM, 1)
    in_specs = [pl.BlockSpec((1, kw, tq), lambda b, g, i: (b, g, i))]
    args = [qt]
    nb = 0
    for k, vt in zip(ksrcs, vtsrcs):
        lk = k.shape[1]
        assert lk % tk == 0
        nb += lk // tk
        in_specs += [pl.BlockSpec((1, lk, kw), lambda b, g, i: (b, 0, g)),
                     pl.BlockSpec((1, MXU_DIM, lk), lambda b, g, i: (b, g, 0))]
        args += [k, vt]
    scratch = [pltpu.VMEM((nb, HEADS_PER_GROUP * tk, MXU_DIM), BF16),
               pltpu.VMEM((nb, MXU_DIM, HEADS_PER_GROUP * tk), BF16),
               pltpu.VMEM((MXU_DIM, tq), F32)]
    return pl.pallas_call(
        functools.partial(_flash_kernel, nsub=nsub, tk=tk, nsrc=len(ksrcs), nchain=nchain),
        grid=(bsz, groups, nq // tq),
        in_specs=in_specs,
        out_specs=pl.BlockSpec((1, tq, MXU_DIM), lambda b, g, i: (b, i, g)),
        out_shape=jax.ShapeDtypeStruct((bsz, nq, groups * MXU_DIM), BF16),
        scratch_shapes=scratch,
        compiler_params=_params(("parallel", "parallel", "arbitrary")),
        name="flash_attn",
    )(*args)


HALO = 16


def _ffn_kernel(xp_ref, x_ref, xn_ref, oap_ref, oa_ref, oan_ref, obp_ref, ob_ref, obn_ref, wo_ref, gtm_ref,
                g_ref, sh_ref, sc_ref, gt_ref, wup_ref, cw_ref, wdn_ref, fg_ref,
                o_ref, h_ref, om_ref, *, final_norm):
    i = pl.program_id(1)
    last = pl.num_programs(1) - 1
    tm = x_ref.shape[1]
    nf = wdn_ref.shape[0]
    half = oa_ref.shape[2]
    rows = tm + 2 * HALO
    own = slice(HALO, HALO + tm)
    for r, (a_ref, b_ref) in ((slice(0, HALO), (oap_ref, obp_ref)), (own, (oa_ref, ob_ref)),
                              (slice(HALO + tm, rows), (oan_ref, obn_ref))):
        om_ref[r, :half] = a_ref[0]
        om_ref[r, half:] = b_ref[0]
    ym = gtm_ref[0] * jnp.dot(om_ref[...], wo_ref[...], preferred_element_type=F32)
    x1 = x_ref[0] + ym[own]
    gain, shift, scale = g_ref[...], sh_ref[0], sc_ref[0]
    keep_p = jnp.where(i > 0, 1.0, 0.0)
    keep_n = jnp.where(i < last, 1.0, 0.0)
    h_ref[0:HALO, :] = (_norm_mod(xp_ref[0] + ym[0:HALO], gain, shift, scale) * keep_p).astype(BF16)
    h_ref[own, :] = _norm_mod(x1, gain, shift, scale).astype(BF16)
    h_ref[HALO + tm:, :] = (_norm_mod(xn_ref[0] + ym[HALO + tm:], gain, shift, scale) * keep_n).astype(BF16)

    def conv(u, f, part):
        w = cw_ref[part, f]
        up = pltpu.roll(u, 1, axis=0)[HALO:HALO + tm]
        un = pltpu.roll(u, rows - 1, axis=0)[HALO:HALO + tm]
        return up * w[0:1] + u[HALO:HALO + tm] * w[1:2] + un * w[2:3] + w[3:4]

    h = h_ref[...]
    up = lambda f: (jnp.dot(h, wup_ref[0, f], preferred_element_type=F32),
                    jnp.dot(h, wup_ref[1, f], preferred_element_type=F32))
    y = None
    ug, uv = up(0)
    for f in range(nf):
        nxt = up(f + 1) if f + 1 < nf else None
        g = conv(ug, f, 0)
        v = conv(uv, f, 1)
        a = (g * (1.0 / (1.0 + jnp.exp(-g))) * v).astype(BF16)
        yf = jnp.dot(a, wdn_ref[f], preferred_element_type=F32)
        y = yf if y is None else y + yf
        if nxt is not None:
            ug, uv = nxt
    out = x1 + gt_ref[0] * y
    if final_norm:
        out = out * lax.rsqrt(jnp.mean(out * out, axis=-1, keepdims=True) + NORM_EPS) * fg_ref[...]
    o_ref[0] = out


def _mix_ffn(x, oa, ob, w_out, gain, mods3, mrow, wup, cw, wdn, final_gain, tm):
    bsz, n, d = x.shape
    half = oa.shape[2]
    tm = min(tm, n)
    hb = tm // HALO
    nh = n // HALO
    final_norm = final_gain is not None
    fg = final_gain if final_norm else gain
    prev = lambda w: pl.BlockSpec((1, HALO, w), lambda b, i: (b, jnp.maximum(i * hb - 1, 0), 0))
    main = lambda w: pl.BlockSpec((1, tm, w), lambda b, i: (b, i, 0))
    nxt = lambda w: pl.BlockSpec((1, HALO, w), lambda b, i: (b, jnp.minimum((i + 1) * hb, nh - 1), 0))
    return pl.pallas_call(
        functools.partial(_ffn_kernel, final_norm=final_norm),
        grid=(bsz, n // tm),
        in_specs=[prev(d), main(d), nxt(d), prev(half), main(half), nxt(half), prev(half), main(half), nxt(half),
                  _resident(w_out.shape, lambda b, i: (0, 0)), _mod_spec(d, 2, mrow),
                  pl.BlockSpec((1, d), lambda b, i: (0, 0)),
                  _mod_spec(d, 3, mrow), _mod_spec(d, 4, mrow), _mod_spec(d, 5, mrow),
                  _resident(wup.shape, lambda b, i: (0, 0, 0, 0)),
                  _resident(cw.shape, lambda b, i: (0, 0, 0, 0)),
                  _resident(wdn.shape, lambda b, i: (0, 0, 0)),
                  pl.BlockSpec((1, d), lambda b, i: (0, 0))],
        out_specs=main(d),
        out_shape=jax.ShapeDtypeStruct((bsz, n, d), F32),
        scratch_shapes=[pltpu.VMEM((tm + 2 * HALO, d), BF16), pltpu.VMEM((tm + 2 * HALO, 2 * half), BF16)],
        compiler_params=_params(("parallel", "parallel")),
        name="mix_ffn",
    )(x, x, x, oa, oa, oa, ob, ob, ob, w_out, mods3, gain.reshape(1, d), mods3, mods3, mods3, wup, cw, wdn,
      fg.reshape(1, d))


OD_Q, OD_C, OD_K, OD_V, OD_END = 0, 512, 2048, 2176, 2304
SUB = 8


def _prep_odd_kernel(xp_ref, x_ref, xn_ref, g_ref, sh_ref, sc_ref, win_ref, cw_ref, chd_ref, shd_ref,
                     q_ref, k_ref, v_ref, x0_ref, u_ref, h_ref, z_ref, *, scale):
    i = pl.program_id(1)
    last = pl.num_programs(1) - 1
    tm = x_ref.shape[1]
    rows = tm + 2 * HALO
    own = slice(HALO, HALO + tm)
    cwid = (OD_K - OD_C) // 3
    gain, shift, mscale = g_ref[...], sh_ref[0], sc_ref[0]
    keep_p = jnp.where(i > 0, 1.0, 0.0)
    keep_n = jnp.where(i < last, 1.0, 0.0)
    h_ref[0:HALO, :] = (_norm_mod(xp_ref[0], gain, shift, mscale) * keep_p).astype(BF16)
    h_ref[own, :] = _norm_mod(x_ref[0], gain, shift, mscale).astype(BF16)
    h_ref[HALO + tm:, :] = (_norm_mod(xn_ref[0], gain, shift, mscale) * keep_n).astype(BF16)
    z_ref[...] = jnp.dot(h_ref[...], win_ref[...], preferred_element_type=F32)
    for c in range((OD_C - OD_Q) // LANES):
        sl = slice(OD_Q + c * LANES, OD_Q + (c + 1) * LANES)
        y = _rope_lanes(z_ref[own, sl], chd_ref[...], shd_ref[...])
        q_ref[0, sl, :] = (y * scale).T.astype(BF16)
    k = _rope_lanes(z_ref[own, OD_K:OD_V], chd_ref[...], shd_ref[...])
    d0, d1 = _dup_heads(k)
    for c, dk in enumerate((d0, d0, d1, d1)):
        k_ref[0, :, c * LANES:(c + 1) * LANES] = dk.astype(BF16)
    d0, d1 = _dup_heads(z_ref[own, OD_V:OD_END])
    for c, dv in enumerate((d0, d0, d1, d1)):
        v_ref[0, c * LANES:(c + 1) * LANES, :] = dv.T.astype(BF16)

    def conv(part):
        sl = slice(OD_C + part * cwid, OD_C + (part + 1) * cwid)
        csl = slice(part * cwid, (part + 1) * cwid)
        ue = z_ref[:, sl]
        up = pltpu.roll(ue, 1, axis=0)[own]
        un = pltpu.roll(ue, rows - 1, axis=0)[own]
        return up * cw_ref[0:1, csl] + ue[own] * cw_ref[1:2, csl] + un * cw_ref[2:3, csl] + cw_ref[3:4, csl]

    x0_ref[0] = conv(0).astype(BF16)
    u_ref[0] = (conv(2) * conv(1)).astype(BF16)


def _prep_odd(x, gain, mods3, w_in, cw, chd, shd, tm):
    bsz, n, d = x.shape
    cols = w_in.shape[1]
    tm = min(tm, n)
    hb = tm // HALO
    nh = n // HALO
    cwid = (OD_K - OD_C) // 3
    row_blk = lambda w: pl.BlockSpec((1, tm, w), lambda b, i: (b, i, 0))
    col_blk = lambda w: pl.BlockSpec((1, w, tm), lambda b, i: (b, 0, i))
    slab = 4 * LANES
    return pl.pallas_call(
        functools.partial(_prep_odd_kernel, scale=HEAD_DIM ** -0.5 * LOG2E),
        grid=(bsz, n // tm),
        in_specs=[pl.BlockSpec((1, HALO, d), lambda b, i: (b, jnp.maximum(i * hb - 1, 0), 0)),
                  pl.BlockSpec((1, tm, d), lambda b, i: (b, i, 0)),
                  pl.BlockSpec((1, HALO, d), lambda b, i: (b, jnp.minimum((i + 1) * hb, nh - 1), 0)),
                  pl.BlockSpec((1, d), lambda b, i: (0, 0)),
                  _mod_spec(d, 0, None), _mod_spec(d, 1, None),
                  _resident(w_in.shape, lambda b, i: (0, 0)),
                  pl.BlockSpec(cw.shape, lambda b, i: (0, 0)),
                  pl.BlockSpec((tm, LANES), lambda b, i: (i, 0)),
                  pl.BlockSpec((tm, LANES), lambda b, i: (i, 0))],
        out_specs=[col_blk(slab), row_blk(slab), col_blk(slab), row_blk(cwid), row_blk(cwid)],
        out_shape=[jax.ShapeDtypeStruct((bsz, slab, n), BF16), jax.ShapeDtypeStruct((bsz, n, slab), BF16),
                   jax.ShapeDtypeStruct((bsz, slab, n), BF16), jax.ShapeDtypeStruct((bsz, n, cwid), BF16),
                   jax.ShapeDtypeStruct((bsz, n, cwid), BF16)],
        scratch_shapes=[pltpu.VMEM((tm + 2 * HALO, d), BF16), pltpu.VMEM((tm + 2 * HALO, cols), F32)],
        compiler_params=_params(("parallel", "parallel")),
        name="prep_odd",
    )(x, x, x, gain.reshape(1, d), mods3, mods3, w_in, cw, chd, shd)


def _kv_ctx_odd_kernel(z_ref, k_ref, vt_ref):
    d0, d1 = _dup_heads(z_ref[0, :, 0:LANES])
    for c, dk in enumerate((d0, d0, d1, d1)):
        k_ref[0, :, c * LANES:(c + 1) * LANES] = dk.astype(BF16)
    d0, d1 = _dup_heads(z_ref[0, :, LANES:2 * LANES])
    for c, dv in enumerate((d0, d0, d1, d1)):
        vt_ref[0, c * LANES:(c + 1) * LANES, :] = dv.T.astype(BF16)


def _kv_ctx_odd(z):
    bsz, n, cols = z.shape
    slab = 4 * LANES
    return pl.pallas_call(
        _kv_ctx_odd_kernel,
        grid=(bsz,),
        in_specs=[pl.BlockSpec((1, n, cols), lambda b: (b, 0, 0))],
        out_specs=[pl.BlockSpec((1, n, slab), lambda b: (b, 0, 0)), pl.BlockSpec((1, slab, n), lambda b: (b, 0, 0))],
        out_shape=[jax.ShapeDtypeStruct((bsz, n, slab), BF16), jax.ShapeDtypeStruct((bsz, slab, n), BF16)],
        compiler_params=_params(("parallel",)),
        name="kv_ctx_odd",
    )(z)


WIN_QB = 2
WIN_TILES = 4


def _window_kernel(qt_ref, kc_ref, vct_ref, kl_ref, vlt_ref, bias_ref, sink_ref, o_ref, kbc, vbct, kbl, vblt,
                   *, lc, nblk, qb, nch):
    qi = pl.program_id(2)
    tq = qb * Q_BLOCK
    nsp = qb + 2
    blk = HEADS_PER_GROUP * Q_BLOCK

    @pl.when(qi == 0)
    def _build():
        kms = _head_lane_masks(lc, 1)
        vms = _head_row_masks(lc)
        for j in range(HEADS_PER_GROUP):
            kbc[j * lc:(j + 1) * lc, :] = kc_ref[0] * kms[j]
            vbct[:, j * lc:(j + 1) * lc] = vct_ref[0] * vms[j]
        kms = _head_lane_masks(Q_BLOCK, 1)
        vms = _head_row_masks(Q_BLOCK)
        for e in (0, nblk + 1):
            kbl[e] = jnp.zeros((blk, MXU_DIM), BF16)
            vblt[e] = jnp.zeros((MXU_DIM, blk), BF16)

        def body(i, carry):
            r0 = pl.multiple_of(i * Q_BLOCK, Q_BLOCK)
            for j in range(HEADS_PER_GROUP):
                kbl[i + 1, j * Q_BLOCK:(j + 1) * Q_BLOCK, :] = kl_ref[0, pl.ds(r0, Q_BLOCK), :] * kms[j]
            return carry

        lax.fori_loop(0, nblk, body, 0)
        for i in range(nblk):
            for j in range(HEADS_PER_GROUP):
                vblt[i + 1, :, j * Q_BLOCK:(j + 1) * Q_BLOCK] = vlt_ref[0, :, i * Q_BLOCK:(i + 1) * Q_BLOCK] * vms[j]

    ntile = nblk // qb
    for h in range(nch):
        t = qi * nch + h
        qt = qt_ref[0, :, h * tq:(h + 1) * tq]
        s_c = jnp.dot(kbc[...], qt, preferred_element_type=F32)
        kspan = kbl[pl.ds(qb * t, nsp)].reshape(nsp * blk, MXU_DIM)
        s_s = jnp.dot(kspan, qt, preferred_element_type=F32) + bias_ref[...]
        edge = {0: jnp.where(t == 0, NEG_INF, 0.0), nsp - 1: jnp.where(t == ntile - 1, NEG_INF, 0.0)}
        pcs, pss, invs = [], [[None] * HEADS_PER_GROUP for _ in range(nsp)], []
        for j in range(HEADS_PER_GROUP):
            sink = sink_ref[0, j][0:1, :]
            segs = [s_c[j * lc:(j + 1) * lc, :]]
            for b in range(nsp):
                sg = s_s[b * blk + j * Q_BLOCK: b * blk + (j + 1) * Q_BLOCK, :]
                segs.append(sg + edge[b] if b in edge else sg)
            m = sink
            for sg in segs:
                m = jnp.maximum(m, jnp.max(sg, axis=0, keepdims=True))
            ps = [jnp.exp2(sg - m) for sg in segs]
            den = jnp.exp2(sink - m)
            for p in ps:
                den = den + jnp.sum(p, axis=0, keepdims=True)
            invs.append(1.0 / den)
            pcs.append(ps[0].astype(BF16))
            for b in range(nsp):
                pss[b][j] = ps[1 + b].astype(BF16)
        ot = jnp.dot(vbct[...], jnp.concatenate(pcs, axis=0), preferred_element_type=F32)
        for b in range(nsp):
            ot = ot + jnp.dot(vblt[qb * t + b], jnp.concatenate(pss[b], axis=0), preferred_element_type=F32)
        o_ref[0, h * tq:(h + 1) * tq, :] = (ot * _rows4(invs, tq)).T.astype(BF16)


def _window_bias(qb):
    nsp = qb + 2
    row = jnp.arange(nsp * HEADS_PER_GROUP * Q_BLOCK)[:, None]
    r = jnp.arange(qb * Q_BLOCK)[None, :]
    b = row // (HEADS_PER_GROUP * Q_BLOCK)
    jj = b * Q_BLOCK + row % Q_BLOCK
    band = (jj >= r) & (jj <= r + 2 * WINDOW)
    return jnp.where(band, 0.0, NEG_INF).astype(F32)


def _window(qt, kc, vct, kl, vlt, sink):
    bsz, qw, n = qt.shape
    groups = qw // MXU_DIM
    lc = kc.shape[1]
    nblk = n // Q_BLOCK
    qb = min(WIN_QB, nblk)
    tq = qb * Q_BLOCK
    nch = next(c for c in (WIN_TILES, 2, 1) if n % (c * tq) == 0)
    nq = n // (tq * nch)
    blk = HEADS_PER_GROUP * Q_BLOCK
    bias = _window_bias(qb)
    sink_t = jnp.broadcast_to((sink.astype(F32) * LOG2E).reshape(groups, HEADS_PER_GROUP, 1, 1),
                              (groups, HEADS_PER_GROUP, SUB, tq))

    return pl.pallas_call(
        functools.partial(_window_kernel, lc=lc, nblk=nblk, qb=qb, nch=nch),
        grid=(bsz, groups, nq),
        in_specs=[pl.BlockSpec((1, MXU_DIM, tq * nch), lambda b, g, i: (b, g, i)),
                  pl.BlockSpec((1, lc, MXU_DIM), lambda b, g, i: (b, 0, g)),
                  pl.BlockSpec((1, MXU_DIM, lc), lambda b, g, i: (b, g, 0)),
                  pl.BlockSpec((1, n, MXU_DIM), lambda b, g, i: (b, 0, g)),
                  pl.BlockSpec((1, MXU_DIM, n), lambda b, g, i: (b, g, 0)),
                  _resident(bias.shape, lambda b, g, i: (0, 0)),
                  pl.BlockSpec((1, HEADS_PER_GROUP, SUB, tq), lambda b, g, i: (g, 0, 0, 0))],
        out_specs=pl.BlockSpec((1, tq * nch, MXU_DIM), lambda b, g, i: (b, i, g)),
        out_shape=jax.ShapeDtypeStruct((bsz, n, qw), BF16),
        scratch_shapes=[pltpu.VMEM((HEADS_PER_GROUP * lc, MXU_DIM), BF16),
                        pltpu.VMEM((MXU_DIM, HEADS_PER_GROUP * lc), BF16),
                        pltpu.VMEM((nblk + 2, blk, MXU_DIM), BF16),
                        pltpu.VMEM((nblk + 2, MXU_DIM, blk), BF16)],
        compiler_params=_params(("parallel", "parallel", "arbitrary")),
        name="window_attn",
    )(qt, kc, vct, kl, vlt, bias, sink_t)


FFT_N2 = 128


def _filter_kernel(z_ref, w1_ref, b1_ref, w2_ref, b2_ref, w3_ref, b3_ref, w4_ref, fr_ref, dl_ref, o_ref):
    def lin(a, w_ref):
        a_hi, a_lo = _split_bf16(a)
        w_hi, w_lo = _split_bf16(w_ref[...])
        return _dot3(a_hi, a_lo, w_hi, w_lo)

    z = z_ref[...]
    h = b1_ref[...]
    for e in range(C_EMB_DIM):
        h = h + z[:, e:e + 1] * w1_ref[e:e + 1, :]
    h = jnp.sin(fr_ref[0:1] * h)
    h = jnp.sin(fr_ref[1:2] * (lin(h, w2_ref) + b2_ref[...]))
    h = jnp.sin(fr_ref[2:3] * (lin(h, w3_ref) + b3_ref[...]))
    h = lin(h, w4_ref)
    t = z[:, 0:1]
    cw = dl_ref.shape[1]
    dec = jnp.exp(-t * dl_ref[...])
    row = lax.broadcasted_iota(jnp.int32, dec.shape, 0) + pl.program_id(0) * z.shape[0]
    o_ref[0] = h[:, :cw] * dec
    o_ref[1] = jnp.where(row == 0, 0.0, h[:, cw:] * dec)


def _filters(zfeat, w1, b1, w2, b2, w3, b3, w4, freq, deltas, tm):
    n = zfeat.shape[0]
    tm = min(tm, n)
    cw = deltas.shape[1]
    full = lambda a: pl.BlockSpec(a.shape, lambda i: (0,) * a.ndim)
    ops = (w1, b1, w2, b2, w3, b3, w4, freq, deltas)
    return pl.pallas_call(
        _filter_kernel,
        grid=(n // tm,),
        in_specs=[pl.BlockSpec((tm, zfeat.shape[1]), lambda i: (i, 0))] + [full(a) for a in ops],
        out_specs=pl.BlockSpec((2, tm, cw), lambda i: (0, i, 0)),
        out_shape=jax.ShapeDtypeStruct((2, n, cw), F32),
        compiler_params=_params(("parallel",)),
        name="hyena_filters",
    )(zfeat, *ops)


def _dft_first_kernel(f_ref, x_ref, o_ref):
    _, n1h, tm2, cw = x_ref.shape
    x = x_ref[0].reshape(n1h * tm2, cw).astype(BF16)
    r = jnp.dot(f_ref[...], x, preferred_element_type=F32)
    o_ref[0] = r.astype(BF16).reshape(o_ref.shape[1:])


def _dft_first(f1, x):
    bsz, n1h, n2, cw = x.shape
    tm2 = HALO
    nkp = f1.shape[0] // (2 * tm2)
    return pl.pallas_call(
        _dft_first_kernel,
        grid=(bsz, n2 // tm2),
        in_specs=[pl.BlockSpec(f1.shape, lambda b, j: (0, 0)),
                  pl.BlockSpec((1, n1h, tm2, cw), lambda b, j: (b, 0, j, 0))],
        out_specs=pl.BlockSpec((1, 2, nkp, tm2, cw), lambda b, j: (b, 0, 0, j, 0)),
        out_shape=jax.ShapeDtypeStruct((bsz, 2, nkp, n2, cw), BF16),
        compiler_params=_params(("parallel", "parallel")),
        name="dft_first",
    )(f1, x)


def _dft_last_gate_kernel(f_ref, b_ref, x0_ref, u_ref, bias_ref, o_ref):
    _, _, nkp, tm2, cw = b_ref.shape
    y = jnp.dot(f_ref[...], b_ref[0].reshape(2 * nkp * tm2, cw), preferred_element_type=F32)
    y = y.reshape(x0_ref.shape[1:])
    o_ref[0] = (x0_ref[0].astype(F32) * (y + u_ref[0].astype(F32) * bias_ref[...])).astype(BF16)


def _dft_last_gate(f2, b, x0, u, bias):
    bsz, _, nkp, n2, cw = b.shape
    tm2 = HALO
    n1h = f2.shape[0] // tm2
    tblk = pl.BlockSpec((1, n1h, tm2, cw), lambda b_, j: (b_, 0, j, 0))
    return pl.pallas_call(
        _dft_last_gate_kernel,
        grid=(bsz, n2 // tm2),
        in_specs=[pl.BlockSpec(f2.shape, lambda b_, j: (0, 0)),
                  pl.BlockSpec((1, 2, nkp, tm2, cw), lambda b_, j: (b_, 0, 0, j, 0)),
                  tblk, tblk, pl.BlockSpec((1, cw), lambda b_, j: (0, 0))],
        out_specs=tblk,
        out_shape=jax.ShapeDtypeStruct((bsz, n1h, n2, cw), BF16),
        compiler_params=_params(("parallel", "parallel")),
        name="dft_last_gate",
    )(f2, b, x0, u, bias.reshape(1, cw))


def _spec_fwd(a, m_ref):
    return jnp.dot(m_ref[0], a.reshape(2 * FFT_N2, a.shape[-1]), preferred_element_type=F32)


def _filter_spec_kernel(a_ref, m_ref, o_ref):
    xf = _spec_fwd(a_ref[0, :, 0], m_ref)
    xb = _spec_fwd(a_ref[1, :, 0], m_ref)
    o_ref[0, 0] = xf[:FFT_N2] + xb[:FFT_N2]
    o_ref[1, 0] = xf[FFT_N2:] - xb[FFT_N2:]


def _conv_spec_kernel(a_ref, k_ref, m_ref, i_ref, o_ref):
    kr, ki = k_ref[0, 0], k_ref[1, 0]
    for b in range(a_ref.shape[0]):
        x = _spec_fwd(a_ref[b, :, 0], m_ref)
        xr, xi = x[:FFT_N2], x[FFT_N2:]
        y = jnp.concatenate([xr * kr - xi * ki, xr * ki + xi * kr], axis=0).astype(BF16)
        bv = jnp.dot(i_ref[0], y, preferred_element_type=F32)
        o_ref[b, :, 0] = bv.astype(BF16).reshape(2, FFT_N2, bv.shape[-1])


def _filter_spec(a, m_fwd):
    _, _, nkp, _, cw = a.shape
    return pl.pallas_call(
        _filter_spec_kernel,
        grid=(nkp,),
        in_specs=[pl.BlockSpec((2, 2, 1, FFT_N2, cw), lambda k: (0, 0, k, 0, 0)),
                  pl.BlockSpec((1, 2 * FFT_N2, 2 * FFT_N2), lambda k: (k, 0, 0))],
        out_specs=pl.BlockSpec((2, 1, FFT_N2, cw), lambda k: (0, k, 0, 0)),
        out_shape=jax.ShapeDtypeStruct((2, nkp, FFT_N2, cw), F32),
        compiler_params=_params(("parallel",)),
        name="filter_spectrum",
    )(a, m_fwd)


def _conv_spec(a, kspec, m_fwd, m_inv):
    bsz, _, nkp, _, cw = a.shape
    blk = pl.BlockSpec((bsz, 2, 1, FFT_N2, cw), lambda k: (0, 0, k, 0, 0))
    mat = pl.BlockSpec((1, 2 * FFT_N2, 2 * FFT_N2), lambda k: (k, 0, 0))
    return pl.pallas_call(
        _conv_spec_kernel,
        grid=(nkp,),
        in_specs=[blk, pl.BlockSpec((2, 1, FFT_N2, cw), lambda k: (0, k, 0, 0)), mat, mat],
        out_specs=blk,
        out_shape=jax.ShapeDtypeStruct(a.shape, BF16),
        compiler_params=_params(("parallel",)),
        name="conv_spectrum",
    )(a, kspec, m_fwd, m_inv)


def _dft_tables(n):
    nfft = 2 * n
    n2 = FFT_N2
    n1 = nfft // n2
    n1h = n1 // 2
    nk = n1h + 1
    nkp = -(-nk // SUB) * SUB
    two_pi = 2.0 * math.pi

    def cs(num, den):
        ang = (num % den).astype(F32) * (two_pi / den)
        return jnp.cos(ang), jnp.sin(ang)

    def pad_k(t, axis):
        widths = [(0, 0)] * t.ndim
        widths[axis] = (0, nkp - nk)
        return jnp.pad(t, widths)

    k1 = jnp.arange(nk, dtype=jnp.int32)
    c1, s1 = cs(k1[:, None] * jnp.arange(n1h, dtype=jnp.int32)[None, :], n1)
    f1 = jnp.concatenate([pad_k(c1, 0), pad_k(-s1, 0)], axis=0)
    kk = k1[:, None, None] + n1 * jnp.arange(n2, dtype=jnp.int32)[None, :, None]
    cg, sg = cs(kk * jnp.arange(n2, dtype=jnp.int32)[None, None, :], nfft)
    g_re, g_im = cg, -sg
    m_fwd = jnp.concatenate([jnp.concatenate([g_re, -g_im], axis=2),
                             jnp.concatenate([g_im, g_re], axis=2)], axis=1)
    gt_re, gt_im = jnp.swapaxes(g_re, 1, 2), jnp.swapaxes(g_im, 1, 2)
    m_inv = jnp.concatenate([jnp.concatenate([gt_re, gt_im], axis=2),
                             jnp.concatenate([-gt_im, gt_re], axis=2)], axis=1)
    wk = jnp.where((k1 == 0) | (k1 == n1h), 1.0, 2.0)[None, :] * (1.0 / nfft)
    c2, s2 = cs(jnp.arange(n1h, dtype=jnp.int32)[:, None] * k1[None, :], n1)
    f2 = jnp.concatenate([pad_k(c2 * wk, 1), pad_k(-s2 * wk, 1)], axis=1)
    eye = jnp.eye(HALO, dtype=F32)
    return tuple(t.astype(BF16) for t in (jnp.kron(f1, eye), pad_k(m_fwd, 0), pad_k(m_inv, 0), jnp.kron(f2, eye)))


def _hyena_long_conv(u, x0, bias, hcat, tabs):
    f1, m_fwd, m_inv, f2 = tabs
    bsz, n, cw = u.shape
    n2 = FFT_N2
    n1h = n // n2
    kspec = _filter_spec(_dft_first(f1, hcat.reshape(2, n1h, n2, cw)), m_fwd)
    u4 = u.reshape(bsz, n1h, n2, cw)
    b_u = _conv_spec(_dft_first(f1, u4), kspec, m_fwd, m_inv)
    return _dft_last_gate(f2, b_u, x0.reshape(bsz, n1h, n2, cw), u4, bias).reshape(bsz, n, cw)


def _axial_angles(rows, rope_dim):
    row_idx = jnp.repeat(jnp.arange(rows), GRID_W).astype(F32)
    col_idx = jnp.tile(jnp.arange(GRID_W), rows).astype(F32)
    d_axis = rope_dim // 2
    inv_freq = ROPE_THETA ** (-jnp.arange(0, d_axis, 2, dtype=F32) / d_axis)
    ang = jnp.concatenate([row_idx[:, None] * inv_freq, col_idx[:, None] * inv_freq], axis=-1)
    return jnp.cos(ang), jnp.sin(ang)


def _rope_tables(n):
    rows = n // GRID_W
    sign = jnp.tile(jnp.array([-1.0, 1.0], F32), LANES // 2)[None, :]
    cos, sin = _axial_angles(rows, HEAD_DIM)
    chd = jnp.tile(jnp.repeat(cos, 2, axis=1), (1, LANES // HEAD_DIM))
    shd = jnp.tile(jnp.repeat(sin, 2, axis=1), (1, LANES // HEAD_DIM)) * sign
    cos, sin = _axial_angles(rows, B_ROPE)
    ones = jnp.ones((n, LANES - 2 * B_ROPE), F32)
    cb = jnp.concatenate([jnp.tile(jnp.repeat(cos, 2, axis=1), (1, 2)), ones], axis=1)
    sb = jnp.concatenate([jnp.tile(jnp.repeat(sin, 2, axis=1), (1, 2)), 0.0 * ones], axis=1) * sign
    return chd, shd, cb, sb


def _even_weights(w_in, a_qn, a_kn, b_qn, b_w_uq, b_kvn, b_w_ukv):
    d = w_in.shape[0]
    aq_w = w_in.shape[1] - (B_Q_RANK + 2 * A_KV_HEADS * HEAD_DIM + B_KV_RANK + B_ROPE)
    o = [0, aq_w, aq_w + B_Q_RANK]
    o += [o[-1] + A_KV_HEADS * HEAD_DIM, o[-1] + 2 * A_KV_HEADS * HEAD_DIM]
    o += [o[-1] + B_KV_RANK, o[-1] + B_KV_RANK + B_ROPE]
    kr = w_in[:, o[5]:o[6]]
    w_aug = jnp.concatenate([w_in[:, :o[5]], kr, kr, jnp.zeros((d, LANES - 2 * B_ROPE), w_in.dtype)], axis=1)
    heads = b_w_uq.shape[1] // (B_NOPE + B_ROPE)
    uq = b_w_uq.reshape(B_Q_RANK, heads // 2, 2, B_NOPE + B_ROPE)
    wuq = jnp.concatenate([uq[:, :, 0, :B_NOPE], uq[:, :, 1, :B_NOPE], uq[:, :, 0, B_NOPE:], uq[:, :, 1, B_NOPE:],
                           jnp.zeros((B_Q_RANK, heads // 2, MXU_DIM - 2 * (B_NOPE + B_ROPE)), b_w_uq.dtype)], axis=2)
    wuq = wuq.reshape(B_Q_RANK, heads // 2 * MXU_DIM)
    ukv = b_w_ukv.reshape(B_KV_RANK, heads, B_NOPE + B_VDIM)
    wukn = ukv[:, :, :B_NOPE].reshape(B_KV_RANK, heads * B_NOPE)
    wuv = ukv[:, :, B_NOPE:].reshape(B_KV_RANK, heads * B_VDIM)
    gains = (jnp.tile(a_qn, aq_w // HEAD_DIM)[None, :], jnp.tile(a_kn, A_KV_HEADS)[None, :],
             b_qn[None, :], b_kvn[None, :])
    return w_aug.astype(BF16), gains + (wuq.astype(BF16), wukn.astype(BF16), wuv.astype(BF16))


def _ffn_weights(w_up, conv_w, conv_b, w_down, nf):
    d, f2 = w_up.shape
    f = f2 // 2
    tf = f // nf
    wup = w_up.reshape(d, 2, nf, tf).transpose(1, 2, 0, 3).astype(BF16)
    cw = jnp.concatenate([conv_w, conv_b[None, :], jnp.zeros((SUB - 4, f2), conv_w.dtype)], axis=0)
    cw = cw.reshape(SUB, 2, nf, tf).transpose(1, 2, 0, 3)
    wdn = w_down.reshape(nf, tf, d).astype(BF16)
    return wup, cw, wdn


FFN_CHUNKS = 1


def kernel(x, c, ctx, c_ctx, w_mod, b_mod, norm_mix, norm_ffn, ev_w_in, ev_w_out, a_q_norm, a_k_norm, b_q_norm, b_w_uq, b_kv_norm, b_w_ukv, od_w_in, od_w_out, d_sink, c_conv_w, c_conv_b, c_filt_w1, c_filt_b1, c_filt_w2, c_filt_b2, c_filt_w3, c_filt_b3, c_filt_w4, c_filt_freq, c_bias, ffn_w_up, ffn_conv_w, ffn_conv_b, ffn_w_down, final_norm):
    bsz, n, d = x.shape
    depth = w_mod.shape[0]
    assert depth == 2 and n % Q_BLOCK == 0 and d % LANES == 0
    tm = ROW_TILE
    rows = -(-(bsz + 1) // SUB) * SUB
    cvec = jnp.concatenate([c, c_ctx[None, :], jnp.zeros((rows - bsz - 1, d), F32)], axis=0)
    mods = _mods(cvec, w_mod, b_mod)
    chd, shd, cb, sb = _rope_tables(n)
    tabs = (chd, shd, cb, sb)

    m3 = mods[0].reshape(rows, 1, N_MOD * d)
    w_aug, prep_w = _even_weights(ev_w_in[0], a_q_norm[0], a_k_norm[0], b_q_norm[0], b_w_uq[0],
                                  b_kv_norm[0], b_w_ukv[0])
    qa_l, ka_l, va_l, qb_l, kb_l, vb_l = _prep_even(x, norm_mix[0], m3, None, w_aug, prep_w, tabs, True, tm)
    qa_c, ka_c, va_c, qb_c, kb_c, vb_c = _prep_even(ctx, norm_mix[0], m3, bsz, w_aug, prep_w, tabs, False, tm)
    tq, tk = ATTN_TQ, ATTN_TK
    oa_l = _flash(qa_l, (ka_c, ka_l), (va_c, va_l), 1, tq, tk)
    ob_l = _flash(qb_l, (kb_c, kb_l), (vb_c, vb_l), 2, tq, tk)
    oa_c = _flash(qa_c, (ka_c,), (va_c,), 1, tq, tk)
    ob_c = _flash(qb_c, (kb_c,), (vb_c,), 2, tq, tk)
    w_out = ev_w_out[0].astype(BF16)
    ffn_w = _ffn_weights(ffn_w_up[0], ffn_conv_w[0], ffn_conv_b[0], ffn_w_down[0], FFN_CHUNKS)
    x = _mix_ffn(x, oa_l, ob_l, w_out, norm_ffn[0], m3, None, *ffn_w, None, tm)
    ctx = _mix_ffn(ctx, oa_c, ob_c, w_out, norm_ffn[0], m3, bsz, *ffn_w, None, tm)

    m3 = mods[1].reshape(rows, 1, N_MOD * d)
    w_in = od_w_in[0].astype(BF16)
    z_c = _inproj(ctx, norm_mix[1], m3, bsz, w_in[:, OD_K:], tm)
    cw = jnp.concatenate([c_conv_w[0], c_conv_b[0][None, :], jnp.zeros((SUB - 4, OD_K - OD_C), F32)], axis=0)
    qd, kd, vd, x0, u = _prep_odd(x, norm_mix[1], m3, w_in, cw, chd, shd, tm)
    kd_c, vd_c = _kv_ctx_odd(z_c)
    od = _window(qd, kd_c, vd_c, kd, vd, d_sink[0])
    t = jnp.linspace(0.0, 1.0, n, dtype=F32)[:, None]
    wpos = 2 * math.pi * jnp.arange(n, dtype=F32)[:, None] / n
    fb = jnp.linspace(1e-4, C_BANDS - 1, C_BANDS, dtype=F32)[None, :]
    zfeat = jnp.concatenate([t, jnp.cos(fb * wpos), -jnp.sin(fb * wpos), jnp.zeros((n, SUB - C_EMB_DIM), F32)], axis=-1)
    cwid = c_bias.shape[1]
    deltas = jnp.abs(jnp.linspace(C_MIN_DECAY, C_MAX_DECAY, cwid, dtype=F32))[None, :]
    w1 = jnp.concatenate([c_filt_w1[0], jnp.zeros((SUB - C_EMB_DIM, c_filt_w1.shape[2]), F32)], axis=0)
    hcat = _filters(zfeat, w1, c_filt_b1[0][None, :], c_filt_w2[0], c_filt_b2[0][None, :], c_filt_w3[0],
                    c_filt_b3[0][None, :], c_filt_w4[0], jnp.concatenate([c_filt_freq[0], jnp.zeros((SUB - 3, c_filt_freq.shape[2]), F32)], axis=0),
                    deltas, tm)
    oc = _hyena_long_conv(u, x0, c_bias[0], hcat, _dft_tables(n))
    ffn_w = _ffn_weights(ffn_w_up[1], ffn_conv_w[1], ffn_conv_b[1], ffn_w_down[1], FFN_CHUNKS)
    return _mix_ffn(x, od, oc, od_w_out[0].astype(BF16), norm_ffn[1], m3, None, *ffn_w, final_norm, tm)
```

```python
import functools
import math

import jax
import jax.numpy as jnp
import numpy as np
from jax import lax
from jax.experimental import pallas as pl
from jax.experimental.pallas import tpu as pltpu

F32 = jnp.float32
BF16 = jnp.bfloat16

GRID_W = 64
HEAD_DIM = 64
ROPE_THETA = 10000.0
NORM_EPS = 1e-6
NEG_INF = -1e30
N_MOD = 6
A_KV_HEADS = 2
B_NOPE = 64
B_ROPE = 32
B_VDIM = 64
B_Q_RANK = 384
B_KV_RANK = 256
C_EMB_DIM = 5
C_BANDS = (C_EMB_DIM - 1) // 2
C_MIN_DECAY = math.log(1e-2) / 1.5
C_MAX_DECAY = math.log(1e-2) / 0.3
WINDOW = 128
Q_BLOCK = 128
LOG2E = math.log2(math.e)

LANES = 128
MXU_DIM = 256
VMEM_LIMIT = 56 * 1024 * 1024
HEADS_PER_GROUP = MXU_DIM // HEAD_DIM

ROW_TILE = 512
ATTN_TQ = 1024
ATTN_TK = 256


def _params(sem, vmem=VMEM_LIMIT):
    return pltpu.CompilerParams(dimension_semantics=sem, vmem_limit_bytes=vmem)


def _resident(shape, index_map):
    return pl.BlockSpec(shape, index_map, pipeline_mode=pl.Buffered(1))


def _norm_mod(x, gain, shift, scale):
    inv = lax.rsqrt(jnp.mean(x * x, axis=-1, keepdims=True) + NORM_EPS)
    return (x * inv) * gain * (1.0 + scale) + shift


def _rope_lanes(x, cos, sin_signed):
    lane = lax.broadcasted_iota(jnp.int32, x.shape, 1)
    nxt = pltpu.roll(x, LANES - 1, axis=1)
    prv = pltpu.roll(x, 1, axis=1)
    swapped = jnp.where(lane % 2 == 0, nxt, prv)
    return x * cos + swapped * sin_signed


def _head_rmsnorm_lanes(x, gain):
    lane = lax.broadcasted_iota(jnp.int32, x.shape, 1)
    lo = lane < HEAD_DIM
    sq = x * x
    s_lo = jnp.sum(jnp.where(lo, sq, 0.0), axis=-1, keepdims=True)
    s_hi = jnp.sum(jnp.where(lo, 0.0, sq), axis=-1, keepdims=True)
    ms = jnp.where(lo, s_lo, s_hi) * (1.0 / HEAD_DIM)
    return x * lax.rsqrt(ms + NORM_EPS) * gain


def _dup_heads(x):
    lane = lax.broadcasted_iota(jnp.int32, x.shape, 1)
    lo = lane < HEAD_DIM
    r = pltpu.roll(x, HEAD_DIM, axis=1)
    return jnp.where(lo, x, r), jnp.where(lo, r, x)


def _split_bf16(x):
    hi = x.astype(BF16)
    lo = (x - hi.astype(F32)).astype(BF16)
    return hi, lo


def _dot3(a_hi, a_lo, b_hi, b_lo):
    d = functools.partial(jnp.dot, preferred_element_type=F32)
    return d(a_hi, b_hi) + d(a_hi, b_lo) + d(a_lo, b_hi)


def _mods_kernel(c_ref, w_ref, b_ref, o_ref):
    c = c_ref[...]
    s = c * (1.0 / (1.0 + jnp.exp(-c)))
    s_hi, s_lo = _split_bf16(s)
    w_hi, w_lo = _split_bf16(w_ref[0])
    o_ref[0] = _dot3(s_hi, s_lo, w_hi, w_lo) + b_ref[0]


def _mods(cvec, w_mod, b_mod):
    depth, d, n = w_mod.shape
    rows = cvec.shape[0]
    tn = 1536
    return pl.pallas_call(
        _mods_kernel,
        grid=(depth, n // tn),
        in_specs=[pl.BlockSpec((rows, d), lambda l, j: (0, 0)),
                  pl.BlockSpec((1, d, tn), lambda l, j: (l, 0, j)),
                  pl.BlockSpec((1, 1, tn), lambda l, j: (l, 0, j))],
        out_specs=pl.BlockSpec((1, rows, tn), lambda l, j: (l, 0, j)),
        out_shape=jax.ShapeDtypeStruct((depth, rows, n), F32),
        compiler_params=_params(("arbitrary", "arbitrary")),
        name="mods",
    )(cvec, w_mod, b_mod.reshape(depth, 1, n))


def _mod_spec(d, chunk, row):
    if row is None:
        return pl.BlockSpec((1, 1, d), lambda b, *_: (b, 0, chunk))
    return pl.BlockSpec((1, 1, d), lambda b, *_: (row, 0, chunk))


def _inproj_kernel(x_ref, g_ref, sh_ref, sc_ref, w_ref, o_ref):
    h = _norm_mod(x_ref[0], g_ref[...], sh_ref[0], sc_ref[0])
    o_ref[0] = jnp.dot(h.astype(BF16), w_ref[...], preferred_element_type=F32)


def _inproj(x, gain, mods3, mrow, w, tm):
    bsz, n, d = x.shape
    cols = w.shape[1]
    tm = min(tm, n)
    return pl.pallas_call(
        _inproj_kernel,
        grid=(bsz, n // tm),
        in_specs=[pl.BlockSpec((1, tm, d), lambda b, i: (b, i, 0)),
                  pl.BlockSpec((1, d), lambda b, i: (0, 0)),
                  _mod_spec(d, 0, mrow), _mod_spec(d, 1, mrow),
                  _resident((d, cols), lambda b, i: (0, 0))],
        out_specs=pl.BlockSpec((1, tm, cols), lambda b, i: (b, i, 0)),
        out_shape=jax.ShapeDtypeStruct((bsz, n, cols), F32),
        compiler_params=_params(("parallel", "parallel")),
        name="inproj",
    )(x, gain.reshape(1, d), mods3, mods3, w)


EV_AQ, EV_BQ, EV_AK, EV_AV, EV_BKV, EV_BKR, EV_END = 0, 512, 896, 1024, 1152, 1408, 1536


def _prep_even_kernel(x_ref, g_ref, sh_ref, sc_ref, win_ref,
                      aqg_ref, akg_ref, bqg_ref, bkvg_ref, wuq_ref, wukn_ref, wuv_ref,
                      chd_ref, shd_ref, cb_ref, sb_ref,
                      qa_ref, ka_ref, va_ref, qb_ref, kb_ref, vb_ref, z_ref, *, use_rope, a_scale, b_scale):
    h = _norm_mod(x_ref[0], g_ref[...], sh_ref[0], sc_ref[0])
    z_ref[0] = jnp.dot(h.astype(BF16), win_ref[...], preferred_element_type=F32)

    def rope_hd(y):
        return _rope_lanes(y, chd_ref[...], shd_ref[...]) if use_rope else y

    def rope_b(y):
        return _rope_lanes(y, cb_ref[...], sb_ref[...]) if use_rope else y

    def put_t(dst, c, y):
        dst[0, c * LANES:(c + 1) * LANES, :] = y.T.astype(BF16)

    for c in range((EV_BQ - EV_AQ) // LANES):
        sl = slice(EV_AQ + c * LANES, EV_AQ + (c + 1) * LANES)
        y = rope_hd(_head_rmsnorm_lanes(z_ref[0, :, sl], aqg_ref[:, c * LANES:(c + 1) * LANES]))
        put_t(qa_ref, c, y * a_scale)
    k = rope_hd(_head_rmsnorm_lanes(z_ref[0, :, EV_AK:EV_AV], akg_ref[...]))
    d0, d1 = _dup_heads(k)
    d0 = d0.astype(BF16)
    d1 = d1.astype(BF16)
    ka_ref[0, :, 0 * LANES:1 * LANES] = d0
    ka_ref[0, :, 1 * LANES:2 * LANES] = d0
    ka_ref[0, :, 2 * LANES:3 * LANES] = d1
    ka_ref[0, :, 3 * LANES:4 * LANES] = d1
    d0, d1 = (d.T.astype(BF16) for d in _dup_heads(z_ref[0, :, EV_AV:EV_BKV]))
    for c, dv in enumerate((d0, d0, d1, d1)):
        va_ref[0, c * LANES:(c + 1) * LANES, :] = dv
    cq = z_ref[0, :, EV_BQ:EV_AK]
    cq = cq * lax.rsqrt(jnp.mean(cq * cq, axis=-1, keepdims=True) + NORM_EPS) * bqg_ref[...]
    qb = jnp.dot(cq.astype(BF16), wuq_ref[...], preferred_element_type=F32)
    for c in range(qb.shape[1] // MXU_DIM):
        lo = slice(c * MXU_DIM, c * MXU_DIM + LANES)
        hi = slice(c * MXU_DIM + LANES, (c + 1) * MXU_DIM)
        put_t(qb_ref, 2 * c, qb[:, lo] * b_scale)
        put_t(qb_ref, 2 * c + 1, rope_b(qb[:, hi]) * b_scale)
    ckv = z_ref[0, :, EV_BKV:EV_BKR]
    ckv = (ckv * lax.rsqrt(jnp.mean(ckv * ckv, axis=-1, keepdims=True) + NORM_EPS) * bkvg_ref[...]).astype(BF16)
    kn = jnp.dot(ckv, wukn_ref[...], preferred_element_type=F32)
    vb = jnp.dot(ckv, wuv_ref[...], preferred_element_type=F32)
    for c in range(vb.shape[1] // LANES):
        put_t(vb_ref, c, vb[:, c * LANES:(c + 1) * LANES])
    kr = rope_b(z_ref[0, :, EV_BKR:EV_END]).astype(BF16)
    for c in range(kn.shape[1] // LANES):
        kb_ref[0, :, c * MXU_DIM:c * MXU_DIM + LANES] = kn[:, c * LANES:(c + 1) * LANES].astype(BF16)
        kb_ref[0, :, c * MXU_DIM + LANES:(c + 1) * MXU_DIM] = kr


def _prep_even(x, gain, mods3, mrow, w_in, wts, tabs, use_rope, tm):
    bsz, n, d = x.shape
    cols = w_in.shape[1]
    tm = min(tm, n)
    aqg, akg, bqg, bkvg, wuq, wukn, wuv = wts
    chd, shd, cb, sb = tabs
    full = lambda a: _resident(a.shape, lambda b, i: (0,) * a.ndim)
    tab = lambda a: pl.BlockSpec((tm, LANES), (lambda b, i: (i, 0)) if use_rope else (lambda b, i: (0, 0)))
    outs = [(4 * LANES, True), (4 * LANES, False), (4 * LANES, True), (wuq.shape[1], True),
            (2 * wukn.shape[1], False), (wuv.shape[1], True)]
    kern = functools.partial(_prep_even_kernel, use_rope=use_rope, a_scale=HEAD_DIM ** -0.5 * LOG2E,
                             b_scale=(B_NOPE + B_ROPE) ** -0.5 * LOG2E)
    return pl.pallas_call(
        kern,
        grid=(bsz, n // tm),
        in_specs=[pl.BlockSpec((1, tm, d), lambda b, i: (b, i, 0)),
                  pl.BlockSpec((1, d), lambda b, i: (0, 0)),
                  _mod_spec(d, 0, mrow), _mod_spec(d, 1, mrow), full(w_in),
                  full(aqg), full(akg), full(bqg), full(bkvg), full(wuq), full(wukn), full(wuv),
                  tab(chd), tab(shd), tab(cb), tab(sb)],
        out_specs=[pl.BlockSpec((1, w, tm), lambda b, i: (b, 0, i)) if t else
                   pl.BlockSpec((1, tm, w), lambda b, i: (b, i, 0)) for w, t in outs],
        out_shape=[jax.ShapeDtypeStruct((bsz, w, n) if t else (bsz, n, w), BF16) for w, t in outs],
        scratch_shapes=[pltpu.VMEM((1, tm, cols), F32)],
        compiler_params=_params(("parallel", "parallel")),
        name="prep_even",
    )(x, gain.reshape(1, d), mods3, mods3, w_in, aqg, akg, bqg, bkvg, wuq, wukn, wuv, chd, shd, cb, sb)


def _head_lane_masks(rows, nsub):
    lane = lax.broadcasted_iota(jnp.int32, (rows, MXU_DIM), 1)
    kms = []
    for j in range(HEADS_PER_GROUP):
        if nsub == 1:
            km = (lane >= j * HEAD_DIM) & (lane < (j + 1) * HEAD_DIM)
        else:
            jj = j % 2
            km = ((lane >= jj * B_NOPE) & (lane < (jj + 1) * B_NOPE)) | \
                 ((lane >= 2 * B_NOPE + jj * B_ROPE) & (lane < 2 * B_NOPE + (jj + 1) * B_ROPE))
        kms.append(jnp.where(km, 1.0, 0.0).astype(BF16))
    return kms


def _head_row_masks(cols):
    row = lax.broadcasted_iota(jnp.int32, (MXU_DIM, cols), 0)
    return [jnp.where((row >= j * HEAD_DIM) & (row < (j + 1) * HEAD_DIM), 1.0, 0.0).astype(BF16)
            for j in range(HEADS_PER_GROUP)]


def _rows4(vals, cols):
    return jnp.concatenate([jnp.broadcast_to(v, (HEAD_DIM, cols)) for v in vals], axis=0)


def _flash_kernel(*refs, nsub, tk, nsrc, nchain):
    qt_ref = refs[0]
    srcs = [(refs[1 + 2 * s], refs[2 + 2 * s]) for s in range(nsrc)]
    o_ref, kb_ref, vbt_ref, acc_ref = refs[1 + 2 * nsrc:]
    hpu = HEADS_PER_GROUP // nsub
    tq = qt_ref.shape[2]
    tqc = tq // nchain
    nb = kb_ref.shape[0]

    @pl.when(pl.program_id(2) == 0)
    def _build():
        kms = _head_lane_masks(tk, nsub)
        vms = _head_row_masks(tk)
        base = 0
        for k_ref, vt_ref in srcs:
            nblk = k_ref.shape[1] // tk

            def body(i, carry, k_ref=k_ref, base=base):
                r0 = pl.multiple_of(i * tk, tk)
                for j in range(HEADS_PER_GROUP):
                    u = j // hpu
                    kb_ref[base + i, j * tk:(j + 1) * tk, :] = \
                        k_ref[0, pl.ds(r0, tk), u * MXU_DIM:(u + 1) * MXU_DIM] * kms[j]
                return carry

            lax.fori_loop(0, nblk, body, 0)
            for i in range(nblk):
                for j in range(HEADS_PER_GROUP):
                    vbt_ref[base + i, :, j * tk:(j + 1) * tk] = vt_ref[0, :, i * tk:(i + 1) * tk] * vms[j]
            base += nblk

    def block(i, state):
        kb, vbt = kb_ref[i], vbt_ref[i]
        new_state = []
        for h in range(nchain):
            cs = slice(h * tqc, (h + 1) * tqc)
            m, l = state[h]
            parts = [jnp.dot(kb[u * hpu * tk:(u + 1) * hpu * tk, :], qt_ref[0, u * MXU_DIM:(u + 1) * MXU_DIM, cs],
                             preferred_element_type=F32) for u in range(nsub)]
            ps, alphas, m_new, l_new = [], [], [], []
            for j in range(HEADS_PER_GROUP):
                sj = parts[j // hpu][(j % hpu) * tk:(j % hpu + 1) * tk, :]
                mj = jnp.maximum(m[j], jnp.max(sj, axis=0, keepdims=True))
                a = jnp.exp2(m[j] - mj)
                p = jnp.exp2(sj - mj)
                l_new.append(a * l[j] + jnp.sum(p, axis=0, keepdims=True))
                m_new.append(mj)
                alphas.append(a)
                ps.append(p.astype(BF16))
            pv = jnp.dot(vbt, jnp.concatenate(ps, axis=0), preferred_element_type=F32)
            acc_ref[:, cs] = acc_ref[:, cs] * _rows4(alphas, tqc) + pv
            new_state.append((m_new, l_new))
        return new_state

    m0 = [jnp.full((1, tqc), NEG_INF, F32)] * HEADS_PER_GROUP
    l0 = [jnp.zeros((1, tqc), F32)] * HEADS_PER_GROUP
    acc_ref[...] = jnp.zeros(acc_ref.shape, F32)
    state = [(m0, l0)] * nchain
    for i in range(nb):
        state = block(i, state)
    for h in range(nchain):
        cs = slice(h * tqc, (h + 1) * tqc)
        ot = acc_ref[:, cs] * _rows4([1.0 / v for v in state[h][1]], tqc)
        o_ref[0, cs, :] = ot.T.astype(BF16)


def _flash(qt, ksrcs, vtsrcs, nsub, tq, tk):
    bsz, qw, nq = qt.shape
    kw = nsub * MXU_DIM
    groups = qw // kw
    tq = min(tq, nq)
    nchain = max(tq // MXU_DIM, 1)
    in_specs = [pl.BlockSpec((1, kw, tq), lambda b, g, i: (b, g, i))]
    args = [qt]
    nb = 0
    for k, vt in zip(ksrcs, vtsrcs):
        lk = k.shape[1]
        assert lk % tk == 0
        nb += lk // tk
        in_specs += [pl.BlockSpec((1, lk, kw), lambda b, g, i: (b, 0, g)),
                     pl.BlockSpec((1, MXU_DIM, lk), lambda b, g, i: (b, g, 0))]
        args += [k, vt]
    scratch = [pltpu.VMEM((nb, HEADS_PER_GROUP * tk, MXU_DIM), BF16),
               pltpu.VMEM((nb, MXU_DIM, HEADS_PER_GROUP * tk), BF16),
               pltpu.VMEM((MXU_DIM, tq), F32)]
    return pl.pallas_call(
        functools.partial(_flash_kernel, nsub=nsub, tk=tk, nsrc=len(ksrcs), nchain=nchain),
        grid=(bsz, groups, nq // tq),
        in_specs=in_specs,
        out_specs=pl.BlockSpec((1, tq, MXU_DIM), lambda b, g, i: (b, i, g)),
        out_shape=jax.ShapeDtypeStruct((bsz, nq, groups * MXU_DIM), BF16),
        scratch_shapes=scratch,
        compiler_params=_params(("parallel", "parallel", "arbitrary")),
        name="flash_attn",
    )(*args)


HALO = 16


def _ffn_kernel(xp_ref, x_ref, xn_ref, oap_ref, oa_ref, oan_ref, obp_ref, ob_ref, obn_ref, wo_ref, gtm_ref,
                g_ref, sh_ref, sc_ref, gt_ref, wup_ref, cw_ref, wdn_ref, fg_ref,
                o_ref, h_ref, om_ref, *, final_norm, nf):
    i = pl.program_id(1)
    last = pl.num_programs(1) - 1
    tm = x_ref.shape[1]
    fdim = wdn_ref.shape[0]
    tf = fdim // nf
    half = oa_ref.shape[2]
    rows = tm + 2 * HALO
    own = slice(HALO, HALO + tm)
    for r, (a_ref, b_ref) in ((slice(0, HALO), (oap_ref, obp_ref)), (own, (oa_ref, ob_ref)),
                              (slice(HALO + tm, rows), (oan_ref, obn_ref))):
        om_ref[r, :half] = a_ref[0]
        om_ref[r, half:] = b_ref[0]
    ym = gtm_ref[0] * jnp.dot(om_ref[...], wo_ref[...], preferred_element_type=F32)
    x1 = x_ref[0] + ym[own]
    gain, shift, scale = g_ref[...], sh_ref[0], sc_ref[0]
    keep_p = jnp.where(i > 0, 1.0, 0.0)
    keep_n = jnp.where(i < last, 1.0, 0.0)
    h_ref[0:HALO, :] = (_norm_mod(xp_ref[0] + ym[0:HALO], gain, shift, scale) * keep_p).astype(BF16)
    h_ref[own, :] = _norm_mod(x1, gain, shift, scale).astype(BF16)
    h_ref[HALO + tm:, :] = (_norm_mod(xn_ref[0] + ym[HALO + tm:], gain, shift, scale) * keep_n).astype(BF16)

    def conv(u, f, part):
        w = cw_ref[:, cols(part, f)]
        up = pltpu.roll(u, 1, axis=0)[HALO:HALO + tm]
        un = pltpu.roll(u, rows - 1, axis=0)[HALO:HALO + tm]
        return up * w[0:1] + u[HALO:HALO + tm] * w[1:2] + un * w[2:3] + w[3:4]

    def cols(part, f):
        return slice(part * fdim + f * tf, part * fdim + (f + 1) * tf)

    h = h_ref[...]
    up = lambda f: (jnp.dot(h, wup_ref[:, cols(0, f)], preferred_element_type=F32),
                    jnp.dot(h, wup_ref[:, cols(1, f)], preferred_element_type=F32))
    y = None
    ug, uv = up(0)
    for f in range(nf):
        nxt = up(f + 1) if f + 1 < nf else None
        g = conv(ug, f, 0)
        v = conv(uv, f, 1)
        a = (g * (1.0 / (1.0 + jnp.exp(-g))) * v).astype(BF16)
        yf = jnp.dot(a, wdn_ref[f * tf:(f + 1) * tf, :], preferred_element_type=F32)
        y = yf if y is None else y + yf
        if nxt is not None:
            ug, uv = nxt
    out = x1 + gt_ref[0] * y
    if final_norm:
        out = out * lax.rsqrt(jnp.mean(out * out, axis=-1, keepdims=True) + NORM_EPS) * fg_ref[...]
    o_ref[0] = out


def _mix_ffn(x, oa, ob, w_out, gain, mods3, mrow, wup, cw, wdn, final_gain, tm):
    bsz, n, d = x.shape
    half = oa.shape[2]
    tm = min(tm, n)
    hb = tm // HALO
    nh = n // HALO
    final_norm = final_gain is not None
    fg = final_gain if final_norm else gain
    prev = lambda w: pl.BlockSpec((1, HALO, w), lambda b, i: (b, jnp.maximum(i * hb - 1, 0), 0))
    main = lambda w: pl.BlockSpec((1, tm, w), lambda b, i: (b, i, 0))
    nxt = lambda w: pl.BlockSpec((1, HALO, w), lambda b, i: (b, jnp.minimum((i + 1) * hb, nh - 1), 0))
    return pl.pallas_call(
        functools.partial(_ffn_kernel, final_norm=final_norm, nf=FFN_CHUNKS),
        grid=(bsz, n // tm),
        in_specs=[prev(d), main(d), nxt(d), prev(half), main(half), nxt(half), prev(half), main(half), nxt(half),
                  _resident(w_out.shape, lambda b, i: (0, 0)), _mod_spec(d, 2, mrow),
                  pl.BlockSpec((1, d), lambda b, i: (0, 0)),
                  _mod_spec(d, 3, mrow), _mod_spec(d, 4, mrow), _mod_spec(d, 5, mrow),
                  _resident(wup.shape, lambda b, i: (0, 0)),
                  _resident(cw.shape, lambda b, i: (0, 0)),
                  _resident(wdn.shape, lambda b, i: (0, 0)),
                  pl.BlockSpec((1, d), lambda b, i: (0, 0))],
        out_specs=main(d),
        out_shape=jax.ShapeDtypeStruct((bsz, n, d), F32),
        scratch_shapes=[pltpu.VMEM((tm + 2 * HALO, d), BF16), pltpu.VMEM((tm + 2 * HALO, 2 * half), BF16)],
        compiler_params=_params(("parallel", "parallel")),
        name="mix_ffn",
    )(x, x, x, oa, oa, oa, ob, ob, ob, w_out, mods3, gain.reshape(1, d), mods3, mods3, mods3, wup, cw, wdn,
      fg.reshape(1, d))


OD_Q, OD_C, OD_K, OD_V, OD_END = 0, 512, 2048, 2176, 2304
SUB = 8


def _prep_odd_kernel(xp_ref, x_ref, xn_ref, g_ref, sh_ref, sc_ref, win_ref, cw_ref, chd_ref, shd_ref,
                     q_ref, k_ref, v_ref, x0_ref, u_ref, h_ref, z_ref, *, scale):
    i = pl.program_id(1)
    last = pl.num_programs(1) - 1
    tm = x_ref.shape[1]
    rows = tm + 2 * HALO
    own = slice(HALO, HALO + tm)
    cwid = (OD_K - OD_C) // 3
    gain, shift, mscale = g_ref[...], sh_ref[0], sc_ref[0]
    keep_p = jnp.where(i > 0, 1.0, 0.0)
    keep_n = jnp.where(i < last, 1.0, 0.0)
    h_ref[0:HALO, :] = (_norm_mod(xp_ref[0], gain, shift, mscale) * keep_p).astype(BF16)
    h_ref[own, :] = _norm_mod(x_ref[0], gain, shift, mscale).astype(BF16)
    h_ref[HALO + tm:, :] = (_norm_mod(xn_ref[0], gain, shift, mscale) * keep_n).astype(BF16)
    z_ref[...] = jnp.dot(h_ref[...], win_ref[...], preferred_element_type=F32)
    for c in range((OD_C - OD_Q) // LANES):
        sl = slice(OD_Q + c * LANES, OD_Q + (c + 1) * LANES)
        y = _rope_lanes(z_ref[own, sl], chd_ref[...], shd_ref[...])
        q_ref[0, sl, :] = (y * scale).T.astype(BF16)
    k = _rope_lanes(z_ref[own, OD_K:OD_V], chd_ref[...], shd_ref[...])
    d0, d1 = _dup_heads(k)
    for c, dk in enumerate((d0, d0, d1, d1)):
        k_ref[0, :, c * LANES:(c + 1) * LANES] = dk.astype(BF16)
    d0, d1 = (d.T.astype(BF16) for d in _dup_heads(z_ref[own, OD_V:OD_END]))
    for c, dv in enumerate((d0, d0, d1, d1)):
        v_ref[0, c * LANES:(c + 1) * LANES, :] = dv

    def conv(part):
        sl = slice(OD_C + part * cwid, OD_C + (part + 1) * cwid)
        csl = slice(part * cwid, (part + 1) * cwid)
        ue = z_ref[:, sl]
        up = pltpu.roll(ue, 1, axis=0)[own]
        un = pltpu.roll(ue, rows - 1, axis=0)[own]
        return up * cw_ref[0:1, csl] + ue[own] * cw_ref[1:2, csl] + un * cw_ref[2:3, csl] + cw_ref[3:4, csl]

    x0_ref[0] = conv(0).astype(BF16)
    u_ref[0] = (conv(2) * conv(1)).astype(BF16)


def _prep_odd(x, gain, mods3, w_in, cw, chd, shd, tm):
    bsz, n, d = x.shape
    cols = w_in.shape[1]
    tm = min(tm, n)
    hb = tm // HALO
    nh = n // HALO
    cwid = (OD_K - OD_C) // 3
    row_blk = lambda w: pl.BlockSpec((1, tm, w), lambda b, i: (b, i, 0))
    col_blk = lambda w: pl.BlockSpec((1, w, tm), lambda b, i: (b, 0, i))
    slab = 4 * LANES
    return pl.pallas_call(
        functools.partial(_prep_odd_kernel, scale=HEAD_DIM ** -0.5 * LOG2E),
        grid=(bsz, n // tm),
        in_specs=[pl.BlockSpec((1, HALO, d), lambda b, i: (b, jnp.maximum(i * hb - 1, 0), 0)),
                  pl.BlockSpec((1, tm, d), lambda b, i: (b, i, 0)),
                  pl.BlockSpec((1, HALO, d), lambda b, i: (b, jnp.minimum((i + 1) * hb, nh - 1), 0)),
                  pl.BlockSpec((1, d), lambda b, i: (0, 0)),
                  _mod_spec(d, 0, None), _mod_spec(d, 1, None),
                  _resident(w_in.shape, lambda b, i: (0, 0)),
                  pl.BlockSpec(cw.shape, lambda b, i: (0, 0)),
                  pl.BlockSpec((tm, LANES), lambda b, i: (i, 0)),
                  pl.BlockSpec((tm, LANES), lambda b, i: (i, 0))],
        out_specs=[col_blk(slab), row_blk(slab), col_blk(slab), row_blk(cwid), row_blk(cwid)],
        out_shape=[jax.ShapeDtypeStruct((bsz, slab, n), BF16), jax.ShapeDtypeStruct((bsz, n, slab), BF16),
                   jax.ShapeDtypeStruct((bsz, slab, n), BF16), jax.ShapeDtypeStruct((bsz, n, cwid), BF16),
                   jax.ShapeDtypeStruct((bsz, n, cwid), BF16)],
        scratch_shapes=[pltpu.VMEM((tm + 2 * HALO, d), BF16), pltpu.VMEM((tm + 2 * HALO, cols), F32)],
        compiler_params=_params(("parallel", "parallel")),
        name="prep_odd",
    )(x, x, x, gain.reshape(1, d), mods3, mods3, w_in, cw, chd, shd)


def _kv_ctx_odd_kernel(z_ref, k_ref, vt_ref):
    d0, d1 = _dup_heads(z_ref[0, :, 0:LANES])
    for c, dk in enumerate((d0, d0, d1, d1)):
        k_ref[0, :, c * LANES:(c + 1) * LANES] = dk.astype(BF16)
    d0, d1 = (d.T.astype(BF16) for d in _dup_heads(z_ref[0, :, LANES:2 * LANES]))
    for c, dv in enumerate((d0, d0, d1, d1)):
        vt_ref[0, c * LANES:(c + 1) * LANES, :] = dv


def _kv_ctx_odd(z):
    bsz, n, cols = z.shape
    slab = 4 * LANES
    return pl.pallas_call(
        _kv_ctx_odd_kernel,
        grid=(bsz,),
        in_specs=[pl.BlockSpec((1, n, cols), lambda b: (b, 0, 0))],
        out_specs=[pl.BlockSpec((1, n, slab), lambda b: (b, 0, 0)), pl.BlockSpec((1, slab, n), lambda b: (b, 0, 0))],
        out_shape=[jax.ShapeDtypeStruct((bsz, n, slab), BF16), jax.ShapeDtypeStruct((bsz, slab, n), BF16)],
        compiler_params=_params(("parallel",)),
        name="kv_ctx_odd",
    )(z)


WIN_QB = 2
WIN_TILES = 4


def _window_kernel(qt_ref, kc_ref, vct_ref, kl_ref, vlt_ref, bias_ref, sink_ref, o_ref, kbc, vbct, kbl, vblt,
                   *, lc, nblk, qb, nch):
    qi = pl.program_id(2)
    tq = qb * Q_BLOCK
    nsp = qb + 2
    blk = HEADS_PER_GROUP * Q_BLOCK

    @pl.when(qi == 0)
    def _build():
        kms = _head_lane_masks(lc, 1)
        vms = _head_row_masks(lc)
        for j in range(HEADS_PER_GROUP):
            kbc[j * lc:(j + 1) * lc, :] = kc_ref[0] * kms[j]
            vbct[:, j * lc:(j + 1) * lc] = vct_ref[0] * vms[j]
        kms = _head_lane_masks(Q_BLOCK, 1)
        vms = _head_row_masks(Q_BLOCK)
        for e in (0, nblk + 1):
            kbl[e] = jnp.zeros((blk, MXU_DIM), BF16)
            vblt[e] = jnp.zeros((MXU_DIM, blk), BF16)

        def body(i, carry):
            r0 = pl.multiple_of(i * Q_BLOCK, Q_BLOCK)
            for j in range(HEADS_PER_GROUP):
                kbl[i + 1, j * Q_BLOCK:(j + 1) * Q_BLOCK, :] = kl_ref[0, pl.ds(r0, Q_BLOCK), :] * kms[j]
            return carry

        lax.fori_loop(0, nblk, body, 0)
        for i in range(nblk):
            for j in range(HEADS_PER_GROUP):
                vblt[i + 1, :, j * Q_BLOCK:(j + 1) * Q_BLOCK] = vlt_ref[0, :, i * Q_BLOCK:(i + 1) * Q_BLOCK] * vms[j]

    ntile = nblk // qb
    for h in range(nch):
        t = qi * nch + h
        qt = qt_ref[0, :, h * tq:(h + 1) * tq]
        s_c = jnp.dot(kbc[...], qt, preferred_element_type=F32)
        kspan = kbl[pl.ds(qb * t, nsp)].reshape(nsp * blk, MXU_DIM)
        s_s = jnp.dot(kspan, qt, preferred_element_type=F32) + bias_ref[...]
        edge = {0: jnp.where(t == 0, NEG_INF, 0.0), nsp - 1: jnp.where(t == ntile - 1, NEG_INF, 0.0)}
        pcs, pss, invs = [], [[None] * HEADS_PER_GROUP for _ in range(nsp)], []
        for j in range(HEADS_PER_GROUP):
            sink = sink_ref[0, j][0:1, :]
            segs = [s_c[j * lc:(j + 1) * lc, :]]
            for b in range(nsp):
                sg = s_s[b * blk + j * Q_BLOCK: b * blk + (j + 1) * Q_BLOCK, :]
                segs.append(sg + edge[b] if b in edge else sg)
            m = sink
            for sg in segs:
                m = jnp.maximum(m, jnp.max(sg, axis=0, keepdims=True))
            ps = [jnp.exp2(sg - m) for sg in segs]
            den = jnp.exp2(sink - m)
            for p in ps:
                den = den + jnp.sum(p, axis=0, keepdims=True)
            invs.append(1.0 / den)
            pcs.append(ps[0].astype(BF16))
            for b in range(nsp):
                pss[b][j] = ps[1 + b].astype(BF16)
        ot = jnp.dot(vbct[...], jnp.concatenate(pcs, axis=0), preferred_element_type=F32)
        for b in range(nsp):
            ot = ot + jnp.dot(vblt[qb * t + b], jnp.concatenate(pss[b], axis=0), preferred_element_type=F32)
        o_ref[0, h * tq:(h + 1) * tq, :] = (ot * _rows4(invs, tq)).T.astype(BF16)


@functools.lru_cache(maxsize=None)
def _window_bias(qb):
    nsp = qb + 2
    row = np.arange(nsp * HEADS_PER_GROUP * Q_BLOCK)[:, None]
    r = np.arange(qb * Q_BLOCK)[None, :]
    b = row // (HEADS_PER_GROUP * Q_BLOCK)
    jj = b * Q_BLOCK + row % Q_BLOCK
    band = (jj >= r) & (jj <= r + 2 * WINDOW)
    return np.where(band, 0.0, NEG_INF).astype(np.float32)


def _window(qt, kc, vct, kl, vlt, sink):
    bsz, qw, n = qt.shape
    groups = qw // MXU_DIM
    lc = kc.shape[1]
    nblk = n // Q_BLOCK
    qb = min(WIN_QB, nblk)
    tq = qb * Q_BLOCK
    nch = next(c for c in (WIN_TILES, 2, 1) if n % (c * tq) == 0)
    nq = n // (tq * nch)
    blk = HEADS_PER_GROUP * Q_BLOCK
    bias = jnp.asarray(_window_bias(qb))
    sink_t = jnp.broadcast_to((sink.astype(F32) * LOG2E).reshape(groups, HEADS_PER_GROUP, 1, 1),
                              (groups, HEADS_PER_GROUP, SUB, tq))

    return pl.pallas_call(
        functools.partial(_window_kernel, lc=lc, nblk=nblk, qb=qb, nch=nch),
        grid=(bsz, groups, nq),
        in_specs=[pl.BlockSpec((1, MXU_DIM, tq * nch), lambda b, g, i: (b, g, i)),
                  pl.BlockSpec((1, lc, MXU_DIM), lambda b, g, i: (b, 0, g)),
                  pl.BlockSpec((1, MXU_DIM, lc), lambda b, g, i: (b, g, 0)),
                  pl.BlockSpec((1, n, MXU_DIM), lambda b, g, i: (b, 0, g)),
                  pl.BlockSpec((1, MXU_DIM, n), lambda b, g, i: (b, g, 0)),
                  _resident(bias.shape, lambda b, g, i: (0, 0)),
                  pl.BlockSpec((1, HEADS_PER_GROUP, SUB, tq), lambda b, g, i: (g, 0, 0, 0))],
        out_specs=pl.BlockSpec((1, tq * nch, MXU_DIM), lambda b, g, i: (b, i, g)),
        out_shape=jax.ShapeDtypeStruct((bsz, n, qw), BF16),
        scratch_shapes=[pltpu.VMEM((HEADS_PER_GROUP * lc, MXU_DIM), BF16),
                        pltpu.VMEM((MXU_DIM, HEADS_PER_GROUP * lc), BF16),
                        pltpu.VMEM((nblk + 2, blk, MXU_DIM), BF16),
                        pltpu.VMEM((nblk + 2, MXU_DIM, blk), BF16)],
        compiler_params=_params(("parallel", "parallel", "arbitrary")),
        name="window_attn",
    )(qt, kc, vct, kl, vlt, bias, sink_t)


FFT_N2 = 128


def _filter_kernel(z_ref, w1_ref, b1_ref, w2_ref, b2_ref, w3_ref, b3_ref, w4_ref, fr_ref, dl_ref, o_ref):
    def lin(a, w_ref):
        a_hi, a_lo = _split_bf16(a)
        w_hi, w_lo = _split_bf16(w_ref[...])
        return _dot3(a_hi, a_lo, w_hi, w_lo)

    z = z_ref[...]
    h = b1_ref[...]
    for e in range(C_EMB_DIM):
        h = h + z[:, e:e + 1] * w1_ref[e:e + 1, :]
    h = jnp.sin(fr_ref[0:1] * h)
    h = jnp.sin(fr_ref[1:2] * (lin(h, w2_ref) + b2_ref[...]))
    h = jnp.sin(fr_ref[2:3] * (lin(h, w3_ref) + b3_ref[...]))
    h = lin(h, w4_ref)
    t = z[:, 0:1]
    cw = dl_ref.shape[1]
    dec = jnp.exp(-t * dl_ref[...])
    row = lax.broadcasted_iota(jnp.int32, dec.shape, 0) + pl.program_id(0) * z.shape[0]
    o_ref[0] = h[:, :cw] * dec
    o_ref[1] = jnp.where(row == 0, 0.0, h[:, cw:] * dec)


def _filters(zfeat, w1, b1, w2, b2, w3, b3, w4, freq, deltas, tm):
    n = zfeat.shape[0]
    tm = min(tm, n)
    cw = deltas.shape[1]
    full = lambda a: pl.BlockSpec(a.shape, lambda i: (0,) * a.ndim)
    ops = (w1, b1, w2, b2, w3, b3, w4, freq, deltas)
    return pl.pallas_call(
        _filter_kernel,
        grid=(n // tm,),
        in_specs=[pl.BlockSpec((tm, zfeat.shape[1]), lambda i: (i, 0))] + [full(a) for a in ops],
        out_specs=pl.BlockSpec((2, tm, cw), lambda i: (0, i, 0)),
        out_shape=jax.ShapeDtypeStruct((2, n, cw), F32),
        compiler_params=_params(("parallel",)),
        name="hyena_filters",
    )(zfeat, *ops)


def _dft_first_kernel(f_ref, x_ref, o_ref):
    _, n1h, tm2, cw = x_ref.shape
    x = x_ref[0].reshape(n1h * tm2, cw).astype(BF16)
    r = jnp.dot(f_ref[...], x, preferred_element_type=F32)
    o_ref[0] = r.astype(BF16).reshape(o_ref.shape[1:])


def _dft_first(f1, x):
    bsz, n1h, n2, cw = x.shape
    tm2 = HALO
    nkp = f1.shape[0] // (2 * tm2)
    return pl.pallas_call(
        _dft_first_kernel,
        grid=(bsz, n2 // tm2),
        in_specs=[pl.BlockSpec(f1.shape, lambda b, j: (0, 0)),
                  pl.BlockSpec((1, n1h, tm2, cw), lambda b, j: (b, 0, j, 0))],
        out_specs=pl.BlockSpec((1, 2, nkp, tm2, cw), lambda b, j: (b, 0, 0, j, 0)),
        out_shape=jax.ShapeDtypeStruct((bsz, 2, nkp, n2, cw), BF16),
        compiler_params=_params(("parallel", "parallel")),
        name="dft_first",
    )(f1, x)


def _dft_last_gate_kernel(f_ref, b_ref, x0_ref, u_ref, bias_ref, o_ref):
    _, _, nkp, tm2, cw = b_ref.shape
    y = jnp.dot(f_ref[...], b_ref[0].reshape(2 * nkp * tm2, cw), preferred_element_type=F32)
    y = y.reshape(x0_ref.shape[1:])
    o_ref[0] = (x0_ref[0].astype(F32) * (y + u_ref[0].astype(F32) * bias_ref[...])).astype(BF16)


def _dft_last_gate(f2, b, x0, u, bias):
    bsz, _, nkp, n2, cw = b.shape
    tm2 = HALO
    n1h = f2.shape[0] // tm2
    tblk = pl.BlockSpec((1, n1h, tm2, cw), lambda b_, j: (b_, 0, j, 0))
    return pl.pallas_call(
        _dft_last_gate_kernel,
        grid=(bsz, n2 // tm2),
        in_specs=[pl.BlockSpec(f2.shape, lambda b_, j: (0, 0)),
                  pl.BlockSpec((1, 2, nkp, tm2, cw), lambda b_, j: (b_, 0, 0, j, 0)),
                  tblk, tblk, pl.BlockSpec((1, cw), lambda b_, j: (0, 0))],
        out_specs=tblk,
        out_shape=jax.ShapeDtypeStruct((bsz, n1h, n2, cw), BF16),
        compiler_params=_params(("parallel", "parallel")),
        name="dft_last_gate",
    )(f2, b, x0, u, bias.reshape(1, cw))


def _spec_fwd(a, m_ref):
    return jnp.dot(m_ref[0], a.reshape(2 * FFT_N2, a.shape[-1]), preferred_element_type=F32)


def _filter_spec_kernel(a_ref, m_ref, o_ref):
    xf = _spec_fwd(a_ref[0, :, 0], m_ref)
    xb = _spec_fwd(a_ref[1, :, 0], m_ref)
    o_ref[0, 0] = xf[:FFT_N2] + xb[:FFT_N2]
    o_ref[1, 0] = xf[FFT_N2:] - xb[FFT_N2:]


def _conv_spec_kernel(a_ref, k_ref, m_ref, i_ref, o_ref):
    kr, ki = k_ref[0, 0], k_ref[1, 0]
    for b in range(a_ref.shape[0]):
        x = _spec_fwd(a_ref[b, :, 0], m_ref)
        xr, xi = x[:FFT_N2], x[FFT_N2:]
        y = jnp.concatenate([xr * kr - xi * ki, xr * ki + xi * kr], axis=0).astype(BF16)
        bv = jnp.dot(i_ref[0], y, preferred_element_type=F32)
        o_ref[b, :, 0] = bv.astype(BF16).reshape(2, FFT_N2, bv.shape[-1])


def _filter_spec(a, m_fwd):
    _, _, nkp, _, cw = a.shape
    return pl.pallas_call(
        _filter_spec_kernel,
        grid=(nkp,),
        in_specs=[pl.BlockSpec((2, 2, 1, FFT_N2, cw), lambda k: (0, 0, k, 0, 0)),
                  pl.BlockSpec((1, 2 * FFT_N2, 2 * FFT_N2), lambda k: (k, 0, 0))],
        out_specs=pl.BlockSpec((2, 1, FFT_N2, cw), lambda k: (0, k, 0, 0)),
        out_shape=jax.ShapeDtypeStruct((2, nkp, FFT_N2, cw), F32),
        compiler_params=_params(("parallel",)),
        name="filter_spectrum",
    )(a, m_fwd)


def _conv_spec(a, kspec, m_fwd, m_inv):
    bsz, _, nkp, _, cw = a.shape
    blk = pl.BlockSpec((bsz, 2, 1, FFT_N2, cw), lambda k: (0, 0, k, 0, 0))
    mat = pl.BlockSpec((1, 2 * FFT_N2, 2 * FFT_N2), lambda k: (k, 0, 0))
    return pl.pallas_call(
        _conv_spec_kernel,
        grid=(nkp,),
        in_specs=[blk, pl.BlockSpec((2, 1, FFT_N2, cw), lambda k: (0, k, 0, 0)), mat, mat],
        out_specs=blk,
        out_shape=jax.ShapeDtypeStruct(a.shape, BF16),
        compiler_params=_params(("parallel",)),
        name="conv_spectrum",
    )(a, kspec, m_fwd, m_inv)


@functools.lru_cache(maxsize=None)
def _dft_tables(n):
    nfft = 2 * n
    n2 = FFT_N2
    n1 = nfft // n2
    n1h = n1 // 2
    nk = n1h + 1
    nkp = -(-nk // SUB) * SUB

    def cs(num, den):
        ang = (num % den).astype(np.float64) * (2.0 * math.pi / den)
        return np.cos(ang), np.sin(ang)

    def pad_k(t, axis):
        widths = [(0, 0)] * t.ndim
        widths[axis] = (0, nkp - nk)
        return np.pad(t, widths)

    k1 = np.arange(nk, dtype=np.int64)
    c1, s1 = cs(k1[:, None] * np.arange(n1h)[None, :], n1)
    f1 = np.concatenate([pad_k(c1, 0), pad_k(-s1, 0)], axis=0)
    kk = k1[:, None, None] + n1 * np.arange(n2)[None, :, None]
    cg, sg = cs(kk * np.arange(n2)[None, None, :], nfft)
    g_re, g_im = cg, -sg
    m_fwd = np.concatenate([np.concatenate([g_re, -g_im], axis=2),
                            np.concatenate([g_im, g_re], axis=2)], axis=1)
    gt_re, gt_im = np.swapaxes(g_re, 1, 2), np.swapaxes(g_im, 1, 2)
    m_inv = np.concatenate([np.concatenate([gt_re, gt_im], axis=2),
                            np.concatenate([-gt_im, gt_re], axis=2)], axis=1)
    wk = np.where((k1 == 0) | (k1 == n1h), 1.0, 2.0)[None, :] * (1.0 / nfft)
    c2, s2 = cs(np.arange(n1h)[:, None] * k1[None, :], n1)
    f2 = np.concatenate([pad_k(c2 * wk, 1), pad_k(-s2 * wk, 1)], axis=1)
    eye = np.eye(HALO)
    return tuple(t.astype(np.float32) for t in (np.kron(f1, eye), pad_k(m_fwd, 0), pad_k(m_inv, 0), np.kron(f2, eye)))


def _hyena_long_conv(u, x0, bias, hcat, tabs):
    f1, m_fwd, m_inv, f2 = tabs
    bsz, n, cw = u.shape
    n2 = FFT_N2
    n1h = n // n2
    kspec = _filter_spec(_dft_first(f1, hcat.reshape(2, n1h, n2, cw)), m_fwd)
    u4 = u.reshape(bsz, n1h, n2, cw)
    b_u = _conv_spec(_dft_first(f1, u4), kspec, m_fwd, m_inv)
    return _dft_last_gate(f2, b_u, x0.reshape(bsz, n1h, n2, cw), u4, bias).reshape(bsz, n, cw)


def _axial_angles(rows, rope_dim):
    row_idx = np.repeat(np.arange(rows), GRID_W).astype(np.float64)
    col_idx = np.tile(np.arange(GRID_W), rows).astype(np.float64)
    d_axis = rope_dim // 2
    inv_freq = ROPE_THETA ** (-np.arange(0, d_axis, 2, dtype=np.float64) / d_axis)
    ang = np.concatenate([row_idx[:, None] * inv_freq, col_idx[:, None] * inv_freq], axis=-1)
    return np.cos(ang), np.sin(ang)


@functools.lru_cache(maxsize=None)
def _rope_tables(n):
    rows = n // GRID_W
    sign = np.tile(np.array([-1.0, 1.0]), LANES // 2)[None, :]
    cos, sin = _axial_angles(rows, HEAD_DIM)
    chd = np.tile(np.repeat(cos, 2, axis=1), (1, LANES // HEAD_DIM))
    shd = np.tile(np.repeat(sin, 2, axis=1), (1, LANES // HEAD_DIM)) * sign
    cos, sin = _axial_angles(rows, B_ROPE)
    ones = np.ones((n, LANES - 2 * B_ROPE))
    cb = np.concatenate([np.tile(np.repeat(cos, 2, axis=1), (1, 2)), ones], axis=1)
    sb = np.concatenate([np.tile(np.repeat(sin, 2, axis=1), (1, 2)), 0.0 * ones], axis=1) * sign
    return tuple(t.astype(np.float32) for t in (chd, shd, cb, sb))


@functools.lru_cache(maxsize=None)
def _hyena_features(n, cwid):
    t = np.linspace(0.0, 1.0, n)[:, None]
    wpos = 2 * math.pi * np.arange(n)[:, None] / n
    fb = np.linspace(1e-4, C_BANDS - 1, C_BANDS)[None, :]
    zfeat = np.concatenate([t, np.cos(fb * wpos), -np.sin(fb * wpos), np.zeros((n, SUB - C_EMB_DIM))], axis=-1)
    deltas = np.abs(np.linspace(C_MIN_DECAY, C_MAX_DECAY, cwid))[None, :]
    return zfeat.astype(np.float32), deltas.astype(np.float32)


def _even_weights(w_in, a_qn, a_kn, b_qn, b_w_uq, b_kvn, b_w_ukv):
    d = w_in.shape[0]
    aq_w = w_in.shape[1] - (B_Q_RANK + 2 * A_KV_HEADS * HEAD_DIM + B_KV_RANK + B_ROPE)
    o = [0, aq_w, aq_w + B_Q_RANK]
    o += [o[-1] + A_KV_HEADS * HEAD_DIM, o[-1] + 2 * A_KV_HEADS * HEAD_DIM]
    o += [o[-1] + B_KV_RANK, o[-1] + B_KV_RANK + B_ROPE]
    kr = w_in[:, o[5]:o[6]]
    w_aug = jnp.concatenate([w_in[:, :o[5]], kr, kr, jnp.zeros((d, LANES - 2 * B_ROPE), w_in.dtype)], axis=1)
    heads = b_w_uq.shape[1] // (B_NOPE + B_ROPE)
    uq = b_w_uq.reshape(B_Q_RANK, heads // 2, 2, B_NOPE + B_ROPE)
    wuq = jnp.concatenate([uq[:, :, 0, :B_NOPE], uq[:, :, 1, :B_NOPE], uq[:, :, 0, B_NOPE:], uq[:, :, 1, B_NOPE:],
                           jnp.zeros((B_Q_RANK, heads // 2, MXU_DIM - 2 * (B_NOPE + B_ROPE)), b_w_uq.dtype)], axis=2)
    wuq = wuq.reshape(B_Q_RANK, heads // 2 * MXU_DIM)
    ukv = b_w_ukv.reshape(B_KV_RANK, heads, B_NOPE + B_VDIM)
    wukn = ukv[:, :, :B_NOPE].reshape(B_KV_RANK, heads * B_NOPE)
    wuv = ukv[:, :, B_NOPE:].reshape(B_KV_RANK, heads * B_VDIM)
    gains = (jnp.tile(a_qn, aq_w // HEAD_DIM)[None, :], jnp.tile(a_kn, A_KV_HEADS)[None, :],
             b_qn[None, :], b_kvn[None, :])
    return w_aug.astype(BF16), gains + (wuq.astype(BF16), wukn.astype(BF16), wuv.astype(BF16))


def _ffn_weights(w_up, conv_w, conv_b, w_down):
    cw = jnp.concatenate([conv_w, conv_b[None, :], jnp.zeros((SUB - 4, conv_w.shape[1]), conv_w.dtype)], axis=0)
    return w_up.astype(BF16), cw, w_down.astype(BF16)


FFN_CHUNKS = 1


def kernel(x, c, ctx, c_ctx, w_mod, b_mod, norm_mix, norm_ffn, ev_w_in, ev_w_out, a_q_norm, a_k_norm, b_q_norm, b_w_uq, b_kv_norm, b_w_ukv, od_w_in, od_w_out, d_sink, c_conv_w, c_conv_b, c_filt_w1, c_filt_b1, c_filt_w2, c_filt_b2, c_filt_w3, c_filt_b3, c_filt_w4, c_filt_freq, c_bias, ffn_w_up, ffn_conv_w, ffn_conv_b, ffn_w_down, final_norm):
    bsz, n, d = x.shape
    depth = w_mod.shape[0]
    assert depth == 2 and n % Q_BLOCK == 0 and d % LANES == 0
    tm = ROW_TILE
    rows = -(-(bsz + 1) // SUB) * SUB
    cvec = jnp.concatenate([c, c_ctx[None, :], jnp.zeros((rows - bsz - 1, d), F32)], axis=0)
    mods = _mods(cvec, w_mod, b_mod)
    chd, shd, cb, sb = (jnp.asarray(t) for t in _rope_tables(n))
    tabs = (chd, shd, cb, sb)

    m3 = mods[0].reshape(rows, 1, N_MOD * d)
    w_aug, prep_w = _even_weights(ev_w_in[0], a_q_norm[0], a_k_norm[0], b_q_norm[0], b_w_uq[0],
                                  b_kv_norm[0], b_w_ukv[0])
    qa_l, ka_l, va_l, qb_l, kb_l, vb_l = _prep_even(x, norm_mix[0], m3, None, w_aug, prep_w, tabs, True, tm)
    qa_c, ka_c, va_c, qb_c, kb_c, vb_c = _prep_even(ctx, norm_mix[0], m3, bsz, w_aug, prep_w, tabs, False, tm)
    tq, tk = ATTN_TQ, ATTN_TK
    oa_l = _flash(qa_l, (ka_c, ka_l), (va_c, va_l), 1, tq, tk)
    ob_l = _flash(qb_l, (kb_c, kb_l), (vb_c, vb_l), 2, tq, tk)
    oa_c = _flash(qa_c, (ka_c,), (va_c,), 1, tq, tk)
    ob_c = _flash(qb_c, (kb_c,), (vb_c,), 2, tq, tk)
    w_out = ev_w_out[0].astype(BF16)
    ffn_w = _ffn_weights(ffn_w_up[0], ffn_conv_w[0], ffn_conv_b[0], ffn_w_down[0])
    x = _mix_ffn(x, oa_l, ob_l, w_out, norm_ffn[0], m3, None, *ffn_w, None, tm)
    ctx = _mix_ffn(ctx, oa_c, ob_c, w_out, norm_ffn[0], m3, bsz, *ffn_w, None, tm)

    m3 = mods[1].reshape(rows, 1, N_MOD * d)
    w_in = od_w_in[0].astype(BF16)
    z_c = _inproj(ctx, norm_mix[1], m3, bsz, w_in[:, OD_K:], tm)
    cw = jnp.concatenate([c_conv_w[0], c_conv_b[0][None, :], jnp.zeros((SUB - 4, OD_K - OD_C), F32)], axis=0)
    qd, kd, vd, x0, u = _prep_odd(x, norm_mix[1], m3, w_in, cw, chd, shd, tm)
    kd_c, vd_c = _kv_ctx_odd(z_c)
    od = _window(qd, kd_c, vd_c, kd, vd, d_sink[0])
    zfeat, deltas = (jnp.asarray(t) for t in _hyena_features(n, c_bias.shape[1]))
    w1 = jnp.concatenate([c_filt_w1[0], jnp.zeros((SUB - C_EMB_DIM, c_filt_w1.shape[2]), F32)], axis=0)
    hcat = _filters(zfeat, w1, c_filt_b1[0][None, :], c_filt_w2[0], c_filt_b2[0][None, :], c_filt_w3[0],
                    c_filt_b3[0][None, :], c_filt_w4[0], jnp.concatenate([c_filt_freq[0], jnp.zeros((SUB - 3, c_filt_freq.shape[2]), F32)], axis=0),
                    deltas, tm)
    oc = _hyena_long_conv(u, x0, c_bias[0], hcat, tuple(jnp.asarray(t).astype(BF16) for t in _dft_tables(n)))
    ffn_w = _ffn_weights(ffn_w_up[1], ffn_conv_w[1], ffn_conv_b[1], ffn_w_down[1])
    return _mix_ffn(x, od, oc, od_w_out[0].astype(BF16), norm_ffn[1], m3, None, *ffn_w, final_norm, tm)
```

```python
import functools
import math

import jax
import jax.numpy as jnp
import numpy as np
from jax import lax
from jax.experimental import pallas as pl
from jax.experimental.pallas import tpu as pltpu

F32 = jnp.float32
BF16 = jnp.bfloat16

GRID_W = 64
HEAD_DIM = 64
ROPE_THETA = 10000.0
NORM_EPS = 1e-6
NEG_INF = -1e30
N_MOD = 6
A_KV_HEADS = 2
B_NOPE = 64
B_ROPE = 32
B_VDIM = 64
B_Q_RANK = 384
B_KV_RANK = 256
C_EMB_DIM = 5
C_BANDS = (C_EMB_DIM - 1) // 2
C_MIN_DECAY = math.log(1e-2) / 1.5
C_MAX_DECAY = math.log(1e-2) / 0.3
WINDOW = 128
Q_BLOCK = 128
LOG2E = math.log2(math.e)

LANES = 128
MXU_DIM = 256
VMEM_LIMIT = 56 * 1024 * 1024
HEADS_PER_GROUP = MXU_DIM // HEAD_DIM

ROW_TILE = 512
ATTN_TQ = 1024
ATTN_TK = 256


def _params(sem, vmem=VMEM_LIMIT):
    return pltpu.CompilerParams(dimension_semantics=sem, vmem_limit_bytes=vmem)


def _resident(shape, index_map):
    return pl.BlockSpec(shape, index_map, pipeline_mode=pl.Buffered(1))


def _norm_mod(x, gain, shift, scale):
    inv = lax.rsqrt(jnp.mean(x * x, axis=-1, keepdims=True) + NORM_EPS)
    return (x * inv) * gain * (1.0 + scale) + shift


def _rope_lanes(x, cos, sin_signed):
    lane = lax.broadcasted_iota(jnp.int32, x.shape, 1)
    nxt = pltpu.roll(x, LANES - 1, axis=1)
    prv = pltpu.roll(x, 1, axis=1)
    swapped = jnp.where(lane % 2 == 0, nxt, prv)
    return x * cos + swapped * sin_signed


def _head_rmsnorm_lanes(x, gain):
    lane = lax.broadcasted_iota(jnp.int32, x.shape, 1)
    lo = lane < HEAD_DIM
    sq = x * x
    s_lo = jnp.sum(jnp.where(lo, sq, 0.0), axis=-1, keepdims=True)
    s_hi = jnp.sum(jnp.where(lo, 0.0, sq), axis=-1, keepdims=True)
    ms = jnp.where(lo, s_lo, s_hi) * (1.0 / HEAD_DIM)
    return x * lax.rsqrt(ms + NORM_EPS) * gain


def _dup_heads(x):
    lane = lax.broadcasted_iota(jnp.int32, x.shape, 1)
    lo = lane < HEAD_DIM
    r = pltpu.roll(x, HEAD_DIM, axis=1)
    return jnp.where(lo, x, r), jnp.where(lo, r, x)


def _split_bf16(x):
    hi = x.astype(BF16)
    lo = (x - hi.astype(F32)).astype(BF16)
    return hi, lo


def _dot3(a_hi, a_lo, b_hi, b_lo):
    d = functools.partial(jnp.dot, preferred_element_type=F32)
    return d(a_hi, b_hi) + d(a_hi, b_lo) + d(a_lo, b_hi)


def _mods_kernel(c_ref, w_ref, b_ref, o_ref):
    c = c_ref[...]
    s = c * (1.0 / (1.0 + jnp.exp(-c)))
    s_hi, s_lo = _split_bf16(s)
    w_hi, w_lo = _split_bf16(w_ref[0])
    o_ref[0] = _dot3(s_hi, s_lo, w_hi, w_lo) + b_ref[0]


def _mods(cvec, w_mod, b_mod):
    depth, d, n = w_mod.shape
    rows = cvec.shape[0]
    tn = 1536
    return pl.pallas_call(
        _mods_kernel,
        grid=(depth, n // tn),
        in_specs=[pl.BlockSpec((rows, d), lambda l, j: (0, 0)),
                  pl.BlockSpec((1, d, tn), lambda l, j: (l, 0, j)),
                  pl.BlockSpec((1, 1, tn), lambda l, j: (l, 0, j))],
        out_specs=pl.BlockSpec((1, rows, tn), lambda l, j: (l, 0, j)),
        out_shape=jax.ShapeDtypeStruct((depth, rows, n), F32),
        compiler_params=_params(("arbitrary", "arbitrary")),
        name="mods",
    )(cvec, w_mod, b_mod.reshape(depth, 1, n))


def _mod_spec(d, chunk, row):
    if row is None:
        return pl.BlockSpec((1, 1, d), lambda b, *_: (b, 0, chunk))
    return pl.BlockSpec((1, 1, d), lambda b, *_: (row, 0, chunk))


def _inproj_kernel(x_ref, g_ref, sh_ref, sc_ref, w_ref, o_ref):
    h = _norm_mod(x_ref[0], g_ref[...], sh_ref[0], sc_ref[0])
    o_ref[0] = jnp.dot(h.astype(BF16), w_ref[...], preferred_element_type=F32)


def _inproj(x, gain, mods3, mrow, w, tm):
    bsz, n, d = x.shape
    cols = w.shape[1]
    tm = min(tm, n)
    return pl.pallas_call(
        _inproj_kernel,
        grid=(bsz, n // tm),
        in_specs=[pl.BlockSpec((1, tm, d), lambda b, i: (b, i, 0)),
                  pl.BlockSpec((1, d), lambda b, i: (0, 0)),
                  _mod_spec(d, 0, mrow), _mod_spec(d, 1, mrow),
                  _resident((d, cols), lambda b, i: (0, 0))],
        out_specs=pl.BlockSpec((1, tm, cols), lambda b, i: (b, i, 0)),
        out_shape=jax.ShapeDtypeStruct((bsz, n, cols), F32),
        compiler_params=_params(("parallel", "parallel")),
        name="inproj",
    )(x, gain.reshape(1, d), mods3, mods3, w)


EV_AQ, EV_BQ, EV_AK, EV_AV, EV_BKV, EV_BKR, EV_END = 0, 512, 896, 1024, 1152, 1408, 1536


def _prep_even_kernel(x_ref, g_ref, sh_ref, sc_ref, win_ref,
                      aqg_ref, akg_ref, bqg_ref, bkvg_ref, wuq_ref, wukn_ref, wuv_ref,
                      chd_ref, shd_ref, cb_ref, sb_ref,
                      qa_ref, ka_ref, va_ref, qb_ref, kb_ref, vb_ref, z_ref, *, use_rope, a_scale, b_scale):
    h = _norm_mod(x_ref[0], g_ref[...], sh_ref[0], sc_ref[0])
    z_ref[0] = jnp.dot(h.astype(BF16), win_ref[...], preferred_element_type=F32)

    def rope_hd(y):
        return _rope_lanes(y, chd_ref[...], shd_ref[...]) if use_rope else y

    def rope_b(y):
        return _rope_lanes(y, cb_ref[...], sb_ref[...]) if use_rope else y

    def put_t(dst, c, y):
        dst[0, c * LANES:(c + 1) * LANES, :] = y.T.astype(BF16)

    for c in range((EV_BQ - EV_AQ) // LANES):
        sl = slice(EV_AQ + c * LANES, EV_AQ + (c + 1) * LANES)
        y = rope_hd(_head_rmsnorm_lanes(z_ref[0, :, sl], aqg_ref[:, c * LANES:(c + 1) * LANES]))
        put_t(qa_ref, c, y * a_scale)
    k = rope_hd(_head_rmsnorm_lanes(z_ref[0, :, EV_AK:EV_AV], akg_ref[...]))
    d0, d1 = _dup_heads(k)
    d0 = d0.astype(BF16)
    d1 = d1.astype(BF16)
    ka_ref[0, :, 0 * LANES:1 * LANES] = d0
    ka_ref[0, :, 1 * LANES:2 * LANES] = d0
    ka_ref[0, :, 2 * LANES:3 * LANES] = d1
    ka_ref[0, :, 3 * LANES:4 * LANES] = d1
    d0, d1 = (d.T.astype(BF16) for d in _dup_heads(z_ref[0, :, EV_AV:EV_BKV]))
    for c, dv in enumerate((d0, d0, d1, d1)):
        va_ref[0, c * LANES:(c + 1) * LANES, :] = dv
    cq = z_ref[0, :, EV_BQ:EV_AK]
    cq = cq * lax.rsqrt(jnp.mean(cq * cq, axis=-1, keepdims=True) + NORM_EPS) * bqg_ref[...]
    qb = jnp.dot(cq.astype(BF16), wuq_ref[...], preferred_element_type=F32)
    for c in range(qb.shape[1] // MXU_DIM):
        lo = slice(c * MXU_DIM, c * MXU_DIM + LANES)
        hi = slice(c * MXU_DIM + LANES, (c + 1) * MXU_DIM)
        put_t(qb_ref, 2 * c, qb[:, lo] * b_scale)
        put_t(qb_ref, 2 * c + 1, rope_b(qb[:, hi]) * b_scale)
    ckv = z_ref[0, :, EV_BKV:EV_BKR]
    ckv = (ckv * lax.rsqrt(jnp.mean(ckv * ckv, axis=-1, keepdims=True) + NORM_EPS) * bkvg_ref[...]).astype(BF16)
    kn = jnp.dot(ckv, wukn_ref[...], preferred_element_type=F32)
    vb = jnp.dot(ckv, wuv_ref[...], preferred_element_type=F32)
    for c in range(vb.shape[1] // LANES):
        put_t(vb_ref, c, vb[:, c * LANES:(c + 1) * LANES])
    kr = rope_b(z_ref[0, :, EV_BKR:EV_END]).astype(BF16)
    for c in range(kn.shape[1] // LANES):
        kb_ref[0, :, c * MXU_DIM:c * MXU_DIM + LANES] = kn[:, c * LANES:(c + 1) * LANES].astype(BF16)
        kb_ref[0, :, c * MXU_DIM + LANES:(c + 1) * MXU_DIM] = kr


def _prep_even(x, gain, mods3, mrow, w_in, wts, tabs, use_rope, tm):
    bsz, n, d = x.shape
    cols = w_in.shape[1]
    tm = min(tm, n)
    aqg, akg, bqg, bkvg, wuq, wukn, wuv = wts
    chd, shd, cb, sb = tabs
    full = lambda a: _resident(a.shape, lambda b, i: (0,) * a.ndim)
    tab = lambda a: pl.BlockSpec((tm, LANES), (lambda b, i: (i, 0)) if use_rope else (lambda b, i: (0, 0)))
    outs = [(4 * LANES, True), (4 * LANES, False), (4 * LANES, True), (wuq.shape[1], True),
            (2 * wukn.shape[1], False), (wuv.shape[1], True)]
    kern = functools.partial(_prep_even_kernel, use_rope=use_rope, a_scale=HEAD_DIM ** -0.5 * LOG2E,
                             b_scale=(B_NOPE + B_ROPE) ** -0.5 * LOG2E)
    return pl.pallas_call(
        kern,
        grid=(bsz, n // tm),
        in_specs=[pl.BlockSpec((1, tm, d), lambda b, i: (b, i, 0)),
                  pl.BlockSpec((1, d), lambda b, i: (0, 0)),
                  _mod_spec(d, 0, mrow), _mod_spec(d, 1, mrow), full(w_in),
                  full(aqg), full(akg), full(bqg), full(bkvg), full(wuq), full(wukn), full(wuv),
                  tab(chd), tab(shd), tab(cb), tab(sb)],
        out_specs=[pl.BlockSpec((1, w, tm), lambda b, i: (b, 0, i)) if t else
                   pl.BlockSpec((1, tm, w), lambda b, i: (b, i, 0)) for w, t in outs],
        out_shape=[jax.ShapeDtypeStruct((bsz, w, n) if t else (bsz, n, w), BF16) for w, t in outs],
        scratch_shapes=[pltpu.VMEM((1, tm, cols), F32)],
        compiler_params=_params(("parallel", "parallel")),
        name="prep_even",
    )(x, gain.reshape(1, d), mods3, mods3, w_in, aqg, akg, bqg, bkvg, wuq, wukn, wuv, chd, shd, cb, sb)


def _head_lane_masks(rows, nsub):
    lane = lax.broadcasted_iota(jnp.int32, (rows, MXU_DIM), 1)
    kms = []
    for j in range(HEADS_PER_GROUP):
        if nsub == 1:
            km = (lane >= j * HEAD_DIM) & (lane < (j + 1) * HEAD_DIM)
        else:
            jj = j % 2
            km = ((lane >= jj * B_NOPE) & (lane < (jj + 1) * B_NOPE)) | \
                 ((lane >= 2 * B_NOPE + jj * B_ROPE) & (lane < 2 * B_NOPE + (jj + 1) * B_ROPE))
        kms.append(jnp.where(km, 1.0, 0.0).astype(BF16))
    return kms


def _head_row_masks(cols):
    row = lax.broadcasted_iota(jnp.int32, (MXU_DIM, cols), 0)
    return [jnp.where((row >= j * HEAD_DIM) & (row < (j + 1) * HEAD_DIM), 1.0, 0.0).astype(BF16)
            for j in range(HEADS_PER_GROUP)]


def _rows4(vals, cols):
    return jnp.concatenate([jnp.broadcast_to(v, (HEAD_DIM, cols)) for v in vals], axis=0)


def _flash_kernel(*refs, nsub, tk, nsrc, nchain):
    qt_ref = refs[0]
    srcs = [(refs[1 + 2 * s], refs[2 + 2 * s]) for s in range(nsrc)]
    o_ref, kb_ref, vbt_ref, acc_ref = refs[1 + 2 * nsrc:]
    hpu = HEADS_PER_GROUP // nsub
    tq = qt_ref.shape[2]
    tqc = tq // nchain
    nb = kb_ref.shape[0]

    @pl.when(pl.program_id(2) == 0)
    def _build():
        kms = _head_lane_masks(tk, nsub)
        vms = _head_row_masks(tk)
        base = 0
        for k_ref, vt_ref in srcs:
            nblk = k_ref.shape[1] // tk

            def body(i, carry, k_ref=k_ref, base=base):
                r0 = pl.multiple_of(i * tk, tk)
                for j in range(HEADS_PER_GROUP):
                    u = j // hpu
                    kb_ref[base + i, j * tk:(j + 1) * tk, :] = \
                        k_ref[0, pl.ds(r0, tk), u * MXU_DIM:(u + 1) * MXU_DIM] * kms[j]
                return carry

            lax.fori_loop(0, nblk, body, 0)
            for i in range(nblk):
                for j in range(HEADS_PER_GROUP):
                    vbt_ref[base + i, :, j * tk:(j + 1) * tk] = vt_ref[0, :, i * tk:(i + 1) * tk] * vms[j]
            base += nblk

    def block(i, state):
        kb, vbt = kb_ref[i], vbt_ref[i]
        new_state = []
        first = state is None
        for h in range(nchain):
            cs = slice(h * tqc, (h + 1) * tqc)
            parts = [jnp.dot(kb[u * hpu * tk:(u + 1) * hpu * tk, :], qt_ref[0, u * MXU_DIM:(u + 1) * MXU_DIM, cs],
                             preferred_element_type=F32) for u in range(nsub)]
            ps, alphas, m_new, l_new = [], [], [], []
            for j in range(HEADS_PER_GROUP):
                sj = parts[j // hpu][(j % hpu) * tk:(j % hpu + 1) * tk, :]
                mj = jnp.max(sj, axis=0, keepdims=True)
                if not first:
                    m, l = state[h]
                    mj = jnp.maximum(m[j], mj)
                    a = jnp.exp2(m[j] - mj)
                    alphas.append(a)
                p = jnp.exp2(sj - mj)
                lj = jnp.sum(p, axis=0, keepdims=True)
                l_new.append(lj if first else a * l[j] + lj)
                m_new.append(mj)
                ps.append(p.astype(BF16))
            pv = jnp.dot(vbt, jnp.concatenate(ps, axis=0), preferred_element_type=F32)
            acc_ref[:, cs] = pv if first else acc_ref[:, cs] * _rows4(alphas, tqc) + pv
            new_state.append((m_new, l_new))
        return new_state

    state = None
    for i in range(nb):
        state = block(i, state)
    for h in range(nchain):
        cs = slice(h * tqc, (h + 1) * tqc)
        ot = acc_ref[:, cs] * _rows4([1.0 / v for v in state[h][1]], tqc)
        o_ref[0, cs, :] = ot.T.astype(BF16)


def _flash(qt, ksrcs, vtsrcs, nsub, tq, tk):
    bsz, qw, nq = qt.shape
    kw = nsub * MXU_DIM
    groups = qw // kw
    tq = min(tq, nq)
    nchain = max(tq // MXU_DIM, 1)
    in_specs = [pl.BlockSpec((1, kw, tq), lambda b, g, i: (b, g, i))]
    args = [qt]
    nb = 0
    for k, vt in zip(ksrcs, vtsrcs):
        lk = k.shape[1]
        assert lk % tk == 0
        nb += lk // tk
        in_specs += [pl.BlockSpec((1, lk, kw), lambda b, g, i: (b, 0, g)),
                     pl.BlockSpec((1, MXU_DIM, lk), lambda b, g, i: (b, g, 0))]
        args += [k, vt]
    scratch = [pltpu.VMEM((nb, HEADS_PER_GROUP * tk, MXU_DIM), BF16),
               pltpu.VMEM((nb, MXU_DIM, HEADS_PER_GROUP * tk), BF16),
               pltpu.VMEM((MXU_DIM, tq), F32)]
    return pl.pallas_call(
        functools.partial(_flash_kernel, nsub=nsub, tk=tk, nsrc=len(ksrcs), nchain=nchain),
        grid=(bsz, groups, nq // tq),
        in_specs=in_specs,
        out_specs=pl.BlockSpec((1, tq, MXU_DIM), lambda b, g, i: (b, i, g)),
        out_shape=jax.ShapeDtypeStruct((bsz, nq, groups * MXU_DIM), BF16),
        scratch_shapes=scratch,
        compiler_params=_params(("parallel", "parallel", "arbitrary")),
        name="flash_attn",
    )(*args)


HALO = 16


def _ffn_kernel(xp_ref, x_ref, xn_ref, oap_ref, oa_ref, oan_ref, obp_ref, ob_ref, obn_ref, wo_ref, gtm_ref,
                g_ref, sh_ref, sc_ref, gt_ref, wup_ref, cw_ref, wdn_ref, fg_ref,
                o_ref, h_ref, om_ref, *, final_norm, nf):
    i = pl.program_id(1)
    last = pl.num_programs(1) - 1
    tm = x_ref.shape[1]
    fdim = wdn_ref.shape[0]
    tf = fdim // nf
    half = oa_ref.shape[2]
    rows = tm + 2 * HALO
    own = slice(HALO, HALO + tm)
    for r, (a_ref, b_ref) in ((slice(0, HALO), (oap_ref, obp_ref)), (own, (oa_ref, ob_ref)),
                              (slice(HALO + tm, rows), (oan_ref, obn_ref))):
        om_ref[r, :half] = a_ref[0]
        om_ref[r, half:] = b_ref[0]
    ym = gtm_ref[0] * jnp.dot(om_ref[...], wo_ref[...], preferred_element_type=F32)
    x1 = x_ref[0] + ym[own]
    gain, shift, scale = g_ref[...], sh_ref[0], sc_ref[0]
    keep_p = jnp.where(i > 0, 1.0, 0.0)
    keep_n = jnp.where(i < last, 1.0, 0.0)
    h_ref[0:HALO, :] = (_norm_mod(xp_ref[0] + ym[0:HALO], gain, shift, scale) * keep_p).astype(BF16)
    h_ref[own, :] = _norm_mod(x1, gain, shift, scale).astype(BF16)
    h_ref[HALO + tm:, :] = (_norm_mod(xn_ref[0] + ym[HALO + tm:], gain, shift, scale) * keep_n).astype(BF16)

    def conv(u, f, part):
        w = cw_ref[:, cols(part, f)]
        up = pltpu.roll(u, 1, axis=0)[HALO:HALO + tm]
        un = pltpu.roll(u, rows - 1, axis=0)[HALO:HALO + tm]
        return up * w[0:1] + u[HALO:HALO + tm] * w[1:2] + un * w[2:3] + w[3:4]

    def cols(part, f):
        return slice(part * fdim + f * tf, part * fdim + (f + 1) * tf)

    h = h_ref[...]
    up = lambda f: (jnp.dot(h, wup_ref[:, cols(0, f)], preferred_element_type=F32),
                    jnp.dot(h, wup_ref[:, cols(1, f)], preferred_element_type=F32))
    y = None
    ug, uv = up(0)
    for f in range(nf):
        nxt = up(f + 1) if f + 1 < nf else None
        g = conv(ug, f, 0)
        v = conv(uv, f, 1)
        a = (g * (1.0 / (1.0 + jnp.exp(-g))) * v).astype(BF16)
        yf = jnp.dot(a, wdn_ref[f * tf:(f + 1) * tf, :], preferred_element_type=F32)
        y = yf if y is None else y + yf
        if nxt is not None:
            ug, uv = nxt
    out = x1 + gt_ref[0] * y
    if final_norm:
        out = out * lax.rsqrt(jnp.mean(out * out, axis=-1, keepdims=True) + NORM_EPS) * fg_ref[...]
    o_ref[0] = out


def _mix_ffn(x, oa, ob, w_out, gain, mods3, mrow, wup, cw, wdn, final_gain, tm):
    bsz, n, d = x.shape
    half = oa.shape[2]
    tm = min(tm, n)
    hb = tm // HALO
    nh = n // HALO
    final_norm = final_gain is not None
    fg = final_gain if final_norm else gain
    prev = lambda w: pl.BlockSpec((1, HALO, w), lambda b, i: (b, jnp.maximum(i * hb - 1, 0), 0))
    main = lambda w: pl.BlockSpec((1, tm, w), lambda b, i: (b, i, 0))
    nxt = lambda w: pl.BlockSpec((1, HALO, w), lambda b, i: (b, jnp.minimum((i + 1) * hb, nh - 1), 0))
    return pl.pallas_call(
        functools.partial(_ffn_kernel, final_norm=final_norm, nf=FFN_CHUNKS),
        grid=(bsz, n // tm),
        in_specs=[prev(d), main(d), nxt(d), prev(half), main(half), nxt(half), prev(half), main(half), nxt(half),
                  _resident(w_out.shape, lambda b, i: (0, 0)), _mod_spec(d, 2, mrow),
                  pl.BlockSpec((1, d), lambda b, i: (0, 0)),
                  _mod_spec(d, 3, mrow), _mod_spec(d, 4, mrow), _mod_spec(d, 5, mrow),
                  _resident(wup.shape, lambda b, i: (0, 0)),
                  _resident(cw.shape, lambda b, i: (0, 0)),
                  _resident(wdn.shape, lambda b, i: (0, 0)),
                  pl.BlockSpec((1, d), lambda b, i: (0, 0))],
        out_specs=main(d),
        out_shape=jax.ShapeDtypeStruct((bsz, n, d), F32),
        scratch_shapes=[pltpu.VMEM((tm + 2 * HALO, d), BF16), pltpu.VMEM((tm + 2 * HALO, 2 * half), BF16)],
        compiler_params=_params(("parallel", "parallel")),
        name="mix_ffn",
    )(x, x, x, oa, oa, oa, ob, ob, ob, w_out, mods3, gain.reshape(1, d), mods3, mods3, mods3, wup, cw, wdn,
      fg.reshape(1, d))


OD_Q, OD_C, OD_K, OD_V, OD_END = 0, 512, 2048, 2176, 2304
SUB = 8


def _prep_odd_kernel(xp_ref, x_ref, xn_ref, g_ref, sh_ref, sc_ref, win_ref, cw_ref, chd_ref, shd_ref,
                     q_ref, k_ref, v_ref, x0_ref, u_ref, h_ref, z_ref, *, scale):
    i = pl.program_id(1)
    last = pl.num_programs(1) - 1
    tm = x_ref.shape[1]
    rows = tm + 2 * HALO
    own = slice(HALO, HALO + tm)
    cwid = (OD_K - OD_C) // 3
    gain, shift, mscale = g_ref[...], sh_ref[0], sc_ref[0]
    keep_p = jnp.where(i > 0, 1.0, 0.0)
    keep_n = jnp.where(i < last, 1.0, 0.0)
    h_ref[0:HALO, :] = (_norm_mod(xp_ref[0], gain, shift, mscale) * keep_p).astype(BF16)
    h_ref[own, :] = _norm_mod(x_ref[0], gain, shift, mscale).astype(BF16)
    h_ref[HALO + tm:, :] = (_norm_mod(xn_ref[0], gain, shift, mscale) * keep_n).astype(BF16)
    z_ref[...] = jnp.dot(h_ref[...], win_ref[...], preferred_element_type=F32)
    for c in range((OD_C - OD_Q) // LANES):
        sl = slice(OD_Q + c * LANES, OD_Q + (c + 1) * LANES)
        y = _rope_lanes(z_ref[own, sl], chd_ref[...], shd_ref[...])
        q_ref[0, sl, :] = (y * scale).T.astype(BF16)
    k = _rope_lanes(z_ref[own, OD_K:OD_V], chd_ref[...], shd_ref[...])
    d0, d1 = _dup_heads(k)
    for c, dk in enumerate((d0, d0, d1, d1)):
        k_ref[0, :, c * LANES:(c + 1) * LANES] = dk.astype(BF16)
    d0, d1 = (d.T.astype(BF16) for d in _dup_heads(z_ref[own, OD_V:OD_END]))
    for c, dv in enumerate((d0, d0, d1, d1)):
        v_ref[0, c * LANES:(c + 1) * LANES, :] = dv

    def conv(part):
        sl = slice(OD_C + part * cwid, OD_C + (part + 1) * cwid)
        csl = slice(part * cwid, (part + 1) * cwid)
        ue = z_ref[:, sl]
        up = pltpu.roll(ue, 1, axis=0)[own]
        un = pltpu.roll(ue, rows - 1, axis=0)[own]
        return up * cw_ref[0:1, csl] + ue[own] * cw_ref[1:2, csl] + un * cw_ref[2:3, csl] + cw_ref[3:4, csl]

    x0_ref[0] = conv(0).astype(BF16)
    u_ref[0] = (conv(2) * conv(1)).astype(BF16)


def _prep_odd(x, gain, mods3, w_in, cw, chd, shd, tm):
    bsz, n, d = x.shape
    cols = w_in.shape[1]
    tm = min(tm, n)
    hb = tm // HALO
    nh = n // HALO
    cwid = (OD_K - OD_C) // 3
    row_blk = lambda w: pl.BlockSpec((1, tm, w), lambda b, i: (b, i, 0))
    col_blk = lambda w: pl.BlockSpec((1, w, tm), lambda b, i: (b, 0, i))
    slab = 4 * LANES
    return pl.pallas_call(
        functools.partial(_prep_odd_kernel, scale=HEAD_DIM ** -0.5 * LOG2E),
        grid=(bsz, n // tm),
        in_specs=[pl.BlockSpec((1, HALO, d), lambda b, i: (b, jnp.maximum(i * hb - 1, 0), 0)),
                  pl.BlockSpec((1, tm, d), lambda b, i: (b, i, 0)),
                  pl.BlockSpec((1, HALO, d), lambda b, i: (b, jnp.minimum((i + 1) * hb, nh - 1), 0)),
                  pl.BlockSpec((1, d), lambda b, i: (0, 0)),
                  _mod_spec(d, 0, None), _mod_spec(d, 1, None),
                  _resident(w_in.shape, lambda b, i: (0, 0)),
                  pl.BlockSpec(cw.shape, lambda b, i: (0, 0)),
                  pl.BlockSpec((tm, LANES), lambda b, i: (i, 0)),
                  pl.BlockSpec((tm, LANES), lambda b, i: (i, 0))],
        out_specs=[col_blk(slab), row_blk(slab), col_blk(slab), row_blk(cwid), row_blk(cwid)],
        out_shape=[jax.ShapeDtypeStruct((bsz, slab, n), BF16), jax.ShapeDtypeStruct((bsz, n, slab), BF16),
                   jax.ShapeDtypeStruct((bsz, slab, n), BF16), jax.ShapeDtypeStruct((bsz, n, cwid), BF16),
                   jax.ShapeDtypeStruct((bsz, n, cwid), BF16)],
        scratch_shapes=[pltpu.VMEM((tm + 2 * HALO, d), BF16), pltpu.VMEM((tm + 2 * HALO, cols), F32)],
        compiler_params=_params(("parallel", "parallel")),
        name="prep_odd",
    )(x, x, x, gain.reshape(1, d), mods3, mods3, w_in, cw, chd, shd)


def _kv_ctx_odd_kernel(z_ref, k_ref, vt_ref):
    d0, d1 = _dup_heads(z_ref[0, :, 0:LANES])
    for c, dk in enumerate((d0, d0, d1, d1)):
        k_ref[0, :, c * LANES:(c + 1) * LANES] = dk.astype(BF16)
    d0, d1 = (d.T.astype(BF16) for d in _dup_heads(z_ref[0, :, LANES:2 * LANES]))
    for c, dv in enumerate((d0, d0, d1, d1)):
        vt_ref[0, c * LANES:(c + 1) * LANES, :] = dv


def _kv_ctx_odd(z):
    bsz, n, cols = z.shape
    slab = 4 * LANES
    return pl.pallas_call(
        _kv_ctx_odd_kernel,
        grid=(bsz,),
        in_specs=[pl.BlockSpec((1, n, cols), lambda b: (b, 0, 0))],
        out_specs=[pl.BlockSpec((1, n, slab), lambda b: (b, 0, 0)), pl.BlockSpec((1, slab, n), lambda b: (b, 0, 0))],
        out_shape=[jax.ShapeDtypeStruct((bsz, n, slab), BF16), jax.ShapeDtypeStruct((bsz, slab, n), BF16)],
        compiler_params=_params(("parallel",)),
        name="kv_ctx_odd",
    )(z)


WIN_QB = 2
WIN_TILES = 4


def _window_kernel(qt_ref, kc_ref, vct_ref, kl_ref, vlt_ref, bias_ref, sink_ref, o_ref, kbc, vbct, kbl, vblt,
                   *, lc, nblk, qb, nch):
    qi = pl.program_id(2)
    tq = qb * Q_BLOCK
    nsp = qb + 2
    blk = HEADS_PER_GROUP * Q_BLOCK

    @pl.when(qi == 0)
    def _build():
        kms = _head_lane_masks(lc, 1)
        vms = _head_row_masks(lc)
        for j in range(HEADS_PER_GROUP):
            kbc[j * lc:(j + 1) * lc, :] = kc_ref[0] * kms[j]
            vbct[:, j * lc:(j + 1) * lc] = vct_ref[0] * vms[j]
        kms = _head_lane_masks(Q_BLOCK, 1)
        vms = _head_row_masks(Q_BLOCK)
        for e in (0, nblk + 1):
            kbl[e] = jnp.zeros((blk, MXU_DIM), BF16)
            vblt[e] = jnp.zeros((MXU_DIM, blk), BF16)

        def body(i, carry):
            r0 = pl.multiple_of(i * Q_BLOCK, Q_BLOCK)
            for j in range(HEADS_PER_GROUP):
                kbl[i + 1, j * Q_BLOCK:(j + 1) * Q_BLOCK, :] = kl_ref[0, pl.ds(r0, Q_BLOCK), :] * kms[j]
            return carry

        lax.fori_loop(0, nblk, body, 0)
        for i in range(nblk):
            for j in range(HEADS_PER_GROUP):
                vblt[i + 1, :, j * Q_BLOCK:(j + 1) * Q_BLOCK] = vlt_ref[0, :, i * Q_BLOCK:(i + 1) * Q_BLOCK] * vms[j]

    ntile = nblk // qb
    for h in range(nch):
        t = qi * nch + h
        qt = qt_ref[0, :, h * tq:(h + 1) * tq]
        s_c = jnp.dot(kbc[...], qt, preferred_element_type=F32)
        kspan = kbl[pl.ds(qb * t, nsp)].reshape(nsp * blk, MXU_DIM)
        s_s = jnp.dot(kspan, qt, preferred_element_type=F32) + bias_ref[...]
        edge = {0: jnp.where(t == 0, NEG_INF, 0.0), nsp - 1: jnp.where(t == ntile - 1, NEG_INF, 0.0)}
        pcs, pss, invs = [], [[None] * HEADS_PER_GROUP for _ in range(nsp)], []
        for j in range(HEADS_PER_GROUP):
            sink = sink_ref[0, j][0:1, :]
            segs = [s_c[j * lc:(j + 1) * lc, :]]
            for b in range(nsp):
                sg = s_s[b * blk + j * Q_BLOCK: b * blk + (j + 1) * Q_BLOCK, :]
                segs.append(sg + edge[b] if b in edge else sg)
            m = sink
            for sg in segs:
                m = jnp.maximum(m, jnp.max(sg, axis=0, keepdims=True))
            ps = [jnp.exp2(sg - m) for sg in segs]
            den = jnp.exp2(sink - m)
            for p in ps:
                den = den + jnp.sum(p, axis=0, keepdims=True)
            invs.append(1.0 / den)
            pcs.append(ps[0].astype(BF16))
            for b in range(nsp):
                pss[b][j] = ps[1 + b].astype(BF16)
        ot = jnp.dot(vbct[...], jnp.concatenate(pcs, axis=0), preferred_element_type=F32)
        for b in range(nsp):
            ot = ot + jnp.dot(vblt[qb * t + b], jnp.concatenate(pss[b], axis=0), preferred_element_type=F32)
        o_ref[0, h * tq:(h + 1) * tq, :] = (ot * _rows4(invs, tq)).T.astype(BF16)


@functools.lru_cache(maxsize=None)
def _window_bias(qb):
    nsp = qb + 2
    row = np.arange(nsp * HEADS_PER_GROUP * Q_BLOCK)[:, None]
    r = np.arange(qb * Q_BLOCK)[None, :]
    b = row // (HEADS_PER_GROUP * Q_BLOCK)
    jj = b * Q_BLOCK + row % Q_BLOCK
    band = (jj >= r) & (jj <= r + 2 * WINDOW)
    return np.where(band, 0.0, NEG_INF).astype(np.float32)


def _window(qt, kc, vct, kl, vlt, sink):
    bsz, qw, n = qt.shape
    groups = qw // MXU_DIM
    lc = kc.shape[1]
    nblk = n // Q_BLOCK
    qb = min(WIN_QB, nblk)
    tq = qb * Q_BLOCK
    nch = next(c for c in (WIN_TILES, 2, 1) if n % (c * tq) == 0)
    nq = n // (tq * nch)
    blk = HEADS_PER_GROUP * Q_BLOCK
    bias = jnp.asarray(_window_bias(qb))
    sink_t = jnp.broadcast_to((sink.astype(F32) * LOG2E).reshape(groups, HEADS_PER_GROUP, 1, 1),
                              (groups, HEADS_PER_GROUP, SUB, tq))

    return pl.pallas_call(
        functools.partial(_window_kernel, lc=lc, nblk=nblk, qb=qb, nch=nch),
        grid=(bsz, groups, nq),
        in_specs=[pl.BlockSpec((1, MXU_DIM, tq * nch), lambda b, g, i: (b, g, i)),
                  pl.BlockSpec((1, lc, MXU_DIM), lambda b, g, i: (b, 0, g)),
                  pl.BlockSpec((1, MXU_DIM, lc), lambda b, g, i: (b, g, 0)),
                  pl.BlockSpec((1, n, MXU_DIM), lambda b, g, i: (b, 0, g)),
                  pl.BlockSpec((1, MXU_DIM, n), lambda b, g, i: (b, g, 0)),
                  _resident(bias.shape, lambda b, g, i: (0, 0)),
                  pl.BlockSpec((1, HEADS_PER_GROUP, SUB, tq), lambda b, g, i: (g, 0, 0, 0))],
        out_specs=pl.BlockSpec((1, tq * nch, MXU_DIM), lambda b, g, i: (b, i, g)),
        out_shape=jax.ShapeDtypeStruct((bsz, n, qw), BF16),
        scratch_shapes=[pltpu.VMEM((HEADS_PER_GROUP * lc, MXU_DIM), BF16),
                        pltpu.VMEM((MXU_DIM, HEADS_PER_GROUP * lc), BF16),
                        pltpu.VMEM((nblk + 2, blk, MXU_DIM), BF16),
                        pltpu.VMEM((nblk + 2, MXU_DIM, blk), BF16)],
        compiler_params=_params(("parallel", "parallel", "arbitrary")),
        name="window_attn",
    )(qt, kc, vct, kl, vlt, bias, sink_t)


FFT_N2 = 128


def _filter_kernel(z_ref, w1_ref, b1_ref, w2_ref, b2_ref, w3_ref, b3_ref, w4_ref, fr_ref, dl_ref, o_ref):
    def lin(a, w_ref):
        a_hi, a_lo = _split_bf16(a)
        w_hi, w_lo = _split_bf16(w_ref[...])
        return _dot3(a_hi, a_lo, w_hi, w_lo)

    z = z_ref[...]
    h = b1_ref[...]
    for e in range(C_EMB_DIM):
        h = h + z[:, e:e + 1] * w1_ref[e:e + 1, :]
    h = jnp.sin(fr_ref[0:1] * h)
    h = jnp.sin(fr_ref[1:2] * (lin(h, w2_ref) + b2_ref[...]))
    h = jnp.sin(fr_ref[2:3] * (lin(h, w3_ref) + b3_ref[...]))
    h = lin(h, w4_ref)
    t = z[:, 0:1]
    cw = dl_ref.shape[1]
    dec = jnp.exp(-t * dl_ref[...])
    row = lax.broadcasted_iota(jnp.int32, dec.shape, 0) + pl.program_id(0) * z.shape[0]
    o_ref[0] = h[:, :cw] * dec
    o_ref[1] = jnp.where(row == 0, 0.0, h[:, cw:] * dec)


def _filters(zfeat, w1, b1, w2, b2, w3, b3, w4, freq, deltas, tm):
    n = zfeat.shape[0]
    tm = min(tm, n)
    cw = deltas.shape[1]
    full = lambda a: pl.BlockSpec(a.shape, lambda i: (0,) * a.ndim)
    ops = (w1, b1, w2, b2, w3, b3, w4, freq, deltas)
    return pl.pallas_call(
        _filter_kernel,
        grid=(n // tm,),
        in_specs=[pl.BlockSpec((tm, zfeat.shape[1]), lambda i: (i, 0))] + [full(a) for a in ops],
        out_specs=pl.BlockSpec((2, tm, cw), lambda i: (0, i, 0)),
        out_shape=jax.ShapeDtypeStruct((2, n, cw), F32),
        compiler_params=_params(("parallel",)),
        name="hyena_filters",
    )(zfeat, *ops)


def _dft_first_kernel(f_ref, x_ref, o_ref):
    _, n1h, tm2, cw = x_ref.shape
    x = x_ref[0].reshape(n1h * tm2, cw).astype(BF16)
    r = jnp.dot(f_ref[...], x, preferred_element_type=F32)
    o_ref[0] = r.astype(BF16).reshape(o_ref.shape[1:])


def _dft_first(f1, x):
    bsz, n1h, n2, cw = x.shape
    tm2 = HALO
    nkp = f1.shape[0] // (2 * tm2)
    return pl.pallas_call(
        _dft_first_kernel,
        grid=(bsz, n2 // tm2),
        in_specs=[pl.BlockSpec(f1.shape, lambda b, j: (0, 0)),
                  pl.BlockSpec((1, n1h, tm2, cw), lambda b, j: (b, 0, j, 0))],
        out_specs=pl.BlockSpec((1, 2, nkp, tm2, cw), lambda b, j: (b, 0, 0, j, 0)),
        out_shape=jax.ShapeDtypeStruct((bsz, 2, nkp, n2, cw), BF16),
        compiler_params=_params(("parallel", "parallel")),
        name="dft_first",
    )(f1, x)


def _dft_last_gate_kernel(f_ref, b_ref, x0_ref, u_ref, bias_ref, o_ref):
    _, _, nkp, tm2, cw = b_ref.shape
    y = jnp.dot(f_ref[...], b_ref[0].reshape(2 * nkp * tm2, cw), preferred_element_type=F32)
    y = y.reshape(x0_ref.shape[1:])
    o_ref[0] = (x0_ref[0].astype(F32) * (y + u_ref[0].astype(F32) * bias_ref[...])).astype(BF16)


def _dft_last_gate(f2, b, x0, u, bias):
    bsz, _, nkp, n2, cw = b.shape
    tm2 = HALO
    n1h = f2.shape[0] // tm2
    tblk = pl.BlockSpec((1, n1h, tm2, cw), lambda b_, j: (b_, 0, j, 0))
    return pl.pallas_call(
        _dft_last_gate_kernel,
        grid=(bsz, n2 // tm2),
        in_specs=[pl.BlockSpec(f2.shape, lambda b_, j: (0, 0)),
                  pl.BlockSpec((1, 2, nkp, tm2, cw), lambda b_, j: (b_, 0, 0, j, 0)),
                  tblk, tblk, pl.BlockSpec((1, cw), lambda b_, j: (0, 0))],
        out_specs=tblk,
        out_shape=jax.ShapeDtypeStruct((bsz, n1h, n2, cw), BF16),
        compiler_params=_params(("parallel", "parallel")),
        name="dft_last_gate",
    )(f2, b, x0, u, bias.reshape(1, cw))


def _spec_fwd(a, m_ref):
    return jnp.dot(m_ref[0], a.reshape(2 * FFT_N2, a.shape[-1]), preferred_element_type=F32)


def _filter_spec_kernel(a_ref, m_ref, o_ref):
    for kk in range(a_ref.shape[2]):
        xf = _spec_fwd(a_ref[0, :, kk], m_ref.at[kk:kk + 1])
        xb = _spec_fwd(a_ref[1, :, kk], m_ref.at[kk:kk + 1])
        o_ref[0, kk] = xf[:FFT_N2] + xb[:FFT_N2]
        o_ref[1, kk] = xf[FFT_N2:] - xb[FFT_N2:]


def _conv_spec_kernel(a_ref, k_ref, m_ref, i_ref, o_ref):
    kr, ki = k_ref[0, 0], k_ref[1, 0]
    for b in range(a_ref.shape[0]):
        x = _spec_fwd(a_ref[b, :, 0], m_ref)
        xr, xi = x[:FFT_N2], x[FFT_N2:]
        y = jnp.concatenate([xr * kr - xi * ki, xr * ki + xi * kr], axis=0).astype(BF16)
        bv = jnp.dot(i_ref[0], y, preferred_element_type=F32)
        o_ref[b, :, 0] = bv.astype(BF16).reshape(2, FFT_N2, bv.shape[-1])


def _filter_spec(a, m_fwd):
    _, _, nkp, _, cw = a.shape
    return pl.pallas_call(
        _filter_spec_kernel,
        grid=(nkp // SUB,),
        in_specs=[pl.BlockSpec((2, 2, SUB, FFT_N2, cw), lambda k: (0, 0, k, 0, 0)),
                  pl.BlockSpec((SUB, 2 * FFT_N2, 2 * FFT_N2), lambda k: (k, 0, 0))],
        out_specs=pl.BlockSpec((2, SUB, FFT_N2, cw), lambda k: (0, k, 0, 0)),
        out_shape=jax.ShapeDtypeStruct((2, nkp, FFT_N2, cw), F32),
        compiler_params=_params(("parallel",)),
        name="filter_spectrum",
    )(a, m_fwd)


def _conv_spec(a, kspec, m_fwd, m_inv):
    bsz, _, nkp, _, cw = a.shape
    blk = pl.BlockSpec((bsz, 2, 1, FFT_N2, cw), lambda k: (0, 0, k, 0, 0))
    mat = pl.BlockSpec((1, 2 * FFT_N2, 2 * FFT_N2), lambda k: (k, 0, 0))
    return pl.pallas_call(
        _conv_spec_kernel,
        grid=(nkp,),
        in_specs=[blk, pl.BlockSpec((2, 1, FFT_N2, cw), lambda k: (0, k, 0, 0)), mat, mat],
        out_specs=blk,
        out_shape=jax.ShapeDtypeStruct(a.shape, BF16),
        compiler_params=_params(("parallel",)),
        name="conv_spectrum",
    )(a, kspec, m_fwd, m_inv)


@functools.lru_cache(maxsize=None)
def _dft_tables(n):
    nfft = 2 * n
    n2 = FFT_N2
    n1 = nfft // n2
    n1h = n1 // 2
    nk = n1h + 1
    nkp = -(-nk // SUB) * SUB

    def cs(num, den):
        ang = (num % den).astype(np.float64) * (2.0 * math.pi / den)
        return np.cos(ang), np.sin(ang)

    def pad_k(t, axis):
        widths = [(0, 0)] * t.ndim
        widths[axis] = (0, nkp - nk)
        return np.pad(t, widths)

    k1 = np.arange(nk, dtype=np.int64)
    c1, s1 = cs(k1[:, None] * np.arange(n1h)[None, :], n1)
    f1 = np.concatenate([pad_k(c1, 0), pad_k(-s1, 0)], axis=0)
    kk = k1[:, None, None] + n1 * np.arange(n2)[None, :, None]
    cg, sg = cs(kk * np.arange(n2)[None, None, :], nfft)
    g_re, g_im = cg, -sg
    m_fwd = np.concatenate([np.concatenate([g_re, -g_im], axis=2),
                            np.concatenate([g_im, g_re], axis=2)], axis=1)
    gt_re, gt_im = np.swapaxes(g_re, 1, 2), np.swapaxes(g_im, 1, 2)
    m_inv = np.concatenate([np.concatenate([gt_re, gt_im], axis=2),
                            np.concatenate([-gt_im, gt_re], axis=2)], axis=1)
    wk = np.where((k1 == 0) | (k1 == n1h), 1.0, 2.0)[None, :] * (1.0 / nfft)
    c2, s2 = cs(np.arange(n1h)[:, None] * k1[None, :], n1)
    f2 = np.concatenate([pad_k(c2 * wk, 1), pad_k(-s2 * wk, 1)], axis=1)
    eye = np.eye(HALO)
    return tuple(t.astype(np.float32) for t in (np.kron(f1, eye), pad_k(m_fwd, 0), pad_k(m_inv, 0), np.kron(f2, eye)))


def _hyena_long_conv(u, x0, bias, hcat, tabs):
    f1, m_fwd, m_inv, f2 = tabs
    bsz, n, cw = u.shape
    n2 = FFT_N2
    n1h = n // n2
    kspec = _filter_spec(_dft_first(f1, hcat.reshape(2, n1h, n2, cw)), m_fwd)
    u4 = u.reshape(bsz, n1h, n2, cw)
    b_u = _conv_spec(_dft_first(f1, u4), kspec, m_fwd, m_inv)
    return _dft_last_gate(f2, b_u, x0.reshape(bsz, n1h, n2, cw), u4, bias).reshape(bsz, n, cw)


def _axial_angles(rows, rope_dim):
    row_idx = np.repeat(np.arange(rows), GRID_W).astype(np.float64)
    col_idx = np.tile(np.arange(GRID_W), rows).astype(np.float64)
    d_axis = rope_dim // 2
    inv_freq = ROPE_THETA ** (-np.arange(0, d_axis, 2, dtype=np.float64) / d_axis)
    ang = np.concatenate([row_idx[:, None] * inv_freq, col_idx[:, None] * inv_freq], axis=-1)
    return np.cos(ang), np.sin(ang)


@functools.lru_cache(maxsize=None)
def _rope_tables(n):
    rows = n // GRID_W
    sign = np.tile(np.array([-1.0, 1.0]), LANES // 2)[None, :]
    cos, sin = _axial_angles(rows, HEAD_DIM)
    chd = np.tile(np.repeat(cos, 2, axis=1), (1, LANES // HEAD_DIM))
    shd = np.tile(np.repeat(sin, 2, axis=1), (1, LANES // HEAD_DIM)) * sign
    cos, sin = _axial_angles(rows, B_ROPE)
    ones = np.ones((n, LANES - 2 * B_ROPE))
    cb = np.concatenate([np.tile(np.repeat(cos, 2, axis=1), (1, 2)), ones], axis=1)
    sb = np.concatenate([np.tile(np.repeat(sin, 2, axis=1), (1, 2)), 0.0 * ones], axis=1) * sign
    return tuple(t.astype(np.float32) for t in (chd, shd, cb, sb))


@functools.lru_cache(maxsize=None)
def _hyena_features(n, cwid):
    t = np.linspace(0.0, 1.0, n)[:, None]
    wpos = 2 * math.pi * np.arange(n)[:, None] / n
    fb = np.linspace(1e-4, C_BANDS - 1, C_BANDS)[None, :]
    zfeat = np.concatenate([t, np.cos(fb * wpos), -np.sin(fb * wpos), np.zeros((n, SUB - C_EMB_DIM))], axis=-1)
    deltas = np.abs(np.linspace(C_MIN_DECAY, C_MAX_DECAY, cwid))[None, :]
    return zfeat.astype(np.float32), deltas.astype(np.float32)


def _even_weights(w_in, a_qn, a_kn, b_qn, b_w_uq, b_kvn, b_w_ukv):
    d = w_in.shape[0]
    aq_w = w_in.shape[1] - (B_Q_RANK + 2 * A_KV_HEADS * HEAD_DIM + B_KV_RANK + B_ROPE)
    o = [0, aq_w, aq_w + B_Q_RANK]
    o += [o[-1] + A_KV_HEADS * HEAD_DIM, o[-1] + 2 * A_KV_HEADS * HEAD_DIM]
    o += [o[-1] + B_KV_RANK, o[-1] + B_KV_RANK + B_ROPE]
    kr = w_in[:, o[5]:o[6]]
    w_aug = jnp.concatenate([w_in[:, :o[5]], kr, kr, jnp.zeros((d, LANES - 2 * B_ROPE), w_in.dtype)], axis=1)
    heads = b_w_uq.shape[1] // (B_NOPE + B_ROPE)
    uq = b_w_uq.reshape(B_Q_RANK, heads // 2, 2, B_NOPE + B_ROPE)
    wuq = jnp.concatenate([uq[:, :, 0, :B_NOPE], uq[:, :, 1, :B_NOPE], uq[:, :, 0, B_NOPE:], uq[:, :, 1, B_NOPE:],
                           jnp.zeros((B_Q_RANK, heads // 2, MXU_DIM - 2 * (B_NOPE + B_ROPE)), b_w_uq.dtype)], axis=2)
    wuq = wuq.reshape(B_Q_RANK, heads // 2 * MXU_DIM)
    ukv = b_w_ukv.reshape(B_KV_RANK, heads, B_NOPE + B_VDIM)
    wukn = ukv[:, :, :B_NOPE].reshape(B_KV_RANK, heads * B_NOPE)
    wuv = ukv[:, :, B_NOPE:].reshape(B_KV_RANK, heads * B_VDIM)
    gains = (jnp.tile(a_qn, aq_w // HEAD_DIM)[None, :], jnp.tile(a_kn, A_KV_HEADS)[None, :],
             b_qn[None, :], b_kvn[None, :])
    return w_aug.astype(BF16), gains + (wuq.astype(BF16), wukn.astype(BF16), wuv.astype(BF16))


def _ffn_weights(w_up, conv_w, conv_b, w_down):
    cw = jnp.concatenate([conv_w, conv_b[None, :], jnp.zeros((SUB - 4, conv_w.shape[1]), conv_w.dtype)], axis=0)
    return w_up.astype(BF16), cw, w_down.astype(BF16)


FFN_CHUNKS = 1


def kernel(x, c, ctx, c_ctx, w_mod, b_mod, norm_mix, norm_ffn, ev_w_in, ev_w_out, a_q_norm, a_k_norm, b_q_norm, b_w_uq, b_kv_norm, b_w_ukv, od_w_in, od_w_out, d_sink, c_conv_w, c_conv_b, c_filt_w1, c_filt_b1, c_filt_w2, c_filt_b2, c_filt_w3, c_filt_b3, c_filt_w4, c_filt_freq, c_bias, ffn_w_up, ffn_conv_w, ffn_conv_b, ffn_w_down, final_norm):
    bsz, n, d = x.shape
    depth = w_mod.shape[0]
    assert depth == 2 and n % Q_BLOCK == 0 and d % LANES == 0
    tm = ROW_TILE
    rows = -(-(bsz + 1) // SUB) * SUB
    cvec = jnp.concatenate([c, c_ctx[None, :], jnp.zeros((rows - bsz - 1, d), F32)], axis=0)
    mods = _mods(cvec, w_mod, b_mod)
    chd, shd, cb, sb = (jnp.asarray(t) for t in _rope_tables(n))
    tabs = (chd, shd, cb, sb)

    m3 = mods[0].reshape(rows, 1, N_MOD * d)
    w_aug, prep_w = _even_weights(ev_w_in[0], a_q_norm[0], a_k_norm[0], b_q_norm[0], b_w_uq[0],
                                  b_kv_norm[0], b_w_ukv[0])
    qa_l, ka_l, va_l, qb_l, kb_l, vb_l = _prep_even(x, norm_mix[0], m3, None, w_aug, prep_w, tabs, True, tm)
    qa_c, ka_c, va_c, qb_c, kb_c, vb_c = _prep_even(ctx, norm_mix[0], m3, bsz, w_aug, prep_w, tabs, False, tm)
    tq, tk = ATTN_TQ, ATTN_TK
    oa_l = _flash(qa_l, (ka_c, ka_l), (va_c, va_l), 1, tq, tk)
    ob_l = _flash(qb_l, (kb_c, kb_l), (vb_c, vb_l), 2, tq, tk)
    oa_c = _flash(qa_c, (ka_c,), (va_c,), 1, tq, tk)
    ob_c = _flash(qb_c, (kb_c,), (vb_c,), 2, tq, tk)
    w_out = ev_w_out[0].astype(BF16)
    ffn_w = _ffn_weights(ffn_w_up[0], ffn_conv_w[0], ffn_conv_b[0], ffn_w_down[0])
    x = _mix_ffn(x, oa_l, ob_l, w_out, norm_ffn[0], m3, None, *ffn_w, None, tm)
    ctx = _mix_ffn(ctx, oa_c, ob_c, w_out, norm_ffn[0], m3, bsz, *ffn_w, None, tm)

    m3 = mods[1].reshape(rows, 1, N_MOD * d)
    w_in = od_w_in[0].astype(BF16)
    z_c = _inproj(ctx, norm_mix[1], m3, bsz, w_in[:, OD_K:], tm)
    cw = jnp.concatenate([c_conv_w[0], c_conv_b[0][None, :], jnp.zeros((SUB - 4, OD_K - OD_C), F32)], axis=0)
    qd, kd, vd, x0, u = _prep_odd(x, norm_mix[1], m3, w_in, cw, chd, shd, tm)
    kd_c, vd_c = _kv_ctx_odd(z_c)
    od = _window(qd, kd_c, vd_c, kd, vd, d_sink[0])
    zfeat, deltas = (jnp.asarray(t) for t in _hyena_features(n, c_bias.shape[1]))
    w1 = jnp.concatenate([c_filt_w1[0], jnp.zeros((SUB - C_EMB_DIM, c_filt_w1.shape[2]), F32)], axis=0)
    hcat = _filters(zfeat, w1, c_filt_b1[0][None, :], c_filt_w2[0], c_filt_b2[0][None, :], c_filt_w3[0],
                    c_filt_b3[0][None, :], c_filt_w4[0], jnp.concatenate([c_filt_freq[0], jnp.zeros((SUB - 3, c_filt_freq.shape[2]), F32)], axis=0),
                    deltas, tm)
    oc = _hyena_long_conv(u, x0, c_bias[0], hcat, tuple(jnp.asarray(t).astype(BF16) for t in _dft_tables(n)))
    ffn_w = _ffn_weights(ffn_w_up[1], ffn_conv_w[1], ffn_conv_b[1], ffn_w_down[1])
    return _mix_ffn(x, od, oc, od_w_out[0].astype(BF16), norm_ffn[1], m3, None, *ffn_w, final_norm, tm)
```

```python
import functools
import math

import jax
import jax.numpy as jnp
import numpy as np
from jax import lax
from jax.experimental import pallas as pl
from jax.experimental.pallas import tpu as pltpu

F32 = jnp.float32
BF16 = jnp.bfloat16

GRID_W = 64
HEAD_DIM = 64
ROPE_THETA = 10000.0
NORM_EPS = 1e-6
NEG_INF = -1e30
N_MOD = 6
A_KV_HEADS = 2
B_NOPE = 64
B_ROPE = 32
B_VDIM = 64
B_Q_RANK = 384
B_KV_RANK = 256
C_EMB_DIM = 5
C_BANDS = (C_EMB_DIM - 1) // 2
C_MIN_DECAY = math.log(1e-2) / 1.5
C_MAX_DECAY = math.log(1e-2) / 0.3
WINDOW = 128
Q_BLOCK = 128
LOG2E = math.log2(math.e)

LANES = 128
MXU_DIM = 256
VMEM_LIMIT = 56 * 1024 * 1024
HEADS_PER_GROUP = MXU_DIM // HEAD_DIM

ROW_TILE = 512
ATTN_TQ = 1024
ATTN_TK = 256


def _params(sem, vmem=VMEM_LIMIT):
    return pltpu.CompilerParams(dimension_semantics=sem, vmem_limit_bytes=vmem)


def _resident(shape, index_map):
    return pl.BlockSpec(shape, index_map, pipeline_mode=pl.Buffered(1))


def _norm_mod(x, gain, shift, scale):
    inv = lax.rsqrt(jnp.mean(x * x, axis=-1, keepdims=True) + NORM_EPS)
    return (x * inv) * gain * (1.0 + scale) + shift


def _rope_lanes(x, cos, sin_signed):
    lane = lax.broadcasted_iota(jnp.int32, x.shape, 1)
    nxt = pltpu.roll(x, LANES - 1, axis=1)
    prv = pltpu.roll(x, 1, axis=1)
    swapped = jnp.where(lane % 2 == 0, nxt, prv)
    return x * cos + swapped * sin_signed


def _head_rmsnorm_lanes(x, gain):
    lane = lax.broadcasted_iota(jnp.int32, x.shape, 1)
    lo = lane < HEAD_DIM
    sq = x * x
    s_lo = jnp.sum(jnp.where(lo, sq, 0.0), axis=-1, keepdims=True)
    s_hi = jnp.sum(jnp.where(lo, 0.0, sq), axis=-1, keepdims=True)
    ms = jnp.where(lo, s_lo, s_hi) * (1.0 / HEAD_DIM)
    return x * lax.rsqrt(ms + NORM_EPS) * gain


def _dup_heads(x):
    lane = lax.broadcasted_iota(jnp.int32, x.shape, 1)
    lo = lane < HEAD_DIM
    r = pltpu.roll(x, HEAD_DIM, axis=1)
    return jnp.where(lo, x, r), jnp.where(lo, r, x)


def _split_bf16(x):
    hi = x.astype(BF16)
    lo = (x - hi.astype(F32)).astype(BF16)
    return hi, lo


def _dot3(a_hi, a_lo, b_hi, b_lo):
    d = functools.partial(jnp.dot, preferred_element_type=F32)
    return d(a_hi, b_hi) + d(a_hi, b_lo) + d(a_lo, b_hi)


def _mods_kernel(c_ref, w_ref, b_ref, o_ref):
    c = c_ref[...]
    s = c * (1.0 / (1.0 + jnp.exp(-c)))
    s_hi, s_lo = _split_bf16(s)
    w_hi, w_lo = _split_bf16(w_ref[0])
    o_ref[0] = _dot3(s_hi, s_lo, w_hi, w_lo) + b_ref[0]


def _mods(cvec, w_mod, b_mod):
    depth, d, n = w_mod.shape
    rows = cvec.shape[0]
    tn = 1536
    return pl.pallas_call(
        _mods_kernel,
        grid=(depth, n // tn),
        in_specs=[pl.BlockSpec((rows, d), lambda l, j: (0, 0)),
                  pl.BlockSpec((1, d, tn), lambda l, j: (l, 0, j)),
                  pl.BlockSpec((1, 1, tn), lambda l, j: (l, 0, j))],
        out_specs=pl.BlockSpec((1, rows, tn), lambda l, j: (l, 0, j)),
        out_shape=jax.ShapeDtypeStruct((depth, rows, n), F32),
        compiler_params=_params(("arbitrary", "arbitrary")),
        name="mods",
    )(cvec, w_mod, b_mod.reshape(depth, 1, n))


def _mod_spec(d, chunk, row):
    if row is None:
        return pl.BlockSpec((1, 1, d), lambda b, *_: (b, 0, chunk))
    return pl.BlockSpec((1, 1, d), lambda b, *_: (row, 0, chunk))


def _inproj_kernel(x_ref, g_ref, sh_ref, sc_ref, w_ref, o_ref):
    h = _norm_mod(x_ref[0], g_ref[...], sh_ref[0], sc_ref[0])
    o_ref[0] = jnp.dot(h.astype(BF16), w_ref[...], preferred_element_type=F32)


def _inproj(x, gain, mods3, mrow, w, tm):
    bsz, n, d = x.shape
    cols = w.shape[1]
    tm = min(tm, n)
    return pl.pallas_call(
        _inproj_kernel,
        grid=(bsz, n // tm),
        in_specs=[pl.BlockSpec((1, tm, d), lambda b, i: (b, i, 0)),
                  pl.BlockSpec((1, d), lambda b, i: (0, 0)),
                  _mod_spec(d, 0, mrow), _mod_spec(d, 1, mrow),
                  _resident((d, cols), lambda b, i: (0, 0))],
        out_specs=pl.BlockSpec((1, tm, cols), lambda b, i: (b, i, 0)),
        out_shape=jax.ShapeDtypeStruct((bsz, n, cols), F32),
        compiler_params=_params(("parallel", "parallel")),
        name="inproj",
    )(x, gain.reshape(1, d), mods3, mods3, w)


EV_AQ, EV_BQ, EV_AK, EV_AV, EV_BKV, EV_BKR, EV_END = 0, 512, 896, 1024, 1152, 1408, 1536


def _prep_even_kernel(x_ref, g_ref, sh_ref, sc_ref, win_ref,
                      aqg_ref, akg_ref, bqg_ref, bkvg_ref, wuq_ref, wukn_ref, wuv_ref,
                      chd_ref, shd_ref, cb_ref, sb_ref,
                      qa_ref, ka_ref, va_ref, qb_ref, kb_ref, vb_ref, z_ref, *, use_rope, a_scale, b_scale):
    h = _norm_mod(x_ref[0], g_ref[...], sh_ref[0], sc_ref[0])
    z_ref[0] = jnp.dot(h.astype(BF16), win_ref[...], preferred_element_type=F32)

    def rope_hd(y):
        return _rope_lanes(y, chd_ref[...], shd_ref[...]) if use_rope else y

    def rope_b(y):
        return _rope_lanes(y, cb_ref[...], sb_ref[...]) if use_rope else y

    def put_t(dst, c, y):
        dst[0, c * LANES:(c + 1) * LANES, :] = y.T.astype(BF16)

    for c in range((EV_BQ - EV_AQ) // LANES):
        sl = slice(EV_AQ + c * LANES, EV_AQ + (c + 1) * LANES)
        y = rope_hd(_head_rmsnorm_lanes(z_ref[0, :, sl], aqg_ref[:, c * LANES:(c + 1) * LANES]))
        put_t(qa_ref, c, y * a_scale)
    k = rope_hd(_head_rmsnorm_lanes(z_ref[0, :, EV_AK:EV_AV], akg_ref[...]))
    d0, d1 = _dup_heads(k)
    d0 = d0.astype(BF16)
    d1 = d1.astype(BF16)
    ka_ref[0, :, 0 * LANES:1 * LANES] = d0
    ka_ref[0, :, 1 * LANES:2 * LANES] = d0
    ka_ref[0, :, 2 * LANES:3 * LANES] = d1
    ka_ref[0, :, 3 * LANES:4 * LANES] = d1
    d0, d1 = (d.T.astype(BF16) for d in _dup_heads(z_ref[0, :, EV_AV:EV_BKV]))
    for c, dv in enumerate((d0, d0, d1, d1)):
        va_ref[0, c * LANES:(c + 1) * LANES, :] = dv
    cq = z_ref[0, :, EV_BQ:EV_AK]
    cq = cq * lax.rsqrt(jnp.mean(cq * cq, axis=-1, keepdims=True) + NORM_EPS) * bqg_ref[...]
    qb = jnp.dot(cq.astype(BF16), wuq_ref[...], preferred_element_type=F32)
    for c in range(qb.shape[1] // MXU_DIM):
        lo = slice(c * MXU_DIM, c * MXU_DIM + LANES)
        hi = slice(c * MXU_DIM + LANES, (c + 1) * MXU_DIM)
        put_t(qb_ref, 2 * c, qb[:, lo] * b_scale)
        put_t(qb_ref, 2 * c + 1, rope_b(qb[:, hi]) * b_scale)
    ckv = z_ref[0, :, EV_BKV:EV_BKR]
    ckv = (ckv * lax.rsqrt(jnp.mean(ckv * ckv, axis=-1, keepdims=True) + NORM_EPS) * bkvg_ref[...]).astype(BF16)
    kn = jnp.dot(ckv, wukn_ref[...], preferred_element_type=F32)
    vb = jnp.dot(ckv, wuv_ref[...], preferred_element_type=F32)
    for c in range(vb.shape[1] // LANES):
        put_t(vb_ref, c, vb[:, c * LANES:(c + 1) * LANES])
    kr = rope_b(z_ref[0, :, EV_BKR:EV_END]).astype(BF16)
    for c in range(kn.shape[1] // LANES):
        kb_ref[0, :, c * MXU_DIM:c * MXU_DIM + LANES] = kn[:, c * LANES:(c + 1) * LANES].astype(BF16)
        kb_ref[0, :, c * MXU_DIM + LANES:(c + 1) * MXU_DIM] = kr


def _prep_even(x, gain, mods3, mrow, w_in, wts, tabs, use_rope, tm):
    bsz, n, d = x.shape
    cols = w_in.shape[1]
    tm = min(tm, n)
    aqg, akg, bqg, bkvg, wuq, wukn, wuv = wts
    chd, shd, cb, sb = tabs
    full = lambda a: _resident(a.shape, lambda b, i: (0,) * a.ndim)
    tab = lambda a: pl.BlockSpec((tm, LANES), (lambda b, i: (i, 0)) if use_rope else (lambda b, i: (0, 0)))
    outs = [(4 * LANES, True), (4 * LANES, False), (4 * LANES, True), (wuq.shape[1], True),
            (2 * wukn.shape[1], False), (wuv.shape[1], True)]
    kern = functools.partial(_prep_even_kernel, use_rope=use_rope, a_scale=HEAD_DIM ** -0.5 * LOG2E,
                             b_scale=(B_NOPE + B_ROPE) ** -0.5 * LOG2E)
    return pl.pallas_call(
        kern,
        grid=(bsz, n // tm),
        in_specs=[pl.BlockSpec((1, tm, d), lambda b, i: (b, i, 0)),
                  pl.BlockSpec((1, d), lambda b, i: (0, 0)),
                  _mod_spec(d, 0, mrow), _mod_spec(d, 1, mrow), full(w_in),
                  full(aqg), full(akg), full(bqg), full(bkvg), full(wuq), full(wukn), full(wuv),
                  tab(chd), tab(shd), tab(cb), tab(sb)],
        out_specs=[pl.BlockSpec((1, w, tm), lambda b, i: (b, 0, i)) if t else
                   pl.BlockSpec((1, tm, w), lambda b, i: (b, i, 0)) for w, t in outs],
        out_shape=[jax.ShapeDtypeStruct((bsz, w, n) if t else (bsz, n, w), BF16) for w, t in outs],
        scratch_shapes=[pltpu.VMEM((1, tm, cols), F32)],
        compiler_params=_params(("parallel", "parallel")),
        name="prep_even",
    )(x, gain.reshape(1, d), mods3, mods3, w_in, aqg, akg, bqg, bkvg, wuq, wukn, wuv, chd, shd, cb, sb)


def _head_lane_masks(rows, nsub):
    lane = lax.broadcasted_iota(jnp.int32, (rows, MXU_DIM), 1)
    kms = []
    for j in range(HEADS_PER_GROUP):
        if nsub == 1:
            km = (lane >= j * HEAD_DIM) & (lane < (j + 1) * HEAD_DIM)
        else:
            jj = j % 2
            km = ((lane >= jj * B_NOPE) & (lane < (jj + 1) * B_NOPE)) | \
                 ((lane >= 2 * B_NOPE + jj * B_ROPE) & (lane < 2 * B_NOPE + (jj + 1) * B_ROPE))
        kms.append(jnp.where(km, 1.0, 0.0).astype(BF16))
    return kms


def _head_row_masks(cols):
    row = lax.broadcasted_iota(jnp.int32, (MXU_DIM, cols), 0)
    return [jnp.where((row >= j * HEAD_DIM) & (row < (j + 1) * HEAD_DIM), 1.0, 0.0).astype(BF16)
            for j in range(HEADS_PER_GROUP)]


def _rows4(vals, cols):
    return jnp.concatenate([jnp.broadcast_to(v, (HEAD_DIM, cols)) for v in vals], axis=0)


def _flash_kernel(*refs, nsub, tk, nsrc, nchain):
    qt_ref = refs[0]
    srcs = [(refs[1 + 2 * s], refs[2 + 2 * s]) for s in range(nsrc)]
    o_ref, kb_ref, vbt_ref, acc_ref = refs[1 + 2 * nsrc:]
    hpu = HEADS_PER_GROUP // nsub
    tq = qt_ref.shape[2]
    tqc = tq // nchain
    nb = kb_ref.shape[0]

    @pl.when(pl.program_id(2) == 0)
    def _build():
        kms = _head_lane_masks(tk, nsub)
        vms = _head_row_masks(tk)
        base = 0
        for k_ref, vt_ref in srcs:
            nblk = k_ref.shape[1] // tk

            def body(i, carry, k_ref=k_ref, base=base):
                r0 = pl.multiple_of(i * tk, tk)
                for j in range(HEADS_PER_GROUP):
                    u = j // hpu
                    kb_ref[base + i, j * tk:(j + 1) * tk, :] = \
                        k_ref[0, pl.ds(r0, tk), u * MXU_DIM:(u + 1) * MXU_DIM] * kms[j]
                return carry

            lax.fori_loop(0, nblk, body, 0)
            for i in range(nblk):
                for j in range(HEADS_PER_GROUP):
                    vbt_ref[base + i, :, j * tk:(j + 1) * tk] = vt_ref[0, :, i * tk:(i + 1) * tk] * vms[j]
            base += nblk

    def block(i, state):
        kb, vbt = kb_ref[i], vbt_ref[i]
        new_state = []
        first = state is None
        for h in range(nchain):
            cs = slice(h * tqc, (h + 1) * tqc)
            parts = [jnp.dot(kb[u * hpu * tk:(u + 1) * hpu * tk, :], qt_ref[0, u * MXU_DIM:(u + 1) * MXU_DIM, cs],
                             preferred_element_type=F32) for u in range(nsub)]
            ps, alphas, m_new, l_new = [], [], [], []
            for j in range(HEADS_PER_GROUP):
                sj = parts[j // hpu][(j % hpu) * tk:(j % hpu + 1) * tk, :]
                mj = jnp.max(sj, axis=0, keepdims=True)
                if not first:
                    m, l = state[h]
                    mj = jnp.maximum(m[j], mj)
                    a = jnp.exp2(m[j] - mj)
                    alphas.append(a)
                p = jnp.exp2(sj - mj)
                lj = jnp.sum(p, axis=0, keepdims=True)
                l_new.append(lj if first else a * l[j] + lj)
                m_new.append(mj)
                ps.append(p.astype(BF16))
            pv = jnp.dot(vbt, jnp.concatenate(ps, axis=0), preferred_element_type=F32)
            acc_ref[:, cs] = pv if first else acc_ref[:, cs] * _rows4(alphas, tqc) + pv
            new_state.append((m_new, l_new))
        return new_state

    state = None
    for i in range(nb):
        state = block(i, state)
    for h in range(nchain):
        cs = slice(h * tqc, (h + 1) * tqc)
        ot = acc_ref[:, cs] * _rows4([1.0 / v for v in state[h][1]], tqc)
        o_ref[0, cs, :] = ot.T.astype(BF16)


def _flash(qt, ksrcs, vtsrcs, nsub, tq, tk):
    bsz, qw, nq = qt.shape
    kw = nsub * MXU_DIM
    groups = qw // kw
    tq = min(tq, nq)
    nchain = max(tq // MXU_DIM, 1)
    in_specs = [pl.BlockSpec((1, kw, tq), lambda b, g, i: (b, g, i))]
    args = [qt]
    nb = 0
    for k, vt in zip(ksrcs, vtsrcs):
        lk = k.shape[1]
        assert lk % tk == 0
        nb += lk // tk
        in_specs += [pl.BlockSpec((1, lk, kw), lambda b, g, i: (b, 0, g)),
                     pl.BlockSpec((1, MXU_DIM, lk), lambda b, g, i: (b, g, 0))]
        args += [k, vt]
    scratch = [pltpu.VMEM((nb, HEADS_PER_GROUP * tk, MXU_DIM), BF16),
               pltpu.VMEM((nb, MXU_DIM, HEADS_PER_GROUP * tk), BF16),
               pltpu.VMEM((MXU_DIM, tq), F32)]
    return pl.pallas_call(
        functools.partial(_flash_kernel, nsub=nsub, tk=tk, nsrc=len(ksrcs), nchain=nchain),
        grid=(bsz, groups, nq // tq),
        in_specs=in_specs,
        out_specs=pl.BlockSpec((1, tq, MXU_DIM), lambda b, g, i: (b, i, g)),
        out_shape=jax.ShapeDtypeStruct((bsz, nq, groups * MXU_DIM), BF16),
        scratch_shapes=scratch,
        compiler_params=_params(("parallel", "parallel", "arbitrary")),
        name="flash_attn",
    )(*args)


HALO = 16


def _ffn_kernel(xp_ref, x_ref, xn_ref, oap_ref, oa_ref, oan_ref, obp_ref, ob_ref, obn_ref, wo_ref, gtm_ref,
                g_ref, sh_ref, sc_ref, gt_ref, wup_ref, cw_ref, wdn_ref, fg_ref,
                o_ref, h_ref, om_ref, *, final_norm, nf):
    i = pl.program_id(1)
    last = pl.num_programs(1) - 1
    tm = x_ref.shape[1]
    fdim = wdn_ref.shape[0]
    tf = fdim // nf
    half = oa_ref.shape[2]
    rows = tm + 2 * HALO
    own = slice(HALO, HALO + tm)
    for r, (a_ref, b_ref) in ((slice(0, HALO), (oap_ref, obp_ref)), (own, (oa_ref, ob_ref)),
                              (slice(HALO + tm, rows), (oan_ref, obn_ref))):
        om_ref[r, :half] = a_ref[0]
        om_ref[r, half:] = b_ref[0]
    ym = gtm_ref[0] * jnp.dot(om_ref[...], wo_ref[...], preferred_element_type=F32)
    x1 = x_ref[0] + ym[own]
    gain, shift, scale = g_ref[...], sh_ref[0], sc_ref[0]
    keep_p = jnp.where(i > 0, 1.0, 0.0)
    keep_n = jnp.where(i < last, 1.0, 0.0)
    h_ref[0:HALO, :] = (_norm_mod(xp_ref[0] + ym[0:HALO], gain, shift, scale) * keep_p).astype(BF16)
    h_ref[own, :] = _norm_mod(x1, gain, shift, scale).astype(BF16)
    h_ref[HALO + tm:, :] = (_norm_mod(xn_ref[0] + ym[HALO + tm:], gain, shift, scale) * keep_n).astype(BF16)

    def conv(u, f, part):
        w = cw_ref[:, cols(part, f)]
        up = pltpu.roll(u, 1, axis=0)[HALO:HALO + tm]
        un = pltpu.roll(u, rows - 1, axis=0)[HALO:HALO + tm]
        return up * w[0:1] + u[HALO:HALO + tm] * w[1:2] + un * w[2:3] + w[3:4]

    def cols(part, f):
        return slice(part * fdim + f * tf, part * fdim + (f + 1) * tf)

    h = h_ref[...]
    up = lambda f: (jnp.dot(h, wup_ref[:, cols(0, f)], preferred_element_type=F32),
                    jnp.dot(h, wup_ref[:, cols(1, f)], preferred_element_type=F32))
    y = None
    ug, uv = up(0)
    for f in range(nf):
        nxt = up(f + 1) if f + 1 < nf else None
        g = conv(ug, f, 0)
        v = conv(uv, f, 1)
        a = (g * (1.0 / (1.0 + jnp.exp(-g))) * v).astype(BF16)
        yf = jnp.dot(a, wdn_ref[f * tf:(f + 1) * tf, :], preferred_element_type=F32)
        y = yf if y is None else y + yf
        if nxt is not None:
            ug, uv = nxt
    out = x1 + gt_ref[0] * y
    if final_norm:
        out = out * lax.rsqrt(jnp.mean(out * out, axis=-1, keepdims=True) + NORM_EPS) * fg_ref[...]
    o_ref[0] = out


def _mix_ffn(x, oa, ob, w_out, gain, mods3, mrow, wup, cw, wdn, final_gain, tm):
    bsz, n, d = x.shape
    half = oa.shape[2]
    tm = min(tm, n)
    hb = tm // HALO
    nh = n // HALO
    final_norm = final_gain is not None
    fg = final_gain if final_norm else gain
    prev = lambda w: pl.BlockSpec((1, HALO, w), lambda b, i: (b, jnp.maximum(i * hb - 1, 0), 0))
    main = lambda w: pl.BlockSpec((1, tm, w), lambda b, i: (b, i, 0))
    nxt = lambda w: pl.BlockSpec((1, HALO, w), lambda b, i: (b, jnp.minimum((i + 1) * hb, nh - 1), 0))
    return pl.pallas_call(
        functools.partial(_ffn_kernel, final_norm=final_norm, nf=FFN_CHUNKS),
        grid=(bsz, n // tm),
        in_specs=[prev(d), main(d), nxt(d), prev(half), main(half), nxt(half), prev(half), main(half), nxt(half),
                  _resident(w_out.shape, lambda b, i: (0, 0)), _mod_spec(d, 2, mrow),
                  pl.BlockSpec((1, d), lambda b, i: (0, 0)),
                  _mod_spec(d, 3, mrow), _mod_spec(d, 4, mrow), _mod_spec(d, 5, mrow),
                  _resident(wup.shape, lambda b, i: (0, 0)),
                  _resident(cw.shape, lambda b, i: (0, 0)),
                  _resident(wdn.shape, lambda b, i: (0, 0)),
                  pl.BlockSpec((1, d), lambda b, i: (0, 0))],
        out_specs=main(d),
        out_shape=jax.ShapeDtypeStruct((bsz, n, d), F32),
        scratch_shapes=[pltpu.VMEM((tm + 2 * HALO, d), BF16), pltpu.VMEM((tm + 2 * HALO, 2 * half), BF16)],
        compiler_params=_params(("parallel", "parallel")),
        name="mix_ffn",
    )(x, x, x, oa, oa, oa, ob, ob, ob, w_out, mods3, gain.reshape(1, d), mods3, mods3, mods3, wup, cw, wdn,
      fg.reshape(1, d))


OD_Q, OD_C, OD_K, OD_V, OD_END = 0, 512, 2048, 2176, 2304
SUB = 8


def _prep_odd_kernel(xp_ref, x_ref, xn_ref, g_ref, sh_ref, sc_ref, win_ref, cw_ref, chd_ref, shd_ref,
                     q_ref, k_ref, v_ref, x0_ref, u_ref, h_ref, z_ref, *, scale):
    i = pl.program_id(1)
    last = pl.num_programs(1) - 1
    tm = x_ref.shape[1]
    rows = tm + 2 * HALO
    own = slice(HALO, HALO + tm)
    cwid = (OD_K - OD_C) // 3
    gain, shift, mscale = g_ref[...], sh_ref[0], sc_ref[0]
    keep_p = jnp.where(i > 0, 1.0, 0.0)
    keep_n = jnp.where(i < last, 1.0, 0.0)
    h_ref[0:HALO, :] = (_norm_mod(xp_ref[0], gain, shift, mscale) * keep_p).astype(BF16)
    h_ref[own, :] = _norm_mod(x_ref[0], gain, shift, mscale).astype(BF16)
    h_ref[HALO + tm:, :] = (_norm_mod(xn_ref[0], gain, shift, mscale) * keep_n).astype(BF16)
    z_ref[...] = jnp.dot(h_ref[...], win_ref[...], preferred_element_type=F32)
    for c in range((OD_C - OD_Q) // LANES):
        sl = slice(OD_Q + c * LANES, OD_Q + (c + 1) * LANES)
        y = _rope_lanes(z_ref[own, sl], chd_ref[...], shd_ref[...])
        q_ref[0, sl, :] = (y * scale).T.astype(BF16)
    k = _rope_lanes(z_ref[own, OD_K:OD_V], chd_ref[...], shd_ref[...])
    d0, d1 = _dup_heads(k)
    for c, dk in enumerate((d0, d0, d1, d1)):
        k_ref[0, :, c * LANES:(c + 1) * LANES] = dk.astype(BF16)
    d0, d1 = (d.T.astype(BF16) for d in _dup_heads(z_ref[own, OD_V:OD_END]))
    for c, dv in enumerate((d0, d0, d1, d1)):
        v_ref[0, c * LANES:(c + 1) * LANES, :] = dv

    def conv(part):
        sl = slice(OD_C + part * cwid, OD_C + (part + 1) * cwid)
        csl = slice(part * cwid, (part + 1) * cwid)
        ue = z_ref[:, sl]
        up = pltpu.roll(ue, 1, axis=0)[own]
        un = pltpu.roll(ue, rows - 1, axis=0)[own]
        return up * cw_ref[0:1, csl] + ue[own] * cw_ref[1:2, csl] + un * cw_ref[2:3, csl] + cw_ref[3:4, csl]

    x0_ref[0] = conv(0).astype(BF16)
    u_ref[0] = (conv(2) * conv(1)).astype(BF16)


def _prep_odd(x, gain, mods3, w_in, cw, chd, shd, tm):
    bsz, n, d = x.shape
    cols = w_in.shape[1]
    tm = min(tm, n)
    hb = tm // HALO
    nh = n // HALO
    cwid = (OD_K - OD_C) // 3
    row_blk = lambda w: pl.BlockSpec((1, tm, w), lambda b, i: (b, i, 0))
    col_blk = lambda w: pl.BlockSpec((1, w, tm), lambda b, i: (b, 0, i))
    slab = 4 * LANES
    return pl.pallas_call(
        functools.partial(_prep_odd_kernel, scale=HEAD_DIM ** -0.5 * LOG2E),
        grid=(bsz, n // tm),
        in_specs=[pl.BlockSpec((1, HALO, d), lambda b, i: (b, jnp.maximum(i * hb - 1, 0), 0)),
                  pl.BlockSpec((1, tm, d), lambda b, i: (b, i, 0)),
                  pl.BlockSpec((1, HALO, d), lambda b, i: (b, jnp.minimum((i + 1) * hb, nh - 1), 0)),
                  pl.BlockSpec((1, d), lambda b, i: (0, 0)),
                  _mod_spec(d, 0, None), _mod_spec(d, 1, None),
                  _resident(w_in.shape, lambda b, i: (0, 0)),
                  pl.BlockSpec(cw.shape, lambda b, i: (0, 0)),
                  pl.BlockSpec((tm, LANES), lambda b, i: (i, 0)),
                  pl.BlockSpec((tm, LANES), lambda b, i: (i, 0))],
        out_specs=[col_blk(slab), row_blk(slab), col_blk(slab), row_blk(cwid), row_blk(cwid)],
        out_shape=[jax.ShapeDtypeStruct((bsz, slab, n), BF16), jax.ShapeDtypeStruct((bsz, n, slab), BF16),
                   jax.ShapeDtypeStruct((bsz, slab, n), BF16), jax.ShapeDtypeStruct((bsz, n, cwid), BF16),
                   jax.ShapeDtypeStruct((bsz, n, cwid), BF16)],
        scratch_shapes=[pltpu.VMEM((tm + 2 * HALO, d), BF16), pltpu.VMEM((tm + 2 * HALO, cols), F32)],
        compiler_params=_params(("parallel", "parallel")),
        name="prep_odd",
    )(x, x, x, gain.reshape(1, d), mods3, mods3, w_in, cw, chd, shd)


def _kv_ctx_odd_kernel(z_ref, k_ref, vt_ref):
    d0, d1 = _dup_heads(z_ref[0, :, 0:LANES])
    for c, dk in enumerate((d0, d0, d1, d1)):
        k_ref[0, :, c * LANES:(c + 1) * LANES] = dk.astype(BF16)
    d0, d1 = (d.T.astype(BF16) for d in _dup_heads(z_ref[0, :, LANES:2 * LANES]))
    for c, dv in enumerate((d0, d0, d1, d1)):
        vt_ref[0, c * LANES:(c + 1) * LANES, :] = dv


def _kv_ctx_odd(z):
    bsz, n, cols = z.shape
    slab = 4 * LANES
    return pl.pallas_call(
        _kv_ctx_odd_kernel,
        grid=(bsz,),
        in_specs=[pl.BlockSpec((1, n, cols), lambda b: (b, 0, 0))],
        out_specs=[pl.BlockSpec((1, n, slab), lambda b: (b, 0, 0)), pl.BlockSpec((1, slab, n), lambda b: (b, 0, 0))],
        out_shape=[jax.ShapeDtypeStruct((bsz, n, slab), BF16), jax.ShapeDtypeStruct((bsz, slab, n), BF16)],
        compiler_params=_params(("parallel",)),
        name="kv_ctx_odd",
    )(z)


WIN_QB = 2
WIN_TILES = 8


def _window_kernel(qt_ref, kc_ref, vct_ref, kl_ref, vlt_ref, bias_ref, sink_ref, o_ref, kbc, vbct, kbl, vblt,
                   *, lc, nblk, qb, nch):
    qi = pl.program_id(2)
    tq = qb * Q_BLOCK
    nsp = qb + 2
    blk = HEADS_PER_GROUP * Q_BLOCK

    @pl.when(qi == 0)
    def _build():
        kms = _head_lane_masks(lc, 1)
        vms = _head_row_masks(lc)
        for j in range(HEADS_PER_GROUP):
            kbc[j * lc:(j + 1) * lc, :] = kc_ref[0] * kms[j]
            vbct[:, j * lc:(j + 1) * lc] = vct_ref[0] * vms[j]
        kms = _head_lane_masks(Q_BLOCK, 1)
        vms = _head_row_masks(Q_BLOCK)
        for e in (0, nblk + 1):
            kbl[e] = jnp.zeros((blk, MXU_DIM), BF16)
            vblt[e] = jnp.zeros((MXU_DIM, blk), BF16)

        def body(i, carry):
            r0 = pl.multiple_of(i * Q_BLOCK, Q_BLOCK)
            for j in range(HEADS_PER_GROUP):
                kbl[i + 1, j * Q_BLOCK:(j + 1) * Q_BLOCK, :] = kl_ref[0, pl.ds(r0, Q_BLOCK), :] * kms[j]
            return carry

        lax.fori_loop(0, nblk, body, 0)
        for i in range(nblk):
            for j in range(HEADS_PER_GROUP):
                vblt[i + 1, :, j * Q_BLOCK:(j + 1) * Q_BLOCK] = vlt_ref[0, :, i * Q_BLOCK:(i + 1) * Q_BLOCK] * vms[j]

    ntile = nblk // qb
    for h in range(nch):
        t = qi * nch + h
        qt = qt_ref[0, :, h * tq:(h + 1) * tq]
        s_c = jnp.dot(kbc[...], qt, preferred_element_type=F32)
        kspan = kbl[pl.ds(qb * t, nsp)].reshape(nsp * blk, MXU_DIM)
        s_s = jnp.dot(kspan, qt, preferred_element_type=F32) + bias_ref[...]
        edge = {0: jnp.where(t == 0, NEG_INF, 0.0), nsp - 1: jnp.where(t == ntile - 1, NEG_INF, 0.0)}
        pcs, pss, invs = [], [[None] * HEADS_PER_GROUP for _ in range(nsp)], []
        for j in range(HEADS_PER_GROUP):
            sink = sink_ref[0, j][0:1, :]
            segs = [s_c[j * lc:(j + 1) * lc, :]]
            for b in range(nsp):
                sg = s_s[b * blk + j * Q_BLOCK: b * blk + (j + 1) * Q_BLOCK, :]
                segs.append(sg + edge[b] if b in edge else sg)
            m = sink
            for sg in segs:
                m = jnp.maximum(m, jnp.max(sg, axis=0, keepdims=True))
            ps = [jnp.exp2(sg - m) for sg in segs]
            den = jnp.exp2(sink - m)
            for p in ps:
                den = den + jnp.sum(p, axis=0, keepdims=True)
            invs.append(1.0 / den)
            pcs.append(ps[0].astype(BF16))
            for b in range(nsp):
                pss[b][j] = ps[1 + b].astype(BF16)
        ot = jnp.dot(vbct[...], jnp.concatenate(pcs, axis=0), preferred_element_type=F32)
        for b in range(nsp):
            ot = ot + jnp.dot(vblt[qb * t + b], jnp.concatenate(pss[b], axis=0), preferred_element_type=F32)
        o_ref[0, h * tq:(h + 1) * tq, :] = (ot * _rows4(invs, tq)).T.astype(BF16)


@functools.lru_cache(maxsize=None)
def _window_bias(qb):
    nsp = qb + 2
    row = np.arange(nsp * HEADS_PER_GROUP * Q_BLOCK)[:, None]
    r = np.arange(qb * Q_BLOCK)[None, :]
    b = row // (HEADS_PER_GROUP * Q_BLOCK)
    jj = b * Q_BLOCK + row % Q_BLOCK
    band = (jj >= r) & (jj <= r + 2 * WINDOW)
    return np.where(band, 0.0, NEG_INF).astype(np.float32)


def _window(qt, kc, vct, kl, vlt, sink):
    bsz, qw, n = qt.shape
    groups = qw // MXU_DIM
    lc = kc.shape[1]
    nblk = n // Q_BLOCK
    qb = min(WIN_QB, nblk)
    tq = qb * Q_BLOCK
    nch = next(c for c in (WIN_TILES, 2, 1) if n % (c * tq) == 0)
    nq = n // (tq * nch)
    blk = HEADS_PER_GROUP * Q_BLOCK
    bias = jnp.asarray(_window_bias(qb))
    sink_t = jnp.broadcast_to((sink.astype(F32) * LOG2E).reshape(groups, HEADS_PER_GROUP, 1, 1),
                              (groups, HEADS_PER_GROUP, SUB, tq))

    return pl.pallas_call(
        functools.partial(_window_kernel, lc=lc, nblk=nblk, qb=qb, nch=nch),
        grid=(bsz, groups, nq),
        in_specs=[pl.BlockSpec((1, MXU_DIM, tq * nch), lambda b, g, i: (b, g, i)),
                  pl.BlockSpec((1, lc, MXU_DIM), lambda b, g, i: (b, 0, g)),
                  pl.BlockSpec((1, MXU_DIM, lc), lambda b, g, i: (b, g, 0)),
                  pl.BlockSpec((1, n, MXU_DIM), lambda b, g, i: (b, 0, g)),
                  pl.BlockSpec((1, MXU_DIM, n), lambda b, g, i: (b, g, 0)),
                  _resident(bias.shape, lambda b, g, i: (0, 0)),
                  pl.BlockSpec((1, HEADS_PER_GROUP, SUB, tq), lambda b, g, i: (g, 0, 0, 0))],
        out_specs=pl.BlockSpec((1, tq * nch, MXU_DIM), lambda b, g, i: (b, i, g)),
        out_shape=jax.ShapeDtypeStruct((bsz, n, qw), BF16),
        scratch_shapes=[pltpu.VMEM((HEADS_PER_GROUP * lc, MXU_DIM), BF16),
                        pltpu.VMEM((MXU_DIM, HEADS_PER_GROUP * lc), BF16),
                        pltpu.VMEM((nblk + 2, blk, MXU_DIM), BF16),
                        pltpu.VMEM((nblk + 2, MXU_DIM, blk), BF16)],
        compiler_params=_params(("parallel", "parallel", "arbitrary")),
        name="window_attn",
    )(qt, kc, vct, kl, vlt, bias, sink_t)


FFT_N2 = 128


def _filter_kernel(z_ref, w1_ref, b1_ref, w2_ref, b2_ref, w3_ref, b3_ref, w4_ref, fr_ref, dl_ref, o_ref):
    def lin(a, w_ref):
        a_hi, a_lo = _split_bf16(a)
        w_hi, w_lo = _split_bf16(w_ref[...])
        return _dot3(a_hi, a_lo, w_hi, w_lo)

    z = z_ref[...]
    h = b1_ref[...]
    for e in range(C_EMB_DIM):
        h = h + z[:, e:e + 1] * w1_ref[e:e + 1, :]
    h = jnp.sin(fr_ref[0:1] * h)
    h = jnp.sin(fr_ref[1:2] * (lin(h, w2_ref) + b2_ref[...]))
    h = jnp.sin(fr_ref[2:3] * (lin(h, w3_ref) + b3_ref[...]))
    h = lin(h, w4_ref)
    t = z[:, 0:1]
    cw = dl_ref.shape[1]
    dec = jnp.exp(-t * dl_ref[...])
    row = lax.broadcasted_iota(jnp.int32, dec.shape, 0) + pl.program_id(0) * z.shape[0]
    o_ref[0] = h[:, :cw] * dec
    o_ref[1] = jnp.where(row == 0, 0.0, h[:, cw:] * dec)


def _filters(zfeat, w1, b1, w2, b2, w3, b3, w4, freq, deltas, tm):
    n = zfeat.shape[0]
    tm = min(tm, n)
    cw = deltas.shape[1]
    full = lambda a: pl.BlockSpec(a.shape, lambda i: (0,) * a.ndim)
    ops = (w1, b1, w2, b2, w3, b3, w4, freq, deltas)
    return pl.pallas_call(
        _filter_kernel,
        grid=(n // tm,),
        in_specs=[pl.BlockSpec((tm, zfeat.shape[1]), lambda i: (i, 0))] + [full(a) for a in ops],
        out_specs=pl.BlockSpec((2, tm, cw), lambda i: (0, i, 0)),
        out_shape=jax.ShapeDtypeStruct((2, n, cw), F32),
        compiler_params=_params(("parallel",)),
        name="hyena_filters",
    )(zfeat, *ops)


def _dft_first_kernel(f_ref, x_ref, o_ref):
    _, n1h, tm2, cw = x_ref.shape
    x = x_ref[0].reshape(n1h * tm2, cw).astype(BF16)
    r = jnp.dot(f_ref[...], x, preferred_element_type=F32)
    o_ref[0] = r.astype(BF16).reshape(o_ref.shape[1:])


def _dft_first(f1, x):
    bsz, n1h, n2, cw = x.shape
    tm2 = HALO
    nkp = f1.shape[0] // (2 * tm2)
    return pl.pallas_call(
        _dft_first_kernel,
        grid=(bsz, n2 // tm2),
        in_specs=[pl.BlockSpec(f1.shape, lambda b, j: (0, 0)),
                  pl.BlockSpec((1, n1h, tm2, cw), lambda b, j: (b, 0, j, 0))],
        out_specs=pl.BlockSpec((1, 2, nkp, tm2, cw), lambda b, j: (b, 0, 0, j, 0)),
        out_shape=jax.ShapeDtypeStruct((bsz, 2, nkp, n2, cw), BF16),
        compiler_params=_params(("parallel", "parallel")),
        name="dft_first",
    )(f1, x)


def _dft_last_gate_kernel(f_ref, b_ref, x0_ref, u_ref, bias_ref, o_ref):
    _, _, nkp, tm2, cw = b_ref.shape
    y = jnp.dot(f_ref[...], b_ref[0].reshape(2 * nkp * tm2, cw), preferred_element_type=F32)
    y = y.reshape(x0_ref.shape[1:])
    o_ref[0] = (x0_ref[0].astype(F32) * (y + u_ref[0].astype(F32) * bias_ref[...])).astype(BF16)


def _dft_last_gate(f2, b, x0, u, bias):
    bsz, _, nkp, n2, cw = b.shape
    tm2 = HALO
    n1h = f2.shape[0] // tm2
    tblk = pl.BlockSpec((1, n1h, tm2, cw), lambda b_, j: (b_, 0, j, 0))
    return pl.pallas_call(
        _dft_last_gate_kernel,
        grid=(bsz, n2 // tm2),
        in_specs=[pl.BlockSpec(f2.shape, lambda b_, j: (0, 0)),
                  pl.BlockSpec((1, 2, nkp, tm2, cw), lambda b_, j: (b_, 0, 0, j, 0)),
                  tblk, tblk, pl.BlockSpec((1, cw), lambda b_, j: (0, 0))],
        out_specs=tblk,
        out_shape=jax.ShapeDtypeStruct((bsz, n1h, n2, cw), BF16),
        compiler_params=_params(("parallel", "parallel")),
        name="dft_last_gate",
    )(f2, b, x0, u, bias.reshape(1, cw))


def _spec_fwd(a, m_ref):
    return jnp.dot(m_ref[0], a.reshape(2 * FFT_N2, a.shape[-1]), preferred_element_type=F32)


def _filter_spec_kernel(a_ref, m_ref, o_ref):
    for kk in range(a_ref.shape[2]):
        xf = _spec_fwd(a_ref[0, :, kk], m_ref.at[kk:kk + 1])
        xb = _spec_fwd(a_ref[1, :, kk], m_ref.at[kk:kk + 1])
        o_ref[0, kk] = xf[:FFT_N2] + xb[:FFT_N2]
        o_ref[1, kk] = xf[FFT_N2:] - xb[FFT_N2:]


def _conv_spec_kernel(a_ref, k_ref, m_ref, i_ref, o_ref):
    kr, ki = k_ref[0, 0], k_ref[1, 0]
    for b in range(a_ref.shape[0]):
        x = _spec_fwd(a_ref[b, :, 0], m_ref)
        xr, xi = x[:FFT_N2], x[FFT_N2:]
        y = jnp.concatenate([xr * kr - xi * ki, xr * ki + xi * kr], axis=0).astype(BF16)
        bv = jnp.dot(i_ref[0], y, preferred_element_type=F32)
        o_ref[b, :, 0] = bv.astype(BF16).reshape(2, FFT_N2, bv.shape[-1])


def _filter_spec(a, m_fwd):
    _, _, nkp, _, cw = a.shape
    return pl.pallas_call(
        _filter_spec_kernel,
        grid=(nkp // SUB,),
        in_specs=[pl.BlockSpec((2, 2, SUB, FFT_N2, cw), lambda k: (0, 0, k, 0, 0)),
                  pl.BlockSpec((SUB, 2 * FFT_N2, 2 * FFT_N2), lambda k: (k, 0, 0))],
        out_specs=pl.BlockSpec((2, SUB, FFT_N2, cw), lambda k: (0, k, 0, 0)),
        out_shape=jax.ShapeDtypeStruct((2, nkp, FFT_N2, cw), F32),
        compiler_params=_params(("parallel",)),
        name="filter_spectrum",
    )(a, m_fwd)


def _conv_spec(a, kspec, m_fwd, m_inv):
    bsz, _, nkp, _, cw = a.shape
    blk = pl.BlockSpec((bsz, 2, 1, FFT_N2, cw), lambda k: (0, 0, k, 0, 0))
    mat = pl.BlockSpec((1, 2 * FFT_N2, 2 * FFT_N2), lambda k: (k, 0, 0))
    return pl.pallas_call(
        _conv_spec_kernel,
        grid=(nkp,),
        in_specs=[blk, pl.BlockSpec((2, 1, FFT_N2, cw), lambda k: (0, k, 0, 0)), mat, mat],
        out_specs=blk,
        out_shape=jax.ShapeDtypeStruct(a.shape, BF16),
        compiler_params=_params(("parallel",)),
        name="conv_spectrum",
    )(a, kspec, m_fwd, m_inv)


@functools.lru_cache(maxsize=None)
def _dft_tables(n):
    nfft = 2 * n
    n2 = FFT_N2
    n1 = nfft // n2
    n1h = n1 // 2
    nk = n1h + 1
    nkp = -(-nk // SUB) * SUB

    def cs(num, den):
        ang = (num % den).astype(np.float64) * (2.0 * math.pi / den)
        return np.cos(ang), np.sin(ang)

    def pad_k(t, axis):
        widths = [(0, 0)] * t.ndim
        widths[axis] = (0, nkp - nk)
        return np.pad(t, widths)

    k1 = np.arange(nk, dtype=np.int64)
    c1, s1 = cs(k1[:, None] * np.arange(n1h)[None, :], n1)
    f1 = np.concatenate([pad_k(c1, 0), pad_k(-s1, 0)], axis=0)
    kk = k1[:, None, None] + n1 * np.arange(n2)[None, :, None]
    cg, sg = cs(kk * np.arange(n2)[None, None, :], nfft)
    g_re, g_im = cg, -sg
    m_fwd = np.concatenate([np.concatenate([g_re, -g_im], axis=2),
                            np.concatenate([g_im, g_re], axis=2)], axis=1)
    gt_re, gt_im = np.swapaxes(g_re, 1, 2), np.swapaxes(g_im, 1, 2)
    m_inv = np.concatenate([np.concatenate([gt_re, gt_im], axis=2),
                            np.concatenate([-gt_im, gt_re], axis=2)], axis=1)
    wk = np.where((k1 == 0) | (k1 == n1h), 1.0, 2.0)[None, :] * (1.0 / nfft)
    c2, s2 = cs(np.arange(n1h)[:, None] * k1[None, :], n1)
    f2 = np.concatenate([pad_k(c2 * wk, 1), pad_k(-s2 * wk, 1)], axis=1)
    eye = np.eye(HALO)
    return tuple(t.astype(np.float32) for t in (np.kron(f1, eye), pad_k(m_fwd, 0), pad_k(m_inv, 0), np.kron(f2, eye)))


def _hyena_long_conv(u, x0, bias, hcat, tabs):
    f1, m_fwd, m_inv, f2 = tabs
    bsz, n, cw = u.shape
    n2 = FFT_N2
    n1h = n // n2
    kspec = _filter_spec(_dft_first(f1, hcat.reshape(2, n1h, n2, cw)), m_fwd)
    u4 = u.reshape(bsz, n1h, n2, cw)
    b_u = _conv_spec(_dft_first(f1, u4), kspec, m_fwd, m_inv)
    return _dft_last_gate(f2, b_u, x0.reshape(bsz, n1h, n2, cw), u4, bias).reshape(bsz, n, cw)


def _axial_angles(rows, rope_dim):
    row_idx = np.repeat(np.arange(rows), GRID_W).astype(np.float64)
    col_idx = np.tile(np.arange(GRID_W), rows).astype(np.float64)
    d_axis = rope_dim // 2
    inv_freq = ROPE_THETA ** (-np.arange(0, d_axis, 2, dtype=np.float64) / d_axis)
    ang = np.concatenate([row_idx[:, None] * inv_freq, col_idx[:, None] * inv_freq], axis=-1)
    return np.cos(ang), np.sin(ang)


@functools.lru_cache(maxsize=None)
def _rope_tables(n):
    rows = n // GRID_W
    sign = np.tile(np.array([-1.0, 1.0]), LANES // 2)[None, :]
    cos, sin = _axial_angles(rows, HEAD_DIM)
    chd = np.tile(np.repeat(cos, 2, axis=1), (1, LANES // HEAD_DIM))
    shd = np.tile(np.repeat(sin, 2, axis=1), (1, LANES // HEAD_DIM)) * sign
    cos, sin = _axial_angles(rows, B_ROPE)
    ones = np.ones((n, LANES - 2 * B_ROPE))
    cb = np.concatenate([np.tile(np.repeat(cos, 2, axis=1), (1, 2)), ones], axis=1)
    sb = np.concatenate([np.tile(np.repeat(sin, 2, axis=1), (1, 2)), 0.0 * ones], axis=1) * sign
    return tuple(t.astype(np.float32) for t in (chd, shd, cb, sb))


@functools.lru_cache(maxsize=None)
def _hyena_features(n, cwid):
    t = np.linspace(0.0, 1.0, n)[:, None]
    wpos = 2 * math.pi * np.arange(n)[:, None] / n
    fb = np.linspace(1e-4, C_BANDS - 1, C_BANDS)[None, :]
    zfeat = np.concatenate([t, np.cos(fb * wpos), -np.sin(fb * wpos), np.zeros((n, SUB - C_EMB_DIM))], axis=-1)
    deltas = np.abs(np.linspace(C_MIN_DECAY, C_MAX_DECAY, cwid))[None, :]
    return zfeat.astype(np.float32), deltas.astype(np.float32)


def _even_weights(w_in, a_qn, a_kn, b_qn, b_w_uq, b_kvn, b_w_ukv):
    d = w_in.shape[0]
    aq_w = w_in.shape[1] - (B_Q_RANK + 2 * A_KV_HEADS * HEAD_DIM + B_KV_RANK + B_ROPE)
    o = [0, aq_w, aq_w + B_Q_RANK]
    o += [o[-1] + A_KV_HEADS * HEAD_DIM, o[-1] + 2 * A_KV_HEADS * HEAD_DIM]
    o += [o[-1] + B_KV_RANK, o[-1] + B_KV_RANK + B_ROPE]
    kr = w_in[:, o[5]:o[6]]
    w_aug = jnp.concatenate([w_in[:, :o[5]], kr, kr, jnp.zeros((d, LANES - 2 * B_ROPE), w_in.dtype)], axis=1)
    heads = b_w_uq.shape[1] // (B_NOPE + B_ROPE)
    uq = b_w_uq.reshape(B_Q_RANK, heads // 2, 2, B_NOPE + B_ROPE)
    wuq = jnp.concatenate([uq[:, :, 0, :B_NOPE], uq[:, :, 1, :B_NOPE], uq[:, :, 0, B_NOPE:], uq[:, :, 1, B_NOPE:],
                           jnp.zeros((B_Q_RANK, heads // 2, MXU_DIM - 2 * (B_NOPE + B_ROPE)), b_w_uq.dtype)], axis=2)
    wuq = wuq.reshape(B_Q_RANK, heads // 2 * MXU_DIM)
    ukv = b_w_ukv.reshape(B_KV_RANK, heads, B_NOPE + B_VDIM)
    wukn = ukv[:, :, :B_NOPE].reshape(B_KV_RANK, heads * B_NOPE)
    wuv = ukv[:, :, B_NOPE:].reshape(B_KV_RANK, heads * B_VDIM)
    gains = (jnp.tile(a_qn, aq_w // HEAD_DIM)[None, :], jnp.tile(a_kn, A_KV_HEADS)[None, :],
             b_qn[None, :], b_kvn[None, :])
    return w_aug.astype(BF16), gains + (wuq.astype(BF16), wukn.astype(BF16), wuv.astype(BF16))


def _ffn_weights(w_up, conv_w, conv_b, w_down):
    cw = jnp.concatenate([conv_w, conv_b[None, :], jnp.zeros((SUB - 4, conv_w.shape[1]), conv_w.dtype)], axis=0)
    return w_up.astype(BF16), cw, w_down.astype(BF16)


FFN_CHUNKS = 1


def kernel(x, c, ctx, c_ctx, w_mod, b_mod, norm_mix, norm_ffn, ev_w_in, ev_w_out, a_q_norm, a_k_norm, b_q_norm, b_w_uq, b_kv_norm, b_w_ukv, od_w_in, od_w_out, d_sink, c_conv_w, c_conv_b, c_filt_w1, c_filt_b1, c_filt_w2, c_filt_b2, c_filt_w3, c_filt_b3, c_filt_w4, c_filt_freq, c_bias, ffn_w_up, ffn_conv_w, ffn_conv_b, ffn_w_down, final_norm):
    bsz, n, d = x.shape
    depth = w_mod.shape[0]
    assert depth == 2 and n % Q_BLOCK == 0 and d % LANES == 0
    tm = ROW_TILE
    rows = -(-(bsz + 1) // SUB) * SUB
    cvec = jnp.concatenate([c, c_ctx[None, :], jnp.zeros((rows - bsz - 1, d), F32)], axis=0)
    mods = _mods(cvec, w_mod, b_mod)
    chd, shd, cb, sb = (jnp.asarray(t) for t in _rope_tables(n))
    tabs = (chd, shd, cb, sb)

    m3 = mods[0].reshape(rows, 1, N_MOD * d)
    w_aug, prep_w = _even_weights(ev_w_in[0], a_q_norm[0], a_k_norm[0], b_q_norm[0], b_w_uq[0],
                                  b_kv_norm[0], b_w_ukv[0])
    qa_l, ka_l, va_l, qb_l, kb_l, vb_l = _prep_even(x, norm_mix[0], m3, None, w_aug, prep_w, tabs, True, tm)
    qa_c, ka_c, va_c, qb_c, kb_c, vb_c = _prep_even(ctx, norm_mix[0], m3, bsz, w_aug, prep_w, tabs, False, tm)
    tq, tk = ATTN_TQ, ATTN_TK
    oa_l = _flash(qa_l, (ka_c, ka_l), (va_c, va_l), 1, tq, tk)
    ob_l = _flash(qb_l, (kb_c, kb_l), (vb_c, vb_l), 2, tq, tk)
    oa_c = _flash(qa_c, (ka_c,), (va_c,), 1, tq, tk)
    ob_c = _flash(qb_c, (kb_c,), (vb_c,), 2, tq, tk)
    w_out = ev_w_out[0].astype(BF16)
    ffn_w = _ffn_weights(ffn_w_up[0], ffn_conv_w[0], ffn_conv_b[0], ffn_w_down[0])
    x = _mix_ffn(x, oa_l, ob_l, w_out, norm_ffn[0], m3, None, *ffn_w, None, tm)
    ctx = _mix_ffn(ctx, oa_c, ob_c, w_out, norm_ffn[0], m3, bsz, *ffn_w, None, tm)

    m3 = mods[1].reshape(rows, 1, N_MOD * d)
    w_in = od_w_in[0].astype(BF16)
    z_c = _inproj(ctx, norm_mix[1], m3, bsz, w_in[:, OD_K:], tm)
    cw = jnp.concatenate([c_conv_w[0], c_conv_b[0][None, :], jnp.zeros((SUB - 4, OD_K - OD_C), F32)], axis=0)
    qd, kd, vd, x0, u = _prep_odd(x, norm_mix[1], m3, w_in, cw, chd, shd, 2 * tm)
    kd_c, vd_c = _kv_ctx_odd(z_c)
    od = _window(qd, kd_c, vd_c, kd, vd, d_sink[0])
    zfeat, deltas = (jnp.asarray(t) for t in _hyena_features(n, c_bias.shape[1]))
    w1 = jnp.concatenate([c_filt_w1[0], jnp.zeros((SUB - C_EMB_DIM, c_filt_w1.shape[2]), F32)], axis=0)
    hcat = _filters(zfeat, w1, c_filt_b1[0][None, :], c_filt_w2[0], c_filt_b2[0][None, :], c_filt_w3[0],
                    c_filt_b3[0][None, :], c_filt_w4[0], jnp.concatenate([c_filt_freq[0], jnp.zeros((SUB - 3, c_filt_freq.shape[2]), F32)], axis=0),
                    deltas, tm)
    oc = _hyena_long_conv(u, x0, c_bias[0], hcat, tuple(jnp.asarray(t).astype(BF16) for t in _dft_tables(n)))
    ffn_w = _ffn_weights(ffn_w_up[1], ffn_conv_w[1], ffn_conv_b[1], ffn_w_down[1])
    return _mix_ffn(x, od, oc, od_w_out[0].astype(BF16), norm_ffn[1], m3, None, *ffn_w, final_norm, tm)
```

```python
import functools
import math

import jax
import jax.numpy as jnp
import numpy as np
from jax import lax
from jax.experimental import pallas as pl
from jax.experimental.pallas import tpu as pltpu

F32 = jnp.float32
BF16 = jnp.bfloat16

GRID_W = 64
HEAD_DIM = 64
ROPE_THETA = 10000.0
NORM_EPS = 1e-6
NEG_INF = -1e30
N_MOD = 6
A_KV_HEADS = 2
B_NOPE = 64
B_ROPE = 32
B_VDIM = 64
B_Q_RANK = 384
B_KV_RANK = 256
C_EMB_DIM = 5
C_BANDS = (C_EMB_DIM - 1) // 2
C_MIN_DECAY = math.log(1e-2) / 1.5
C_MAX_DECAY = math.log(1e-2) / 0.3
WINDOW = 128
Q_BLOCK = 128
LOG2E = math.log2(math.e)

LANES = 128
MXU_DIM = 256
VMEM_LIMIT = 56 * 1024 * 1024
HEADS_PER_GROUP = MXU_DIM // HEAD_DIM

ROW_TILE = 512
ATTN_TQ = 1024
ATTN_TK = 256


def _params(sem, vmem=VMEM_LIMIT):
    return pltpu.CompilerParams(dimension_semantics=sem, vmem_limit_bytes=vmem)


def _resident(shape, index_map):
    return pl.BlockSpec(shape, index_map, pipeline_mode=pl.Buffered(1))


def _norm_mod(x, gain, shift, scale):
    inv = lax.rsqrt(jnp.mean(x * x, axis=-1, keepdims=True) + NORM_EPS)
    return (x * inv) * gain * (1.0 + scale) + shift


def _rope_lanes(x, cos, sin_signed):
    lane = lax.broadcasted_iota(jnp.int32, x.shape, 1)
    nxt = pltpu.roll(x, LANES - 1, axis=1)
    prv = pltpu.roll(x, 1, axis=1)
    swapped = jnp.where(lane % 2 == 0, nxt, prv)
    return x * cos + swapped * sin_signed


def _head_rmsnorm_lanes(x, gain):
    lane = lax.broadcasted_iota(jnp.int32, x.shape, 1)
    lo = lane < HEAD_DIM
    sq = x * x
    s_lo = jnp.sum(jnp.where(lo, sq, 0.0), axis=-1, keepdims=True)
    s_hi = jnp.sum(jnp.where(lo, 0.0, sq), axis=-1, keepdims=True)
    ms = jnp.where(lo, s_lo, s_hi) * (1.0 / HEAD_DIM)
    return x * lax.rsqrt(ms + NORM_EPS) * gain


def _dup_heads(x):
    lane = lax.broadcasted_iota(jnp.int32, x.shape, 1)
    lo = lane < HEAD_DIM
    r = pltpu.roll(x, HEAD_DIM, axis=1)
    return jnp.where(lo, x, r), jnp.where(lo, r, x)


def _split_bf16(x):
    hi = x.astype(BF16)
    lo = (x - hi.astype(F32)).astype(BF16)
    return hi, lo


def _dot3(a_hi, a_lo, b_hi, b_lo):
    d = functools.partial(jnp.dot, preferred_element_type=F32)
    return d(a_hi, b_hi) + d(a_hi, b_lo) + d(a_lo, b_hi)


def _mods_kernel(c_ref, w_ref, b_ref, o_ref):
    c = c_ref[...]
    s = c * (1.0 / (1.0 + jnp.exp(-c)))
    s_hi, s_lo = _split_bf16(s)
    w_hi, w_lo = _split_bf16(w_ref[0])
    o_ref[0] = _dot3(s_hi, s_lo, w_hi, w_lo) + b_ref[0]


def _mods(cvec, w_mod, b_mod):
    depth, d, n = w_mod.shape
    rows = cvec.shape[0]
    tn = 1536
    return pl.pallas_call(
        _mods_kernel,
        grid=(depth, n // tn),
        in_specs=[pl.BlockSpec((rows, d), lambda l, j: (0, 0)),
                  pl.BlockSpec((1, d, tn), lambda l, j: (l, 0, j)),
                  pl.BlockSpec((1, 1, tn), lambda l, j: (l, 0, j))],
        out_specs=pl.BlockSpec((1, rows, tn), lambda l, j: (l, 0, j)),
        out_shape=jax.ShapeDtypeStruct((depth, rows, n), F32),
        compiler_params=_params(("arbitrary", "arbitrary")),
        name="mods",
    )(cvec, w_mod, b_mod.reshape(depth, 1, n))


def _mod_spec(d, chunk, row):
    if row is None:
        return pl.BlockSpec((1, 1, d), lambda b, *_: (b, 0, chunk))
    return pl.BlockSpec((1, 1, d), lambda b, *_: (row, 0, chunk))


def _inproj_kernel(x_ref, g_ref, sh_ref, sc_ref, w_ref, o_ref):
    h = _norm_mod(x_ref[0], g_ref[...], sh_ref[0], sc_ref[0])
    o_ref[0] = jnp.dot(h.astype(BF16), w_ref[...], preferred_element_type=F32)


def _inproj(x, gain, mods3, mrow, w, tm):
    bsz, n, d = x.shape
    cols = w.shape[1]
    tm = min(tm, n)
    return pl.pallas_call(
        _inproj_kernel,
        grid=(bsz, n // tm),
        in_specs=[pl.BlockSpec((1, tm, d), lambda b, i: (b, i, 0)),
                  pl.BlockSpec((1, d), lambda b, i: (0, 0)),
                  _mod_spec(d, 0, mrow), _mod_spec(d, 1, mrow),
                  _resident((d, cols), lambda b, i: (0, 0))],
        out_specs=pl.BlockSpec((1, tm, cols), lambda b, i: (b, i, 0)),
        out_shape=jax.ShapeDtypeStruct((bsz, n, cols), F32),
        compiler_params=_params(("parallel", "parallel")),
        name="inproj",
    )(x, gain.reshape(1, d), mods3, mods3, w)


EV_AQ, EV_BQ, EV_AK, EV_AV, EV_BKV, EV_BKR, EV_END = 0, 512, 896, 1024, 1152, 1408, 1536


def _prep_even_kernel(x_ref, g_ref, sh_ref, sc_ref, win_ref,
                      aqg_ref, akg_ref, bqg_ref, bkvg_ref, wuq_ref, wukn_ref, wuv_ref,
                      chd_ref, shd_ref, cb_ref, sb_ref,
                      qa_ref, ka_ref, va_ref, qb_ref, kb_ref, vb_ref, z_ref, *, use_rope, a_scale, b_scale):
    h = _norm_mod(x_ref[0], g_ref[...], sh_ref[0], sc_ref[0])
    z_ref[0] = jnp.dot(h.astype(BF16), win_ref[...], preferred_element_type=F32)

    def rope_hd(y):
        return _rope_lanes(y, chd_ref[...], shd_ref[...]) if use_rope else y

    def rope_b(y):
        return _rope_lanes(y, cb_ref[...], sb_ref[...]) if use_rope else y

    def put_t(dst, c, y):
        dst[0, c * LANES:(c + 1) * LANES, :] = y.T.astype(BF16)

    for c in range((EV_BQ - EV_AQ) // LANES):
        sl = slice(EV_AQ + c * LANES, EV_AQ + (c + 1) * LANES)
        y = rope_hd(_head_rmsnorm_lanes(z_ref[0, :, sl], aqg_ref[:, c * LANES:(c + 1) * LANES]))
        put_t(qa_ref, c, y * a_scale)
    k = rope_hd(_head_rmsnorm_lanes(z_ref[0, :, EV_AK:EV_AV], akg_ref[...]))
    d0, d1 = _dup_heads(k)
    d0 = d0.astype(BF16)
    d1 = d1.astype(BF16)
    ka_ref[0, :, 0 * LANES:1 * LANES] = d0
    ka_ref[0, :, 1 * LANES:2 * LANES] = d0
    ka_ref[0, :, 2 * LANES:3 * LANES] = d1
    ka_ref[0, :, 3 * LANES:4 * LANES] = d1
    d0, d1 = (d.T.astype(BF16) for d in _dup_heads(z_ref[0, :, EV_AV:EV_BKV]))
    for c, dv in enumerate((d0, d0, d1, d1)):
        va_ref[0, c * LANES:(c + 1) * LANES, :] = dv
    cq = z_ref[0, :, EV_BQ:EV_AK]
    cq = cq * lax.rsqrt(jnp.mean(cq * cq, axis=-1, keepdims=True) + NORM_EPS) * bqg_ref[...]
    qb = jnp.dot(cq.astype(BF16), wuq_ref[...], preferred_element_type=F32)
    for c in range(qb.shape[1] // MXU_DIM):
        lo = slice(c * MXU_DIM, c * MXU_DIM + LANES)
        hi = slice(c * MXU_DIM + LANES, (c + 1) * MXU_DIM)
        put_t(qb_ref, 2 * c, qb[:, lo] * b_scale)
        put_t(qb_ref, 2 * c + 1, rope_b(qb[:, hi]) * b_scale)
    ckv = z_ref[0, :, EV_BKV:EV_BKR]
    ckv = (ckv * lax.rsqrt(jnp.mean(ckv * ckv, axis=-1, keepdims=True) + NORM_EPS) * bkvg_ref[...]).astype(BF16)
    kn = jnp.dot(ckv, wukn_ref[...], preferred_element_type=F32)
    vb = jnp.dot(ckv, wuv_ref[...], preferred_element_type=F32)
    for c in range(vb.shape[1] // LANES):
        put_t(vb_ref, c, vb[:, c * LANES:(c + 1) * LANES])
    kr = rope_b(z_ref[0, :, EV_BKR:EV_END]).astype(BF16)
    for c in range(kn.shape[1] // LANES):
        kb_ref[0, :, c * MXU_DIM:c * MXU_DIM + LANES] = kn[:, c * LANES:(c + 1) * LANES].astype(BF16)
        kb_ref[0, :, c * MXU_DIM + LANES:(c + 1) * MXU_DIM] = kr


def _prep_even(x, gain, mods3, mrow, w_in, wts, tabs, use_rope, tm):
    bsz, n, d = x.shape
    cols = w_in.shape[1]
    tm = min(tm, n)
    aqg, akg, bqg, bkvg, wuq, wukn, wuv = wts
    chd, shd, cb, sb = tabs
    full = lambda a: _resident(a.shape, lambda b, i: (0,) * a.ndim)
    tab = lambda a: pl.BlockSpec((tm, LANES), (lambda b, i: (i, 0)) if use_rope else (lambda b, i: (0, 0)))
    outs = [(4 * LANES, True), (4 * LANES, False), (4 * LANES, True), (wuq.shape[1], True),
            (2 * wukn.shape[1], False), (wuv.shape[1], True)]
    kern = functools.partial(_prep_even_kernel, use_rope=use_rope, a_scale=HEAD_DIM ** -0.5 * LOG2E,
                             b_scale=(B_NOPE + B_ROPE) ** -0.5 * LOG2E)
    return pl.pallas_call(
        kern,
        grid=(bsz, n // tm),
        in_specs=[pl.BlockSpec((1, tm, d), lambda b, i: (b, i, 0)),
                  pl.BlockSpec((1, d), lambda b, i: (0, 0)),
                  _mod_spec(d, 0, mrow), _mod_spec(d, 1, mrow), full(w_in),
                  full(aqg), full(akg), full(bqg), full(bkvg), full(wuq), full(wukn), full(wuv),
                  tab(chd), tab(shd), tab(cb), tab(sb)],
        out_specs=[pl.BlockSpec((1, w, tm), lambda b, i: (b, 0, i)) if t else
                   pl.BlockSpec((1, tm, w), lambda b, i: (b, i, 0)) for w, t in outs],
        out_shape=[jax.ShapeDtypeStruct((bsz, w, n) if t else (bsz, n, w), BF16) for w, t in outs],
        scratch_shapes=[pltpu.VMEM((1, tm, cols), F32)],
        compiler_params=_params(("parallel", "parallel")),
        name="prep_even",
    )(x, gain.reshape(1, d), mods3, mods3, w_in, aqg, akg, bqg, bkvg, wuq, wukn, wuv, chd, shd, cb, sb)


def _head_lane_masks(rows, nsub):
    lane = lax.broadcasted_iota(jnp.int32, (rows, MXU_DIM), 1)
    kms = []
    for j in range(HEADS_PER_GROUP):
        if nsub == 1:
            km = (lane >= j * HEAD_DIM) & (lane < (j + 1) * HEAD_DIM)
        else:
            jj = j % 2
            km = ((lane >= jj * B_NOPE) & (lane < (jj + 1) * B_NOPE)) | \
                 ((lane >= 2 * B_NOPE + jj * B_ROPE) & (lane < 2 * B_NOPE + (jj + 1) * B_ROPE))
        kms.append(jnp.where(km, 1.0, 0.0).astype(BF16))
    return kms


def _head_row_masks(cols):
    row = lax.broadcasted_iota(jnp.int32, (MXU_DIM, cols), 0)
    return [jnp.where((row >= j * HEAD_DIM) & (row < (j + 1) * HEAD_DIM), 1.0, 0.0).astype(BF16)
            for j in range(HEADS_PER_GROUP)]


def _rows4(vals, cols):
    return jnp.concatenate([jnp.broadcast_to(v, (HEAD_DIM, cols)) for v in vals], axis=0)


def _flash_kernel(*refs, nsub, tk, nsrc, nchain):
    qt_ref = refs[0]
    srcs = [(refs[1 + 2 * s], refs[2 + 2 * s]) for s in range(nsrc)]
    o_ref, kb_ref, vbt_ref, acc_ref = refs[1 + 2 * nsrc:]
    hpu = HEADS_PER_GROUP // nsub
    tq = qt_ref.shape[2]
    tqc = tq // nchain
    nb = kb_ref.shape[0]

    @pl.when(pl.program_id(2) == 0)
    def _build():
        kms = _head_lane_masks(tk, nsub)
        vms = _head_row_masks(tk)
        base = 0
        for k_ref, vt_ref in srcs:
            nblk = k_ref.shape[1] // tk

            def body(i, carry, k_ref=k_ref, base=base):
                r0 = pl.multiple_of(i * tk, tk)
                for j in range(HEADS_PER_GROUP):
                    u = j // hpu
                    kb_ref[base + i, j * tk:(j + 1) * tk, :] = \
                        k_ref[0, pl.ds(r0, tk), u * MXU_DIM:(u + 1) * MXU_DIM] * kms[j]
                return carry

            lax.fori_loop(0, nblk, body, 0)
            for i in range(nblk):
                for j in range(HEADS_PER_GROUP):
                    vbt_ref[base + i, :, j * tk:(j + 1) * tk] = vt_ref[0, :, i * tk:(i + 1) * tk] * vms[j]
            base += nblk

    def block(i, state):
        kb, vbt = kb_ref[i], vbt_ref[i]
        new_state = []
        first = state is None
        for h in range(nchain):
            cs = slice(h * tqc, (h + 1) * tqc)
            parts = [jnp.dot(kb[u * hpu * tk:(u + 1) * hpu * tk, :], qt_ref[0, u * MXU_DIM:(u + 1) * MXU_DIM, cs],
                             preferred_element_type=F32) for u in range(nsub)]
            ps, alphas, m_new, l_new = [], [], [], []
            for j in range(HEADS_PER_GROUP):
                sj = parts[j // hpu][(j % hpu) * tk:(j % hpu + 1) * tk, :]
                mj = jnp.max(sj, axis=0, keepdims=True)
                if not first:
                    m, l = state[h]
                    mj = jnp.maximum(m[j], mj)
                    a = jnp.exp2(m[j] - mj)
                    alphas.append(a)
                p = jnp.exp2(sj - mj)
                lj = jnp.sum(p, axis=0, keepdims=True)
                l_new.append(lj if first else a * l[j] + lj)
                m_new.append(mj)
                ps.append(p.astype(BF16))
            pv = jnp.dot(vbt, jnp.concatenate(ps, axis=0), preferred_element_type=F32)
            acc_ref[:, cs] = pv if first else acc_ref[:, cs] * _rows4(alphas, tqc) + pv
            new_state.append((m_new, l_new))
        return new_state

    state = None
    for i in range(nb):
        state = block(i, state)
    for h in range(nchain):
        cs = slice(h * tqc, (h + 1) * tqc)
        ot = acc_ref[:, cs] * _rows4([1.0 / v for v in state[h][1]], tqc)
        o_ref[0, cs, :] = ot.T.astype(BF16)


def _flash(qt, ksrcs, vtsrcs, nsub, tq, tk):
    bsz, qw, nq = qt.shape
    kw = nsub * MXU_DIM
    groups = qw // kw
    tq = min(tq, nq)
    nchain = max(tq // (2 * MXU_DIM), 1)
    in_specs = [pl.BlockSpec((1, kw, tq), lambda b, g, i: (b, g, i))]
    args = [qt]
    nb = 0
    for k, vt in zip(ksrcs, vtsrcs):
        lk = k.shape[1]
        assert lk % tk == 0
        nb += lk // tk
        in_specs += [pl.BlockSpec((1, lk, kw), lambda b, g, i: (b, 0, g)),
                     pl.BlockSpec((1, MXU_DIM, lk), lambda b, g, i: (b, g, 0))]
        args += [k, vt]
    scratch = [pltpu.VMEM((nb, HEADS_PER_GROUP * tk, MXU_DIM), BF16),
               pltpu.VMEM((nb, MXU_DIM, HEADS_PER_GROUP * tk), BF16),
               pltpu.VMEM((MXU_DIM, tq), F32)]
    return pl.pallas_call(
        functools.partial(_flash_kernel, nsub=nsub, tk=tk, nsrc=len(ksrcs), nchain=nchain),
        grid=(bsz, groups, nq // tq),
        in_specs=in_specs,
        out_specs=pl.BlockSpec((1, tq, MXU_DIM), lambda b, g, i: (b, i, g)),
        out_shape=jax.ShapeDtypeStruct((bsz, nq, groups * MXU_DIM), BF16),
        scratch_shapes=scratch,
        compiler_params=_params(("parallel", "parallel", "arbitrary")),
        name="flash_attn",
    )(*args)


HALO = 16


def _ffn_kernel(xp_ref, x_ref, xn_ref, oap_ref, oa_ref, oan_ref, obp_ref, ob_ref, obn_ref, wo_ref, gtm_ref,
                g_ref, sh_ref, sc_ref, gt_ref, wup_ref, cw_ref, wdn_ref, fg_ref,
                o_ref, h_ref, om_ref, *, final_norm, nf):
    i = pl.program_id(1)
    last = pl.num_programs(1) - 1
    tm = x_ref.shape[1]
    fdim = wdn_ref.shape[0]
    tf = fdim // nf
    half = oa_ref.shape[2]
    rows = tm + 2 * HALO
    own = slice(HALO, HALO + tm)
    for r, (a_ref, b_ref) in ((slice(0, HALO), (oap_ref, obp_ref)), (own, (oa_ref, ob_ref)),
                              (slice(HALO + tm, rows), (oan_ref, obn_ref))):
        om_ref[r, :half] = a_ref[0]
        om_ref[r, half:] = b_ref[0]
    ym = gtm_ref[0] * jnp.dot(om_ref[...], wo_ref[...], preferred_element_type=F32)
    x1 = x_ref[0] + ym[own]
    gain, shift, scale = g_ref[...], sh_ref[0], sc_ref[0]
    keep_p = jnp.where(i > 0, 1.0, 0.0)
    keep_n = jnp.where(i < last, 1.0, 0.0)
    h_ref[0:HALO, :] = (_norm_mod(xp_ref[0] + ym[0:HALO], gain, shift, scale) * keep_p).astype(BF16)
    h_ref[own, :] = _norm_mod(x1, gain, shift, scale).astype(BF16)
    h_ref[HALO + tm:, :] = (_norm_mod(xn_ref[0] + ym[HALO + tm:], gain, shift, scale) * keep_n).astype(BF16)

    def conv(u, f, part):
        w = cw_ref[:, cols(part, f)]
        up = pltpu.roll(u, 1, axis=0)[HALO:HALO + tm]
        un = pltpu.roll(u, rows - 1, axis=0)[HALO:HALO + tm]
        return up * w[0:1] + u[HALO:HALO + tm] * w[1:2] + un * w[2:3] + w[3:4]

    def cols(part, f):
        return slice(part * fdim + f * tf, part * fdim + (f + 1) * tf)

    h = h_ref[...]
    up = lambda f: (jnp.dot(h, wup_ref[:, cols(0, f)], preferred_element_type=F32),
                    jnp.dot(h, wup_ref[:, cols(1, f)], preferred_element_type=F32))
    y = None
    ug, uv = up(0)
    for f in range(nf):
        nxt = up(f + 1) if f + 1 < nf else None
        g = conv(ug, f, 0)
        v = conv(uv, f, 1)
        a = (g * (1.0 / (1.0 + jnp.exp(-g))) * v).astype(BF16)
        yf = jnp.dot(a, wdn_ref[f * tf:(f + 1) * tf, :], preferred_element_type=F32)
        y = yf if y is None else y + yf
        if nxt is not None:
            ug, uv = nxt
    out = x1 + gt_ref[0] * y
    if final_norm:
        out = out * lax.rsqrt(jnp.mean(out * out, axis=-1, keepdims=True) + NORM_EPS) * fg_ref[...]
    o_ref[0] = out


def _mix_ffn(x, oa, ob, w_out, gain, mods3, mrow, wup, cw, wdn, final_gain, tm):
    bsz, n, d = x.shape
    half = oa.shape[2]
    tm = min(tm, n)
    hb = tm // HALO
    nh = n // HALO
    final_norm = final_gain is not None
    fg = final_gain if final_norm else gain
    prev = lambda w: pl.BlockSpec((1, HALO, w), lambda b, i: (b, jnp.maximum(i * hb - 1, 0), 0))
    main = lambda w: pl.BlockSpec((1, tm, w), lambda b, i: (b, i, 0))
    nxt = lambda w: pl.BlockSpec((1, HALO, w), lambda b, i: (b, jnp.minimum((i + 1) * hb, nh - 1), 0))
    return pl.pallas_call(
        functools.partial(_ffn_kernel, final_norm=final_norm, nf=FFN_CHUNKS),
        grid=(bsz, n // tm),
        in_specs=[prev(d), main(d), nxt(d), prev(half), main(half), nxt(half), prev(half), main(half), nxt(half),
                  _resident(w_out.shape, lambda b, i: (0, 0)), _mod_spec(d, 2, mrow),
                  pl.BlockSpec((1, d), lambda b, i: (0, 0)),
                  _mod_spec(d, 3, mrow), _mod_spec(d, 4, mrow), _mod_spec(d, 5, mrow),
                  _resident(wup.shape, lambda b, i: (0, 0)),
                  _resident(cw.shape, lambda b, i: (0, 0)),
                  _resident(wdn.shape, lambda b, i: (0, 0)),
                  pl.BlockSpec((1, d), lambda b, i: (0, 0))],
        out_specs=main(d),
        out_shape=jax.ShapeDtypeStruct((bsz, n, d), F32),
        scratch_shapes=[pltpu.VMEM((tm + 2 * HALO, d), BF16), pltpu.VMEM((tm + 2 * HALO, 2 * half), BF16)],
        compiler_params=_params(("parallel", "parallel")),
        name="mix_ffn",
    )(x, x, x, oa, oa, oa, ob, ob, ob, w_out, mods3, gain.reshape(1, d), mods3, mods3, mods3, wup, cw, wdn,
      fg.reshape(1, d))


OD_Q, OD_C, OD_K, OD_V, OD_END = 0, 512, 2048, 2176, 2304
SUB = 8


def _prep_odd_kernel(xp_ref, x_ref, xn_ref, g_ref, sh_ref, sc_ref, win_ref, cw_ref, chd_ref, shd_ref,
                     q_ref, k_ref, v_ref, x0_ref, u_ref, h_ref, z_ref, *, scale):
    i = pl.program_id(1)
    last = pl.num_programs(1) - 1
    tm = x_ref.shape[1]
    rows = tm + 2 * HALO
    own = slice(HALO, HALO + tm)
    cwid = (OD_K - OD_C) // 3
    gain, shift, mscale = g_ref[...], sh_ref[0], sc_ref[0]
    keep_p = jnp.where(i > 0, 1.0, 0.0)
    keep_n = jnp.where(i < last, 1.0, 0.0)
    h_ref[0:HALO, :] = (_norm_mod(xp_ref[0], gain, shift, mscale) * keep_p).astype(BF16)
    h_ref[own, :] = _norm_mod(x_ref[0], gain, shift, mscale).astype(BF16)
    h_ref[HALO + tm:, :] = (_norm_mod(xn_ref[0], gain, shift, mscale) * keep_n).astype(BF16)
    z_ref[...] = jnp.dot(h_ref[...], win_ref[...], preferred_element_type=F32)
    for c in range((OD_C - OD_Q) // LANES):
        sl = slice(OD_Q + c * LANES, OD_Q + (c + 1) * LANES)
        y = _rope_lanes(z_ref[own, sl], chd_ref[...], shd_ref[...])
        q_ref[0, sl, :] = (y * scale).T.astype(BF16)
    k = _rope_lanes(z_ref[own, OD_K:OD_V], chd_ref[...], shd_ref[...])
    d0, d1 = _dup_heads(k)
    for c, dk in enumerate((d0, d0, d1, d1)):
        k_ref[0, :, c * LANES:(c + 1) * LANES] = dk.astype(BF16)
    d0, d1 = (d.T.astype(BF16) for d in _dup_heads(z_ref[own, OD_V:OD_END]))
    for c, dv in enumerate((d0, d0, d1, d1)):
        v_ref[0, c * LANES:(c + 1) * LANES, :] = dv

    def conv(part):
        sl = slice(OD_C + part * cwid, OD_C + (part + 1) * cwid)
        csl = slice(part * cwid, (part + 1) * cwid)
        ue = z_ref[:, sl]
        up = pltpu.roll(ue, 1, axis=0)[own]
        un = pltpu.roll(ue, rows - 1, axis=0)[own]
        return up * cw_ref[0:1, csl] + ue[own] * cw_ref[1:2, csl] + un * cw_ref[2:3, csl] + cw_ref[3:4, csl]

    x0_ref[0] = conv(0).astype(BF16)
    u_ref[0] = (conv(2) * conv(1)).astype(BF16)


def _prep_odd(x, gain, mods3, w_in, cw, chd, shd, tm):
    bsz, n, d = x.shape
    cols = w_in.shape[1]
    tm = min(tm, n)
    hb = tm // HALO
    nh = n // HALO
    cwid = (OD_K - OD_C) // 3
    row_blk = lambda w: pl.BlockSpec((1, tm, w), lambda b, i: (b, i, 0))
    col_blk = lambda w: pl.BlockSpec((1, w, tm), lambda b, i: (b, 0, i))
    slab = 4 * LANES
    return pl.pallas_call(
        functools.partial(_prep_odd_kernel, scale=HEAD_DIM ** -0.5 * LOG2E),
        grid=(bsz, n // tm),
        in_specs=[pl.BlockSpec((1, HALO, d), lambda b, i: (b, jnp.maximum(i * hb - 1, 0), 0)),
                  pl.BlockSpec((1, tm, d), lambda b, i: (b, i, 0)),
                  pl.BlockSpec((1, HALO, d), lambda b, i: (b, jnp.minimum((i + 1) * hb, nh - 1), 0)),
                  pl.BlockSpec((1, d), lambda b, i: (0, 0)),
                  _mod_spec(d, 0, None), _mod_spec(d, 1, None),
                  _resident(w_in.shape, lambda b, i: (0, 0)),
                  pl.BlockSpec(cw.shape, lambda b, i: (0, 0)),
                  pl.BlockSpec((tm, LANES), lambda b, i: (i, 0)),
                  pl.BlockSpec((tm, LANES), lambda b, i: (i, 0))],
        out_specs=[col_blk(slab), row_blk(slab), col_blk(slab), row_blk(cwid), row_blk(cwid)],
        out_shape=[jax.ShapeDtypeStruct((bsz, slab, n), BF16), jax.ShapeDtypeStruct((bsz, n, slab), BF16),
                   jax.ShapeDtypeStruct((bsz, slab, n), BF16), jax.ShapeDtypeStruct((bsz, n, cwid), BF16),
                   jax.ShapeDtypeStruct((bsz, n, cwid), BF16)],
        scratch_shapes=[pltpu.VMEM((tm + 2 * HALO, d), BF16), pltpu.VMEM((tm + 2 * HALO, cols), F32)],
        compiler_params=_params(("parallel", "parallel")),
        name="prep_odd",
    )(x, x, x, gain.reshape(1, d), mods3, mods3, w_in, cw, chd, shd)


def _kv_ctx_odd_kernel(z_ref, k_ref, vt_ref):
    d0, d1 = _dup_heads(z_ref[0, :, 0:LANES])
    for c, dk in enumerate((d0, d0, d1, d1)):
        k_ref[0, :, c * LANES:(c + 1) * LANES] = dk.astype(BF16)
    d0, d1 = (d.T.astype(BF16) for d in _dup_heads(z_ref[0, :, LANES:2 * LANES]))
    for c, dv in enumerate((d0, d0, d1, d1)):
        vt_ref[0, c * LANES:(c + 1) * LANES, :] = dv


def _kv_ctx_odd(z):
    bsz, n, cols = z.shape
    slab = 4 * LANES
    return pl.pallas_call(
        _kv_ctx_odd_kernel,
        grid=(bsz,),
        in_specs=[pl.BlockSpec((1, n, cols), lambda b: (b, 0, 0))],
        out_specs=[pl.BlockSpec((1, n, slab), lambda b: (b, 0, 0)), pl.BlockSpec((1, slab, n), lambda b: (b, 0, 0))],
        out_shape=[jax.ShapeDtypeStruct((bsz, n, slab), BF16), jax.ShapeDtypeStruct((bsz, slab, n), BF16)],
        compiler_params=_params(("parallel",)),
        name="kv_ctx_odd",
    )(z)


WIN_QB = 2
WIN_TILES = 8


def _window_kernel(qt_ref, kc_ref, vct_ref, kl_ref, vlt_ref, bias_ref, sink_ref, o_ref, kbc, vbct, kbl, vblt,
                   *, lc, nblk, qb, nch):
    qi = pl.program_id(2)
    tq = qb * Q_BLOCK
    nsp = qb + 2
    blk = HEADS_PER_GROUP * Q_BLOCK

    @pl.when(qi == 0)
    def _build():
        kms = _head_lane_masks(lc, 1)
        vms = _head_row_masks(lc)
        for j in range(HEADS_PER_GROUP):
            kbc[j * lc:(j + 1) * lc, :] = kc_ref[0] * kms[j]
            vbct[:, j * lc:(j + 1) * lc] = vct_ref[0] * vms[j]
        kms = _head_lane_masks(Q_BLOCK, 1)
        vms = _head_row_masks(Q_BLOCK)
        for e in (0, nblk + 1):
            kbl[e] = jnp.zeros((blk, MXU_DIM), BF16)
            vblt[e] = jnp.zeros((MXU_DIM, blk), BF16)

        def body(i, carry):
            r0 = pl.multiple_of(i * Q_BLOCK, Q_BLOCK)
            for j in range(HEADS_PER_GROUP):
                kbl[i + 1, j * Q_BLOCK:(j + 1) * Q_BLOCK, :] = kl_ref[0, pl.ds(r0, Q_BLOCK), :] * kms[j]
            return carry

        lax.fori_loop(0, nblk, body, 0)
        for i in range(nblk):
            for j in range(HEADS_PER_GROUP):
                vblt[i + 1, :, j * Q_BLOCK:(j + 1) * Q_BLOCK] = vlt_ref[0, :, i * Q_BLOCK:(i + 1) * Q_BLOCK] * vms[j]

    ntile = nblk // qb
    for h in range(nch):
        t = qi * nch + h
        qt = qt_ref[0, :, h * tq:(h + 1) * tq]
        s_c = jnp.dot(kbc[...], qt, preferred_element_type=F32)
        kspan = kbl[pl.ds(qb * t, nsp)].reshape(nsp * blk, MXU_DIM)
        s_s = jnp.dot(kspan, qt, preferred_element_type=F32) + bias_ref[...]
        edge = {0: jnp.where(t == 0, NEG_INF, 0.0), nsp - 1: jnp.where(t == ntile - 1, NEG_INF, 0.0)}
        pcs, pss, invs = [], [[None] * HEADS_PER_GROUP for _ in range(nsp)], []
        for j in range(HEADS_PER_GROUP):
            sink = sink_ref[0, j][0:1, :]
            segs = [s_c[j * lc:(j + 1) * lc, :]]
            for b in range(nsp):
                sg = s_s[b * blk + j * Q_BLOCK: b * blk + (j + 1) * Q_BLOCK, :]
                segs.append(sg + edge[b] if b in edge else sg)
            m = sink
            for sg in segs:
                m = jnp.maximum(m, jnp.max(sg, axis=0, keepdims=True))
            ps = [jnp.exp2(sg - m) for sg in segs]
            den = jnp.exp2(sink - m)
            for p in ps:
                den = den + jnp.sum(p, axis=0, keepdims=True)
            invs.append(1.0 / den)
            pcs.append(ps[0].astype(BF16))
            for b in range(nsp):
                pss[b][j] = ps[1 + b].astype(BF16)
        ot = jnp.dot(vbct[...], jnp.concatenate(pcs, axis=0), preferred_element_type=F32)
        for b in range(nsp):
            ot = ot + jnp.dot(vblt[qb * t + b], jnp.concatenate(pss[b], axis=0), preferred_element_type=F32)
        o_ref[0, h * tq:(h + 1) * tq, :] = (ot * _rows4(invs, tq)).T.astype(BF16)


@functools.lru_cache(maxsize=None)
def _window_bias(qb):
    nsp = qb + 2
    row = np.arange(nsp * HEADS_PER_GROUP * Q_BLOCK)[:, None]
    r = np.arange(qb * Q_BLOCK)[None, :]
    b = row // (HEADS_PER_GROUP * Q_BLOCK)
    jj = b * Q_BLOCK + row % Q_BLOCK
    band = (jj >= r) & (jj <= r + 2 * WINDOW)
    return np.where(band, 0.0, NEG_INF).astype(np.float32)


def _window(qt, kc, vct, kl, vlt, sink):
    bsz, qw, n = qt.shape
    groups = qw // MXU_DIM
    lc = kc.shape[1]
    nblk = n // Q_BLOCK
    qb = min(WIN_QB, nblk)
    tq = qb * Q_BLOCK
    nch = next(c for c in (WIN_TILES, 2, 1) if n % (c * tq) == 0)
    nq = n // (tq * nch)
    blk = HEADS_PER_GROUP * Q_BLOCK
    bias = jnp.asarray(_window_bias(qb))
    sink_t = jnp.broadcast_to((sink.astype(F32) * LOG2E).reshape(groups, HEADS_PER_GROUP, 1, 1),
                              (groups, HEADS_PER_GROUP, SUB, tq))

    return pl.pallas_call(
        functools.partial(_window_kernel, lc=lc, nblk=nblk, qb=qb, nch=nch),
        grid=(bsz, groups, nq),
        in_specs=[pl.BlockSpec((1, MXU_DIM, tq * nch), lambda b, g, i: (b, g, i)),
                  pl.BlockSpec((1, lc, MXU_DIM), lambda b, g, i: (b, 0, g)),
                  pl.BlockSpec((1, MXU_DIM, lc), lambda b, g, i: (b, g, 0)),
                  pl.BlockSpec((1, n, MXU_DIM), lambda b, g, i: (b, 0, g)),
                  pl.BlockSpec((1, MXU_DIM, n), lambda b, g, i: (b, g, 0)),
                  _resident(bias.shape, lambda b, g, i: (0, 0)),
                  pl.BlockSpec((1, HEADS_PER_GROUP, SUB, tq), lambda b, g, i: (g, 0, 0, 0))],
        out_specs=pl.BlockSpec((1, tq * nch, MXU_DIM), lambda b, g, i: (b, i, g)),
        out_shape=jax.ShapeDtypeStruct((bsz, n, qw), BF16),
        scratch_shapes=[pltpu.VMEM((HEADS_PER_GROUP * lc, MXU_DIM), BF16),
                        pltpu.VMEM((MXU_DIM, HEADS_PER_GROUP * lc), BF16),
                        pltpu.VMEM((nblk + 2, blk, MXU_DIM), BF16),
                        pltpu.VMEM((nblk + 2, MXU_DIM, blk), BF16)],
        compiler_params=_params(("parallel", "parallel", "arbitrary")),
        name="window_attn",
    )(qt, kc, vct, kl, vlt, bias, sink_t)


FFT_N2 = 128


def _filter_kernel(z_ref, w1_ref, b1_ref, w2_ref, b2_ref, w3_ref, b3_ref, w4_ref, fr_ref, dl_ref, o_ref):
    def lin(a, w_ref):
        a_hi, a_lo = _split_bf16(a)
        w_hi, w_lo = _split_bf16(w_ref[...])
        return _dot3(a_hi, a_lo, w_hi, w_lo)

    z = z_ref[...]
    h = b1_ref[...]
    for e in range(C_EMB_DIM):
        h = h + z[:, e:e + 1] * w1_ref[e:e + 1, :]
    h = jnp.sin(fr_ref[0:1] * h)
    h = jnp.sin(fr_ref[1:2] * (lin(h, w2_ref) + b2_ref[...]))
    h = jnp.sin(fr_ref[2:3] * (lin(h, w3_ref) + b3_ref[...]))
    h = lin(h, w4_ref)
    t = z[:, 0:1]
    cw = dl_ref.shape[1]
    dec = jnp.exp(-t * dl_ref[...])
    row = lax.broadcasted_iota(jnp.int32, dec.shape, 0) + pl.program_id(0) * z.shape[0]
    o_ref[0] = h[:, :cw] * dec
    o_ref[1] = jnp.where(row == 0, 0.0, h[:, cw:] * dec)


def _filters(zfeat, w1, b1, w2, b2, w3, b3, w4, freq, deltas, tm):
    n = zfeat.shape[0]
    tm = min(tm, n)
    cw = deltas.shape[1]
    full = lambda a: pl.BlockSpec(a.shape, lambda i: (0,) * a.ndim)
    ops = (w1, b1, w2, b2, w3, b3, w4, freq, deltas)
    return pl.pallas_call(
        _filter_kernel,
        grid=(n // tm,),
        in_specs=[pl.BlockSpec((tm, zfeat.shape[1]), lambda i: (i, 0))] + [full(a) for a in ops],
        out_specs=pl.BlockSpec((2, tm, cw), lambda i: (0, i, 0)),
        out_shape=jax.ShapeDtypeStruct((2, n, cw), F32),
        compiler_params=_params(("parallel",)),
        name="hyena_filters",
    )(zfeat, *ops)


def _dft_first_kernel(f_ref, x_ref, o_ref):
    _, n1h, tm2, cw = x_ref.shape
    x = x_ref[0].reshape(n1h * tm2, cw).astype(BF16)
    r = jnp.dot(f_ref[...], x, preferred_element_type=F32)
    o_ref[0] = r.astype(BF16).reshape(o_ref.shape[1:])


def _dft_first(f1, x):
    bsz, n1h, n2, cw = x.shape
    tm2 = HALO
    nkp = f1.shape[0] // (2 * tm2)
    return pl.pallas_call(
        _dft_first_kernel,
        grid=(bsz, n2 // tm2),
        in_specs=[pl.BlockSpec(f1.shape, lambda b, j: (0, 0)),
                  pl.BlockSpec((1, n1h, tm2, cw), lambda b, j: (b, 0, j, 0))],
        out_specs=pl.BlockSpec((1, 2, nkp, tm2, cw), lambda b, j: (b, 0, 0, j, 0)),
        out_shape=jax.ShapeDtypeStruct((bsz, 2, nkp, n2, cw), BF16),
        compiler_params=_params(("parallel", "parallel")),
        name="dft_first",
    )(f1, x)


def _dft_last_gate_kernel(f_ref, b_ref, x0_ref, u_ref, bias_ref, o_ref):
    _, _, nkp, tm2, cw = b_ref.shape
    y = jnp.dot(f_ref[...], b_ref[0].reshape(2 * nkp * tm2, cw), preferred_element_type=F32)
    y = y.reshape(x0_ref.shape[1:])
    o_ref[0] = (x0_ref[0].astype(F32) * (y + u_ref[0].astype(F32) * bias_ref[...])).astype(BF16)


def _dft_last_gate(f2, b, x0, u, bias):
    bsz, _, nkp, n2, cw = b.shape
    tm2 = HALO
    n1h = f2.shape[0] // tm2
    tblk = pl.BlockSpec((1, n1h, tm2, cw), lambda b_, j: (b_, 0, j, 0))
    return pl.pallas_call(
        _dft_last_gate_kernel,
        grid=(bsz, n2 // tm2),
        in_specs=[pl.BlockSpec(f2.shape, lambda b_, j: (0, 0)),
                  pl.BlockSpec((1, 2, nkp, tm2, cw), lambda b_, j: (b_, 0, 0, j, 0)),
                  tblk, tblk, pl.BlockSpec((1, cw), lambda b_, j: (0, 0))],
        out_specs=tblk,
        out_shape=jax.ShapeDtypeStruct((bsz, n1h, n2, cw), BF16),
        compiler_params=_params(("parallel", "parallel")),
        name="dft_last_gate",
    )(f2, b, x0, u, bias.reshape(1, cw))


def _spec_fwd(a, m_ref):
    return jnp.dot(m_ref[0], a.reshape(2 * FFT_N2, a.shape[-1]), preferred_element_type=F32)


def _filter_spec_kernel(a_ref, m_ref, o_ref):
    for kk in range(a_ref.shape[2]):
        xf = _spec_fwd(a_ref[0, :, kk], m_ref.at[kk:kk + 1])
        xb = _spec_fwd(a_ref[1, :, kk], m_ref.at[kk:kk + 1])
        o_ref[0, kk] = xf[:FFT_N2] + xb[:FFT_N2]
        o_ref[1, kk] = xf[FFT_N2:] - xb[FFT_N2:]


def _conv_spec_kernel(a_ref, k_ref, m_ref, i_ref, o_ref):
    kr, ki = k_ref[0, 0], k_ref[1, 0]
    for b in range(a_ref.shape[0]):
        x = _spec_fwd(a_ref[b, :, 0], m_ref)
        xr, xi = x[:FFT_N2], x[FFT_N2:]
        y = jnp.concatenate([xr * kr - xi * ki, xr * ki + xi * kr], axis=0).astype(BF16)
        bv = jnp.dot(i_ref[0], y, preferred_element_type=F32)
        o_ref[b, :, 0] = bv.astype(BF16).reshape(2, FFT_N2, bv.shape[-1])


def _filter_spec(a, m_fwd):
    _, _, nkp, _, cw = a.shape
    return pl.pallas_call(
        _filter_spec_kernel,
        grid=(nkp // SUB,),
        in_specs=[pl.BlockSpec((2, 2, SUB, FFT_N2, cw), lambda k: (0, 0, k, 0, 0)),
                  pl.BlockSpec((SUB, 2 * FFT_N2, 2 * FFT_N2), lambda k: (k, 0, 0))],
        out_specs=pl.BlockSpec((2, SUB, FFT_N2, cw), lambda k: (0, k, 0, 0)),
        out_shape=jax.ShapeDtypeStruct((2, nkp, FFT_N2, cw), F32),
        compiler_params=_params(("parallel",)),
        name="filter_spectrum",
    )(a, m_fwd)


def _conv_spec(a, kspec, m_fwd, m_inv):
    bsz, _, nkp, _, cw = a.shape
    blk = pl.BlockSpec((bsz, 2, 1, FFT_N2, cw), lambda k: (0, 0, k, 0, 0))
    mat = pl.BlockSpec((1, 2 * FFT_N2, 2 * FFT_N2), lambda k: (k, 0, 0))
    return pl.pallas_call(
        _conv_spec_kernel,
        grid=(nkp,),
        in_specs=[blk, pl.BlockSpec((2, 1, FFT_N2, cw), lambda k: (0, k, 0, 0)), mat, mat],
        out_specs=blk,
        out_shape=jax.ShapeDtypeStruct(a.shape, BF16),
        compiler_params=_params(("parallel",)),
        name="conv_spectrum",
    )(a, kspec, m_fwd, m_inv)


@functools.lru_cache(maxsize=None)
def _dft_tables(n):
    nfft = 2 * n
    n2 = FFT_N2
    n1 = nfft // n2
    n1h = n1 // 2
    nk = n1h + 1
    nkp = -(-nk // SUB) * SUB

    def cs(num, den):
        ang = (num % den).astype(np.float64) * (2.0 * math.pi / den)
        return np.cos(ang), np.sin(ang)

    def pad_k(t, axis):
        widths = [(0, 0)] * t.ndim
        widths[axis] = (0, nkp - nk)
        return np.pad(t, widths)

    k1 = np.arange(nk, dtype=np.int64)
    c1, s1 = cs(k1[:, None] * np.arange(n1h)[None, :], n1)
    f1 = np.concatenate([pad_k(c1, 0), pad_k(-s1, 0)], axis=0)
    kk = k1[:, None, None] + n1 * np.arange(n2)[None, :, None]
    cg, sg = cs(kk * np.arange(n2)[None, None, :], nfft)
    g_re, g_im = cg, -sg
    m_fwd = np.concatenate([np.concatenate([g_re, -g_im], axis=2),
                            np.concatenate([g_im, g_re], axis=2)], axis=1)
    gt_re, gt_im = np.swapaxes(g_re, 1, 2), np.swapaxes(g_im, 1, 2)
    m_inv = np.concatenate([np.concatenate([gt_re, gt_im], axis=2),
                            np.concatenate([-gt_im, gt_re], axis=2)], axis=1)
    wk = np.where((k1 == 0) | (k1 == n1h), 1.0, 2.0)[None, :] * (1.0 / nfft)
    c2, s2 = cs(np.arange(n1h)[:, None] * k1[None, :], n1)
    f2 = np.concatenate([pad_k(c2 * wk, 1), pad_k(-s2 * wk, 1)], axis=1)
    eye = np.eye(HALO)
    return tuple(t.astype(np.float32) for t in (np.kron(f1, eye), pad_k(m_fwd, 0), pad_k(m_inv, 0), np.kron(f2, eye)))


def _hyena_long_conv(u, x0, bias, hcat, tabs):
    f1, m_fwd, m_inv, f2 = tabs
    bsz, n, cw = u.shape
    n2 = FFT_N2
    n1h = n // n2
    kspec = _filter_spec(_dft_first(f1, hcat.reshape(2, n1h, n2, cw)), m_fwd)
    u4 = u.reshape(bsz, n1h, n2, cw)
    b_u = _conv_spec(_dft_first(f1, u4), kspec, m_fwd, m_inv)
    return _dft_last_gate(f2, b_u, x0.reshape(bsz, n1h, n2, cw), u4, bias).reshape(bsz, n, cw)


def _axial_angles(rows, rope_dim):
    row_idx = np.repeat(np.arange(rows), GRID_W).astype(np.float64)
    col_idx = np.tile(np.arange(GRID_W), rows).astype(np.float64)
    d_axis = rope_dim // 2
    inv_freq = ROPE_THETA ** (-np.arange(0, d_axis, 2, dtype=np.float64) / d_axis)
    ang = np.concatenate([row_idx[:, None] * inv_freq, col_idx[:, None] * inv_freq], axis=-1)
    return np.cos(ang), np.sin(ang)


@functools.lru_cache(maxsize=None)
def _rope_tables(n):
    rows = n // GRID_W
    sign = np.tile(np.array([-1.0, 1.0]), LANES // 2)[None, :]
    cos, sin = _axial_angles(rows, HEAD_DIM)
    chd = np.tile(np.repeat(cos, 2, axis=1), (1, LANES // HEAD_DIM))
    shd = np.tile(np.repeat(sin, 2, axis=1), (1, LANES // HEAD_DIM)) * sign
    cos, sin = _axial_angles(rows, B_ROPE)
    ones = np.ones((n, LANES - 2 * B_ROPE))
    cb = np.concatenate([np.tile(np.repeat(cos, 2, axis=1), (1, 2)), ones], axis=1)
    sb = np.concatenate([np.tile(np.repeat(sin, 2, axis=1), (1, 2)), 0.0 * ones], axis=1) * sign
    return tuple(t.astype(np.float32) for t in (chd, shd, cb, sb))


@functools.lru_cache(maxsize=None)
def _hyena_features(n, cwid):
    t = np.linspace(0.0, 1.0, n)[:, None]
    wpos = 2 * math.pi * np.arange(n)[:, None] / n
    fb = np.linspace(1e-4, C_BANDS - 1, C_BANDS)[None, :]
    zfeat = np.concatenate([t, np.cos(fb * wpos), -np.sin(fb * wpos), np.zeros((n, SUB - C_EMB_DIM))], axis=-1)
    deltas = np.abs(np.linspace(C_MIN_DECAY, C_MAX_DECAY, cwid))[None, :]
    return zfeat.astype(np.float32), deltas.astype(np.float32)


def _even_weights(w_in, a_qn, a_kn, b_qn, b_w_uq, b_kvn, b_w_ukv):
    d = w_in.shape[0]
    aq_w = w_in.shape[1] - (B_Q_RANK + 2 * A_KV_HEADS * HEAD_DIM + B_KV_RANK + B_ROPE)
    o = [0, aq_w, aq_w + B_Q_RANK]
    o += [o[-1] + A_KV_HEADS * HEAD_DIM, o[-1] + 2 * A_KV_HEADS * HEAD_DIM]
    o += [o[-1] + B_KV_RANK, o[-1] + B_KV_RANK + B_ROPE]
    kr = w_in[:, o[5]:o[6]]
    w_aug = jnp.concatenate([w_in[:, :o[5]], kr, kr, jnp.zeros((d, LANES - 2 * B_ROPE), w_in.dtype)], axis=1)
    heads = b_w_uq.shape[1] // (B_NOPE + B_ROPE)
    uq = b_w_uq.reshape(B_Q_RANK, heads // 2, 2, B_NOPE + B_ROPE)
    wuq = jnp.concatenate([uq[:, :, 0, :B_NOPE], uq[:, :, 1, :B_NOPE], uq[:, :, 0, B_NOPE:], uq[:, :, 1, B_NOPE:],
                           jnp.zeros((B_Q_RANK, heads // 2, MXU_DIM - 2 * (B_NOPE + B_ROPE)), b_w_uq.dtype)], axis=2)
    wuq = wuq.reshape(B_Q_RANK, heads // 2 * MXU_DIM)
    ukv = b_w_ukv.reshape(B_KV_RANK, heads, B_NOPE + B_VDIM)
    wukn = ukv[:, :, :B_NOPE].reshape(B_KV_RANK, heads * B_NOPE)
    wuv = ukv[:, :, B_NOPE:].reshape(B_KV_RANK, heads * B_VDIM)
    gains = (jnp.tile(a_qn, aq_w // HEAD_DIM)[None, :], jnp.tile(a_kn, A_KV_HEADS)[None, :],
             b_qn[None, :], b_kvn[None, :])
    return w_aug.astype(BF16), gains + (wuq.astype(BF16), wukn.astype(BF16), wuv.astype(BF16))


def _ffn_weights(w_up, conv_w, conv_b, w_down):
    cw = jnp.concatenate([conv_w, conv_b[None, :], jnp.zeros((SUB - 4, conv_w.shape[1]), conv_w.dtype)], axis=0)
    return w_up.astype(BF16), cw, w_down.astype(BF16)


FFN_CHUNKS = 1


def kernel(x, c, ctx, c_ctx, w_mod, b_mod, norm_mix, norm_ffn, ev_w_in, ev_w_out, a_q_norm, a_k_norm, b_q_norm, b_w_uq, b_kv_norm, b_w_ukv, od_w_in, od_w_out, d_sink, c_conv_w, c_conv_b, c_filt_w1, c_filt_b1, c_filt_w2, c_filt_b2, c_filt_w3, c_filt_b3, c_filt_w4, c_filt_freq, c_bias, ffn_w_up, ffn_conv_w, ffn_conv_b, ffn_w_down, final_norm):
    bsz, n, d = x.shape
    depth = w_mod.shape[0]
    assert depth == 2 and n % Q_BLOCK == 0 and d % LANES == 0
    tm = ROW_TILE
    rows = -(-(bsz + 1) // SUB) * SUB
    cvec = jnp.concatenate([c, c_ctx[None, :], jnp.zeros((rows - bsz - 1, d), F32)], axis=0)
    mods = _mods(cvec, w_mod, b_mod)
    chd, shd, cb, sb = (jnp.asarray(t) for t in _rope_tables(n))
    tabs = (chd, shd, cb, sb)

    m3 = mods[0].reshape(rows, 1, N_MOD * d)
    w_aug, prep_w = _even_weights(ev_w_in[0], a_q_norm[0], a_k_norm[0], b_q_norm[0], b_w_uq[0],
                                  b_kv_norm[0], b_w_ukv[0])
    qa_l, ka_l, va_l, qb_l, kb_l, vb_l = _prep_even(x, norm_mix[0], m3, None, w_aug, prep_w, tabs, True, tm)
    qa_c, ka_c, va_c, qb_c, kb_c, vb_c = _prep_even(ctx, norm_mix[0], m3, bsz, w_aug, prep_w, tabs, False, tm)
    tq, tk = ATTN_TQ, ATTN_TK
    oa_l = _flash(qa_l, (ka_c, ka_l), (va_c, va_l), 1, tq, tk)
    ob_l = _flash(qb_l, (kb_c, kb_l), (vb_c, vb_l), 2, tq, tk)
    oa_c = _flash(qa_c, (ka_c,), (va_c,), 1, tq, tk)
    ob_c = _flash(qb_c, (kb_c,), (vb_c,), 2, tq, tk)
    w_out = ev_w_out[0].astype(BF16)
    ffn_w = _ffn_weights(ffn_w_up[0], ffn_conv_w[0], ffn_conv_b[0], ffn_w_down[0])
    x = _mix_ffn(x, oa_l, ob_l, w_out, norm_ffn[0], m3, None, *ffn_w, None, tm)
    ctx = _mix_ffn(ctx, oa_c, ob_c, w_out, norm_ffn[0], m3, bsz, *ffn_w, None, tm)

    m3 = mods[1].reshape(rows, 1, N_MOD * d)
    w_in = od_w_in[0].astype(BF16)
    z_c = _inproj(ctx, norm_mix[1], m3, bsz, w_in[:, OD_K:], tm)
    cw = jnp.concatenate([c_conv_w[0], c_conv_b[0][None, :], jnp.zeros((SUB - 4, OD_K - OD_C), F32)], axis=0)
    qd, kd, vd, x0, u = _prep_odd(x, norm_mix[1], m3, w_in, cw, chd, shd, 2 * tm)
    kd_c, vd_c = _kv_ctx_odd(z_c)
    od = _window(qd, kd_c, vd_c, kd, vd, d_sink[0])
    zfeat, deltas = (jnp.asarray(t) for t in _hyena_features(n, c_bias.shape[1]))
    w1 = jnp.concatenate([c_filt_w1[0], jnp.zeros((SUB - C_EMB_DIM, c_filt_w1.shape[2]), F32)], axis=0)
    hcat = _filters(zfeat, w1, c_filt_b1[0][None, :], c_filt_w2[0], c_filt_b2[0][None, :], c_filt_w3[0],
                    c_filt_b3[0][None, :], c_filt_w4[0], jnp.concatenate([c_filt_freq[0], jnp.zeros((SUB - 3, c_filt_freq.shape[2]), F32)], axis=0),
                    deltas, tm)
    oc = _hyena_long_conv(u, x0, c_bias[0], hcat, tuple(jnp.asarray(t).astype(BF16) for t in _dft_tables(n)))
    ffn_w = _ffn_weights(ffn_w_up[1], ffn_conv_w[1], ffn_conv_b[1], ffn_w_down[1])
    return _mix_ffn(x, od, oc, od_w_out[0].astype(BF16), norm_ffn[1], m3, None, *ffn_w, final_norm, tm)
```

```python
import functools
import math

import jax
import jax.numpy as jnp
import numpy as np
from jax import lax
from jax.experimental import pallas as pl
from jax.experimental.pallas import tpu as pltpu

F32 = jnp.float32
BF16 = jnp.bfloat16

GRID_W = 64
HEAD_DIM = 64
ROPE_THETA = 10000.0
NORM_EPS = 1e-6
NEG_INF = -1e30
N_MOD = 6
A_KV_HEADS = 2
B_NOPE = 64
B_ROPE = 32
B_VDIM = 64
B_Q_RANK = 384
B_KV_RANK = 256
C_EMB_DIM = 5
C_BANDS = (C_EMB_DIM - 1) // 2
C_MIN_DECAY = math.log(1e-2) / 1.5
C_MAX_DECAY = math.log(1e-2) / 0.3
WINDOW = 128
Q_BLOCK = 128
LOG2E = math.log2(math.e)

LANES = 128
MXU_DIM = 256
VMEM_LIMIT = 56 * 1024 * 1024
HEADS_PER_GROUP = MXU_DIM // HEAD_DIM

ROW_TILE = 512
ATTN_TQ = 1024
ATTN_TK = 256


def _params(sem, vmem=VMEM_LIMIT):
    return pltpu.CompilerParams(dimension_semantics=sem, vmem_limit_bytes=vmem)


def _resident(shape, index_map):
    return pl.BlockSpec(shape, index_map, pipeline_mode=pl.Buffered(1))


def _norm_mod(x, gain, shift, scale):
    inv = lax.rsqrt(jnp.mean(x * x, axis=-1, keepdims=True) + NORM_EPS)
    return (x * inv) * gain * (1.0 + scale) + shift


def _rope_lanes(x, cos, sin_signed):
    lane = lax.broadcasted_iota(jnp.int32, x.shape, 1)
    nxt = pltpu.roll(x, LANES - 1, axis=1)
    prv = pltpu.roll(x, 1, axis=1)
    swapped = jnp.where(lane % 2 == 0, nxt, prv)
    return x * cos + swapped * sin_signed


def _head_rmsnorm_lanes(x, gain):
    lane = lax.broadcasted_iota(jnp.int32, x.shape, 1)
    lo = lane < HEAD_DIM
    sq = x * x
    s_lo = jnp.sum(jnp.where(lo, sq, 0.0), axis=-1, keepdims=True)
    s_hi = jnp.sum(jnp.where(lo, 0.0, sq), axis=-1, keepdims=True)
    ms = jnp.where(lo, s_lo, s_hi) * (1.0 / HEAD_DIM)
    return x * lax.rsqrt(ms + NORM_EPS) * gain


def _dup_heads(x):
    lane = lax.broadcasted_iota(jnp.int32, x.shape, 1)
    lo = lane < HEAD_DIM
    r = pltpu.roll(x, HEAD_DIM, axis=1)
    return jnp.where(lo, x, r), jnp.where(lo, r, x)


def _split_bf16(x):
    hi = x.astype(BF16)
    lo = (x - hi.astype(F32)).astype(BF16)
    return hi, lo


def _dot3(a_hi, a_lo, b_hi, b_lo):
    d = functools.partial(jnp.dot, preferred_element_type=F32)
    return d(a_hi, b_hi) + d(a_hi, b_lo) + d(a_lo, b_hi)


def _mods_kernel(c_ref, w_ref, b_ref, o_ref):
    c = c_ref[...]
    s = c * (1.0 / (1.0 + jnp.exp(-c)))
    s_hi, s_lo = _split_bf16(s)
    w_hi, w_lo = _split_bf16(w_ref[0])
    o_ref[0] = _dot3(s_hi, s_lo, w_hi, w_lo) + b_ref[0]


def _mods(cvec, w_mod, b_mod):
    depth, d, n = w_mod.shape
    rows = cvec.shape[0]
    tn = 1536
    return pl.pallas_call(
        _mods_kernel,
        grid=(depth, n // tn),
        in_specs=[pl.BlockSpec((rows, d), lambda l, j: (0, 0)),
                  pl.BlockSpec((1, d, tn), lambda l, j: (l, 0, j)),
                  pl.BlockSpec((1, 1, tn), lambda l, j: (l, 0, j))],
        out_specs=pl.BlockSpec((1, rows, tn), lambda l, j: (l, 0, j)),
        out_shape=jax.ShapeDtypeStruct((depth, rows, n), F32),
        compiler_params=_params(("arbitrary", "arbitrary")),
        name="mods",
    )(cvec, w_mod, b_mod.reshape(depth, 1, n))


def _mod_spec(d, chunk, row):
    if row is None:
        return pl.BlockSpec((1, 1, d), lambda b, *_: (b, 0, chunk))
    return pl.BlockSpec((1, 1, d), lambda b, *_: (row, 0, chunk))


def _inproj_kernel(x_ref, g_ref, sh_ref, sc_ref, w_ref, o_ref):
    h = _norm_mod(x_ref[0], g_ref[...], sh_ref[0], sc_ref[0])
    o_ref[0] = jnp.dot(h.astype(BF16), w_ref[...], preferred_element_type=F32)


def _inproj(x, gain, mods3, mrow, w, tm):
    bsz, n, d = x.shape
    cols = w.shape[1]
    tm = min(tm, n)
    return pl.pallas_call(
        _inproj_kernel,
        grid=(bsz, n // tm),
        in_specs=[pl.BlockSpec((1, tm, d), lambda b, i: (b, i, 0)),
                  pl.BlockSpec((1, d), lambda b, i: (0, 0)),
                  _mod_spec(d, 0, mrow), _mod_spec(d, 1, mrow),
                  _resident((d, cols), lambda b, i: (0, 0))],
        out_specs=pl.BlockSpec((1, tm, cols), lambda b, i: (b, i, 0)),
        out_shape=jax.ShapeDtypeStruct((bsz, n, cols), F32),
        compiler_params=_params(("parallel", "parallel")),
        name="inproj",
    )(x, gain.reshape(1, d), mods3, mods3, w)


EV_AQ, EV_BQ, EV_AK, EV_AV, EV_BKV, EV_BKR, EV_END = 0, 512, 896, 1024, 1152, 1408, 1536


def _prep_even_kernel(x_ref, g_ref, sh_ref, sc_ref, win_ref,
                      aqg_ref, akg_ref, bqg_ref, bkvg_ref, wuq_ref, wukn_ref, wuv_ref,
                      chd_ref, shd_ref, cb_ref, sb_ref,
                      qa_ref, ka_ref, va_ref, qb_ref, kb_ref, vb_ref, z_ref, *, use_rope, a_scale, b_scale):
    h = _norm_mod(x_ref[0], g_ref[...], sh_ref[0], sc_ref[0])
    z_ref[0] = jnp.dot(h.astype(BF16), win_ref[...], preferred_element_type=F32)

    def rope_hd(y):
        return _rope_lanes(y, chd_ref[...], shd_ref[...]) if use_rope else y

    def rope_b(y):
        return _rope_lanes(y, cb_ref[...], sb_ref[...]) if use_rope else y

    def put_t(dst, c, y):
        dst[0, c * LANES:(c + 1) * LANES, :] = y.T.astype(BF16)

    for c in range((EV_BQ - EV_AQ) // LANES):
        sl = slice(EV_AQ + c * LANES, EV_AQ + (c + 1) * LANES)
        y = rope_hd(_head_rmsnorm_lanes(z_ref[0, :, sl], aqg_ref[:, c * LANES:(c + 1) * LANES]))
        put_t(qa_ref, c, y * a_scale)
    k = rope_hd(_head_rmsnorm_lanes(z_ref[0, :, EV_AK:EV_AV], akg_ref[...]))
    d0, d1 = _dup_heads(k)
    d0 = d0.astype(BF16)
    d1 = d1.astype(BF16)
    ka_ref[0, :, 0 * LANES:1 * LANES] = d0
    ka_ref[0, :, 1 * LANES:2 * LANES] = d0
    ka_ref[0, :, 2 * LANES:3 * LANES] = d1
    ka_ref[0, :, 3 * LANES:4 * LANES] = d1
    d0, d1 = (d.T.astype(BF16) for d in _dup_heads(z_ref[0, :, EV_AV:EV_BKV]))
    for c, dv in enumerate((d0, d0, d1, d1)):
        va_ref[0, c * LANES:(c + 1) * LANES, :] = dv
    cq = z_ref[0, :, EV_BQ:EV_AK]
    cq = cq * lax.rsqrt(jnp.mean(cq * cq, axis=-1, keepdims=True) + NORM_EPS) * bqg_ref[...]
    qb = jnp.dot(cq.astype(BF16), wuq_ref[...], preferred_element_type=F32)
    for c in range(qb.shape[1] // MXU_DIM):
        lo = slice(c * MXU_DIM, c * MXU_DIM + LANES)
        hi = slice(c * MXU_DIM + LANES, (c + 1) * MXU_DIM)
        put_t(qb_ref, 2 * c, qb[:, lo] * b_scale)
        put_t(qb_ref, 2 * c + 1, rope_b(qb[:, hi]) * b_scale)
    ckv = z_ref[0, :, EV_BKV:EV_BKR]
    ckv = (ckv * lax.rsqrt(jnp.mean(ckv * ckv, axis=-1, keepdims=True) + NORM_EPS) * bkvg_ref[...]).astype(BF16)
    kn = jnp.dot(ckv, wukn_ref[...], preferred_element_type=F32)
    vb = jnp.dot(ckv, wuv_ref[...], preferred_element_type=F32)
    for c in range(vb.shape[1] // LANES):
        put_t(vb_ref, c, vb[:, c * LANES:(c + 1) * LANES])
    kr = rope_b(z_ref[0, :, EV_BKR:EV_END]).astype(BF16)
    for c in range(kn.shape[1] // LANES):
        kb_ref[0, :, c * MXU_DIM:c * MXU_DIM + LANES] = kn[:, c * LANES:(c + 1) * LANES].astype(BF16)
        kb_ref[0, :, c * MXU_DIM + LANES:(c + 1) * MXU_DIM] = kr


def _prep_even(x, gain, mods3, mrow, w_in, wts, tabs, use_rope, tm):
    bsz, n, d = x.shape
    cols = w_in.shape[1]
    tm = min(tm, n)
    aqg, akg, bqg, bkvg, wuq, wukn, wuv = wts
    chd, shd, cb, sb = tabs
    full = lambda a: _resident(a.shape, lambda b, i: (0,) * a.ndim)
    tab = lambda a: pl.BlockSpec((tm, LANES), (lambda b, i: (i, 0)) if use_rope else (lambda b, i: (0, 0)))
    outs = [(4 * LANES, True), (4 * LANES, False), (4 * LANES, True), (wuq.shape[1], True),
            (2 * wukn.shape[1], False), (wuv.shape[1], True)]
    kern = functools.partial(_prep_even_kernel, use_rope=use_rope, a_scale=HEAD_DIM ** -0.5 * LOG2E,
                             b_scale=(B_NOPE + B_ROPE) ** -0.5 * LOG2E)
    return pl.pallas_call(
        kern,
        grid=(bsz, n // tm),
        in_specs=[pl.BlockSpec((1, tm, d), lambda b, i: (b, i, 0)),
                  pl.BlockSpec((1, d), lambda b, i: (0, 0)),
                  _mod_spec(d, 0, mrow), _mod_spec(d, 1, mrow), full(w_in),
                  full(aqg), full(akg), full(bqg), full(bkvg), full(wuq), full(wukn), full(wuv),
                  tab(chd), tab(shd), tab(cb), tab(sb)],
        out_specs=[pl.BlockSpec((1, w, tm), lambda b, i: (b, 0, i)) if t else
                   pl.BlockSpec((1, tm, w), lambda b, i: (b, i, 0)) for w, t in outs],
        out_shape=[jax.ShapeDtypeStruct((bsz, w, n) if t else (bsz, n, w), BF16) for w, t in outs],
        scratch_shapes=[pltpu.VMEM((1, tm, cols), F32)],
        compiler_params=_params(("parallel", "parallel")),
        name="prep_even",
    )(x, gain.reshape(1, d), mods3, mods3, w_in, aqg, akg, bqg, bkvg, wuq, wukn, wuv, chd, shd, cb, sb)


def _head_lane_masks(rows, nsub):
    lane = lax.broadcasted_iota(jnp.int32, (rows, MXU_DIM), 1)
    kms = []
    for j in range(HEADS_PER_GROUP):
        if nsub == 1:
            km = (lane >= j * HEAD_DIM) & (lane < (j + 1) * HEAD_DIM)
        else:
            jj = j % 2
            km = ((lane >= jj * B_NOPE) & (lane < (jj + 1) * B_NOPE)) | \
                 ((lane >= 2 * B_NOPE + jj * B_ROPE) & (lane < 2 * B_NOPE + (jj + 1) * B_ROPE))
        kms.append(jnp.where(km, 1.0, 0.0).astype(BF16))
    return kms


def _head_row_masks(cols):
    row = lax.broadcasted_iota(jnp.int32, (MXU_DIM, cols), 0)
    return [jnp.where((row >= j * HEAD_DIM) & (row < (j + 1) * HEAD_DIM), 1.0, 0.0).astype(BF16)
            for j in range(HEADS_PER_GROUP)]


def _rows4(vals, cols):
    return jnp.concatenate([jnp.broadcast_to(v, (HEAD_DIM, cols)) for v in vals], axis=0)


def _flash_kernel(*refs, nsub, tk, nsrc, nchain):
    qt_ref = refs[0]
    srcs = [(refs[1 + 2 * s], refs[2 + 2 * s]) for s in range(nsrc)]
    o_ref, kb_ref, vbt_ref, acc_ref = refs[1 + 2 * nsrc:]
    hpu = HEADS_PER_GROUP // nsub
    tq = qt_ref.shape[2]
    tqc = tq // nchain
    nb = kb_ref.shape[0]

    @pl.when(pl.program_id(2) == 0)
    def _build():
        kms = _head_lane_masks(tk, nsub)
        vms = _head_row_masks(tk)
        base = 0
        for k_ref, vt_ref in srcs:
            nblk = k_ref.shape[1] // tk

            def body(i, carry, k_ref=k_ref, base=base):
                r0 = pl.multiple_of(i * tk, tk)
                for j in range(HEADS_PER_GROUP):
                    u = j // hpu
                    kb_ref[base + i, j * tk:(j + 1) * tk, :] = \
                        k_ref[0, pl.ds(r0, tk), u * MXU_DIM:(u + 1) * MXU_DIM] * kms[j]
                return carry

            lax.fori_loop(0, nblk, body, 0)
            for i in range(nblk):
                for j in range(HEADS_PER_GROUP):
                    vbt_ref[base + i, :, j * tk:(j + 1) * tk] = vt_ref[0, :, i * tk:(i + 1) * tk] * vms[j]
            base += nblk

    def block(i, state):
        kb, vbt = kb_ref[i], vbt_ref[i]
        new_state = []
        first = state is None
        for h in range(nchain):
            cs = slice(h * tqc, (h + 1) * tqc)
            parts = [jnp.dot(kb[u * hpu * tk:(u + 1) * hpu * tk, :], qt_ref[0, u * MXU_DIM:(u + 1) * MXU_DIM, cs],
                             preferred_element_type=F32) for u in range(nsub)]
            ps, alphas, m_new, l_new = [], [], [], []
            for j in range(HEADS_PER_GROUP):
                sj = parts[j // hpu][(j % hpu) * tk:(j % hpu + 1) * tk, :]
                mj = jnp.max(sj, axis=0, keepdims=True)
                if not first:
                    m, l = state[h]
                    mj = jnp.maximum(m[j], mj)
                    a = jnp.exp2(m[j] - mj)
                    alphas.append(a)
                p = jnp.exp2(sj - mj)
                lj = jnp.sum(p, axis=0, keepdims=True)
                l_new.append(lj if first else a * l[j] + lj)
                m_new.append(mj)
                ps.append(p.astype(BF16))
            pv = jnp.dot(vbt, jnp.concatenate(ps, axis=0), preferred_element_type=F32)
            acc_ref[:, cs] = pv if first else acc_ref[:, cs] * _rows4(alphas, tqc) + pv
            new_state.append((m_new, l_new))
        return new_state

    state = None
    for i in range(nb):
        state = block(i, state)
    for h in range(nchain):
        cs = slice(h * tqc, (h + 1) * tqc)
        ot = acc_ref[:, cs] * _rows4([1.0 / v for v in state[h][1]], tqc)
        o_ref[0, cs, :] = ot.T.astype(BF16)


def _flash(qt, ksrcs, vtsrcs, nsub, tq, tk):
    bsz, qw, nq = qt.shape
    kw = nsub * MXU_DIM
    groups = qw // kw
    tq = min(tq, nq)
    nchain = max(tq // MXU_DIM, 1)
    in_specs = [pl.BlockSpec((1, kw, tq), lambda b, g, i: (b, g, i))]
    args = [qt]
    nb = 0
    for k, vt in zip(ksrcs, vtsrcs):
        lk = k.shape[1]
        assert lk % tk == 0
        nb += lk // tk
        in_specs += [pl.BlockSpec((1, lk, kw), lambda b, g, i: (b, 0, g)),
                     pl.BlockSpec((1, MXU_DIM, lk), lambda b, g, i: (b, g, 0))]
        args += [k, vt]
    scratch = [pltpu.VMEM((nb, HEADS_PER_GROUP * tk, MXU_DIM), BF16),
               pltpu.VMEM((nb, MXU_DIM, HEADS_PER_GROUP * tk), BF16),
               pltpu.VMEM((MXU_DIM, tq), F32)]
    return pl.pallas_call(
        functools.partial(_flash_kernel, nsub=nsub, tk=tk, nsrc=len(ksrcs), nchain=nchain),
        grid=(bsz, groups, nq // tq),
        in_specs=in_specs,
        out_specs=pl.BlockSpec((1, tq, MXU_DIM), lambda b, g, i: (b, i, g)),
        out_shape=jax.ShapeDtypeStruct((bsz, nq, groups * MXU_DIM), BF16),
        scratch_shapes=scratch,
        compiler_params=_params(("parallel", "parallel", "arbitrary")),
        name="flash_attn",
    )(*args)


HALO = 16


def _ffn_kernel(xp_ref, x_ref, xn_ref, oap_ref, oa_ref, oan_ref, obp_ref, ob_ref, obn_ref, wo_ref, gtm_ref,
                g_ref, sh_ref, sc_ref, gt_ref, wup_ref, cw_ref, wdn_ref, fg_ref,
                o_ref, h_ref, om_ref, *, final_norm, nf):
    i = pl.program_id(1)
    last = pl.num_programs(1) - 1
    tm = x_ref.shape[1]
    fdim = wdn_ref.shape[0]
    tf = fdim // nf
    half = oa_ref.shape[2]
    rows = tm + 2 * HALO
    own = slice(HALO, HALO + tm)
    for r, (a_ref, b_ref) in ((slice(0, HALO), (oap_ref, obp_ref)), (own, (oa_ref, ob_ref)),
                              (slice(HALO + tm, rows), (oan_ref, obn_ref))):
        om_ref[r, :half] = a_ref[0]
        om_ref[r, half:] = b_ref[0]
    ym = gtm_ref[0] * jnp.dot(om_ref[...], wo_ref[...], preferred_element_type=F32)
    x1 = x_ref[0] + ym[own]
    gain, shift, scale = g_ref[...], sh_ref[0], sc_ref[0]
    keep_p = jnp.where(i > 0, 1.0, 0.0)
    keep_n = jnp.where(i < last, 1.0, 0.0)
    h_ref[0:HALO, :] = (_norm_mod(xp_ref[0] + ym[0:HALO], gain, shift, scale) * keep_p).astype(BF16)
    h_ref[own, :] = _norm_mod(x1, gain, shift, scale).astype(BF16)
    h_ref[HALO + tm:, :] = (_norm_mod(xn_ref[0] + ym[HALO + tm:], gain, shift, scale) * keep_n).astype(BF16)

    def conv(u, f, part):
        w = cw_ref[:, cols(part, f)]
        up = pltpu.roll(u, 1, axis=0)[HALO:HALO + tm]
        un = pltpu.roll(u, rows - 1, axis=0)[HALO:HALO + tm]
        return up * w[0:1] + u[HALO:HALO + tm] * w[1:2] + un * w[2:3] + w[3:4]

    def cols(part, f):
        return slice(part * fdim + f * tf, part * fdim + (f + 1) * tf)

    h = h_ref[...]
    up = lambda f: (jnp.dot(h, wup_ref[:, cols(0, f)], preferred_element_type=F32),
                    jnp.dot(h, wup_ref[:, cols(1, f)], preferred_element_type=F32))
    y = None
    ug, uv = up(0)
    for f in range(nf):
        nxt = up(f + 1) if f + 1 < nf else None
        g = conv(ug, f, 0)
        v = conv(uv, f, 1)
        a = (g * (1.0 / (1.0 + jnp.exp(-g))) * v).astype(BF16)
        yf = jnp.dot(a, wdn_ref[f * tf:(f + 1) * tf, :], preferred_element_type=F32)
        y = yf if y is None else y + yf
        if nxt is not None:
            ug, uv = nxt
    out = x1 + gt_ref[0] * y
    if final_norm:
        out = out * lax.rsqrt(jnp.mean(out * out, axis=-1, keepdims=True) + NORM_EPS) * fg_ref[...]
    o_ref[0] = out


def _mix_ffn(x, oa, ob, w_out, gain, mods3, mrow, wup, cw, wdn, final_gain, tm):
    bsz, n, d = x.shape
    half = oa.shape[2]
    tm = min(tm, n)
    hb = tm // HALO
    nh = n // HALO
    final_norm = final_gain is not None
    fg = final_gain if final_norm else gain
    prev = lambda w: pl.BlockSpec((1, HALO, w), lambda b, i: (b, jnp.maximum(i * hb - 1, 0), 0))
    main = lambda w: pl.BlockSpec((1, tm, w), lambda b, i: (b, i, 0))
    nxt = lambda w: pl.BlockSpec((1, HALO, w), lambda b, i: (b, jnp.minimum((i + 1) * hb, nh - 1), 0))
    return pl.pallas_call(
        functools.partial(_ffn_kernel, final_norm=final_norm, nf=FFN_CHUNKS),
        grid=(bsz, n // tm),
        in_specs=[prev(d), main(d), nxt(d), prev(half), main(half), nxt(half), prev(half), main(half), nxt(half),
                  _resident(w_out.shape, lambda b, i: (0, 0)), _mod_spec(d, 2, mrow),
                  pl.BlockSpec((1, d), lambda b, i: (0, 0)),
                  _mod_spec(d, 3, mrow), _mod_spec(d, 4, mrow), _mod_spec(d, 5, mrow),
                  _resident(wup.shape, lambda b, i: (0, 0)),
                  _resident(cw.shape, lambda b, i: (0, 0)),
                  _resident(wdn.shape, lambda b, i: (0, 0)),
                  pl.BlockSpec((1, d), lambda b, i: (0, 0))],
        out_specs=main(d),
        out_shape=jax.ShapeDtypeStruct((bsz, n, d), F32),
        scratch_shapes=[pltpu.VMEM((tm + 2 * HALO, d), BF16), pltpu.VMEM((tm + 2 * HALO, 2 * half), BF16)],
        compiler_params=_params(("parallel", "parallel")),
        name="mix_ffn",
    )(x, x, x, oa, oa, oa, ob, ob, ob, w_out, mods3, gain.reshape(1, d), mods3, mods3, mods3, wup, cw, wdn,
      fg.reshape(1, d))


OD_Q, OD_C, OD_K, OD_V, OD_END = 0, 512, 2048, 2176, 2304
SUB = 8


def _prep_odd_kernel(xp_ref, x_ref, xn_ref, g_ref, sh_ref, sc_ref, win_ref, cw_ref, chd_ref, shd_ref,
                     q_ref, k_ref, v_ref, x0_ref, u_ref, h_ref, z_ref, *, scale):
    i = pl.program_id(1)
    last = pl.num_programs(1) - 1
    tm = x_ref.shape[1]
    rows = tm + 2 * HALO
    own = slice(HALO, HALO + tm)
    cwid = (OD_K - OD_C) // 3
    gain, shift, mscale = g_ref[...], sh_ref[0], sc_ref[0]
    keep_p = jnp.where(i > 0, 1.0, 0.0)
    keep_n = jnp.where(i < last, 1.0, 0.0)
    h_ref[0:HALO, :] = (_norm_mod(xp_ref[0], gain, shift, mscale) * keep_p).astype(BF16)
    h_ref[own, :] = _norm_mod(x_ref[0], gain, shift, mscale).astype(BF16)
    h_ref[HALO + tm:, :] = (_norm_mod(xn_ref[0], gain, shift, mscale) * keep_n).astype(BF16)
    z_ref[...] = jnp.dot(h_ref[...], win_ref[...], preferred_element_type=F32)
    for c in range((OD_C - OD_Q) // LANES):
        sl = slice(OD_Q + c * LANES, OD_Q + (c + 1) * LANES)
        y = _rope_lanes(z_ref[own, sl], chd_ref[...], shd_ref[...])
        q_ref[0, sl, :] = (y * scale).T.astype(BF16)
    k = _rope_lanes(z_ref[own, OD_K:OD_V], chd_ref[...], shd_ref[...])
    d0, d1 = _dup_heads(k)
    for c, dk in enumerate((d0, d0, d1, d1)):
        k_ref[0, :, c * LANES:(c + 1) * LANES] = dk.astype(BF16)
    d0, d1 = (d.T.astype(BF16) for d in _dup_heads(z_ref[own, OD_V:OD_END]))
    for c, dv in enumerate((d0, d0, d1, d1)):
        v_ref[0, c * LANES:(c + 1) * LANES, :] = dv

    def conv(part):
        sl = slice(OD_C + part * cwid, OD_C + (part + 1) * cwid)
        csl = slice(part * cwid, (part + 1) * cwid)
        ue = z_ref[:, sl]
        up = pltpu.roll(ue, 1, axis=0)[own]
        un = pltpu.roll(ue, rows - 1, axis=0)[own]
        return up * cw_ref[0:1, csl] + ue[own] * cw_ref[1:2, csl] + un * cw_ref[2:3, csl] + cw_ref[3:4, csl]

    x0_ref[0] = conv(0).astype(BF16)
    u_ref[0] = (conv(2) * conv(1)).astype(BF16)


def _prep_odd(x, gain, mods3, w_in, cw, chd, shd, tm):
    bsz, n, d = x.shape
    cols = w_in.shape[1]
    tm = min(tm, n)
    hb = tm // HALO
    nh = n // HALO
    cwid = (OD_K - OD_C) // 3
    row_blk = lambda w: pl.BlockSpec((1, tm, w), lambda b, i: (b, i, 0))
    col_blk = lambda w: pl.BlockSpec((1, w, tm), lambda b, i: (b, 0, i))
    slab = 4 * LANES
    return pl.pallas_call(
        functools.partial(_prep_odd_kernel, scale=HEAD_DIM ** -0.5 * LOG2E),
        grid=(bsz, n // tm),
        in_specs=[pl.BlockSpec((1, HALO, d), lambda b, i: (b, jnp.maximum(i * hb - 1, 0), 0)),
                  pl.BlockSpec((1, tm, d), lambda b, i: (b, i, 0)),
                  pl.BlockSpec((1, HALO, d), lambda b, i: (b, jnp.minimum((i + 1) * hb, nh - 1), 0)),
                  pl.BlockSpec((1, d), lambda b, i: (0, 0)),
                  _mod_spec(d, 0, None), _mod_spec(d, 1, None),
                  _resident(w_in.shape, lambda b, i: (0, 0)),
                  pl.BlockSpec(cw.shape, lambda b, i: (0, 0)),
                  pl.BlockSpec((tm, LANES), lambda b, i: (i, 0)),
                  pl.BlockSpec((tm, LANES), lambda b, i: (i, 0))],
        out_specs=[col_blk(slab), row_blk(slab), col_blk(slab), row_blk(cwid), row_blk(cwid)],
        out_shape=[jax.ShapeDtypeStruct((bsz, slab, n), BF16), jax.ShapeDtypeStruct((bsz, n, slab), BF16),
                   jax.ShapeDtypeStruct((bsz, slab, n), BF16), jax.ShapeDtypeStruct((bsz, n, cwid), BF16),
                   jax.ShapeDtypeStruct((bsz, n, cwid), BF16)],
        scratch_shapes=[pltpu.VMEM((tm + 2 * HALO, d), BF16), pltpu.VMEM((tm + 2 * HALO, cols), F32)],
        compiler_params=_params(("parallel", "parallel")),
        name="prep_odd",
    )(x, x, x, gain.reshape(1, d), mods3, mods3, w_in, cw, chd, shd)


def _kv_ctx_odd_kernel(z_ref, k_ref, vt_ref):
    d0, d1 = _dup_heads(z_ref[0, :, 0:LANES])
    for c, dk in enumerate((d0, d0, d1, d1)):
        k_ref[0, :, c * LANES:(c + 1) * LANES] = dk.astype(BF16)
    d0, d1 = (d.T.astype(BF16) for d in _dup_heads(z_ref[0, :, LANES:2 * LANES]))
    for c, dv in enumerate((d0, d0, d1, d1)):
        vt_ref[0, c * LANES:(c + 1) * LANES, :] = dv


def _kv_ctx_odd(z):
    bsz, n, cols = z.shape
    slab = 4 * LANES
    return pl.pallas_call(
        _kv_ctx_odd_kernel,
        grid=(bsz,),
        in_specs=[pl.BlockSpec((1, n, cols), lambda b: (b, 0, 0))],
        out_specs=[pl.BlockSpec((1, n, slab), lambda b: (b, 0, 0)), pl.BlockSpec((1, slab, n), lambda b: (b, 0, 0))],
        out_shape=[jax.ShapeDtypeStruct((bsz, n, slab), BF16), jax.ShapeDtypeStruct((bsz, slab, n), BF16)],
        compiler_params=_params(("parallel",)),
        name="kv_ctx_odd",
    )(z)


WIN_QB = 2
WIN_TILES = 16


def _window_kernel(qt_ref, kc_ref, vct_ref, kl_ref, vlt_ref, bias_ref, sink_ref, o_ref, kbc, vbct, kbl, vblt,
                   *, lc, nblk, qb, nch):
    qi = pl.program_id(2)
    tq = qb * Q_BLOCK
    nsp = qb + 2
    blk = HEADS_PER_GROUP * Q_BLOCK

    @pl.when(qi == 0)
    def _build():
        kms = _head_lane_masks(lc, 1)
        vms = _head_row_masks(lc)
        for j in range(HEADS_PER_GROUP):
            kbc[j * lc:(j + 1) * lc, :] = kc_ref[0] * kms[j]
            vbct[:, j * lc:(j + 1) * lc] = vct_ref[0] * vms[j]
        kms = _head_lane_masks(Q_BLOCK, 1)
        vms = _head_row_masks(Q_BLOCK)
        for e in (0, nblk + 1):
            kbl[e] = jnp.zeros((blk, MXU_DIM), BF16)
            vblt[e] = jnp.zeros((MXU_DIM, blk), BF16)

        def body(i, carry):
            r0 = pl.multiple_of(i * Q_BLOCK, Q_BLOCK)
            for j in range(HEADS_PER_GROUP):
                kbl[i + 1, j * Q_BLOCK:(j + 1) * Q_BLOCK, :] = kl_ref[0, pl.ds(r0, Q_BLOCK), :] * kms[j]
            return carry

        lax.fori_loop(0, nblk, body, 0)
        for i in range(nblk):
            for j in range(HEADS_PER_GROUP):
                vblt[i + 1, :, j * Q_BLOCK:(j + 1) * Q_BLOCK] = vlt_ref[0, :, i * Q_BLOCK:(i + 1) * Q_BLOCK] * vms[j]

    ntile = nblk // qb
    for h in range(nch):
        t = qi * nch + h
        qt = qt_ref[0, :, h * tq:(h + 1) * tq]
        s_c = jnp.dot(kbc[...], qt, preferred_element_type=F32)
        kspan = kbl[pl.ds(qb * t, nsp)].reshape(nsp * blk, MXU_DIM)
        s_s = jnp.dot(kspan, qt, preferred_element_type=F32) + bias_ref[...]
        edge = {0: jnp.where(t == 0, NEG_INF, 0.0), nsp - 1: jnp.where(t == ntile - 1, NEG_INF, 0.0)}
        pcs, pss, invs = [], [[None] * HEADS_PER_GROUP for _ in range(nsp)], []
        for j in range(HEADS_PER_GROUP):
            sink = sink_ref[0, j][0:1, :]
            segs = [s_c[j * lc:(j + 1) * lc, :]]
            for b in range(nsp):
                sg = s_s[b * blk + j * Q_BLOCK: b * blk + (j + 1) * Q_BLOCK, :]
                segs.append(sg + edge[b] if b in edge else sg)
            m = sink
            for sg in segs:
                m = jnp.maximum(m, jnp.max(sg, axis=0, keepdims=True))
            ps = [jnp.exp2(sg - m) for sg in segs]
            den = jnp.exp2(sink - m)
            for p in ps:
                den = den + jnp.sum(p, axis=0, keepdims=True)
            invs.append(1.0 / den)
            pcs.append(ps[0].astype(BF16))
            for b in range(nsp):
                pss[b][j] = ps[1 + b].astype(BF16)
        ot = jnp.dot(vbct[...], jnp.concatenate(pcs, axis=0), preferred_element_type=F32)
        for b in range(nsp):
            ot = ot + jnp.dot(vblt[qb * t + b], jnp.concatenate(pss[b], axis=0), preferred_element_type=F32)
        o_ref[0, h * tq:(h + 1) * tq, :] = (ot * _rows4(invs, tq)).T.astype(BF16)


@functools.lru_cache(maxsize=None)
def _window_bias(qb):
    nsp = qb + 2
    row = np.arange(nsp * HEADS_PER_GROUP * Q_BLOCK)[:, None]
    r = np.arange(qb * Q_BLOCK)[None, :]
    b = row // (HEADS_PER_GROUP * Q_BLOCK)
    jj = b * Q_BLOCK + row % Q_BLOCK
    band = (jj >= r) & (jj <= r + 2 * WINDOW)
    return np.where(band, 0.0, NEG_INF).astype(np.float32)


def _window(qt, kc, vct, kl, vlt, sink):
    bsz, qw, n = qt.shape
    groups = qw // MXU_DIM
    lc = kc.shape[1]
    nblk = n // Q_BLOCK
    qb = min(WIN_QB, nblk)
    tq = qb * Q_BLOCK
    nch = next(c for c in (WIN_TILES, 2, 1) if n % (c * tq) == 0)
    nq = n // (tq * nch)
    blk = HEADS_PER_GROUP * Q_BLOCK
    bias = jnp.asarray(_window_bias(qb))
    sink_t = jnp.broadcast_to((sink.astype(F32) * LOG2E).reshape(groups, HEADS_PER_GROUP, 1, 1),
                              (groups, HEADS_PER_GROUP, SUB, tq))

    return pl.pallas_call(
        functools.partial(_window_kernel, lc=lc, nblk=nblk, qb=qb, nch=nch),
        grid=(bsz, groups, nq),
        in_specs=[pl.BlockSpec((1, MXU_DIM, tq * nch), lambda b, g, i: (b, g, i)),
                  pl.BlockSpec((1, lc, MXU_DIM), lambda b, g, i: (b, 0, g)),
                  pl.BlockSpec((1, MXU_DIM, lc), lambda b, g, i: (b, g, 0)),
                  pl.BlockSpec((1, n, MXU_DIM), lambda b, g, i: (b, 0, g)),
                  pl.BlockSpec((1, MXU_DIM, n), lambda b, g, i: (b, g, 0)),
                  _resident(bias.shape, lambda b, g, i: (0, 0)),
                  pl.BlockSpec((1, HEADS_PER_GROUP, SUB, tq), lambda b, g, i: (g, 0, 0, 0))],
        out_specs=pl.BlockSpec((1, tq * nch, MXU_DIM), lambda b, g, i: (b, i, g)),
        out_shape=jax.ShapeDtypeStruct((bsz, n, qw), BF16),
        scratch_shapes=[pltpu.VMEM((HEADS_PER_GROUP * lc, MXU_DIM), BF16),
                        pltpu.VMEM((MXU_DIM, HEADS_PER_GROUP * lc), BF16),
                        pltpu.VMEM((nblk + 2, blk, MXU_DIM), BF16),
                        pltpu.VMEM((nblk + 2, MXU_DIM, blk), BF16)],
        compiler_params=_params(("parallel", "parallel", "arbitrary")),
        name="window_attn",
    )(qt, kc, vct, kl, vlt, bias, sink_t)


FFT_N2 = 128


def _filter_kernel(z_ref, w1_ref, b1_ref, w2_ref, b2_ref, w3_ref, b3_ref, w4_ref, fr_ref, dl_ref, o_ref):
    def lin(a, w_ref):
        a_hi, a_lo = _split_bf16(a)
        w_hi, w_lo = _split_bf16(w_ref[...])
        return _dot3(a_hi, a_lo, w_hi, w_lo)

    z = z_ref[...]
    h = b1_ref[...]
    for e in range(C_EMB_DIM):
        h = h + z[:, e:e + 1] * w1_ref[e:e + 1, :]
    h = jnp.sin(fr_ref[0:1] * h)
    h = jnp.sin(fr_ref[1:2] * (lin(h, w2_ref) + b2_ref[...]))
    h = jnp.sin(fr_ref[2:3] * (lin(h, w3_ref) + b3_ref[...]))
    h = lin(h, w4_ref)
    t = z[:, 0:1]
    cw = dl_ref.shape[1]
    dec = jnp.exp(-t * dl_ref[...])
    row = lax.broadcasted_iota(jnp.int32, dec.shape, 0) + pl.program_id(0) * z.shape[0]
    o_ref[0] = h[:, :cw] * dec
    o_ref[1] = jnp.where(row == 0, 0.0, h[:, cw:] * dec)


def _filters(zfeat, w1, b1, w2, b2, w3, b3, w4, freq, deltas, tm):
    n = zfeat.shape[0]
    tm = min(tm, n)
    cw = deltas.shape[1]
    full = lambda a: pl.BlockSpec(a.shape, lambda i: (0,) * a.ndim)
    ops = (w1, b1, w2, b2, w3, b3, w4, freq, deltas)
    return pl.pallas_call(
        _filter_kernel,
        grid=(n // tm,),
        in_specs=[pl.BlockSpec((tm, zfeat.shape[1]), lambda i: (i, 0))] + [full(a) for a in ops],
        out_specs=pl.BlockSpec((2, tm, cw), lambda i: (0, i, 0)),
        out_shape=jax.ShapeDtypeStruct((2, n, cw), F32),
        compiler_params=_params(("parallel",)),
        name="hyena_filters",
    )(zfeat, *ops)


def _dft_first_kernel(f_ref, x_ref, o_ref):
    _, n1h, tm2, cw = x_ref.shape
    x = x_ref[0].reshape(n1h * tm2, cw).astype(BF16)
    r = jnp.dot(f_ref[...], x, preferred_element_type=F32)
    o_ref[0] = r.astype(BF16).reshape(o_ref.shape[1:])


def _dft_first(f1, x):
    bsz, n1h, n2, cw = x.shape
    tm2 = HALO
    nkp = f1.shape[0] // (2 * tm2)
    return pl.pallas_call(
        _dft_first_kernel,
        grid=(bsz, n2 // tm2),
        in_specs=[pl.BlockSpec(f1.shape, lambda b, j: (0, 0)),
                  pl.BlockSpec((1, n1h, tm2, cw), lambda b, j: (b, 0, j, 0))],
        out_specs=pl.BlockSpec((1, 2, nkp, tm2, cw), lambda b, j: (b, 0, 0, j, 0)),
        out_shape=jax.ShapeDtypeStruct((bsz, 2, nkp, n2, cw), BF16),
        compiler_params=_params(("parallel", "parallel")),
        name="dft_first",
    )(f1, x)


def _dft_last_gate_kernel(f_ref, b_ref, x0_ref, u_ref, bias_ref, o_ref):
    _, _, nkp, tm2, cw = b_ref.shape
    y = jnp.dot(f_ref[...], b_ref[0].reshape(2 * nkp * tm2, cw), preferred_element_type=F32)
    y = y.reshape(x0_ref.shape[1:])
    o_ref[0] = (x0_ref[0].astype(F32) * (y + u_ref[0].astype(F32) * bias_ref[...])).astype(BF16)


def _dft_last_gate(f2, b, x0, u, bias):
    bsz, _, nkp, n2, cw = b.shape
    tm2 = HALO
    n1h = f2.shape[0] // tm2
    tblk = pl.BlockSpec((1, n1h, tm2, cw), lambda b_, j: (b_, 0, j, 0))
    return pl.pallas_call(
        _dft_last_gate_kernel,
        grid=(bsz, n2 // tm2),
        in_specs=[pl.BlockSpec(f2.shape, lambda b_, j: (0, 0)),
                  pl.BlockSpec((1, 2, nkp, tm2, cw), lambda b_, j: (b_, 0, 0, j, 0)),
                  tblk, tblk, pl.BlockSpec((1, cw), lambda b_, j: (0, 0))],
        out_specs=tblk,
        out_shape=jax.ShapeDtypeStruct((bsz, n1h, n2, cw), BF16),
        compiler_params=_params(("parallel", "parallel")),
        name="dft_last_gate",
    )(f2, b, x0, u, bias.reshape(1, cw))


def _spec_fwd(a, m_ref):
    return jnp.dot(m_ref[0], a.reshape(2 * FFT_N2, a.shape[-1]), preferred_element_type=F32)


def _filter_spec_kernel(a_ref, m_ref, o_ref):
    for kk in range(a_ref.shape[2]):
        xf = _spec_fwd(a_ref[0, :, kk], m_ref.at[kk:kk + 1])
        xb = _spec_fwd(a_ref[1, :, kk], m_ref.at[kk:kk + 1])
        o_ref[0, kk] = xf[:FFT_N2] + xb[:FFT_N2]
        o_ref[1, kk] = xf[FFT_N2:] - xb[FFT_N2:]


def _conv_spec_kernel(a_ref, k_ref, m_ref, i_ref, o_ref):
    kr, ki = k_ref[0, 0], k_ref[1, 0]
    for b in range(a_ref.shape[0]):
        x = _spec_fwd(a_ref[b, :, 0], m_ref)
        xr, xi = x[:FFT_N2], x[FFT_N2:]
        y = jnp.concatenate([xr * kr - xi * ki, xr * ki + xi * kr], axis=0).astype(BF16)
        bv = jnp.dot(i_ref[0], y, preferred_element_type=F32)
        o_ref[b, :, 0] = bv.astype(BF16).reshape(2, FFT_N2, bv.shape[-1])


def _filter_spec(a, m_fwd):
    _, _, nkp, _, cw = a.shape
    return pl.pallas_call(
        _filter_spec_kernel,
        grid=(nkp // SUB,),
        in_specs=[pl.BlockSpec((2, 2, SUB, FFT_N2, cw), lambda k: (0, 0, k, 0, 0)),
                  pl.BlockSpec((SUB, 2 * FFT_N2, 2 * FFT_N2), lambda k: (k, 0, 0))],
        out_specs=pl.BlockSpec((2, SUB, FFT_N2, cw), lambda k: (0, k, 0, 0)),
        out_shape=jax.ShapeDtypeStruct((2, nkp, FFT_N2, cw), F32),
        compiler_params=_params(("parallel",)),
        name="filter_spectrum",
    )(a, m_fwd)


def _conv_spec(a, kspec, m_fwd, m_inv):
    bsz, _, nkp, _, cw = a.shape
    blk = pl.BlockSpec((bsz, 2, 1, FFT_N2, cw), lambda k: (0, 0, k, 0, 0))
    mat = pl.BlockSpec((1, 2 * FFT_N2, 2 * FFT_N2), lambda k: (k, 0, 0))
    return pl.pallas_call(
        _conv_spec_kernel,
        grid=(nkp,),
        in_specs=[blk, pl.BlockSpec((2, 1, FFT_N2, cw), lambda k: (0, k, 0, 0)), mat, mat],
        out_specs=blk,
        out_shape=jax.ShapeDtypeStruct(a.shape, BF16),
        compiler_params=_params(("parallel",)),
        name="conv_spectrum",
    )(a, kspec, m_fwd, m_inv)


@functools.lru_cache(maxsize=None)
def _dft_tables(n):
    nfft = 2 * n
    n2 = FFT_N2
    n1 = nfft // n2
    n1h = n1 // 2
    nk = n1h + 1
    nkp = -(-nk // SUB) * SUB

    def cs(num, den):
        ang = (num % den).astype(np.float64) * (2.0 * math.pi / den)
        return np.cos(ang), np.sin(ang)

    def pad_k(t, axis):
        widths = [(0, 0)] * t.ndim
        widths[axis] = (0, nkp - nk)
        return np.pad(t, widths)

    k1 = np.arange(nk, dtype=np.int64)
    c1, s1 = cs(k1[:, None] * np.arange(n1h)[None, :], n1)
    f1 = np.concatenate([pad_k(c1, 0), pad_k(-s1, 0)], axis=0)
    kk = k1[:, None, None] + n1 * np.arange(n2)[None, :, None]
    cg, sg = cs(kk * np.arange(n2)[None, None, :], nfft)
    g_re, g_im = cg, -sg
    m_fwd = np.concatenate([np.concatenate([g_re, -g_im], axis=2),
                            np.concatenate([g_im, g_re], axis=2)], axis=1)
    gt_re, gt_im = np.swapaxes(g_re, 1, 2), np.swapaxes(g_im, 1, 2)
    m_inv = np.concatenate([np.concatenate([gt_re, gt_im], axis=2),
                            np.concatenate([-gt_im, gt_re], axis=2)], axis=1)
    wk = np.where((k1 == 0) | (k1 == n1h), 1.0, 2.0)[None, :] * (1.0 / nfft)
    c2, s2 = cs(np.arange(n1h)[:, None] * k1[None, :], n1)
    f2 = np.concatenate([pad_k(c2 * wk, 1), pad_k(-s2 * wk, 1)], axis=1)
    eye = np.eye(HALO)
    return tuple(t.astype(np.float32) for t in (np.kron(f1, eye), pad_k(m_fwd, 0), pad_k(m_inv, 0), np.kron(f2, eye)))


def _hyena_long_conv(u, x0, bias, hcat, tabs):
    f1, m_fwd, m_inv, f2 = tabs
    bsz, n, cw = u.shape
    n2 = FFT_N2
    n1h = n // n2
    kspec = _filter_spec(_dft_first(f1, hcat.reshape(2, n1h, n2, cw)), m_fwd)
    u4 = u.reshape(bsz, n1h, n2, cw)
    b_u = _conv_spec(_dft_first(f1, u4), kspec, m_fwd, m_inv)
    return _dft_last_gate(f2, b_u, x0.reshape(bsz, n1h, n2, cw), u4, bias).reshape(bsz, n, cw)


def _axial_angles(rows, rope_dim):
    row_idx = np.repeat(np.arange(rows), GRID_W).astype(np.float64)
    col_idx = np.tile(np.arange(GRID_W), rows).astype(np.float64)
    d_axis = rope_dim // 2
    inv_freq = ROPE_THETA ** (-np.arange(0, d_axis, 2, dtype=np.float64) / d_axis)
    ang = np.concatenate([row_idx[:, None] * inv_freq, col_idx[:, None] * inv_freq], axis=-1)
    return np.cos(ang), np.sin(ang)


@functools.lru_cache(maxsize=None)
def _rope_tables(n):
    rows = n // GRID_W
    sign = np.tile(np.array([-1.0, 1.0]), LANES // 2)[None, :]
    cos, sin = _axial_angles(rows, HEAD_DIM)
    chd = np.tile(np.repeat(cos, 2, axis=1), (1, LANES // HEAD_DIM))
    shd = np.tile(np.repeat(sin, 2, axis=1), (1, LANES // HEAD_DIM)) * sign
    cos, sin = _axial_angles(rows, B_ROPE)
    ones = np.ones((n, LANES - 2 * B_ROPE))
    cb = np.concatenate([np.tile(np.repeat(cos, 2, axis=1), (1, 2)), ones], axis=1)
    sb = np.concatenate([np.tile(np.repeat(sin, 2, axis=1), (1, 2)), 0.0 * ones], axis=1) * sign
    return tuple(t.astype(np.float32) for t in (chd, shd, cb, sb))


@functools.lru_cache(maxsize=None)
def _hyena_features(n, cwid):
    t = np.linspace(0.0, 1.0, n)[:, None]
    wpos = 2 * math.pi * np.arange(n)[:, None] / n
    fb = np.linspace(1e-4, C_BANDS - 1, C_BANDS)[None, :]
    zfeat = np.concatenate([t, np.cos(fb * wpos), -np.sin(fb * wpos), np.zeros((n, SUB - C_EMB_DIM))], axis=-1)
    deltas = np.abs(np.linspace(C_MIN_DECAY, C_MAX_DECAY, cwid))[None, :]
    return zfeat.astype(np.float32), deltas.astype(np.float32)


def _even_weights(w_in, a_qn, a_kn, b_qn, b_w_uq, b_kvn, b_w_ukv):
    d = w_in.shape[0]
    aq_w = w_in.shape[1] - (B_Q_RANK + 2 * A_KV_HEADS * HEAD_DIM + B_KV_RANK + B_ROPE)
    o = [0, aq_w, aq_w + B_Q_RANK]
    o += [o[-1] + A_KV_HEADS * HEAD_DIM, o[-1] + 2 * A_KV_HEADS * HEAD_DIM]
    o += [o[-1] + B_KV_RANK, o[-1] + B_KV_RANK + B_ROPE]
    kr = w_in[:, o[5]:o[6]]
    w_aug = jnp.concatenate([w_in[:, :o[5]], kr, kr, jnp.zeros((d, LANES - 2 * B_ROPE), w_in.dtype)], axis=1)
    heads = b_w_uq.shape[1] // (B_NOPE + B_ROPE)
    uq = b_w_uq.reshape(B_Q_RANK, heads // 2, 2, B_NOPE + B_ROPE)
    wuq = jnp.concatenate([uq[:, :, 0, :B_NOPE], uq[:, :, 1, :B_NOPE], uq[:, :, 0, B_NOPE:], uq[:, :, 1, B_NOPE:],
                           jnp.zeros((B_Q_RANK, heads // 2, MXU_DIM - 2 * (B_NOPE + B_ROPE)), b_w_uq.dtype)], axis=2)
    wuq = wuq.reshape(B_Q_RANK, heads // 2 * MXU_DIM)
    ukv = b_w_ukv.reshape(B_KV_RANK, heads, B_NOPE + B_VDIM)
    wukn = ukv[:, :, :B_NOPE].reshape(B_KV_RANK, heads * B_NOPE)
    wuv = ukv[:, :, B_NOPE:].reshape(B_KV_RANK, heads * B_VDIM)
    gains = (jnp.tile(a_qn, aq_w // HEAD_DIM)[None, :], jnp.tile(a_kn, A_KV_HEADS)[None, :],
             b_qn[None, :], b_kvn[None, :])
    return w_aug.astype(BF16), gains + (wuq.astype(BF16), wukn.astype(BF16), wuv.astype(BF16))


def _ffn_weights(w_up, conv_w, conv_b, w_down):
    cw = jnp.concatenate([conv_w, conv_b[None, :], jnp.zeros((SUB - 4, conv_w.shape[1]), conv_w.dtype)], axis=0)
    return w_up.astype(BF16), cw, w_down.astype(BF16)


FFN_CHUNKS = 1


def kernel(x, c, ctx, c_ctx, w_mod, b_mod, norm_mix, norm_ffn, ev_w_in, ev_w_out, a_q_norm, a_k_norm, b_q_norm, b_w_uq, b_kv_norm, b_w_ukv, od_w_in, od_w_out, d_sink, c_conv_w, c_conv_b, c_filt_w1, c_filt_b1, c_filt_w2, c_filt_b2, c_filt_w3, c_filt_b3, c_filt_w4, c_filt_freq, c_bias, ffn_w_up, ffn_conv_w, ffn_conv_b, ffn_w_down, final_norm):
    bsz, n, d = x.shape
    depth = w_mod.shape[0]
    assert depth == 2 and n % Q_BLOCK == 0 and d % LANES == 0
    tm = ROW_TILE
    rows = -(-(bsz + 1) // SUB) * SUB
    cvec = jnp.concatenate([c, c_ctx[None, :], jnp.zeros((rows - bsz - 1, d), F32)], axis=0)
    mods = _mods(cvec, w_mod, b_mod)
    chd, shd, cb, sb = (jnp.asarray(t) for t in _rope_tables(n))
    tabs = (chd, shd, cb, sb)

    m3 = mods[0].reshape(rows, 1, N_MOD * d)
    w_aug, prep_w = _even_weights(ev_w_in[0], a_q_norm[0], a_k_norm[0], b_q_norm[0], b_w_uq[0],
                                  b_kv_norm[0], b_w_ukv[0])
    qa_l, ka_l, va_l, qb_l, kb_l, vb_l = _prep_even(x, norm_mix[0], m3, None, w_aug, prep_w, tabs, True, tm)
    qa_c, ka_c, va_c, qb_c, kb_c, vb_c = _prep_even(ctx, norm_mix[0], m3, bsz, w_aug, prep_w, tabs, False, tm)
    tq, tk = ATTN_TQ, ATTN_TK
    oa_l = _flash(qa_l, (ka_c, ka_l), (va_c, va_l), 1, tq, tk)
    ob_l = _flash(qb_l, (kb_c, kb_l), (vb_c, vb_l), 2, tq, tk)
    oa_c = _flash(qa_c, (ka_c,), (va_c,), 1, tq, tk)
    ob_c = _flash(qb_c, (kb_c,), (vb_c,), 2, tq, tk)
    w_out = ev_w_out[0].astype(BF16)
    ffn_w = _ffn_weights(ffn_w_up[0], ffn_conv_w[0], ffn_conv_b[0], ffn_w_down[0])
    x = _mix_ffn(x, oa_l, ob_l, w_out, norm_ffn[0], m3, None, *ffn_w, None, tm)
    ctx = _mix_ffn(ctx, oa_c, ob_c, w_out, norm_ffn[0], m3, bsz, *ffn_w, None, tm)

    m3 = mods[1].reshape(rows, 1, N_MOD * d)
    w_in = od_w_in[0].astype(BF16)
    z_c = _inproj(ctx, norm_mix[1], m3, bsz, w_in[:, OD_K:], tm)
    cw = jnp.concatenate([c_conv_w[0], c_conv_b[0][None, :], jnp.zeros((SUB - 4, OD_K - OD_C), F32)], axis=0)
    qd, kd, vd, x0, u = _prep_odd(x, norm_mix[1], m3, w_in, cw, chd, shd, 2 * tm)
    kd_c, vd_c = _kv_ctx_odd(z_c)
    od = _window(qd, kd_c, vd_c, kd, vd, d_sink[0])
    zfeat, deltas = (jnp.asarray(t) for t in _hyena_features(n, c_bias.shape[1]))
    w1 = jnp.concatenate([c_filt_w1[0], jnp.zeros((SUB - C_EMB_DIM, c_filt_w1.shape[2]), F32)], axis=0)
    hcat = _filters(zfeat, w1, c_filt_b1[0][None, :], c_filt_w2[0], c_filt_b2[0][None, :], c_filt_w3[0],
                    c_filt_b3[0][None, :], c_filt_w4[0], jnp.concatenate([c_filt_freq[0], jnp.zeros((SUB - 3, c_filt_freq.shape[2]), F32)], axis=0),
                    deltas, tm)
    oc = _hyena_long_conv(u, x0, c_bias[0], hcat, tuple(jnp.asarray(t).astype(BF16) for t in _dft_tables(n)))
    ffn_w = _ffn_weights(ffn_w_up[1], ffn_conv_w[1], ffn_conv_b[1], ffn_w_down[1])
    return _mix_ffn(x, od, oc, od_w_out[0].astype(BF16), norm_ffn[1], m3, None, *ffn_w, final_norm, tm)
```

```python
import functools
import math

import jax
import jax.numpy as jnp
import numpy as np
from jax import lax
from jax.experimental import pallas as pl
from jax.experimental.pallas import tpu as pltpu

F32 = jnp.float32
BF16 = jnp.bfloat16

GRID_W = 64
HEAD_DIM = 64
ROPE_THETA = 10000.0
NORM_EPS = 1e-6
NEG_INF = -1e30
N_MOD = 6
A_KV_HEADS = 2
B_NOPE = 64
B_ROPE = 32
B_VDIM = 64
B_Q_RANK = 384
B_KV_RANK = 256
C_EMB_DIM = 5
C_BANDS = (C_EMB_DIM - 1) // 2
C_MIN_DECAY = math.log(1e-2) / 1.5
C_MAX_DECAY = math.log(1e-2) / 0.3
WINDOW = 128
Q_BLOCK = 128
LOG2E = math.log2(math.e)

LANES = 128
MXU_DIM = 256
VMEM_LIMIT = 56 * 1024 * 1024
HEADS_PER_GROUP = MXU_DIM // HEAD_DIM

ROW_TILE = 512
ATTN_TQ = 2048
ATTN_TK = 256


def _params(sem, vmem=VMEM_LIMIT):
    return pltpu.CompilerParams(dimension_semantics=sem, vmem_limit_bytes=vmem)


def _resident(shape, index_map):
    return pl.BlockSpec(shape, index_map, pipeline_mode=pl.Buffered(1))


def _norm_mod(x, gain, shift, scale):
    inv = lax.rsqrt(jnp.mean(x * x, axis=-1, keepdims=True) + NORM_EPS)
    return (x * inv) * gain * (1.0 + scale) + shift


def _rope_lanes(x, cos, sin_signed):
    lane = lax.broadcasted_iota(jnp.int32, x.shape, 1)
    nxt = pltpu.roll(x, LANES - 1, axis=1)
    prv = pltpu.roll(x, 1, axis=1)
    swapped = jnp.where(lane % 2 == 0, nxt, prv)
    return x * cos + swapped * sin_signed


def _head_rmsnorm_lanes(x, gain):
    lane = lax.broadcasted_iota(jnp.int32, x.shape, 1)
    lo = lane < HEAD_DIM
    sq = x * x
    s_lo = jnp.sum(jnp.where(lo, sq, 0.0), axis=-1, keepdims=True)
    s_hi = jnp.sum(jnp.where(lo, 0.0, sq), axis=-1, keepdims=True)
    ms = jnp.where(lo, s_lo, s_hi) * (1.0 / HEAD_DIM)
    return x * lax.rsqrt(ms + NORM_EPS) * gain


def _dup_heads(x):
    lane = lax.broadcasted_iota(jnp.int32, x.shape, 1)
    lo = lane < HEAD_DIM
    r = pltpu.roll(x, HEAD_DIM, axis=1)
    return jnp.where(lo, x, r), jnp.where(lo, r, x)


def _split_bf16(x):
    hi = x.astype(BF16)
    lo = (x - hi.astype(F32)).astype(BF16)
    return hi, lo


def _dot3(a_hi, a_lo, b_hi, b_lo):
    d = functools.partial(jnp.dot, preferred_element_type=F32)
    return d(a_hi, b_hi) + d(a_hi, b_lo) + d(a_lo, b_hi)


def _mods_kernel(c_ref, w_ref, b_ref, o_ref):
    c = c_ref[...]
    s = c * (1.0 / (1.0 + jnp.exp(-c)))
    s_hi, s_lo = _split_bf16(s)
    w_hi, w_lo = _split_bf16(w_ref[0])
    o_ref[0] = _dot3(s_hi, s_lo, w_hi, w_lo) + b_ref[0]


def _mods(cvec, w_mod, b_mod):
    depth, d, n = w_mod.shape
    rows = cvec.shape[0]
    tn = 1536
    return pl.pallas_call(
        _mods_kernel,
        grid=(depth, n // tn),
        in_specs=[pl.BlockSpec((rows, d), lambda l, j: (0, 0)),
                  pl.BlockSpec((1, d, tn), lambda l, j: (l, 0, j)),
                  pl.BlockSpec((1, 1, tn), lambda l, j: (l, 0, j))],
        out_specs=pl.BlockSpec((1, rows, tn), lambda l, j: (l, 0, j)),
        out_shape=jax.ShapeDtypeStruct((depth, rows, n), F32),
        compiler_params=_params(("arbitrary", "arbitrary")),
        name="mods",
    )(cvec, w_mod, b_mod.reshape(depth, 1, n))


def _mod_spec(d, chunk, row):
    if row is None:
        return pl.BlockSpec((1, 1, d), lambda b, *_: (b, 0, chunk))
    return pl.BlockSpec((1, 1, d), lambda b, *_: (row, 0, chunk))


def _inproj_kernel(x_ref, g_ref, sh_ref, sc_ref, w_ref, o_ref):
    h = _norm_mod(x_ref[0], g_ref[...], sh_ref[0], sc_ref[0])
    o_ref[0] = jnp.dot(h.astype(BF16), w_ref[...], preferred_element_type=F32)


def _inproj(x, gain, mods3, mrow, w, tm):
    bsz, n, d = x.shape
    cols = w.shape[1]
    tm = min(tm, n)
    return pl.pallas_call(
        _inproj_kernel,
        grid=(bsz, n // tm),
        in_specs=[pl.BlockSpec((1, tm, d), lambda b, i: (b, i, 0)),
                  pl.BlockSpec((1, d), lambda b, i: (0, 0)),
                  _mod_spec(d, 0, mrow), _mod_spec(d, 1, mrow),
                  _resident((d, cols), lambda b, i: (0, 0))],
        out_specs=pl.BlockSpec((1, tm, cols), lambda b, i: (b, i, 0)),
        out_shape=jax.ShapeDtypeStruct((bsz, n, cols), F32),
        compiler_params=_params(("parallel", "parallel")),
        name="inproj",
    )(x, gain.reshape(1, d), mods3, mods3, w)


EV_AQ, EV_BQ, EV_AK, EV_AV, EV_BKV, EV_BKR, EV_END = 0, 512, 896, 1024, 1152, 1408, 1536


def _prep_even_kernel(x_ref, g_ref, sh_ref, sc_ref, win_ref,
                      aqg_ref, akg_ref, bqg_ref, bkvg_ref, wuq_ref, wukn_ref, wuv_ref,
                      chd_ref, shd_ref, cb_ref, sb_ref,
                      qa_ref, ka_ref, va_ref, qb_ref, kb_ref, vb_ref, z_ref, *, use_rope, a_scale, b_scale):
    h = _norm_mod(x_ref[0], g_ref[...], sh_ref[0], sc_ref[0])
    z_ref[0] = jnp.dot(h.astype(BF16), win_ref[...], preferred_element_type=F32)

    def rope_hd(y):
        return _rope_lanes(y, chd_ref[...], shd_ref[...]) if use_rope else y

    def rope_b(y):
        return _rope_lanes(y, cb_ref[...], sb_ref[...]) if use_rope else y

    def put_t(dst, c, y):
        dst[0, c * LANES:(c + 1) * LANES, :] = y.T.astype(BF16)

    for c in range((EV_BQ - EV_AQ) // LANES):
        sl = slice(EV_AQ + c * LANES, EV_AQ + (c + 1) * LANES)
        y = rope_hd(_head_rmsnorm_lanes(z_ref[0, :, sl], aqg_ref[:, c * LANES:(c + 1) * LANES]))
        put_t(qa_ref, c, y * a_scale)
    k = rope_hd(_head_rmsnorm_lanes(z_ref[0, :, EV_AK:EV_AV], akg_ref[...]))
    d0, d1 = _dup_heads(k)
    d0 = d0.astype(BF16)
    d1 = d1.astype(BF16)
    ka_ref[0, :, 0 * LANES:1 * LANES] = d0
    ka_ref[0, :, 1 * LANES:2 * LANES] = d0
    ka_ref[0, :, 2 * LANES:3 * LANES] = d1
    ka_ref[0, :, 3 * LANES:4 * LANES] = d1
    d0, d1 = (d.T.astype(BF16) for d in _dup_heads(z_ref[0, :, EV_AV:EV_BKV]))
    for c, dv in enumerate((d0, d0, d1, d1)):
        va_ref[0, c * LANES:(c + 1) * LANES, :] = dv
    cq = z_ref[0, :, EV_BQ:EV_AK]
    cq = cq * lax.rsqrt(jnp.mean(cq * cq, axis=-1, keepdims=True) + NORM_EPS) * bqg_ref[...]
    qb = jnp.dot(cq.astype(BF16), wuq_ref[...], preferred_element_type=F32)
    for c in range(qb.shape[1] // MXU_DIM):
        lo = slice(c * MXU_DIM, c * MXU_DIM + LANES)
        hi = slice(c * MXU_DIM + LANES, (c + 1) * MXU_DIM)
        put_t(qb_ref, 2 * c, qb[:, lo] * b_scale)
        put_t(qb_ref, 2 * c + 1, rope_b(qb[:, hi]) * b_scale)
    ckv = z_ref[0, :, EV_BKV:EV_BKR]
    ckv = (ckv * lax.rsqrt(jnp.mean(ckv * ckv, axis=-1, keepdims=True) + NORM_EPS) * bkvg_ref[...]).astype(BF16)
    kn = jnp.dot(ckv, wukn_ref[...], preferred_element_type=F32)
    vb = jnp.dot(ckv, wuv_ref[...], preferred_element_type=F32)
    for c in range(vb.shape[1] // LANES):
        put_t(vb_ref, c, vb[:, c * LANES:(c + 1) * LANES])
    kr = rope_b(z_ref[0, :, EV_BKR:EV_END]).astype(BF16)
    for c in range(kn.shape[1] // LANES):
        kb_ref[0, :, c * MXU_DIM:c * MXU_DIM + LANES] = kn[:, c * LANES:(c + 1) * LANES].astype(BF16)
        kb_ref[0, :, c * MXU_DIM + LANES:(c + 1) * MXU_DIM] = kr


def _prep_even(x, gain, mods3, mrow, w_in, wts, tabs, use_rope, tm):
    bsz, n, d = x.shape
    cols = w_in.shape[1]
    tm = min(tm, n)
    aqg, akg, bqg, bkvg, wuq, wukn, wuv = wts
    chd, shd, cb, sb = tabs
    full = lambda a: _resident(a.shape, lambda b, i: (0,) * a.ndim)
    tab = lambda a: pl.BlockSpec((tm, LANES), (lambda b, i: (i, 0)) if use_rope else (lambda b, i: (0, 0)))
    outs = [(4 * LANES, True), (4 * LANES, False), (4 * LANES, True), (wuq.shape[1], True),
            (2 * wukn.shape[1], False), (wuv.shape[1], True)]
    kern = functools.partial(_prep_even_kernel, use_rope=use_rope, a_scale=HEAD_DIM ** -0.5 * LOG2E,
                             b_scale=(B_NOPE + B_ROPE) ** -0.5 * LOG2E)
    return pl.pallas_call(
        kern,
        grid=(bsz, n // tm),
        in_specs=[pl.BlockSpec((1, tm, d), lambda b, i: (b, i, 0)),
                  pl.BlockSpec((1, d), lambda b, i: (0, 0)),
                  _mod_spec(d, 0, mrow), _mod_spec(d, 1, mrow), full(w_in),
                  full(aqg), full(akg), full(bqg), full(bkvg), full(wuq), full(wukn), full(wuv),
                  tab(chd), tab(shd), tab(cb), tab(sb)],
        out_specs=[pl.BlockSpec((1, w, tm), lambda b, i: (b, 0, i)) if t else
                   pl.BlockSpec((1, tm, w), lambda b, i: (b, i, 0)) for w, t in outs],
        out_shape=[jax.ShapeDtypeStruct((bsz, w, n) if t else (bsz, n, w), BF16) for w, t in outs],
        scratch_shapes=[pltpu.VMEM((1, tm, cols), F32)],
        compiler_params=_params(("parallel", "parallel")),
        name="prep_even",
    )(x, gain.reshape(1, d), mods3, mods3, w_in, aqg, akg, bqg, bkvg, wuq, wukn, wuv, chd, shd, cb, sb)


def _head_lane_masks(rows, nsub):
    lane = lax.broadcasted_iota(jnp.int32, (rows, MXU_DIM), 1)
    kms = []
    for j in range(HEADS_PER_GROUP):
        if nsub == 1:
            km = (lane >= j * HEAD_DIM) & (lane < (j + 1) * HEAD_DIM)
        else:
            jj = j % 2
            km = ((lane >= jj * B_NOPE) & (lane < (jj + 1) * B_NOPE)) | \
                 ((lane >= 2 * B_NOPE + jj * B_ROPE) & (lane < 2 * B_NOPE + (jj + 1) * B_ROPE))
        kms.append(jnp.where(km, 1.0, 0.0).astype(BF16))
    return kms


def _head_row_masks(cols):
    row = lax.broadcasted_iota(jnp.int32, (MXU_DIM, cols), 0)
    return [jnp.where((row >= j * HEAD_DIM) & (row < (j + 1) * HEAD_DIM), 1.0, 0.0).astype(BF16)
            for j in range(HEADS_PER_GROUP)]


def _rows4(vals, cols):
    return jnp.concatenate([jnp.broadcast_to(v, (HEAD_DIM, cols)) for v in vals], axis=0)


def _flash_kernel(*refs, nsub, tk, nsrc, nchain):
    qt_ref = refs[0]
    srcs = [(refs[1 + 2 * s], refs[2 + 2 * s]) for s in range(nsrc)]
    o_ref, kb_ref, vbt_ref, acc_ref = refs[1 + 2 * nsrc:]
    hpu = HEADS_PER_GROUP // nsub
    tq = qt_ref.shape[2]
    tqc = tq // nchain
    nb = kb_ref.shape[0]

    @pl.when(pl.program_id(2) == 0)
    def _build():
        kms = _head_lane_masks(tk, nsub)
        vms = _head_row_masks(tk)
        base = 0
        for k_ref, vt_ref in srcs:
            nblk = k_ref.shape[1] // tk

            def body(i, carry, k_ref=k_ref, base=base):
                r0 = pl.multiple_of(i * tk, tk)
                for j in range(HEADS_PER_GROUP):
                    u = j // hpu
                    kb_ref[base + i, j * tk:(j + 1) * tk, :] = \
                        k_ref[0, pl.ds(r0, tk), u * MXU_DIM:(u + 1) * MXU_DIM] * kms[j]
                return carry

            lax.fori_loop(0, nblk, body, 0)
            for i in range(nblk):
                for j in range(HEADS_PER_GROUP):
                    vbt_ref[base + i, :, j * tk:(j + 1) * tk] = vt_ref[0, :, i * tk:(i + 1) * tk] * vms[j]
            base += nblk

    def block(i, state):
        kb, vbt = kb_ref[i], vbt_ref[i]
        new_state = []
        first = state is None
        for h in range(nchain):
            cs = slice(h * tqc, (h + 1) * tqc)
            parts = [jnp.dot(kb[u * hpu * tk:(u + 1) * hpu * tk, :], qt_ref[0, u * MXU_DIM:(u + 1) * MXU_DIM, cs],
                             preferred_element_type=F32) for u in range(nsub)]
            ps, alphas, m_new, l_new = [], [], [], []
            for j in range(HEADS_PER_GROUP):
                sj = parts[j // hpu][(j % hpu) * tk:(j % hpu + 1) * tk, :]
                mj = jnp.max(sj, axis=0, keepdims=True)
                if not first:
                    m, l = state[h]
                    mj = jnp.maximum(m[j], mj)
                    a = jnp.exp2(m[j] - mj)
                    alphas.append(a)
                p = jnp.exp2(sj - mj)
                lj = jnp.sum(p, axis=0, keepdims=True)
                l_new.append(lj if first else a * l[j] + lj)
                m_new.append(mj)
                ps.append(p.astype(BF16))
            pv = jnp.dot(vbt, jnp.concatenate(ps, axis=0), preferred_element_type=F32)
            acc_ref[:, cs] = pv if first else acc_ref[:, cs] * _rows4(alphas, tqc) + pv
            new_state.append((m_new, l_new))
        return new_state

    state = None
    for i in range(nb):
        state = block(i, state)
    for h in range(nchain):
        cs = slice(h * tqc, (h + 1) * tqc)
        ot = acc_ref[:, cs] * _rows4([1.0 / v for v in state[h][1]], tqc)
        o_ref[0, cs, :] = ot.T.astype(BF16)


def _flash(qt, ksrcs, vtsrcs, nsub, tq, tk):
    bsz, qw, nq = qt.shape
    kw = nsub * MXU_DIM
    groups = qw // kw
    tq = min(tq, nq)
    nchain = max(tq // MXU_DIM, 1)
    in_specs = [pl.BlockSpec((1, kw, tq), lambda b, g, i: (b, g, i))]
    args = [qt]
    nb = 0
    for k, vt in zip(ksrcs, vtsrcs):
        lk = k.shape[1]
        assert lk % tk == 0
        nb += lk // tk
        in_specs += [pl.BlockSpec((1, lk, kw), lambda b, g, i: (b, 0, g)),
                     pl.BlockSpec((1, MXU_DIM, lk), lambda b, g, i: (b, g, 0))]
        args += [k, vt]
    scratch = [pltpu.VMEM((nb, HEADS_PER_GROUP * tk, MXU_DIM), BF16),
               pltpu.VMEM((nb, MXU_DIM, HEADS_PER_GROUP * tk), BF16),
               pltpu.VMEM((MXU_DIM, tq), F32)]
    return pl.pallas_call(
        functools.partial(_flash_kernel, nsub=nsub, tk=tk, nsrc=len(ksrcs), nchain=nchain),
        grid=(bsz, groups, nq // tq),
        in_specs=in_specs,
        out_specs=pl.BlockSpec((1, tq, MXU_DIM), lambda b, g, i: (b, i, g)),
        out_shape=jax.ShapeDtypeStruct((bsz, nq, groups * MXU_DIM), BF16),
        scratch_shapes=scratch,
        compiler_params=_params(("parallel", "parallel", "arbitrary")),
        name="flash_attn",
    )(*args)


HALO = 16


def _ffn_kernel(xp_ref, x_ref, xn_ref, oap_ref, oa_ref, oan_ref, obp_ref, ob_ref, obn_ref, wo_ref, gtm_ref,
                g_ref, sh_ref, sc_ref, gt_ref, wup_ref, cw_ref, wdn_ref, fg_ref,
                o_ref, h_ref, om_ref, *, final_norm, nf):
    i = pl.program_id(1)
    last = pl.num_programs(1) - 1
    tm = x_ref.shape[1]
    fdim = wdn_ref.shape[0]
    tf = fdim // nf
    half = oa_ref.shape[2]
    rows = tm + 2 * HALO
    own = slice(HALO, HALO + tm)
    for r, (a_ref, b_ref) in ((slice(0, HALO), (oap_ref, obp_ref)), (own, (oa_ref, ob_ref)),
                              (slice(HALO + tm, rows), (oan_ref, obn_ref))):
        om_ref[r, :half] = a_ref[0]
        om_ref[r, half:] = b_ref[0]
    ym = gtm_ref[0] * jnp.dot(om_ref[...], wo_ref[...], preferred_element_type=F32)
    x1 = x_ref[0] + ym[own]
    gain, shift, scale = g_ref[...], sh_ref[0], sc_ref[0]
    keep_p = jnp.where(i > 0, 1.0, 0.0)
    keep_n = jnp.where(i < last, 1.0, 0.0)
    h_ref[0:HALO, :] = (_norm_mod(xp_ref[0] + ym[0:HALO], gain, shift, scale) * keep_p).astype(BF16)
    h_ref[own, :] = _norm_mod(x1, gain, shift, scale).astype(BF16)
    h_ref[HALO + tm:, :] = (_norm_mod(xn_ref[0] + ym[HALO + tm:], gain, shift, scale) * keep_n).astype(BF16)

    def conv(u, f, part):
        w = cw_ref[:, cols(part, f)]
        up = pltpu.roll(u, 1, axis=0)[HALO:HALO + tm]
        un = pltpu.roll(u, rows - 1, axis=0)[HALO:HALO + tm]
        return up * w[0:1] + u[HALO:HALO + tm] * w[1:2] + un * w[2:3] + w[3:4]

    def cols(part, f):
        return slice(part * fdim + f * tf, part * fdim + (f + 1) * tf)

    h = h_ref[...]
    up = lambda f: (jnp.dot(h, wup_ref[:, cols(0, f)], preferred_element_type=F32),
                    jnp.dot(h, wup_ref[:, cols(1, f)], preferred_element_type=F32))
    y = None
    ug, uv = up(0)
    for f in range(nf):
        nxt = up(f + 1) if f + 1 < nf else None
        g = conv(ug, f, 0)
        v = conv(uv, f, 1)
        a = (g * (1.0 / (1.0 + jnp.exp(-g))) * v).astype(BF16)
        yf = jnp.dot(a, wdn_ref[f * tf:(f + 1) * tf, :], preferred_element_type=F32)
        y = yf if y is None else y + yf
        if nxt is not None:
            ug, uv = nxt
    out = x1 + gt_ref[0] * y
    if final_norm:
        out = out * lax.rsqrt(jnp.mean(out * out, axis=-1, keepdims=True) + NORM_EPS) * fg_ref[...]
    o_ref[0] = out


def _mix_ffn(x, oa, ob, w_out, gain, mods3, mrow, wup, cw, wdn, final_gain, tm):
    bsz, n, d = x.shape
    half = oa.shape[2]
    tm = min(tm, n)
    hb = tm // HALO
    nh = n // HALO
    final_norm = final_gain is not None
    fg = final_gain if final_norm else gain
    prev = lambda w: pl.BlockSpec((1, HALO, w), lambda b, i: (b, jnp.maximum(i * hb - 1, 0), 0))
    main = lambda w: pl.BlockSpec((1, tm, w), lambda b, i: (b, i, 0))
    nxt = lambda w: pl.BlockSpec((1, HALO, w), lambda b, i: (b, jnp.minimum((i + 1) * hb, nh - 1), 0))
    return pl.pallas_call(
        functools.partial(_ffn_kernel, final_norm=final_norm, nf=FFN_CHUNKS),
        grid=(bsz, n // tm),
        in_specs=[prev(d), main(d), nxt(d), prev(half), main(half), nxt(half), prev(half), main(half), nxt(half),
                  _resident(w_out.shape, lambda b, i: (0, 0)), _mod_spec(d, 2, mrow),
                  pl.BlockSpec((1, d), lambda b, i: (0, 0)),
                  _mod_spec(d, 3, mrow), _mod_spec(d, 4, mrow), _mod_spec(d, 5, mrow),
                  _resident(wup.shape, lambda b, i: (0, 0)),
                  _resident(cw.shape, lambda b, i: (0, 0)),
                  _resident(wdn.shape, lambda b, i: (0, 0)),
                  pl.BlockSpec((1, d), lambda b, i: (0, 0))],
        out_specs=main(d),
        out_shape=jax.ShapeDtypeStruct((bsz, n, d), F32),
        scratch_shapes=[pltpu.VMEM((tm + 2 * HALO, d), BF16), pltpu.VMEM((tm + 2 * HALO, 2 * half), BF16)],
        compiler_params=_params(("parallel", "parallel")),
        name="mix_ffn",
    )(x, x, x, oa, oa, oa, ob, ob, ob, w_out, mods3, gain.reshape(1, d), mods3, mods3, mods3, wup, cw, wdn,
      fg.reshape(1, d))


OD_Q, OD_C, OD_K, OD_V, OD_END = 0, 512, 2048, 2176, 2304
SUB = 8


def _prep_odd_kernel(xp_ref, x_ref, xn_ref, g_ref, sh_ref, sc_ref, win_ref, cw_ref, chd_ref, shd_ref,
                     q_ref, k_ref, v_ref, x0_ref, u_ref, h_ref, z_ref, *, scale):
    i = pl.program_id(1)
    last = pl.num_programs(1) - 1
    tm = x_ref.shape[1]
    rows = tm + 2 * HALO
    own = slice(HALO, HALO + tm)
    cwid = (OD_K - OD_C) // 3
    gain, shift, mscale = g_ref[...], sh_ref[0], sc_ref[0]
    keep_p = jnp.where(i > 0, 1.0, 0.0)
    keep_n = jnp.where(i < last, 1.0, 0.0)
    h_ref[0:HALO, :] = (_norm_mod(xp_ref[0], gain, shift, mscale) * keep_p).astype(BF16)
    h_ref[own, :] = _norm_mod(x_ref[0], gain, shift, mscale).astype(BF16)
    h_ref[HALO + tm:, :] = (_norm_mod(xn_ref[0], gain, shift, mscale) * keep_n).astype(BF16)
    z_ref[...] = jnp.dot(h_ref[...], win_ref[...], preferred_element_type=F32)
    for c in range((OD_C - OD_Q) // LANES):
        sl = slice(OD_Q + c * LANES, OD_Q + (c + 1) * LANES)
        y = _rope_lanes(z_ref[own, sl], chd_ref[...], shd_ref[...])
        q_ref[0, sl, :] = (y * scale).T.astype(BF16)
    k = _rope_lanes(z_ref[own, OD_K:OD_V], chd_ref[...], shd_ref[...])
    d0, d1 = _dup_heads(k)
    for c, dk in enumerate((d0, d0, d1, d1)):
        k_ref[0, :, c * LANES:(c + 1) * LANES] = dk.astype(BF16)
    d0, d1 = (d.T.astype(BF16) for d in _dup_heads(z_ref[own, OD_V:OD_END]))
    for c, dv in enumerate((d0, d0, d1, d1)):
        v_ref[0, c * LANES:(c + 1) * LANES, :] = dv

    def conv(part):
        sl = slice(OD_C + part * cwid, OD_C + (part + 1) * cwid)
        csl = slice(part * cwid, (part + 1) * cwid)
        ue = z_ref[:, sl]
        up = pltpu.roll(ue, 1, axis=0)[own]
        un = pltpu.roll(ue, rows - 1, axis=0)[own]
        return up * cw_ref[0:1, csl] + ue[own] * cw_ref[1:2, csl] + un * cw_ref[2:3, csl] + cw_ref[3:4, csl]

    x0_ref[0] = conv(0).astype(BF16)
    u_ref[0] = (conv(2) * conv(1)).astype(BF16)


def _prep_odd(x, gain, mods3, w_in, cw, chd, shd, tm):
    bsz, n, d = x.shape
    cols = w_in.shape[1]
    tm = min(tm, n)
    hb = tm // HALO
    nh = n // HALO
    cwid = (OD_K - OD_C) // 3
    row_blk = lambda w: pl.BlockSpec((1, tm, w), lambda b, i: (b, i, 0))
    col_blk = lambda w: pl.BlockSpec((1, w, tm), lambda b, i: (b, 0, i))
    slab = 4 * LANES
    return pl.pallas_call(
        functools.partial(_prep_odd_kernel, scale=HEAD_DIM ** -0.5 * LOG2E),
        grid=(bsz, n // tm),
        in_specs=[pl.BlockSpec((1, HALO, d), lambda b, i: (b, jnp.maximum(i * hb - 1, 0), 0)),
                  pl.BlockSpec((1, tm, d), lambda b, i: (b, i, 0)),
                  pl.BlockSpec((1, HALO, d), lambda b, i: (b, jnp.minimum((i + 1) * hb, nh - 1), 0)),
                  pl.BlockSpec((1, d), lambda b, i: (0, 0)),
                  _mod_spec(d, 0, None), _mod_spec(d, 1, None),
                  _resident(w_in.shape, lambda b, i: (0, 0)),
                  pl.BlockSpec(cw.shape, lambda b, i: (0, 0)),
                  pl.BlockSpec((tm, LANES), lambda b, i: (i, 0)),
                  pl.BlockSpec((tm, LANES), lambda b, i: (i, 0))],
        out_specs=[col_blk(slab), row_blk(slab), col_blk(slab), row_blk(cwid), row_blk(cwid)],
        out_shape=[jax.ShapeDtypeStruct((bsz, slab, n), BF16), jax.ShapeDtypeStruct((bsz, n, slab), BF16),
                   jax.ShapeDtypeStruct((bsz, slab, n), BF16), jax.ShapeDtypeStruct((bsz, n, cwid), BF16),
                   jax.ShapeDtypeStruct((bsz, n, cwid), BF16)],
        scratch_shapes=[pltpu.VMEM((tm + 2 * HALO, d), BF16), pltpu.VMEM((tm + 2 * HALO, cols), F32)],
        compiler_params=_params(("parallel", "parallel")),
        name="prep_odd",
    )(x, x, x, gain.reshape(1, d), mods3, mods3, w_in, cw, chd, shd)


def _kv_ctx_odd_kernel(z_ref, k_ref, vt_ref):
    d0, d1 = _dup_heads(z_ref[0, :, 0:LANES])
    for c, dk in enumerate((d0, d0, d1, d1)):
        k_ref[0, :, c * LANES:(c + 1) * LANES] = dk.astype(BF16)
    d0, d1 = (d.T.astype(BF16) for d in _dup_heads(z_ref[0, :, LANES:2 * LANES]))
    for c, dv in enumerate((d0, d0, d1, d1)):
        vt_ref[0, c * LANES:(c + 1) * LANES, :] = dv


def _kv_ctx_odd(z):
    bsz, n, cols = z.shape
    slab = 4 * LANES
    return pl.pallas_call(
        _kv_ctx_odd_kernel,
        grid=(bsz,),
        in_specs=[pl.BlockSpec((1, n, cols), lambda b: (b, 0, 0))],
        out_specs=[pl.BlockSpec((1, n, slab), lambda b: (b, 0, 0)), pl.BlockSpec((1, slab, n), lambda b: (b, 0, 0))],
        out_shape=[jax.ShapeDtypeStruct((bsz, n, slab), BF16), jax.ShapeDtypeStruct((bsz, slab, n), BF16)],
        compiler_params=_params(("parallel",)),
        name="kv_ctx_odd",
    )(z)


WIN_QB = 2
WIN_TILES = 16


def _window_kernel(qt_ref, kc_ref, vct_ref, kl_ref, vlt_ref, bias_ref, sink_ref, o_ref, kbc, vbct, kbl, vblt,
                   *, lc, nblk, qb, nch):
    qi = pl.program_id(2)
    tq = qb * Q_BLOCK
    nsp = qb + 2
    blk = HEADS_PER_GROUP * Q_BLOCK

    @pl.when(qi == 0)
    def _build():
        kms = _head_lane_masks(lc, 1)
        vms = _head_row_masks(lc)
        for j in range(HEADS_PER_GROUP):
            kbc[j * lc:(j + 1) * lc, :] = kc_ref[0] * kms[j]
            vbct[:, j * lc:(j + 1) * lc] = vct_ref[0] * vms[j]
        kms = _head_lane_masks(Q_BLOCK, 1)
        vms = _head_row_masks(Q_BLOCK)
        for e in (0, nblk + 1):
            kbl[e] = jnp.zeros((blk, MXU_DIM), BF16)
            vblt[e] = jnp.zeros((MXU_DIM, blk), BF16)

        def body(i, carry):
            r0 = pl.multiple_of(i * Q_BLOCK, Q_BLOCK)
            for j in range(HEADS_PER_GROUP):
                kbl[i + 1, j * Q_BLOCK:(j + 1) * Q_BLOCK, :] = kl_ref[0, pl.ds(r0, Q_BLOCK), :] * kms[j]
            return carry

        lax.fori_loop(0, nblk, body, 0)
        for i in range(nblk):
            for j in range(HEADS_PER_GROUP):
                vblt[i + 1, :, j * Q_BLOCK:(j + 1) * Q_BLOCK] = vlt_ref[0, :, i * Q_BLOCK:(i + 1) * Q_BLOCK] * vms[j]

    ntile = nblk // qb
    for h in range(nch):
        t = qi * nch + h
        qt = qt_ref[0, :, h * tq:(h + 1) * tq]
        s_c = jnp.dot(kbc[...], qt, preferred_element_type=F32)
        kspan = kbl[pl.ds(qb * t, nsp)].reshape(nsp * blk, MXU_DIM)
        s_s = jnp.dot(kspan, qt, preferred_element_type=F32) + bias_ref[...]
        edge = {0: jnp.where(t == 0, NEG_INF, 0.0), nsp - 1: jnp.where(t == ntile - 1, NEG_INF, 0.0)}
        pcs, pss, invs = [], [[None] * HEADS_PER_GROUP for _ in range(nsp)], []
        for j in range(HEADS_PER_GROUP):
            sink = sink_ref[0, j][0:1, :]
            segs = [s_c[j * lc:(j + 1) * lc, :]]
            for b in range(nsp):
                sg = s_s[b * blk + j * Q_BLOCK: b * blk + (j + 1) * Q_BLOCK, :]
                segs.append(sg + edge[b] if b in edge else sg)
            m = sink
            for sg in segs:
                m = jnp.maximum(m, jnp.max(sg, axis=0, keepdims=True))
            ps = [jnp.exp2(sg - m) for sg in segs]
            den = jnp.exp2(sink - m)
            for p in ps:
                den = den + jnp.sum(p, axis=0, keepdims=True)
            invs.append(1.0 / den)
            pcs.append(ps[0].astype(BF16))
            for b in range(nsp):
                pss[b][j] = ps[1 + b].astype(BF16)
        ot = jnp.dot(vbct[...], jnp.concatenate(pcs, axis=0), preferred_element_type=F32)
        for b in range(nsp):
            ot = ot + jnp.dot(vblt[qb * t + b], jnp.concatenate(pss[b], axis=0), preferred_element_type=F32)
        o_ref[0, h * tq:(h + 1) * tq, :] = (ot * _rows4(invs, tq)).T.astype(BF16)


@functools.lru_cache(maxsize=None)
def _window_bias(qb):
    nsp = qb + 2
    row = np.arange(nsp * HEADS_PER_GROUP * Q_BLOCK)[:, None]
    r = np.arange(qb * Q_BLOCK)[None, :]
    b = row // (HEADS_PER_GROUP * Q_BLOCK)
    jj = b * Q_BLOCK + row % Q_BLOCK
    band = (jj >= r) & (jj <= r + 2 * WINDOW)
    return np.where(band, 0.0, NEG_INF).astype(np.float32)


def _window(qt, kc, vct, kl, vlt, sink):
    bsz, qw, n = qt.shape
    groups = qw // MXU_DIM
    lc = kc.shape[1]
    nblk = n // Q_BLOCK
    qb = min(WIN_QB, nblk)
    tq = qb * Q_BLOCK
    nch = next(c for c in (WIN_TILES, 2, 1) if n % (c * tq) == 0)
    nq = n // (tq * nch)
    blk = HEADS_PER_GROUP * Q_BLOCK
    bias = jnp.asarray(_window_bias(qb))
    sink_t = jnp.broadcast_to((sink.astype(F32) * LOG2E).reshape(groups, HEADS_PER_GROUP, 1, 1),
                              (groups, HEADS_PER_GROUP, SUB, tq))

    return pl.pallas_call(
        functools.partial(_window_kernel, lc=lc, nblk=nblk, qb=qb, nch=nch),
        grid=(bsz, groups, nq),
        in_specs=[pl.BlockSpec((1, MXU_DIM, tq * nch), lambda b, g, i: (b, g, i)),
                  pl.BlockSpec((1, lc, MXU_DIM), lambda b, g, i: (b, 0, g)),
                  pl.BlockSpec((1, MXU_DIM, lc), lambda b, g, i: (b, g, 0)),
                  pl.BlockSpec((1, n, MXU_DIM), lambda b, g, i: (b, 0, g)),
                  pl.BlockSpec((1, MXU_DIM, n), lambda b, g, i: (b, g, 0)),
                  _resident(bias.shape, lambda b, g, i: (0, 0)),
                  pl.BlockSpec((1, HEADS_PER_GROUP, SUB, tq), lambda b, g, i: (g, 0, 0, 0))],
        out_specs=pl.BlockSpec((1, tq * nch, MXU_DIM), lambda b, g, i: (b, i, g)),
        out_shape=jax.ShapeDtypeStruct((bsz, n, qw), BF16),
        scratch_shapes=[pltpu.VMEM((HEADS_PER_GROUP * lc, MXU_DIM), BF16),
                        pltpu.VMEM((MXU_DIM, HEADS_PER_GROUP * lc), BF16),
                        pltpu.VMEM((nblk + 2, blk, MXU_DIM), BF16),
                        pltpu.VMEM((nblk + 2, MXU_DIM, blk), BF16)],
        compiler_params=_params(("parallel", "parallel", "arbitrary")),
        name="window_attn",
    )(qt, kc, vct, kl, vlt, bias, sink_t)


FFT_N2 = 128


def _filter_kernel(z_ref, w1_ref, b1_ref, w2_ref, b2_ref, w3_ref, b3_ref, w4_ref, fr_ref, dl_ref, o_ref):
    def lin(a, w_ref):
        a_hi, a_lo = _split_bf16(a)
        w_hi, w_lo = _split_bf16(w_ref[...])
        return _dot3(a_hi, a_lo, w_hi, w_lo)

    z = z_ref[...]
    h = b1_ref[...]
    for e in range(C_EMB_DIM):
        h = h + z[:, e:e + 1] * w1_ref[e:e + 1, :]
    h = jnp.sin(fr_ref[0:1] * h)
    h = jnp.sin(fr_ref[1:2] * (lin(h, w2_ref) + b2_ref[...]))
    h = jnp.sin(fr_ref[2:3] * (lin(h, w3_ref) + b3_ref[...]))
    h = lin(h, w4_ref)
    t = z[:, 0:1]
    cw = dl_ref.shape[1]
    dec = jnp.exp(-t * dl_ref[...])
    row = lax.broadcasted_iota(jnp.int32, dec.shape, 0) + pl.program_id(0) * z.shape[0]
    o_ref[0] = h[:, :cw] * dec
    o_ref[1] = jnp.where(row == 0, 0.0, h[:, cw:] * dec)


def _filters(zfeat, w1, b1, w2, b2, w3, b3, w4, freq, deltas, tm):
    n = zfeat.shape[0]
    tm = min(tm, n)
    cw = deltas.shape[1]
    full = lambda a: pl.BlockSpec(a.shape, lambda i: (0,) * a.ndim)
    ops = (w1, b1, w2, b2, w3, b3, w4, freq, deltas)
    return pl.pallas_call(
        _filter_kernel,
        grid=(n // tm,),
        in_specs=[pl.BlockSpec((tm, zfeat.shape[1]), lambda i: (i, 0))] + [full(a) for a in ops],
        out_specs=pl.BlockSpec((2, tm, cw), lambda i: (0, i, 0)),
        out_shape=jax.ShapeDtypeStruct((2, n, cw), F32),
        compiler_params=_params(("parallel",)),
        name="hyena_filters",
    )(zfeat, *ops)


def _dft_first_kernel(f_ref, x_ref, o_ref):
    _, n1h, tm2, cw = x_ref.shape
    x = x_ref[0].reshape(n1h * tm2, cw).astype(BF16)
    r = jnp.dot(f_ref[...], x, preferred_element_type=F32)
    o_ref[0] = r.astype(BF16).reshape(o_ref.shape[1:])


def _dft_first(f1, x):
    bsz, n1h, n2, cw = x.shape
    tm2 = HALO
    nkp = f1.shape[0] // (2 * tm2)
    return pl.pallas_call(
        _dft_first_kernel,
        grid=(bsz, n2 // tm2),
        in_specs=[pl.BlockSpec(f1.shape, lambda b, j: (0, 0)),
                  pl.BlockSpec((1, n1h, tm2, cw), lambda b, j: (b, 0, j, 0))],
        out_specs=pl.BlockSpec((1, 2, nkp, tm2, cw), lambda b, j: (b, 0, 0, j, 0)),
        out_shape=jax.ShapeDtypeStruct((bsz, 2, nkp, n2, cw), BF16),
        compiler_params=_params(("parallel", "parallel")),
        name="dft_first",
    )(f1, x)


def _dft_last_gate_kernel(f_ref, b_ref, x0_ref, u_ref, bias_ref, o_ref):
    _, _, nkp, tm2, cw = b_ref.shape
    y = jnp.dot(f_ref[...], b_ref[0].reshape(2 * nkp * tm2, cw), preferred_element_type=F32)
    y = y.reshape(x0_ref.shape[1:])
    o_ref[0] = (x0_ref[0].astype(F32) * (y + u_ref[0].astype(F32) * bias_ref[...])).astype(BF16)


def _dft_last_gate(f2, b, x0, u, bias):
    bsz, _, nkp, n2, cw = b.shape
    tm2 = HALO
    n1h = f2.shape[0] // tm2
    tblk = pl.BlockSpec((1, n1h, tm2, cw), lambda b_, j: (b_, 0, j, 0))
    return pl.pallas_call(
        _dft_last_gate_kernel,
        grid=(bsz, n2 // tm2),
        in_specs=[pl.BlockSpec(f2.shape, lambda b_, j: (0, 0)),
                  pl.BlockSpec((1, 2, nkp, tm2, cw), lambda b_, j: (b_, 0, 0, j, 0)),
                  tblk, tblk, pl.BlockSpec((1, cw), lambda b_, j: (0, 0))],
        out_specs=tblk,
        out_shape=jax.ShapeDtypeStruct((bsz, n1h, n2, cw), BF16),
        compiler_params=_params(("parallel", "parallel")),
        name="dft_last_gate",
    )(f2, b, x0, u, bias.reshape(1, cw))


def _spec_fwd(a, m_ref):
    return jnp.dot(m_ref[0], a.reshape(2 * FFT_N2, a.shape[-1]), preferred_element_type=F32)


def _filter_spec_kernel(a_ref, m_ref, o_ref):
    for kk in range(a_ref.shape[2]):
        xf = _spec_fwd(a_ref[0, :, kk], m_ref.at[kk:kk + 1])
        xb = _spec_fwd(a_ref[1, :, kk], m_ref.at[kk:kk + 1])
        o_ref[0, kk] = xf[:FFT_N2] + xb[:FFT_N2]
        o_ref[1, kk] = xf[FFT_N2:] - xb[FFT_N2:]


def _conv_spec_kernel(a_ref, k_ref, m_ref, i_ref, o_ref):
    kr, ki = k_ref[0, 0], k_ref[1, 0]
    for b in range(a_ref.shape[0]):
        x = _spec_fwd(a_ref[b, :, 0], m_ref)
        xr, xi = x[:FFT_N2], x[FFT_N2:]
        y = jnp.concatenate([xr * kr - xi * ki, xr * ki + xi * kr], axis=0).astype(BF16)
        bv = jnp.dot(i_ref[0], y, preferred_element_type=F32)
        o_ref[b, :, 0] = bv.astype(BF16).reshape(2, FFT_N2, bv.shape[-1])


def _filter_spec(a, m_fwd):
    _, _, nkp, _, cw = a.shape
    return pl.pallas_call(
        _filter_spec_kernel,
        grid=(nkp // SUB,),
        in_specs=[pl.BlockSpec((2, 2, SUB, FFT_N2, cw), lambda k: (0, 0, k, 0, 0)),
                  pl.BlockSpec((SUB, 2 * FFT_N2, 2 * FFT_N2), lambda k: (k, 0, 0))],
        out_specs=pl.BlockSpec((2, SUB, FFT_N2, cw), lambda k: (0, k, 0, 0)),
        out_shape=jax.ShapeDtypeStruct((2, nkp, FFT_N2, cw), F32),
        compiler_params=_params(("parallel",)),
        name="filter_spectrum",
    )(a, m_fwd)


def _conv_spec(a, kspec, m_fwd, m_inv):
    bsz, _, nkp, _, cw = a.shape
    blk = pl.BlockSpec((bsz, 2, 1, FFT_N2, cw), lambda k: (0, 0, k, 0, 0))
    mat = pl.BlockSpec((1, 2 * FFT_N2, 2 * FFT_N2), lambda k: (k, 0, 0))
    return pl.pallas_call(
        _conv_spec_kernel,
        grid=(nkp,),
        in_specs=[blk, pl.BlockSpec((2, 1, FFT_N2, cw), lambda k: (0, k, 0, 0)), mat, mat],
        out_specs=blk,
        out_shape=jax.ShapeDtypeStruct(a.shape, BF16),
        compiler_params=_params(("parallel",)),
        name="conv_spectrum",
    )(a, kspec, m_fwd, m_inv)


@functools.lru_cache(maxsize=None)
def _dft_tables(n):
    nfft = 2 * n
    n2 = FFT_N2
    n1 = nfft // n2
    n1h = n1 // 2
    nk = n1h + 1
    nkp = -(-nk // SUB) * SUB

    def cs(num, den):
        ang = (num % den).astype(np.float64) * (2.0 * math.pi / den)
        return np.cos(ang), np.sin(ang)

    def pad_k(t, axis):
        widths = [(0, 0)] * t.ndim
        widths[axis] = (0, nkp - nk)
        return np.pad(t, widths)

    k1 = np.arange(nk, dtype=np.int64)
    c1, s1 = cs(k1[:, None] * np.arange(n1h)[None, :], n1)
    f1 = np.concatenate([pad_k(c1, 0), pad_k(-s1, 0)], axis=0)
    kk = k1[:, None, None] + n1 * np.arange(n2)[None, :, None]
    cg, sg = cs(kk * np.arange(n2)[None, None, :], nfft)
    g_re, g_im = cg, -sg
    m_fwd = np.concatenate([np.concatenate([g_re, -g_im], axis=2),
                            np.concatenate([g_im, g_re], axis=2)], axis=1)
    gt_re, gt_im = np.swapaxes(g_re, 1, 2), np.swapaxes(g_im, 1, 2)
    m_inv = np.concatenate([np.concatenate([gt_re, gt_im], axis=2),
                            np.concatenate([-gt_im, gt_re], axis=2)], axis=1)
    wk = np.where((k1 == 0) | (k1 == n1h), 1.0, 2.0)[None, :] * (1.0 / nfft)
    c2, s2 = cs(np.arange(n1h)[:, None] * k1[None, :], n1)
    f2 = np.concatenate([pad_k(c2 * wk, 1), pad_k(-s2 * wk, 1)], axis=1)
    eye = np.eye(HALO)
    return tuple(t.astype(np.float32) for t in (np.kron(f1, eye), pad_k(m_fwd, 0), pad_k(m_inv, 0), np.kron(f2, eye)))


def _hyena_long_conv(u, x0, bias, hcat, tabs):
    f1, m_fwd, m_inv, f2 = tabs
    bsz, n, cw = u.shape
    n2 = FFT_N2
    n1h = n // n2
    kspec = _filter_spec(_dft_first(f1, hcat.reshape(2, n1h, n2, cw)), m_fwd)
    u4 = u.reshape(bsz, n1h, n2, cw)
    b_u = _conv_spec(_dft_first(f1, u4), kspec, m_fwd, m_inv)
    return _dft_last_gate(f2, b_u, x0.reshape(bsz, n1h, n2, cw), u4, bias).reshape(bsz, n, cw)


def _axial_angles(rows, rope_dim):
    row_idx = np.repeat(np.arange(rows), GRID_W).astype(np.float64)
    col_idx = np.tile(np.arange(GRID_W), rows).astype(np.float64)
    d_axis = rope_dim // 2
    inv_freq = ROPE_THETA ** (-np.arange(0, d_axis, 2, dtype=np.float64) / d_axis)
    ang = np.concatenate([row_idx[:, None] * inv_freq, col_idx[:, None] * inv_freq], axis=-1)
    return np.cos(ang), np.sin(ang)


@functools.lru_cache(maxsize=None)
def _rope_tables(n):
    rows = n // GRID_W
    sign = np.tile(np.array([-1.0, 1.0]), LANES // 2)[None, :]
    cos, sin = _axial_angles(rows, HEAD_DIM)
    chd = np.tile(np.repeat(cos, 2, axis=1), (1, LANES // HEAD_DIM))
    shd = np.tile(np.repeat(sin, 2, axis=1), (1, LANES // HEAD_DIM)) * sign
    cos, sin = _axial_angles(rows, B_ROPE)
    ones = np.ones((n, LANES - 2 * B_ROPE))
    cb = np.concatenate([np.tile(np.repeat(cos, 2, axis=1), (1, 2)), ones], axis=1)
    sb = np.concatenate([np.tile(np.repeat(sin, 2, axis=1), (1, 2)), 0.0 * ones], axis=1) * sign
    return tuple(t.astype(np.float32) for t in (chd, shd, cb, sb))


@functools.lru_cache(maxsize=None)
def _hyena_features(n, cwid):
    t = np.linspace(0.0, 1.0, n)[:, None]
    wpos = 2 * math.pi * np.arange(n)[:, None] / n
    fb = np.linspace(1e-4, C_BANDS - 1, C_BANDS)[None, :]
    zfeat = np.concatenate([t, np.cos(fb * wpos), -np.sin(fb * wpos), np.zeros((n, SUB - C_EMB_DIM))], axis=-1)
    deltas = np.abs(np.linspace(C_MIN_DECAY, C_MAX_DECAY, cwid))[None, :]
    return zfeat.astype(np.float32), deltas.astype(np.float32)


def _even_weights(w_in, a_qn, a_kn, b_qn, b_w_uq, b_kvn, b_w_ukv):
    d = w_in.shape[0]
    aq_w = w_in.shape[1] - (B_Q_RANK + 2 * A_KV_HEADS * HEAD_DIM + B_KV_RANK + B_ROPE)
    o = [0, aq_w, aq_w + B_Q_RANK]
    o += [o[-1] + A_KV_HEADS * HEAD_DIM, o[-1] + 2 * A_KV_HEADS * HEAD_DIM]
    o += [o[-1] + B_KV_RANK, o[-1] + B_KV_RANK + B_ROPE]
    kr = w_in[:, o[5]:o[6]]
    w_aug = jnp.concatenate([w_in[:, :o[5]], kr, kr, jnp.zeros((d, LANES - 2 * B_ROPE), w_in.dtype)], axis=1)
    heads = b_w_uq.shape[1] // (B_NOPE + B_ROPE)
    uq = b_w_uq.reshape(B_Q_RANK, heads // 2, 2, B_NOPE + B_ROPE)
    wuq = jnp.concatenate([uq[:, :, 0, :B_NOPE], uq[:, :, 1, :B_NOPE], uq[:, :, 0, B_NOPE:], uq[:, :, 1, B_NOPE:],
                           jnp.zeros((B_Q_RANK, heads // 2, MXU_DIM - 2 * (B_NOPE + B_ROPE)), b_w_uq.dtype)], axis=2)
    wuq = wuq.reshape(B_Q_RANK, heads // 2 * MXU_DIM)
    ukv = b_w_ukv.reshape(B_KV_RANK, heads, B_NOPE + B_VDIM)
    wukn = ukv[:, :, :B_NOPE].reshape(B_KV_RANK, heads * B_NOPE)
    wuv = ukv[:, :, B_NOPE:].reshape(B_KV_RANK, heads * B_VDIM)
    gains = (jnp.tile(a_qn, aq_w // HEAD_DIM)[None, :], jnp.tile(a_kn, A_KV_HEADS)[None, :],
             b_qn[None, :], b_kvn[None, :])
    return w_aug.astype(BF16), gains + (wuq.astype(BF16), wukn.astype(BF16), wuv.astype(BF16))


def _ffn_weights(w_up, conv_w, conv_b, w_down):
    cw = jnp.concatenate([conv_w, conv_b[None, :], jnp.zeros((SUB - 4, conv_w.shape[1]), conv_w.dtype)], axis=0)
    return w_up.astype(BF16), cw, w_down.astype(BF16)


FFN_CHUNKS = 1


def kernel(x, c, ctx, c_ctx, w_mod, b_mod, norm_mix, norm_ffn, ev_w_in, ev_w_out, a_q_norm, a_k_norm, b_q_norm, b_w_uq, b_kv_norm, b_w_ukv, od_w_in, od_w_out, d_sink, c_conv_w, c_conv_b, c_filt_w1, c_filt_b1, c_filt_w2, c_filt_b2, c_filt_w3, c_filt_b3, c_filt_w4, c_filt_freq, c_bias, ffn_w_up, ffn_conv_w, ffn_conv_b, ffn_w_down, final_norm):
    bsz, n, d = x.shape
    depth = w_mod.shape[0]
    assert depth == 2 and n % Q_BLOCK == 0 and d % LANES == 0
    tm = ROW_TILE
    rows = -(-(bsz + 1) // SUB) * SUB
    cvec = jnp.concatenate([c, c_ctx[None, :], jnp.zeros((rows - bsz - 1, d), F32)], axis=0)
    mods = _mods(cvec, w_mod, b_mod)
    chd, shd, cb, sb = (jnp.asarray(t) for t in _rope_tables(n))
    tabs = (chd, shd, cb, sb)

    m3 = mods[0].reshape(rows, 1, N_MOD * d)
    w_aug, prep_w = _even_weights(ev_w_in[0], a_q_norm[0], a_k_norm[0], b_q_norm[0], b_w_uq[0],
                                  b_kv_norm[0], b_w_ukv[0])
    qa_l, ka_l, va_l, qb_l, kb_l, vb_l = _prep_even(x, norm_mix[0], m3, None, w_aug, prep_w, tabs, True, tm)
    qa_c, ka_c, va_c, qb_c, kb_c, vb_c = _prep_even(ctx, norm_mix[0], m3, bsz, w_aug, prep_w, tabs, False, tm)
    tq, tk = ATTN_TQ, ATTN_TK
    oa_l = _flash(qa_l, (ka_c, ka_l), (va_c, va_l), 1, tq, tk)
    ob_l = _flash(qb_l, (kb_c, kb_l), (vb_c, vb_l), 2, tq, tk)
    oa_c = _flash(qa_c, (ka_c,), (va_c,), 1, tq, tk)
    ob_c = _flash(qb_c, (kb_c,), (vb_c,), 2, tq, tk)
    w_out = ev_w_out[0].astype(BF16)
    ffn_w = _ffn_weights(ffn_w_up[0], ffn_conv_w[0], ffn_conv_b[0], ffn_w_down[0])
    x = _mix_ffn(x, oa_l, ob_l, w_out, norm_ffn[0], m3, None, *ffn_w, None, tm)
    ctx = _mix_ffn(ctx, oa_c, ob_c, w_out, norm_ffn[0], m3, bsz, *ffn_w, None, tm)

    m3 = mods[1].reshape(rows, 1, N_MOD * d)
    w_in = od_w_in[0].astype(BF16)
    z_c = _inproj(ctx, norm_mix[1], m3, bsz, w_in[:, OD_K:], tm)
    cw = jnp.concatenate([c_conv_w[0], c_conv_b[0][None, :], jnp.zeros((SUB - 4, OD_K - OD_C), F32)], axis=0)
    qd, kd, vd, x0, u = _prep_odd(x, norm_mix[1], m3, w_in, cw, chd, shd, 2 * tm)
    kd_c, vd_c = _kv_ctx_odd(z_c)
    od = _window(qd, kd_c, vd_c, kd, vd, d_sink[0])
    zfeat, deltas = (jnp.asarray(t) for t in _hyena_features(n, c_bias.shape[1]))
    w1 = jnp.concatenate([c_filt_w1[0], jnp.zeros((SUB - C_EMB_DIM, c_filt_w1.shape[2]), F32)], axis=0)
    hcat = _filters(zfeat, w1, c_filt_b1[0][None, :], c_filt_w2[0], c_filt_b2[0][None, :], c_filt_w3[0],
                    c_filt_b3[0][None, :], c_filt_w4[0], jnp.concatenate([c_filt_freq[0], jnp.zeros((SUB - 3, c_filt_freq.shape[2]), F32)], axis=0),
                    deltas, tm)
    oc = _hyena_long_conv(u, x0, c_bias[0], hcat, tuple(jnp.asarray(t).astype(BF16) for t in _dft_tables(n)))
    ffn_w = _ffn_weights(ffn_w_up[1], ffn_conv_w[1], ffn_conv_b[1], ffn_w_down[1])
    return _mix_ffn(x, od, oc, od_w_out[0].astype(BF16), norm_ffn[1], m3, None, *ffn_w, final_norm, tm)
```

```python
import functools
import math

import jax
import jax.numpy as jnp
import numpy as np
from jax import lax
from jax.experimental import pallas as pl
from jax.experimental.pallas import tpu as pltpu

F32 = jnp.float32
BF16 = jnp.bfloat16

GRID_W = 64
HEAD_DIM = 64
ROPE_THETA = 10000.0
NORM_EPS = 1e-6
NEG_INF = -1e30
N_MOD = 6
A_KV_HEADS = 2
B_NOPE = 64
B_ROPE = 32
B_VDIM = 64
B_Q_RANK = 384
B_KV_RANK = 256
C_EMB_DIM = 5
C_BANDS = (C_EMB_DIM - 1) // 2
C_MIN_DECAY = math.log(1e-2) / 1.5
C_MAX_DECAY = math.log(1e-2) / 0.3
WINDOW = 128
Q_BLOCK = 128
LOG2E = math.log2(math.e)

LANES = 128
MXU_DIM = 256
VMEM_LIMIT = 56 * 1024 * 1024
HEADS_PER_GROUP = MXU_DIM // HEAD_DIM

ROW_TILE = 512
ATTN_TQ = 1024
ATTN_TK = 256


def _params(sem, vmem=VMEM_LIMIT):
    return pltpu.CompilerParams(dimension_semantics=sem, vmem_limit_bytes=vmem)


def _resident(shape, index_map):
    return pl.BlockSpec(shape, index_map, pipeline_mode=pl.Buffered(1))


def _norm_mod(x, gain, shift, scale):
    inv = lax.rsqrt(jnp.mean(x * x, axis=-1, keepdims=True) + NORM_EPS)
    return (x * inv) * gain * (1.0 + scale) + shift


def _rope_lanes(x, cos, sin_signed):
    lane = lax.broadcasted_iota(jnp.int32, x.shape, 1)
    nxt = pltpu.roll(x, LANES - 1, axis=1)
    prv = pltpu.roll(x, 1, axis=1)
    swapped = jnp.where(lane % 2 == 0, nxt, prv)
    return x * cos + swapped * sin_signed


def _head_rmsnorm_lanes(x, gain):
    lane = lax.broadcasted_iota(jnp.int32, x.shape, 1)
    lo = lane < HEAD_DIM
    sq = x * x
    s_lo = jnp.sum(jnp.where(lo, sq, 0.0), axis=-1, keepdims=True)
    s_hi = jnp.sum(jnp.where(lo, 0.0, sq), axis=-1, keepdims=True)
    ms = jnp.where(lo, s_lo, s_hi) * (1.0 / HEAD_DIM)
    return x * lax.rsqrt(ms + NORM_EPS) * gain


def _dup_heads(x):
    lane = lax.broadcasted_iota(jnp.int32, x.shape, 1)
    lo = lane < HEAD_DIM
    r = pltpu.roll(x, HEAD_DIM, axis=1)
    return jnp.where(lo, x, r), jnp.where(lo, r, x)


def _split_bf16(x):
    hi = x.astype(BF16)
    lo = (x - hi.astype(F32)).astype(BF16)
    return hi, lo


def _dot3(a_hi, a_lo, b_hi, b_lo):
    d = functools.partial(jnp.dot, preferred_element_type=F32)
    return d(a_hi, b_hi) + d(a_hi, b_lo) + d(a_lo, b_hi)


def _mods_kernel(c_ref, w_ref, b_ref, o_ref):
    c = c_ref[...]
    s = c * (1.0 / (1.0 + jnp.exp(-c)))
    s_hi, s_lo = _split_bf16(s)
    w_hi, w_lo = _split_bf16(w_ref[0])
    o_ref[0] = _dot3(s_hi, s_lo, w_hi, w_lo) + b_ref[0]


def _mods(cvec, w_mod, b_mod):
    depth, d, n = w_mod.shape
    rows = cvec.shape[0]
    tn = 1536
    return pl.pallas_call(
        _mods_kernel,
        grid=(depth, n // tn),
        in_specs=[pl.BlockSpec((rows, d), lambda l, j: (0, 0)),
                  pl.BlockSpec((1, d, tn), lambda l, j: (l, 0, j)),
                  pl.BlockSpec((1, 1, tn), lambda l, j: (l, 0, j))],
        out_specs=pl.BlockSpec((1, rows, tn), lambda l, j: (l, 0, j)),
        out_shape=jax.ShapeDtypeStruct((depth, rows, n), F32),
        compiler_params=_params(("arbitrary", "arbitrary")),
        name="mods",
    )(cvec, w_mod, b_mod.reshape(depth, 1, n))


def _mod_spec(d, chunk, row):
    if row is None:
        return pl.BlockSpec((1, 1, d), lambda b, *_: (b, 0, chunk))
    return pl.BlockSpec((1, 1, d), lambda b, *_: (row, 0, chunk))


def _inproj_kernel(x_ref, g_ref, sh_ref, sc_ref, w_ref, o_ref):
    h = _norm_mod(x_ref[0], g_ref[...], sh_ref[0], sc_ref[0])
    o_ref[0] = jnp.dot(h.astype(BF16), w_ref[...], preferred_element_type=F32)


def _inproj(x, gain, mods3, mrow, w, tm):
    bsz, n, d = x.shape
    cols = w.shape[1]
    tm = min(tm, n)
    return pl.pallas_call(
        _inproj_kernel,
        grid=(bsz, n // tm),
        in_specs=[pl.BlockSpec((1, tm, d), lambda b, i: (b, i, 0)),
                  pl.BlockSpec((1, d), lambda b, i: (0, 0)),
                  _mod_spec(d, 0, mrow), _mod_spec(d, 1, mrow),
                  _resident((d, cols), lambda b, i: (0, 0))],
        out_specs=pl.BlockSpec((1, tm, cols), lambda b, i: (b, i, 0)),
        out_shape=jax.ShapeDtypeStruct((bsz, n, cols), F32),
        compiler_params=_params(("parallel", "parallel")),
        name="inproj",
    )(x, gain.reshape(1, d), mods3, mods3, w)


EV_AQ, EV_BQ, EV_AK, EV_AV, EV_BKV, EV_BKR, EV_END = 0, 512, 896, 1024, 1152, 1408, 1536


def _prep_even_kernel(x_ref, g_ref, sh_ref, sc_ref, win_ref,
                      aqg_ref, akg_ref, bqg_ref, bkvg_ref, wuq_ref, wukn_ref, wuv_ref,
                      chd_ref, shd_ref, cb_ref, sb_ref,
                      qa_ref, ka_ref, va_ref, qb_ref, kb_ref, vb_ref, z_ref, *, use_rope, a_scale, b_scale):
    h = _norm_mod(x_ref[0], g_ref[...], sh_ref[0], sc_ref[0])
    z_ref[0] = jnp.dot(h.astype(BF16), win_ref[...], preferred_element_type=F32)

    def rope_hd(y):
        return _rope_lanes(y, chd_ref[...], shd_ref[...]) if use_rope else y

    def rope_b(y):
        return _rope_lanes(y, cb_ref[...], sb_ref[...]) if use_rope else y

    def put_t(dst, c, y):
        dst[0, c * LANES:(c + 1) * LANES, :] = y.T.astype(BF16)

    for c in range((EV_BQ - EV_AQ) // LANES):
        sl = slice(EV_AQ + c * LANES, EV_AQ + (c + 1) * LANES)
        y = rope_hd(_head_rmsnorm_lanes(z_ref[0, :, sl], aqg_ref[:, c * LANES:(c + 1) * LANES]))
        put_t(qa_ref, c, y * a_scale)
    k = rope_hd(_head_rmsnorm_lanes(z_ref[0, :, EV_AK:EV_AV], akg_ref[...]))
    d0, d1 = _dup_heads(k)
    d0 = d0.astype(BF16)
    d1 = d1.astype(BF16)
    ka_ref[0, :, 0 * LANES:1 * LANES] = d0
    ka_ref[0, :, 1 * LANES:2 * LANES] = d0
    ka_ref[0, :, 2 * LANES:3 * LANES] = d1
    ka_ref[0, :, 3 * LANES:4 * LANES] = d1
    d0, d1 = (d.T.astype(BF16) for d in _dup_heads(z_ref[0, :, EV_AV:EV_BKV]))
    for c, dv in enumerate((d0, d0, d1, d1)):
        va_ref[0, c * LANES:(c + 1) * LANES, :] = dv
    cq = z_ref[0, :, EV_BQ:EV_AK]
    cq = cq * lax.rsqrt(jnp.mean(cq * cq, axis=-1, keepdims=True) + NORM_EPS) * bqg_ref[...]
    qb = jnp.dot(cq.astype(BF16), wuq_ref[...], preferred_element_type=F32)
    for c in range(qb.shape[1] // MXU_DIM):
        lo = slice(c * MXU_DIM, c * MXU_DIM + LANES)
        hi = slice(c * MXU_DIM + LANES, (c + 1) * MXU_DIM)
        put_t(qb_ref, 2 * c, qb[:, lo] * b_scale)
        put_t(qb_ref, 2 * c + 1, rope_b(qb[:, hi]) * b_scale)
    ckv = z_ref[0, :, EV_BKV:EV_BKR]
    ckv = (ckv * lax.rsqrt(jnp.mean(ckv * ckv, axis=-1, keepdims=True) + NORM_EPS) * bkvg_ref[...]).astype(BF16)
    kn = jnp.dot(ckv, wukn_ref[...], preferred_element_type=F32)
    vb = jnp.dot(ckv, wuv_ref[...], preferred_element_type=F32)
    for c in range(vb.shape[1] // LANES):
        put_t(vb_ref, c, vb[:, c * LANES:(c + 1) * LANES])
    kr = rope_b(z_ref[0, :, EV_BKR:EV_END]).astype(BF16)
    for c in range(kn.shape[1] // LANES):
        kb_ref[0, :, c * MXU_DIM:c * MXU_DIM + LANES] = kn[:, c * LANES:(c + 1) * LANES].astype(BF16)
        kb_ref[0, :, c * MXU_DIM + LANES:(c + 1) * MXU_DIM] = kr


def _prep_even(x, gain, mods3, mrow, w_in, wts, tabs, use_rope, tm):
    bsz, n, d = x.shape
    cols = w_in.shape[1]
    tm = min(tm, n)
    aqg, akg, bqg, bkvg, wuq, wukn, wuv = wts
    chd, shd, cb, sb = tabs
    full = lambda a: _resident(a.shape, lambda b, i: (0,) * a.ndim)
    tab = lambda a: pl.BlockSpec((tm, LANES), (lambda b, i: (i, 0)) if use_rope else (lambda b, i: (0, 0)))
    outs = [(4 * LANES, True), (4 * LANES, False), (4 * LANES, True), (wuq.shape[1], True),
            (2 * wukn.shape[1], False), (wuv.shape[1], True)]
    kern = functools.partial(_prep_even_kernel, use_rope=use_rope, a_scale=HEAD_DIM ** -0.5 * LOG2E,
                             b_scale=(B_NOPE + B_ROPE) ** -0.5 * LOG2E)
    return pl.pallas_call(
        kern,
        grid=(bsz, n // tm),
        in_specs=[pl.BlockSpec((1, tm, d), lambda b, i: (b, i, 0)),
                  pl.BlockSpec((1, d), lambda b, i: (0, 0)),
                  _mod_spec(d, 0, mrow), _mod_spec(d, 1, mrow), full(w_in),
                  full(aqg), full(akg), full(bqg), full(bkvg), full(wuq), full(wukn), full(wuv),
                  tab(chd), tab(shd), tab(cb), tab(sb)],
        out_specs=[pl.BlockSpec((1, w, tm), lambda b, i: (b, 0, i)) if t else
                   pl.BlockSpec((1, tm, w), lambda b, i: (b, i, 0)) for w, t in outs],
        out_shape=[jax.ShapeDtypeStruct((bsz, w, n) if t else (bsz, n, w), BF16) for w, t in outs],
        scratch_shapes=[pltpu.VMEM((1, tm, cols), F32)],
        compiler_params=_params(("parallel", "parallel")),
        name="prep_even",
    )(x, gain.reshape(1, d), mods3, mods3, w_in, aqg, akg, bqg, bkvg, wuq, wukn, wuv, chd, shd, cb, sb)


def _head_lane_masks(rows, nsub):
    lane = lax.broadcasted_iota(jnp.int32, (rows, MXU_DIM), 1)
    kms = []
    for j in range(HEADS_PER_GROUP):
        if nsub == 1:
            km = (lane >= j * HEAD_DIM) & (lane < (j + 1) * HEAD_DIM)
        else:
            jj = j % 2
            km = ((lane >= jj * B_NOPE) & (lane < (jj + 1) * B_NOPE)) | \
                 ((lane >= 2 * B_NOPE + jj * B_ROPE) & (lane < 2 * B_NOPE + (jj + 1) * B_ROPE))
        kms.append(jnp.where(km, 1.0, 0.0).astype(BF16))
    return kms


def _head_row_masks(cols):
    row = lax.broadcasted_iota(jnp.int32, (MXU_DIM, cols), 0)
    return [jnp.where((row >= j * HEAD_DIM) & (row < (j + 1) * HEAD_DIM), 1.0, 0.0).astype(BF16)
            for j in range(HEADS_PER_GROUP)]


def _rows4(vals, cols):
    return jnp.concatenate([jnp.broadcast_to(v, (HEAD_DIM, cols)) for v in vals], axis=0)


def _flash_kernel(*refs, nsub, tk, nsrc, nchain):
    qt_ref = refs[0]
    srcs = [(refs[1 + 2 * s], refs[2 + 2 * s]) for s in range(nsrc)]
    o_ref, kb_ref, vbt_ref, acc_ref = refs[1 + 2 * nsrc:]
    hpu = HEADS_PER_GROUP // nsub
    tq = qt_ref.shape[2]
    tqc = tq // nchain
    nb = kb_ref.shape[0]

    @pl.when(pl.program_id(2) == 0)
    def _build():
        kms = _head_lane_masks(tk, nsub)
        vms = _head_row_masks(tk)
        base = 0
        for k_ref, vt_ref in srcs:
            nblk = k_ref.shape[1] // tk

            def body(i, carry, k_ref=k_ref, base=base):
                r0 = pl.multiple_of(i * tk, tk)
                for j in range(HEADS_PER_GROUP):
                    u = j // hpu
                    kb_ref[base + i, j * tk:(j + 1) * tk, :] = \
                        k_ref[0, pl.ds(r0, tk), u * MXU_DIM:(u + 1) * MXU_DIM] * kms[j]
                return carry

            lax.fori_loop(0, nblk, body, 0)
            for i in range(nblk):
                for j in range(HEADS_PER_GROUP):
                    vbt_ref[base + i, :, j * tk:(j + 1) * tk] = vt_ref[0, :, i * tk:(i + 1) * tk] * vms[j]
            base += nblk

    def block(i, state):
        kb, vbt = kb_ref[i], vbt_ref[i]
        new_state = []
        first = state is None
        for h in range(nchain):
            cs = slice(h * tqc, (h + 1) * tqc)
            parts = [jnp.dot(kb[u * hpu * tk:(u + 1) * hpu * tk, :], qt_ref[0, u * MXU_DIM:(u + 1) * MXU_DIM, cs],
                             preferred_element_type=F32) for u in range(nsub)]
            ps, alphas, m_new, l_new = [], [], [], []
            for j in range(HEADS_PER_GROUP):
                sj = parts[j // hpu][(j % hpu) * tk:(j % hpu + 1) * tk, :]
                mj = jnp.max(sj, axis=0, keepdims=True)
                if not first:
                    m, l = state[h]
                    mj = jnp.maximum(m[j], mj)
                    a = jnp.exp2(m[j] - mj)
                    alphas.append(a)
                p = jnp.exp2(sj - mj)
                lj = jnp.sum(p, axis=0, keepdims=True)
                l_new.append(lj if first else a * l[j] + lj)
                m_new.append(mj)
                ps.append(p.astype(BF16))
            pv = jnp.dot(vbt, jnp.concatenate(ps, axis=0), preferred_element_type=F32)
            acc_ref[:, cs] = pv if first else acc_ref[:, cs] * _rows4(alphas, tqc) + pv
            new_state.append((m_new, l_new))
        return new_state

    state = None
    for i in range(nb):
        state = block(i, state)
    for h in range(nchain):
        cs = slice(h * tqc, (h + 1) * tqc)
        ot = acc_ref[:, cs] * _rows4([1.0 / v for v in state[h][1]], tqc)
        o_ref[0, cs, :] = ot.T.astype(BF16)


def _flash(qt, ksrcs, vtsrcs, nsub, tq, tk):
    bsz, qw, nq = qt.shape
    kw = nsub * MXU_DIM
    groups = qw // kw
    tq = min(tq, nq)
    nchain = max(tq // MXU_DIM, 1)
    in_specs = [pl.BlockSpec((1, kw, tq), lambda b, g, i: (b, g, i))]
    args = [qt]
    nb = 0
    for k, vt in zip(ksrcs, vtsrcs):
        lk = k.shape[1]
        assert lk % tk == 0
        nb += lk // tk
        in_specs += [pl.BlockSpec((1, lk, kw), lambda b, g, i: (b, 0, g)),
                     pl.BlockSpec((1, MXU_DIM, lk), lambda b, g, i: (b, g, 0))]
        args += [k, vt]
    scratch = [pltpu.VMEM((nb, HEADS_PER_GROUP * tk, MXU_DIM), BF16),
               pltpu.VMEM((nb, MXU_DIM, HEADS_PER_GROUP * tk), BF16),
               pltpu.VMEM((MXU_DIM, tq), F32)]
    return pl.pallas_call(
        functools.partial(_flash_kernel, nsub=nsub, tk=tk, nsrc=len(ksrcs), nchain=nchain),
        grid=(bsz, groups, nq // tq),
        in_specs=in_specs,
        out_specs=pl.BlockSpec((1, tq, MXU_DIM), lambda b, g, i: (b, i, g)),
        out_shape=jax.ShapeDtypeStruct((bsz, nq, groups * MXU_DIM), BF16),
        scratch_shapes=scratch,
        compiler_params=_params(("parallel", "parallel", "arbitrary")),
        name="flash_attn",
    )(*args)


HALO = 16


def _ffn_kernel(xp_ref, x_ref, xn_ref, oap_ref, oa_ref, oan_ref, obp_ref, ob_ref, obn_ref, wo_ref, gtm_ref,
                g_ref, sh_ref, sc_ref, gt_ref, wup_ref, cw_ref, wdn_ref, fg_ref,
                o_ref, h_ref, om_ref, *, final_norm, nf):
    i = pl.program_id(1)
    last = pl.num_programs(1) - 1
    tm = x_ref.shape[1]
    fdim = wdn_ref.shape[0]
    tf = fdim // nf
    half = oa_ref.shape[2]
    rows = tm + 2 * HALO
    own = slice(HALO, HALO + tm)
    for r, (a_ref, b_ref) in ((slice(0, HALO), (oap_ref, obp_ref)), (own, (oa_ref, ob_ref)),
                              (slice(HALO + tm, rows), (oan_ref, obn_ref))):
        om_ref[r, :half] = a_ref[0]
        om_ref[r, half:] = b_ref[0]
    ym = gtm_ref[0] * jnp.dot(om_ref[...], wo_ref[...], preferred_element_type=F32)
    x1 = x_ref[0] + ym[own]
    gain, shift, scale = g_ref[...], sh_ref[0], sc_ref[0]
    keep_p = jnp.where(i > 0, 1.0, 0.0)
    keep_n = jnp.where(i < last, 1.0, 0.0)
    h_ref[0:HALO, :] = (_norm_mod(xp_ref[0] + ym[0:HALO], gain, shift, scale) * keep_p).astype(BF16)
    h_ref[own, :] = _norm_mod(x1, gain, shift, scale).astype(BF16)
    h_ref[HALO + tm:, :] = (_norm_mod(xn_ref[0] + ym[HALO + tm:], gain, shift, scale) * keep_n).astype(BF16)

    def conv(u, f, part):
        w = cw_ref[:, cols(part, f)]
        up = pltpu.roll(u, 1, axis=0)[HALO:HALO + tm]
        un = pltpu.roll(u, rows - 1, axis=0)[HALO:HALO + tm]
        return up * w[0:1] + u[HALO:HALO + tm] * w[1:2] + un * w[2:3] + w[3:4]

    def cols(part, f):
        return slice(part * fdim + f * tf, part * fdim + (f + 1) * tf)

    h = h_ref[...]
    def up(f):
        if nf > 1:
            return (jnp.dot(h, wup_ref[:, cols(0, f)], preferred_element_type=F32),
                    jnp.dot(h, wup_ref[:, cols(1, f)], preferred_element_type=F32))
        r = jnp.dot(h, wup_ref[...], preferred_element_type=F32)
        return r[:, :fdim], r[:, fdim:]
    y = None
    ug, uv = up(0)
    for f in range(nf):
        nxt = up(f + 1) if f + 1 < nf else None
        g = conv(ug, f, 0)
        v = conv(uv, f, 1)
        a = (g * (1.0 / (1.0 + jnp.exp(-g))) * v).astype(BF16)
        yf = jnp.dot(a, wdn_ref[f * tf:(f + 1) * tf, :], preferred_element_type=F32)
        y = yf if y is None else y + yf
        if nxt is not None:
            ug, uv = nxt
    out = x1 + gt_ref[0] * y
    if final_norm:
        out = out * lax.rsqrt(jnp.mean(out * out, axis=-1, keepdims=True) + NORM_EPS) * fg_ref[...]
    o_ref[0] = out


def _mix_ffn(x, oa, ob, w_out, gain, mods3, mrow, wup, cw, wdn, final_gain, tm):
    bsz, n, d = x.shape
    half = oa.shape[2]
    tm = min(tm, n)
    hb = tm // HALO
    nh = n // HALO
    final_norm = final_gain is not None
    fg = final_gain if final_norm else gain
    prev = lambda w: pl.BlockSpec((1, HALO, w), lambda b, i: (b, jnp.maximum(i * hb - 1, 0), 0))
    main = lambda w: pl.BlockSpec((1, tm, w), lambda b, i: (b, i, 0))
    nxt = lambda w: pl.BlockSpec((1, HALO, w), lambda b, i: (b, jnp.minimum((i + 1) * hb, nh - 1), 0))
    return pl.pallas_call(
        functools.partial(_ffn_kernel, final_norm=final_norm, nf=FFN_CHUNKS),
        grid=(bsz, n // tm),
        in_specs=[prev(d), main(d), nxt(d), prev(half), main(half), nxt(half), prev(half), main(half), nxt(half),
                  _resident(w_out.shape, lambda b, i: (0, 0)), _mod_spec(d, 2, mrow),
                  pl.BlockSpec((1, d), lambda b, i: (0, 0)),
                  _mod_spec(d, 3, mrow), _mod_spec(d, 4, mrow), _mod_spec(d, 5, mrow),
                  _resident(wup.shape, lambda b, i: (0, 0)),
                  _resident(cw.shape, lambda b, i: (0, 0)),
                  _resident(wdn.shape, lambda b, i: (0, 0)),
                  pl.BlockSpec((1, d), lambda b, i: (0, 0))],
        out_specs=main(d),
        out_shape=jax.ShapeDtypeStruct((bsz, n, d), F32),
        scratch_shapes=[pltpu.VMEM((tm + 2 * HALO, d), BF16), pltpu.VMEM((tm + 2 * HALO, 2 * half), BF16)],
        compiler_params=_params(("parallel", "parallel")),
        name="mix_ffn",
    )(x, x, x, oa, oa, oa, ob, ob, ob, w_out, mods3, gain.reshape(1, d), mods3, mods3, mods3, wup, cw, wdn,
      fg.reshape(1, d))


OD_Q, OD_C, OD_K, OD_V, OD_END = 0, 512, 2048, 2176, 2304
SUB = 8


def _prep_odd_kernel(xp_ref, x_ref, xn_ref, g_ref, sh_ref, sc_ref, win_ref, cw_ref, chd_ref, shd_ref,
                     q_ref, k_ref, v_ref, x0_ref, u_ref, h_ref, z_ref, *, scale):
    i = pl.program_id(1)
    last = pl.num_programs(1) - 1
    tm = x_ref.shape[1]
    rows = tm + 2 * HALO
    own = slice(HALO, HALO + tm)
    cwid = (OD_K - OD_C) // 3
    gain, shift, mscale = g_ref[...], sh_ref[0], sc_ref[0]
    keep_p = jnp.where(i > 0, 1.0, 0.0)
    keep_n = jnp.where(i < last, 1.0, 0.0)
    h_ref[0:HALO, :] = (_norm_mod(xp_ref[0], gain, shift, mscale) * keep_p).astype(BF16)
    h_ref[own, :] = _norm_mod(x_ref[0], gain, shift, mscale).astype(BF16)
    h_ref[HALO + tm:, :] = (_norm_mod(xn_ref[0], gain, shift, mscale) * keep_n).astype(BF16)
    z_ref[...] = jnp.dot(h_ref[...], win_ref[...], preferred_element_type=F32)
    for c in range((OD_C - OD_Q) // LANES):
        sl = slice(OD_Q + c * LANES, OD_Q + (c + 1) * LANES)
        y = _rope_lanes(z_ref[own, sl], chd_ref[...], shd_ref[...])
        q_ref[0, sl, :] = (y * scale).T.astype(BF16)
    k = _rope_lanes(z_ref[own, OD_K:OD_V], chd_ref[...], shd_ref[...])
    d0, d1 = _dup_heads(k)
    for c, dk in enumerate((d0, d0, d1, d1)):
        k_ref[0, :, c * LANES:(c + 1) * LANES] = dk.astype(BF16)
    d0, d1 = (d.T.astype(BF16) for d in _dup_heads(z_ref[own, OD_V:OD_END]))
    for c, dv in enumerate((d0, d0, d1, d1)):
        v_ref[0, c * LANES:(c + 1) * LANES, :] = dv

    def conv(part):
        sl = slice(OD_C + part * cwid, OD_C + (part + 1) * cwid)
        csl = slice(part * cwid, (part + 1) * cwid)
        ue = z_ref[:, sl]
        up = pltpu.roll(ue, 1, axis=0)[own]
        un = pltpu.roll(ue, rows - 1, axis=0)[own]
        return up * cw_ref[0:1, csl] + ue[own] * cw_ref[1:2, csl] + un * cw_ref[2:3, csl] + cw_ref[3:4, csl]

    x0_ref[0] = conv(0).astype(BF16)
    u_ref[0] = (conv(2) * conv(1)).astype(BF16)


def _prep_odd(x, gain, mods3, w_in, cw, chd, shd, tm):
    bsz, n, d = x.shape
    cols = w_in.shape[1]
    tm = min(tm, n)
    hb = tm // HALO
    nh = n // HALO
    cwid = (OD_K - OD_C) // 3
    row_blk = lambda w: pl.BlockSpec((1, tm, w), lambda b, i: (b, i, 0))
    col_blk = lambda w: pl.BlockSpec((1, w, tm), lambda b, i: (b, 0, i))
    slab = 4 * LANES
    return pl.pallas_call(
        functools.partial(_prep_odd_kernel, scale=HEAD_DIM ** -0.5 * LOG2E),
        grid=(bsz, n // tm),
        in_specs=[pl.BlockSpec((1, HALO, d), lambda b, i: (b, jnp.maximum(i * hb - 1, 0), 0)),
                  pl.BlockSpec((1, tm, d), lambda b, i: (b, i, 0)),
                  pl.BlockSpec((1, HALO, d), lambda b, i: (b, jnp.minimum((i + 1) * hb, nh - 1), 0)),
                  pl.BlockSpec((1, d), lambda b, i: (0, 0)),
                  _mod_spec(d, 0, None), _mod_spec(d, 1, None),
                  _resident(w_in.shape, lambda b, i: (0, 0)),
                  pl.BlockSpec(cw.shape, lambda b, i: (0, 0)),
                  pl.BlockSpec((tm, LANES), lambda b, i: (i, 0)),
                  pl.BlockSpec((tm, LANES), lambda b, i: (i, 0))],
        out_specs=[col_blk(slab), row_blk(slab), col_blk(slab), row_blk(cwid), row_blk(cwid)],
        out_shape=[jax.ShapeDtypeStruct((bsz, slab, n), BF16), jax.ShapeDtypeStruct((bsz, n, slab), BF16),
                   jax.ShapeDtypeStruct((bsz, slab, n), BF16), jax.ShapeDtypeStruct((bsz, n, cwid), BF16),
                   jax.ShapeDtypeStruct((bsz, n, cwid), BF16)],
        scratch_shapes=[pltpu.VMEM((tm + 2 * HALO, d), BF16), pltpu.VMEM((tm + 2 * HALO, cols), F32)],
        compiler_params=_params(("parallel", "parallel")),
        name="prep_odd",
    )(x, x, x, gain.reshape(1, d), mods3, mods3, w_in, cw, chd, shd)


def _kv_ctx_odd_kernel(z_ref, k_ref, vt_ref):
    d0, d1 = _dup_heads(z_ref[0, :, 0:LANES])
    for c, dk in enumerate((d0, d0, d1, d1)):
        k_ref[0, :, c * LANES:(c + 1) * LANES] = dk.astype(BF16)
    d0, d1 = (d.T.astype(BF16) for d in _dup_heads(z_ref[0, :, LANES:2 * LANES]))
    for c, dv in enumerate((d0, d0, d1, d1)):
        vt_ref[0, c * LANES:(c + 1) * LANES, :] = dv


def _kv_ctx_odd(z):
    bsz, n, cols = z.shape
    slab = 4 * LANES
    return pl.pallas_call(
        _kv_ctx_odd_kernel,
        grid=(bsz,),
        in_specs=[pl.BlockSpec((1, n, cols), lambda b: (b, 0, 0))],
        out_specs=[pl.BlockSpec((1, n, slab), lambda b: (b, 0, 0)), pl.BlockSpec((1, slab, n), lambda b: (b, 0, 0))],
        out_shape=[jax.ShapeDtypeStruct((bsz, n, slab), BF16), jax.ShapeDtypeStruct((bsz, slab, n), BF16)],
        compiler_params=_params(("parallel",)),
        name="kv_ctx_odd",
    )(z)


WIN_QB = 2
WIN_TILES = 16


def _window_kernel(qt_ref, kc_ref, vct_ref, kl_ref, vlt_ref, bias_ref, sink_ref, o_ref, kbc, vbct, kbl, vblt,
                   *, lc, nblk, qb, nch):
    qi = pl.program_id(2)
    tq = qb * Q_BLOCK
    nsp = qb + 2
    blk = HEADS_PER_GROUP * Q_BLOCK

    @pl.when(qi == 0)
    def _build():
        kms = _head_lane_masks(lc, 1)
        vms = _head_row_masks(lc)
        for j in range(HEADS_PER_GROUP):
            kbc[j * lc:(j + 1) * lc, :] = kc_ref[0] * kms[j]
            vbct[:, j * lc:(j + 1) * lc] = vct_ref[0] * vms[j]
        kms = _head_lane_masks(Q_BLOCK, 1)
        vms = _head_row_masks(Q_BLOCK)
        for e in (0, nblk + 1):
            kbl[e] = jnp.zeros((blk, MXU_DIM), BF16)
            vblt[e] = jnp.zeros((MXU_DIM, blk), BF16)

        def body(i, carry):
            r0 = pl.multiple_of(i * Q_BLOCK, Q_BLOCK)
            for j in range(HEADS_PER_GROUP):
                kbl[i + 1, j * Q_BLOCK:(j + 1) * Q_BLOCK, :] = kl_ref[0, pl.ds(r0, Q_BLOCK), :] * kms[j]
            return carry

        lax.fori_loop(0, nblk, body, 0)
        for i in range(nblk):
            for j in range(HEADS_PER_GROUP):
                vblt[i + 1, :, j * Q_BLOCK:(j + 1) * Q_BLOCK] = vlt_ref[0, :, i * Q_BLOCK:(i + 1) * Q_BLOCK] * vms[j]

    ntile = nblk // qb
    for h in range(nch):
        t = qi * nch + h
        qt = qt_ref[0, :, h * tq:(h + 1) * tq]
        s_c = jnp.dot(kbc[...], qt, preferred_element_type=F32)
        kspan = kbl[pl.ds(qb * t, nsp)].reshape(nsp * blk, MXU_DIM)
        s_s = jnp.dot(kspan, qt, preferred_element_type=F32) + bias_ref[...]
        edge = {0: jnp.where(t == 0, NEG_INF, 0.0), nsp - 1: jnp.where(t == ntile - 1, NEG_INF, 0.0)}
        pcs, pss, invs = [], [[None] * HEADS_PER_GROUP for _ in range(nsp)], []
        for j in range(HEADS_PER_GROUP):
            sink = sink_ref[0, j][0:1, :]
            segs = [s_c[j * lc:(j + 1) * lc, :]]
            for b in range(nsp):
                sg = s_s[b * blk + j * Q_BLOCK: b * blk + (j + 1) * Q_BLOCK, :]
                segs.append(sg + edge[b] if b in edge else sg)
            m = sink
            for sg in segs:
                m = jnp.maximum(m, jnp.max(sg, axis=0, keepdims=True))
            ps = [jnp.exp2(sg - m) for sg in segs]
            den = jnp.exp2(sink - m)
            for p in ps:
                den = den + jnp.sum(p, axis=0, keepdims=True)
            invs.append(1.0 / den)
            pcs.append(ps[0].astype(BF16))
            for b in range(nsp):
                pss[b][j] = ps[1 + b].astype(BF16)
        ot = jnp.dot(vbct[...], jnp.concatenate(pcs, axis=0), preferred_element_type=F32)
        for b in range(nsp):
            ot = ot + jnp.dot(vblt[qb * t + b], jnp.concatenate(pss[b], axis=0), preferred_element_type=F32)
        o_ref[0, h * tq:(h + 1) * tq, :] = (ot * _rows4(invs, tq)).T.astype(BF16)


@functools.lru_cache(maxsize=None)
def _window_bias(qb):
    nsp = qb + 2
    row = np.arange(nsp * HEADS_PER_GROUP * Q_BLOCK)[:, None]
    r = np.arange(qb * Q_BLOCK)[None, :]
    b = row // (HEADS_PER_GROUP * Q_BLOCK)
    jj = b * Q_BLOCK + row % Q_BLOCK
    band = (jj >= r) & (jj <= r + 2 * WINDOW)
    return np.where(band, 0.0, NEG_INF).astype(np.float32)


def _window(qt, kc, vct, kl, vlt, sink):
    bsz, qw, n = qt.shape
    groups = qw // MXU_DIM
    lc = kc.shape[1]
    nblk = n // Q_BLOCK
    qb = min(WIN_QB, nblk)
    tq = qb * Q_BLOCK
    nch = next(c for c in (WIN_TILES, 2, 1) if n % (c * tq) == 0)
    nq = n // (tq * nch)
    blk = HEADS_PER_GROUP * Q_BLOCK
    bias = jnp.asarray(_window_bias(qb))
    sink_t = jnp.broadcast_to((sink.astype(F32) * LOG2E).reshape(groups, HEADS_PER_GROUP, 1, 1),
                              (groups, HEADS_PER_GROUP, SUB, tq))

    return pl.pallas_call(
        functools.partial(_window_kernel, lc=lc, nblk=nblk, qb=qb, nch=nch),
        grid=(bsz, groups, nq),
        in_specs=[pl.BlockSpec((1, MXU_DIM, tq * nch), lambda b, g, i: (b, g, i)),
                  pl.BlockSpec((1, lc, MXU_DIM), lambda b, g, i: (b, 0, g)),
                  pl.BlockSpec((1, MXU_DIM, lc), lambda b, g, i: (b, g, 0)),
                  pl.BlockSpec((1, n, MXU_DIM), lambda b, g, i: (b, 0, g)),
                  pl.BlockSpec((1, MXU_DIM, n), lambda b, g, i: (b, g, 0)),
                  _resident(bias.shape, lambda b, g, i: (0, 0)),
                  pl.BlockSpec((1, HEADS_PER_GROUP, SUB, tq), lambda b, g, i: (g, 0, 0, 0))],
        out_specs=pl.BlockSpec((1, tq * nch, MXU_DIM), lambda b, g, i: (b, i, g)),
        out_shape=jax.ShapeDtypeStruct((bsz, n, qw), BF16),
        scratch_shapes=[pltpu.VMEM((HEADS_PER_GROUP * lc, MXU_DIM), BF16),
                        pltpu.VMEM((MXU_DIM, HEADS_PER_GROUP * lc), BF16),
                        pltpu.VMEM((nblk + 2, blk, MXU_DIM), BF16),
                        pltpu.VMEM((nblk + 2, MXU_DIM, blk), BF16)],
        compiler_params=_params(("parallel", "parallel", "arbitrary")),
        name="window_attn",
    )(qt, kc, vct, kl, vlt, bias, sink_t)


FFT_N2 = 128


def _filter_kernel(z_ref, w1_ref, b1_ref, w2_ref, b2_ref, w3_ref, b3_ref, w4_ref, fr_ref, dl_ref, o_ref):
    def lin(a, w_ref):
        a_hi, a_lo = _split_bf16(a)
        w_hi, w_lo = _split_bf16(w_ref[...])
        return _dot3(a_hi, a_lo, w_hi, w_lo)

    z = z_ref[...]
    h = b1_ref[...]
    for e in range(C_EMB_DIM):
        h = h + z[:, e:e + 1] * w1_ref[e:e + 1, :]
    h = jnp.sin(fr_ref[0:1] * h)
    h = jnp.sin(fr_ref[1:2] * (lin(h, w2_ref) + b2_ref[...]))
    h = jnp.sin(fr_ref[2:3] * (lin(h, w3_ref) + b3_ref[...]))
    h = lin(h, w4_ref)
    t = z[:, 0:1]
    cw = dl_ref.shape[1]
    dec = jnp.exp(-t * dl_ref[...])
    row = lax.broadcasted_iota(jnp.int32, dec.shape, 0) + pl.program_id(0) * z.shape[0]
    o_ref[0] = h[:, :cw] * dec
    o_ref[1] = jnp.where(row == 0, 0.0, h[:, cw:] * dec)


def _filters(zfeat, w1, b1, w2, b2, w3, b3, w4, freq, deltas, tm):
    n = zfeat.shape[0]
    tm = min(tm, n)
    cw = deltas.shape[1]
    full = lambda a: pl.BlockSpec(a.shape, lambda i: (0,) * a.ndim)
    ops = (w1, b1, w2, b2, w3, b3, w4, freq, deltas)
    return pl.pallas_call(
        _filter_kernel,
        grid=(n // tm,),
        in_specs=[pl.BlockSpec((tm, zfeat.shape[1]), lambda i: (i, 0))] + [full(a) for a in ops],
        out_specs=pl.BlockSpec((2, tm, cw), lambda i: (0, i, 0)),
        out_shape=jax.ShapeDtypeStruct((2, n, cw), F32),
        compiler_params=_params(("parallel",)),
        name="hyena_filters",
    )(zfeat, *ops)


def _dft_first_kernel(f_ref, x_ref, o_ref):
    _, n1h, tm2, cw = x_ref.shape
    x = x_ref[0].reshape(n1h * tm2, cw).astype(BF16)
    r = jnp.dot(f_ref[...], x, preferred_element_type=F32)
    o_ref[0] = r.astype(BF16).reshape(o_ref.shape[1:])


def _dft_first(f1, x):
    bsz, n1h, n2, cw = x.shape
    tm2 = HALO
    nkp = f1.shape[0] // (2 * tm2)
    return pl.pallas_call(
        _dft_first_kernel,
        grid=(bsz, n2 // tm2),
        in_specs=[pl.BlockSpec(f1.shape, lambda b, j: (0, 0)),
                  pl.BlockSpec((1, n1h, tm2, cw), lambda b, j: (b, 0, j, 0))],
        out_specs=pl.BlockSpec((1, 2, nkp, tm2, cw), lambda b, j: (b, 0, 0, j, 0)),
        out_shape=jax.ShapeDtypeStruct((bsz, 2, nkp, n2, cw), BF16),
        compiler_params=_params(("parallel", "parallel")),
        name="dft_first",
    )(f1, x)


def _dft_last_gate_kernel(f_ref, b_ref, x0_ref, u_ref, bias_ref, o_ref):
    _, _, nkp, tm2, cw = b_ref.shape
    y = jnp.dot(f_ref[...], b_ref[0].reshape(2 * nkp * tm2, cw), preferred_element_type=F32)
    y = y.reshape(x0_ref.shape[1:])
    o_ref[0] = (x0_ref[0].astype(F32) * (y + u_ref[0].astype(F32) * bias_ref[...])).astype(BF16)


def _dft_last_gate(f2, b, x0, u, bias):
    bsz, _, nkp, n2, cw = b.shape
    tm2 = HALO
    n1h = f2.shape[0] // tm2
    tblk = pl.BlockSpec((1, n1h, tm2, cw), lambda b_, j: (b_, 0, j, 0))
    return pl.pallas_call(
        _dft_last_gate_kernel,
        grid=(bsz, n2 // tm2),
        in_specs=[pl.BlockSpec(f2.shape, lambda b_, j: (0, 0)),
                  pl.BlockSpec((1, 2, nkp, tm2, cw), lambda b_, j: (b_, 0, 0, j, 0)),
                  tblk, tblk, pl.BlockSpec((1, cw), lambda b_, j: (0, 0))],
        out_specs=tblk,
        out_shape=jax.ShapeDtypeStruct((bsz, n1h, n2, cw), BF16),
        compiler_params=_params(("parallel", "parallel")),
        name="dft_last_gate",
    )(f2, b, x0, u, bias.reshape(1, cw))


def _spec_fwd(a, m_ref):
    return jnp.dot(m_ref[0], a.reshape(2 * FFT_N2, a.shape[-1]), preferred_element_type=F32)


def _filter_spec_kernel(a_ref, m_ref, o_ref):
    for kk in range(a_ref.shape[2]):
        xf = _spec_fwd(a_ref[0, :, kk], m_ref.at[kk:kk + 1])
        xb = _spec_fwd(a_ref[1, :, kk], m_ref.at[kk:kk + 1])
        o_ref[0, kk] = xf[:FFT_N2] + xb[:FFT_N2]
        o_ref[1, kk] = xf[FFT_N2:] - xb[FFT_N2:]


def _conv_spec_kernel(a_ref, k_ref, m_ref, i_ref, o_ref):
    kr, ki = k_ref[0, 0], k_ref[1, 0]
    for b in range(a_ref.shape[0]):
        x = _spec_fwd(a_ref[b, :, 0], m_ref)
        xr, xi = x[:FFT_N2], x[FFT_N2:]
        y = jnp.concatenate([xr * kr - xi * ki, xr * ki + xi * kr], axis=0).astype(BF16)
        bv = jnp.dot(i_ref[0], y, preferred_element_type=F32)
        o_ref[b, :, 0] = bv.astype(BF16).reshape(2, FFT_N2, bv.shape[-1])


def _filter_spec(a, m_fwd):
    _, _, nkp, _, cw = a.shape
    return pl.pallas_call(
        _filter_spec_kernel,
        grid=(nkp // SUB,),
        in_specs=[pl.BlockSpec((2, 2, SUB, FFT_N2, cw), lambda k: (0, 0, k, 0, 0)),
                  pl.BlockSpec((SUB, 2 * FFT_N2, 2 * FFT_N2), lambda k: (k, 0, 0))],
        out_specs=pl.BlockSpec((2, SUB, FFT_N2, cw), lambda k: (0, k, 0, 0)),
        out_shape=jax.ShapeDtypeStruct((2, nkp, FFT_N2, cw), F32),
        compiler_params=_params(("parallel",)),
        name="filter_spectrum",
    )(a, m_fwd)


def _conv_spec(a, kspec, m_fwd, m_inv):
    bsz, _, nkp, _, cw = a.shape
    blk = pl.BlockSpec((bsz, 2, 1, FFT_N2, cw), lambda k: (0, 0, k, 0, 0))
    mat = pl.BlockSpec((1, 2 * FFT_N2, 2 * FFT_N2), lambda k: (k, 0, 0))
    return pl.pallas_call(
        _conv_spec_kernel,
        grid=(nkp,),
        in_specs=[blk, pl.BlockSpec((2, 1, FFT_N2, cw), lambda k: (0, k, 0, 0)), mat, mat],
        out_specs=blk,
        out_shape=jax.ShapeDtypeStruct(a.shape, BF16),
        compiler_params=_params(("parallel",)),
        name="conv_spectrum",
    )(a, kspec, m_fwd, m_inv)


@functools.lru_cache(maxsize=None)
def _dft_tables(n):
    nfft = 2 * n
    n2 = FFT_N2
    n1 = nfft // n2
    n1h = n1 // 2
    nk = n1h + 1
    nkp = -(-nk // SUB) * SUB

    def cs(num, den):
        ang = (num % den).astype(np.float64) * (2.0 * math.pi / den)
        return np.cos(ang), np.sin(ang)

    def pad_k(t, axis):
        widths = [(0, 0)] * t.ndim
        widths[axis] = (0, nkp - nk)
        return np.pad(t, widths)

    k1 = np.arange(nk, dtype=np.int64)
    c1, s1 = cs(k1[:, None] * np.arange(n1h)[None, :], n1)
    f1 = np.concatenate([pad_k(c1, 0), pad_k(-s1, 0)], axis=0)
    kk = k1[:, None, None] + n1 * np.arange(n2)[None, :, None]
    cg, sg = cs(kk * np.arange(n2)[None, None, :], nfft)
    g_re, g_im = cg, -sg
    m_fwd = np.concatenate([np.concatenate([g_re, -g_im], axis=2),
                            np.concatenate([g_im, g_re], axis=2)], axis=1)
    gt_re, gt_im = np.swapaxes(g_re, 1, 2), np.swapaxes(g_im, 1, 2)
    m_inv = np.concatenate([np.concatenate([gt_re, gt_im], axis=2),
                            np.concatenate([-gt_im, gt_re], axis=2)], axis=1)
    wk = np.where((k1 == 0) | (k1 == n1h), 1.0, 2.0)[None, :] * (1.0 / nfft)
    c2, s2 = cs(np.arange(n1h)[:, None] * k1[None, :], n1)
    f2 = np.concatenate([pad_k(c2 * wk, 1), pad_k(-s2 * wk, 1)], axis=1)
    eye = np.eye(HALO)
    return tuple(t.astype(np.float32) for t in (np.kron(f1, eye), pad_k(m_fwd, 0), pad_k(m_inv, 0), np.kron(f2, eye)))


def _hyena_long_conv(u, x0, bias, hcat, tabs):
    f1, m_fwd, m_inv, f2 = tabs
    bsz, n, cw = u.shape
    n2 = FFT_N2
    n1h = n // n2
    kspec = _filter_spec(_dft_first(f1, hcat.reshape(2, n1h, n2, cw)), m_fwd)
    u4 = u.reshape(bsz, n1h, n2, cw)
    b_u = _conv_spec(_dft_first(f1, u4), kspec, m_fwd, m_inv)
    return _dft_last_gate(f2, b_u, x0.reshape(bsz, n1h, n2, cw), u4, bias).reshape(bsz, n, cw)


def _axial_angles(rows, rope_dim):
    row_idx = np.repeat(np.arange(rows), GRID_W).astype(np.float64)
    col_idx = np.tile(np.arange(GRID_W), rows).astype(np.float64)
    d_axis = rope_dim // 2
    inv_freq = ROPE_THETA ** (-np.arange(0, d_axis, 2, dtype=np.float64) / d_axis)
    ang = np.concatenate([row_idx[:, None] * inv_freq, col_idx[:, None] * inv_freq], axis=-1)
    return np.cos(ang), np.sin(ang)


@functools.lru_cache(maxsize=None)
def _rope_tables(n):
    rows = n // GRID_W
    sign = np.tile(np.array([-1.0, 1.0]), LANES // 2)[None, :]
    cos, sin = _axial_angles(rows, HEAD_DIM)
    chd = np.tile(np.repeat(cos, 2, axis=1), (1, LANES // HEAD_DIM))
    shd = np.tile(np.repeat(sin, 2, axis=1), (1, LANES // HEAD_DIM)) * sign
    cos, sin = _axial_angles(rows, B_ROPE)
    ones = np.ones((n, LANES - 2 * B_ROPE))
    cb = np.concatenate([np.tile(np.repeat(cos, 2, axis=1), (1, 2)), ones], axis=1)
    sb = np.concatenate([np.tile(np.repeat(sin, 2, axis=1), (1, 2)), 0.0 * ones], axis=1) * sign
    return tuple(t.astype(np.float32) for t in (chd, shd, cb, sb))


@functools.lru_cache(maxsize=None)
def _hyena_features(n, cwid):
    t = np.linspace(0.0, 1.0, n)[:, None]
    wpos = 2 * math.pi * np.arange(n)[:, None] / n
    fb = np.linspace(1e-4, C_BANDS - 1, C_BANDS)[None, :]
    zfeat = np.concatenate([t, np.cos(fb * wpos), -np.sin(fb * wpos), np.zeros((n, SUB - C_EMB_DIM))], axis=-1)
    deltas = np.abs(np.linspace(C_MIN_DECAY, C_MAX_DECAY, cwid))[None, :]
    return zfeat.astype(np.float32), deltas.astype(np.float32)


def _even_weights(w_in, a_qn, a_kn, b_qn, b_w_uq, b_kvn, b_w_ukv):
    d = w_in.shape[0]
    aq_w = w_in.shape[1] - (B_Q_RANK + 2 * A_KV_HEADS * HEAD_DIM + B_KV_RANK + B_ROPE)
    o = [0, aq_w, aq_w + B_Q_RANK]
    o += [o[-1] + A_KV_HEADS * HEAD_DIM, o[-1] + 2 * A_KV_HEADS * HEAD_DIM]
    o += [o[-1] + B_KV_RANK, o[-1] + B_KV_RANK + B_ROPE]
    kr = w_in[:, o[5]:o[6]]
    w_aug = jnp.concatenate([w_in[:, :o[5]], kr, kr, jnp.zeros((d, LANES - 2 * B_ROPE), w_in.dtype)], axis=1)
    heads = b_w_uq.shape[1] // (B_NOPE + B_ROPE)
    uq = b_w_uq.reshape(B_Q_RANK, heads // 2, 2, B_NOPE + B_ROPE)
    wuq = jnp.concatenate([uq[:, :, 0, :B_NOPE], uq[:, :, 1, :B_NOPE], uq[:, :, 0, B_NOPE:], uq[:, :, 1, B_NOPE:],
                           jnp.zeros((B_Q_RANK, heads // 2, MXU_DIM - 2 * (B_NOPE + B_ROPE)), b_w_uq.dtype)], axis=2)
    wuq = wuq.reshape(B_Q_RANK, heads // 2 * MXU_DIM)
    ukv = b_w_ukv.reshape(B_KV_RANK, heads, B_NOPE + B_VDIM)
    wukn = ukv[:, :, :B_NOPE].reshape(B_KV_RANK, heads * B_NOPE)
    wuv = ukv[:, :, B_NOPE:].reshape(B_KV_RANK, heads * B_VDIM)
    gains = (jnp.tile(a_qn, aq_w // HEAD_DIM)[None, :], jnp.tile(a_kn, A_KV_HEADS)[None, :],
             b_qn[None, :], b_kvn[None, :])
    return w_aug.astype(BF16), gains + (wuq.astype(BF16), wukn.astype(BF16), wuv.astype(BF16))


def _ffn_weights(w_up, conv_w, conv_b, w_down):
    cw = jnp.concatenate([conv_w, conv_b[None, :], jnp.zeros((SUB - 4, conv_w.shape[1]), conv_w.dtype)], axis=0)
    return w_up.astype(BF16), cw, w_down.astype(BF16)


FFN_CHUNKS = 1


def kernel(x, c, ctx, c_ctx, w_mod, b_mod, norm_mix, norm_ffn, ev_w_in, ev_w_out, a_q_norm, a_k_norm, b_q_norm, b_w_uq, b_kv_norm, b_w_ukv, od_w_in, od_w_out, d_sink, c_conv_w, c_conv_b, c_filt_w1, c_filt_b1, c_filt_w2, c_filt_b2, c_filt_w3, c_filt_b3, c_filt_w4, c_filt_freq, c_bias, ffn_w_up, ffn_conv_w, ffn_conv_b, ffn_w_down, final_norm):
    bsz, n, d = x.shape
    depth = w_mod.shape[0]
    assert depth == 2 and n % Q_BLOCK == 0 and d % LANES == 0
    tm = ROW_TILE
    rows = -(-(bsz + 1) // SUB) * SUB
    cvec = jnp.concatenate([c, c_ctx[None, :], jnp.zeros((rows - bsz - 1, d), F32)], axis=0)
    mods = _mods(cvec, w_mod, b_mod)
    chd, shd, cb, sb = (jnp.asarray(t) for t in _rope_tables(n))
    tabs = (chd, shd, cb, sb)

    m3 = mods[0].reshape(rows, 1, N_MOD * d)
    w_aug, prep_w = _even_weights(ev_w_in[0], a_q_norm[0], a_k_norm[0], b_q_norm[0], b_w_uq[0],
                                  b_kv_norm[0], b_w_ukv[0])
    qa_l, ka_l, va_l, qb_l, kb_l, vb_l = _prep_even(x, norm_mix[0], m3, None, w_aug, prep_w, tabs, True, tm)
    qa_c, ka_c, va_c, qb_c, kb_c, vb_c = _prep_even(ctx, norm_mix[0], m3, bsz, w_aug, prep_w, tabs, False, tm)
    tq, tk = ATTN_TQ, ATTN_TK
    oa_l = _flash(qa_l, (ka_c, ka_l), (va_c, va_l), 1, tq, tk)
    ob_l = _flash(qb_l, (kb_c, kb_l), (vb_c, vb_l), 2, tq, tk)
    oa_c = _flash(qa_c, (ka_c,), (va_c,), 1, tq, tk)
    ob_c = _flash(qb_c, (kb_c,), (vb_c,), 2, tq, tk)
    w_out = ev_w_out[0].astype(BF16)
    ffn_w = _ffn_weights(ffn_w_up[0], ffn_conv_w[0], ffn_conv_b[0], ffn_w_down[0])
    x = _mix_ffn(x, oa_l, ob_l, w_out, norm_ffn[0], m3, None, *ffn_w, None, tm)
    ctx = _mix_ffn(ctx, oa_c, ob_c, w_out, norm_ffn[0], m3, bsz, *ffn_w, None, tm)

    m3 = mods[1].reshape(rows, 1, N_MOD * d)
    w_in = od_w_in[0].astype(BF16)
    z_c = _inproj(ctx, norm_mix[1], m3, bsz, w_in[:, OD_K:], tm)
    cw = jnp.concatenate([c_conv_w[0], c_conv_b[0][None, :], jnp.zeros((SUB - 4, OD_K - OD_C), F32)], axis=0)
    qd, kd, vd, x0, u = _prep_odd(x, norm_mix[1], m3, w_in, cw, chd, shd, 2 * tm)
    kd_c, vd_c = _kv_ctx_odd(z_c)
    od = _window(qd, kd_c, vd_c, kd, vd, d_sink[0])
    zfeat, deltas = (jnp.asarray(t) for t in _hyena_features(n, c_bias.shape[1]))
    w1 = jnp.concatenate([c_filt_w1[0], jnp.zeros((SUB - C_EMB_DIM, c_filt_w1.shape[2]), F32)], axis=0)
    hcat = _filters(zfeat, w1, c_filt_b1[0][None, :], c_filt_w2[0], c_filt_b2[0][None, :], c_filt_w3[0],
                    c_filt_b3[0][None, :], c_filt_w4[0], jnp.concatenate([c_filt_freq[0], jnp.zeros((SUB - 3, c_filt_freq.shape[2]), F32)], axis=0),
                    deltas, tm)
    oc = _hyena_long_conv(u, x0, c_bias[0], hcat, tuple(jnp.asarray(t).astype(BF16) for t in _dft_tables(n)))
    ffn_w = _ffn_weights(ffn_w_up[1], ffn_conv_w[1], ffn_conv_b[1], ffn_w_down[1])
    return _mix_ffn(x, od, oc, od_w_out[0].astype(BF16), norm_ffn[1], m3, None, *ffn_w, final_norm, tm)
```
